```python
import jax, jax.numpy as jnp
from jax import lax
import numpy as np

D_MODEL = 1024
BATCH = 2
SEQ = 8192
DEPTH = 2
DEC_BATCH = 128
DEC_SEQ = 1
PAST_LEN = 8192
PAGE_SIZE = 128

GROUP_W = D_MODEL // 4
EPS = 1e-6
NEG_BIG = -1e30
LRU_W = GROUP_W
LRU_BLOCKS = 4
LRU_BD = LRU_W // LRU_BLOCKS
CONV_W = 4
LRU_C = 8.0
ATT_HEADS = 4
ATT_KV_HEADS = 2
ATT_GROUPS = ATT_HEADS // ATT_KV_HEADS
HEAD_DIM = GROUP_W // ATT_HEADS
ROT_DIM = HEAD_DIM // 4
ROPE_THETA = 500000.0
WINDOW = 128
HG_HEADS = 4
HG_DK = GROUP_W // HG_HEADS
HG_DV = GROUP_W // HG_HEADS
ML_HEADS = 4
ML_D = GROUP_W // ML_HEADS
CHUNK = 64
D_FF = 4 * D_MODEL
PLE_DIM = 256

IN_SIZES = (LRU_W, LRU_W,
            ATT_HEADS * HEAD_DIM, ATT_KV_HEADS * HEAD_DIM, ATT_KV_HEADS * HEAD_DIM,
            GROUP_W, GROUP_W, GROUP_W, GROUP_W,
            GROUP_W, GROUP_W, GROUP_W, GROUP_W, ML_HEADS, ML_HEADS)
D_IN = sum(IN_SIZES)
SPLIT_IDX = tuple(int(v) for v in np.cumsum(IN_SIZES)[:-1])

kernel_name = 'hybrid_rglru_swa_hgrn2_mlstm_step'

F32 = jnp.float32


def _rmsnorm(x, g):
    xf = x.astype(F32)
    y = xf * lax.rsqrt(jnp.mean(xf * xf, axis=-1, keepdims=True) + EPS)
    return (y * g.astype(F32)).astype(x.dtype)


def _rope(x, pos):
    half = ROT_DIM // 2
    inv = jnp.power(ROPE_THETA, -jnp.arange(half, dtype=F32) * (2.0 / ROT_DIM))
    ang = pos.astype(F32)[:, None] * inv[None, :]
    cos = jnp.cos(ang)[None, :, None, :]
    sin = jnp.sin(ang)[None, :, None, :]
    xf = x.astype(F32)
    x1, x2, rest = xf[..., :half], xf[..., half:ROT_DIM], xf[..., ROT_DIM:]
    return jnp.concatenate([x1 * cos - x2 * sin, x2 * cos + x1 * sin, rest], axis=-1).astype(x.dtype)


def _lin_comb(left, right):
    a1, b1 = left
    a2, b2 = right
    return a1 * a2, a2 * b1 + b2


def _rglru(xa, ga, h0, conv0, conv_w, conv_b, wr, br, wi, bi, lam):
    bsz, t, _ = xa.shape
    xp = jnp.concatenate([conv0.astype(xa.dtype), xa], axis=1)
    xc = conv_b
    for j in range(CONV_W):
        xc = xc + xp[:, j:j + t] * conv_w[j]
    xh = xc.reshape(bsz, t, LRU_BLOCKS, LRU_BD)
    r = jax.nn.sigmoid(jnp.einsum('bthi,hij->bthj', xh, wr).reshape(bsz, t, LRU_W) + br)
    ig = jax.nn.sigmoid(jnp.einsum('bthi,hij->bthj', xh, wi).reshape(bsz, t, LRU_W) + bi)
    log_a = -LRU_C * r.astype(F32) * jax.nn.softplus(-lam.astype(F32))
    a = jnp.exp(log_a)
    bx = jnp.sqrt(-jnp.expm1(2.0 * log_a)) * (ig * xc).astype(F32)
    a_cum, b_cum = lax.associative_scan(_lin_comb, (a, bx), axis=1)
    h = a_cum * h0.astype(F32)[:, None, :] + b_cum
    y = h.astype(xa.dtype) * jax.nn.gelu(ga)
    return y, h[:, -1].astype(h0.dtype), xp[:, -(CONV_W - 1):]


def _sink_softmax(s, sinks):
    sk = jnp.broadcast_to(sinks.astype(F32).reshape(ATT_KV_HEADS, ATT_GROUPS, 1, 1), s.shape[:-1] + (1,))
    return jax.nn.softmax(jnp.concatenate([s, sk], axis=-1), axis=-1)[..., :-1]


def _swa_prompt(q, k, v, sinks):
    bsz, t = q.shape[:2]
    nb = t // WINDOW
    qb = q.reshape(bsz, nb, WINDOW, ATT_KV_HEADS, ATT_GROUPS, HEAD_DIM)

    def band(a):
        ap = jnp.pad(a, ((0, 0), (WINDOW, 0), (0, 0), (0, 0))).reshape(bsz, nb + 1, WINDOW, ATT_KV_HEADS, HEAD_DIM)
        return jnp.concatenate([ap[:, :-1], ap[:, 1:]], axis=2)

    kb, vb = band(k), band(v)
    s = jnp.einsum('bnqkgd,bnskd->bnkgqs', qb, kb).astype(F32) * (HEAD_DIM ** -0.5)
    qi = jnp.arange(WINDOW)[:, None]
    kj = jnp.arange(2 * WINDOW)[None, :]
    rel = qi + WINDOW - kj
    kpos = jnp.arange(nb)[:, None, None] * WINDOW - WINDOW + kj[None]
    mask = (rel >= 0)[None] & (rel < WINDOW)[None] & (kpos >= 0)
    s = jnp.where(mask[None, :, None, None], s, NEG_BIG)
    pr = _sink_softmax(s, sinks).astype(v.dtype)
    o = jnp.einsum('bnkgqs,bnskd->bnqkgd', pr, vb)
    return o.reshape(bsz, t, ATT_HEADS * HEAD_DIM)


def _swa_sample(q, k, v, kbuf, vbuf, sinks, pos):
    bsz, t = q.shape[:2]
    wb = kbuf.shape[1]
    kk = jnp.concatenate([kbuf.astype(k.dtype), k], axis=1)
    vv = jnp.concatenate([vbuf.astype(v.dtype), v], axis=1)
    qg = q.reshape(bsz, t, ATT_KV_HEADS, ATT_GROUPS, HEAD_DIM)
    s = jnp.einsum('btkgd,bskd->bkgts', qg, kk).astype(F32) * (HEAD_DIM ** -0.5)
    kpos = pos[0] - wb + jnp.arange(wb + t)
    rel = pos[:, None] - kpos[None, :]
    mask = (rel >= 0) & (rel < WINDOW)
    s = jnp.where(mask, s, NEG_BIG)
    pr = _sink_softmax(s, sinks).astype(v.dtype)
    o = jnp.einsum('bkgts,bskd->btkgd', pr, vv).reshape(bsz, t, ATT_HEADS * HEAD_DIM)
    return o, kk[:, t:], vv[:, t:]


def _chunked_scan(step, carry, xs):
    t = xs[0].shape[1]
    l = CHUNK if t % CHUNK == 0 else t
    nc = t // l

    def to_chunks(a):
        return jnp.moveaxis(a.reshape((a.shape[0], nc, l) + a.shape[2:]), 1, 0)

    carry, ys = lax.scan(step, carry, tuple(to_chunks(a) for a in xs))
    ys = jnp.moveaxis(ys, 0, 1)
    return ys.reshape((ys.shape[0], t) + ys.shape[3:]), carry


def _hgrn2_chunk(s0, inp):
    q, k, logf, v = inp
    l = q.shape[1]
    qf, kf, vf = q.astype(F32), k.astype(F32), v.astype(F32)
    b = jnp.cumsum(logf.astype(F32), axis=1)
    causal = jnp.tril(jnp.ones((l, l), dtype=bool))
    diff = b[:, :, None] - b[:, None, :]
    decay = jnp.exp(jnp.where(causal[None, :, :, None, None], diff, NEG_BIG))
    att = jnp.einsum('bthd,bshd,btshd->btsh', qf, kf, decay)
    s0f = s0.astype(F32)
    o = jnp.einsum('bthd,bhde->bthe', qf * jnp.exp(b), s0f) + jnp.einsum('btsh,bshe->bthe', att, vf)
    b_end = b[:, -1]
    s_new = jnp.exp(b_end)[..., None] * s0f + jnp.einsum('bshd,bshe->bhde', kf * jnp.exp(b_end[:, None] - b), vf)
    return s_new.astype(s0.dtype), o.astype(q.dtype)


def _hgrn2(cq, cf, ci, cg, lb, s0, norm_g):
    bsz, t, _ = cq.shape
    q = jax.nn.silu(cq).reshape(bsz, t, HG_HEADS, HG_DK)
    lbf = lb.astype(F32).reshape(HG_HEADS, HG_DK)
    f = lbf + (1.0 - lbf) * jax.nn.sigmoid(cf.astype(F32).reshape(bsz, t, HG_HEADS, HG_DK))
    logf = jnp.log(f)
    k = 1.0 - f
    v = ci.reshape(bsz, t, HG_HEADS, HG_DV)
    o, s_new = _chunked_scan(_hgrn2_chunk, s0, (q, k, logf, v))
    o = _rmsnorm(o, norm_g).reshape(bsz, t, GROUP_W) * jax.nn.silu(cg)
    return o, s_new


def _mlstm_chunk(carry, inp):
    c0, n0, m0 = carry
    q, k, v, ig, logf = inp
    l = q.shape[1]
    qf, kf, vf = q.astype(F32), k.astype(F32), v.astype(F32)
    c0f, n0f, m0f = c0.astype(F32), n0.astype(F32), m0.astype(F32)
    causal = jnp.tril(jnp.ones((l, l), dtype=bool))
    fcum = jnp.cumsum(logf, axis=1)
    a_inter = fcum + m0f[:, None, :]
    dmat = fcum[:, :, None, :] - fcum[:, None, :, :] + ig[:, None, :, :]
    dmat = jnp.where(causal[None, :, :, None], dmat, NEG_BIG)
    m_t = jnp.maximum(a_inter, jnp.max(dmat, axis=2))
    wmat = jnp.exp(dmat - m_t[:, :, None, :])
    s = jnp.einsum('bthd,bshd->btsh', qf, kf) * wmat
    inter = jnp.exp(a_inter - m_t)
    num = inter[..., None] * jnp.einsum('bthd,bhde->bthe', qf, c0f) + jnp.einsum('btsh,bshe->bthe', s, vf)
    den = inter * jnp.einsum('bthd,bhd->bth', qf, n0f) + jnp.sum(s, axis=2)
    h = num / jnp.maximum(jnp.abs(den), jnp.exp(-m_t))[..., None]
    g_end = fcum[:, -1:, :] - fcum + ig
    f_end = fcum[:, -1] + m0f
    m_new = jnp.maximum(f_end, jnp.max(g_end, axis=1))
    w_end = jnp.exp(g_end - m_new[:, None])
    decay0 = jnp.exp(f_end - m_new)
    c_new = decay0[..., None, None] * c0f + jnp.einsum('bsh,bshd,bshe->bhde', w_end, kf, vf)
    n_new = decay0[..., None] * n0f + jnp.einsum('bsh,bshd->bhd', w_end, kf)
    return (c_new.astype(c0.dtype), n_new.astype(n0.dtype), m_new.astype(m0.dtype)), h.astype(q.dtype)


def _mlstm(dq, dk, dv, do, di, df, ib, fb, c0, n0, m0, norm_g):
    bsz, t, _ = dq.shape
    shp = (bsz, t, ML_HEADS, ML_D)
    q = dq.reshape(shp)
    k = dk.reshape(shp) * (ML_D ** -0.5)
    v = dv.reshape(shp)
    ig = (di + ib).astype(F32)
    logf = jax.nn.log_sigmoid((df + fb).astype(F32))
    hh, (c_new, n_new, m_new) = _chunked_scan(_mlstm_chunk, (c0, n0, m0), (q, k, v, ig, logf))
    y = _rmsnorm(hh, norm_g).reshape(bsz, t, GROUP_W) * jax.nn.sigmoid(do)
    return y, c_new, n_new, m_new


def _layer(h, p_l, lst, w, lb, pos, wbuf, prompt):
    bsz, t, _ = h.shape
    h0, conv0, kbuf, vbuf, s0, c0, n0, m0 = lst
    u = _rmsnorm(h, w['norm1_g']) @ w['w_in']
    (a_x, a_g, b_q, b_k, b_v, c_q, c_f, c_i, c_g,
     d_q, d_k, d_v, d_o, d_i, d_f) = jnp.split(u, SPLIT_IDX, axis=-1)
    y_a, h_new, conv_new = _rglru(a_x, a_g, h0, conv0, w['conv_w'], w['conv_b'], w['lru_wr'],
                                  w['lru_br'], w['lru_wi'], w['lru_bi'], w['lru_lam'])
    q = _rope(_rmsnorm(b_q.reshape(bsz, t, ATT_HEADS, HEAD_DIM), w['q_norm_g']), pos)
    k = _rope(_rmsnorm(b_k.reshape(bsz, t, ATT_KV_HEADS, HEAD_DIM), w['k_norm_g']), pos)
    v = b_v.reshape(bsz, t, ATT_KV_HEADS, HEAD_DIM)
    if prompt:
        y_b = _swa_prompt(q, k, v, w['attn_sinks'])
        k_new, v_new = k[:, t - wbuf:], v[:, t - wbuf:]
    else:
        y_b, k_new, v_new = _swa_sample(q, k, v, kbuf, vbuf, w['attn_sinks'], pos)
    y_c, s_new = _hgrn2(c_q, c_f, c_i, c_g, lb, s0, w['hgrn_norm_g'])
    y_d, c_new, n_new, m_new = _mlstm(d_q, d_k, d_v, d_o, d_i, d_f, w['mlstm_ib'], w['mlstm_fb'],
                                      c0, n0, m0, w['mlstm_norm_g'])
    h = h + jnp.concatenate([y_a, y_b, y_c, y_d], axis=-1) @ w['w_out']
    f = _rmsnorm(h, w['norm2_g']) @ w['w_up']
    h = h + jnp.square(jax.nn.relu(f)) @ w['w_down']
    h = h + jax.nn.sigmoid(h @ w['w_ple_gate']) * (p_l @ w['w_ple_proj'])
    return h, (h_new, conv_new, k_new, v_new, s_new, c_new, n_new, m_new)


def _run_group(x, p, st, wts, lbs, pos, wbuf, prompt):
    bsz = x.shape[0]
    dt = x.dtype
    h = x
    outs = [[] for _ in range(8)]
    for l in range(DEPTH):
        if prompt:
            lst = (jnp.zeros((bsz, LRU_W), dt), jnp.zeros((bsz, CONV_W - 1, LRU_W), dt), None, None,
                   jnp.zeros((bsz, HG_HEADS, HG_DK, HG_DV), dt), jnp.zeros((bsz, ML_HEADS, ML_D, ML_D), dt),
                   jnp.zeros((bsz, ML_HEADS, ML_D), dt), jnp.zeros((bsz, ML_HEADS), dt))
        else:
            lst = tuple(s[l] for s in st)
        lw = {name: arr[l] for name, arr in wts.items()}
        h, new = _layer(h, p[l], lst, lw, lbs[l], pos, wbuf, prompt)
        for o, n in zip(outs, new):
            o.append(n)
    return h, tuple(jnp.stack(o) for o in outs)


def setup_inputs(seed: int = 0) -> dict:
    key = jax.random.key(seed)
    ks = iter(jax.random.split(key, 48))

    def nrm(shape, scale):
        return scale * jax.random.normal(next(ks), shape, jnp.float32)

    wbuf = min(WINDOW, PAST_LEN)
    lam_u = jax.random.uniform(next(ks), (DEPTH, LRU_W), jnp.float32, 0.9, 0.999)
    return {
        'x_prompt': nrm((BATCH, SEQ, D_MODEL), 1.0),
        'x_sample': nrm((DEC_BATCH, DEC_SEQ, D_MODEL), 1.0),
        'p_prompt': nrm((DEPTH, BATCH, SEQ, PLE_DIM), 1.0),
        'p_sample': nrm((DEPTH, DEC_BATCH, DEC_SEQ, PLE_DIM), 1.0),
        'state_rglru_h': nrm((DEPTH, DEC_BATCH, LRU_W), 0.5),
        'state_rglru_conv': nrm((DEPTH, DEC_BATCH, CONV_W - 1, LRU_W), 1.0),
        'cache_swa_k': nrm((DEPTH, DEC_BATCH, wbuf, ATT_KV_HEADS, HEAD_DIM), 1.0),
        'cache_swa_v': nrm((DEPTH, DEC_BATCH, wbuf, ATT_KV_HEADS, HEAD_DIM), 1.0),
        'state_hgrn_s': nrm((DEPTH, DEC_BATCH, HG_HEADS, HG_DK, HG_DV), 0.5),
        'state_mlstm_c': nrm((DEPTH, DEC_BATCH, ML_HEADS, ML_D, ML_D), 0.3),
        'state_mlstm_n': nrm((DEPTH, DEC_BATCH, ML_HEADS, ML_D), 0.3),
        'state_mlstm_m': nrm((DEPTH, DEC_BATCH, ML_HEADS), 1.0),
        'norm1_g': 1.0 + nrm((DEPTH, D_MODEL), 0.05),
        'w_in': nrm((DEPTH, D_MODEL, D_IN), D_MODEL ** -0.5),
        'conv_w': nrm((DEPTH, CONV_W, LRU_W), CONV_W ** -0.5),
        'conv_b': nrm((DEPTH, LRU_W), 0.02),
        'lru_wr': nrm((DEPTH, LRU_BLOCKS, LRU_BD, LRU_BD), LRU_BD ** -0.5),
        'lru_br': nrm((DEPTH, LRU_W), 0.02),
        'lru_wi': nrm((DEPTH, LRU_BLOCKS, LRU_BD, LRU_BD), LRU_BD ** -0.5),
        'lru_bi': nrm((DEPTH, LRU_W), 0.02),
        'lru_lam': jnp.log(lam_u) - jnp.log1p(-lam_u),
        'q_norm_g': 1.0 + nrm((DEPTH, HEAD_DIM), 0.05),
        'k_norm_g': 1.0 + nrm((DEPTH, HEAD_DIM), 0.05),
        'attn_sinks': nrm((DEPTH, ATT_HEADS), 0.5),
        'hgrn_gamma': nrm((DEPTH, HG_HEADS * HG_DK), 1.0),
        'hgrn_norm_g': 1.0 + nrm((DEPTH, HG_DV), 0.05),
        'mlstm_ib': nrm((DEPTH, ML_HEADS), 0.1),
        'mlstm_fb': jnp.linspace(3.0, 6.0, ML_HEADS, dtype=jnp.float32)[None] + nrm((DEPTH, ML_HEADS), 0.1),
        'mlstm_norm_g': 1.0 + nrm((DEPTH, ML_D), 0.05),
        'w_out': nrm((DEPTH, D_MODEL, D_MODEL), D_MODEL ** -0.5),
        'norm2_g': 1.0 + nrm((DEPTH, D_MODEL), 0.05),
        'w_up': nrm((DEPTH, D_MODEL, D_FF), D_MODEL ** -0.5),
        'w_down': nrm((DEPTH, D_FF, D_MODEL), D_FF ** -0.5),
        'w_ple_gate': nrm((DEPTH, D_MODEL, D_MODEL), D_MODEL ** -0.5),
        'w_ple_proj': nrm((DEPTH, PLE_DIM, D_MODEL), PLE_DIM ** -0.5),
    }


def reference(x_prompt, x_sample, p_prompt, p_sample, state_rglru_h, state_rglru_conv, cache_swa_k,
              cache_swa_v, state_hgrn_s, state_mlstm_c, state_mlstm_n, state_mlstm_m, norm1_g, w_in,
              conv_w, conv_b, lru_wr, lru_br, lru_wi, lru_bi, lru_lam, q_norm_g, k_norm_g, attn_sinks,
              hgrn_gamma, hgrn_norm_g, mlstm_ib, mlstm_fb, mlstm_norm_g, w_out, norm2_g, w_up, w_down,
              w_ple_gate, w_ple_proj):
    wts = {'norm1_g': norm1_g, 'w_in': w_in, 'conv_w': conv_w, 'conv_b': conv_b, 'lru_wr': lru_wr,
           'lru_br': lru_br, 'lru_wi': lru_wi, 'lru_bi': lru_bi, 'lru_lam': lru_lam,
           'q_norm_g': q_norm_g, 'k_norm_g': k_norm_g, 'attn_sinks': attn_sinks,
           'hgrn_norm_g': hgrn_norm_g, 'mlstm_ib': mlstm_ib, 'mlstm_fb': mlstm_fb,
           'mlstm_norm_g': mlstm_norm_g, 'w_out': w_out, 'norm2_g': norm2_g, 'w_up': w_up,
           'w_down': w_down, 'w_ple_gate': w_ple_gate, 'w_ple_proj': w_ple_proj}
    pf = jax.nn.softmax(hgrn_gamma.astype(F32), axis=0)
    lbs = jnp.cumsum(pf, axis=0) - pf[:1]
    wbuf = cache_swa_k.shape[2]
    pos_p = jnp.arange(x_prompt.shape[1])
    pos_s = PAST_LEN + jnp.arange(x_sample.shape[1])
    y_prompt, st_p = _run_group(x_prompt, p_prompt, None, wts, lbs, pos_p, wbuf, True)
    st_in = (state_rglru_h, state_rglru_conv, cache_swa_k, cache_swa_v, state_hgrn_s,
             state_mlstm_c, state_mlstm_n, state_mlstm_m)
    y_sample, st_s = _run_group(x_sample, p_sample, st_in, wts, lbs, pos_s, wbuf, False)
    (p_h, p_conv, p_k, p_v, p_s, p_c, p_n, p_m) = st_p
    (s_h, s_conv, s_k, s_v, s_s, s_c, s_n, s_m) = st_s
    return (y_prompt, y_sample, p_h, p_conv, p_k, p_v, p_s, p_c, p_n, p_m,
            s_h, s_conv, s_k, s_v, s_s, s_c, s_n, s_m)
```

```python
import functools

import jax
import jax.numpy as jnp
from jax import lax
from jax.experimental import pallas as pl
from jax.experimental.pallas import tpu as pltpu

F32 = jnp.float32
BF16 = jnp.bfloat16

D_MODEL = 1024
GROUP_W = 256
HEAD_DIM = 64
N_HEADS = 4
EPS = 1e-6
NEG_BIG = -1e30
LRU_C = 8.0
CONV_W = 4
ROT_DIM = 16
ROPE_THETA = 500000.0
WINDOW = 128
D_FF = 4096
PLE_DIM = 256
LANES = 128
BLK = 128

C_AX, C_AG, C_BQ, C_BK, C_BV = 0, 256, 512, 768, 896
C_CQ, C_CF, C_CI, C_CG = 1024, 1280, 1536, 1792
C_DQ, C_DK, C_DV, C_DO = 2048, 2304, 2560, 2816
C_DI, C_DF = 3072, 3200
N_IN = 3328

VMEM_LIMIT = 56 * 1024 * 1024


def _dot(a, b):
    return jnp.dot(a, b, preferred_element_type=F32)


def _dot_nt(a, b):
    return lax.dot_general(a, b, (((1,), (1,)), ((), ())), preferred_element_type=F32)


def _sigmoid(x):
    return jax.nn.sigmoid(x)


def _gelu_tanh(x):
    return 0.5 * x * (1.0 + jnp.tanh(0.7978845608028654 * (x + 0.044715 * (x * x * x))))


def _log_sigmoid(x):
    return jnp.minimum(x, 0.0) - jnp.log1p(jnp.exp(-jnp.abs(x)))


def _softplus(x):
    return jnp.maximum(x, 0.0) + jnp.log1p(jnp.exp(-jnp.abs(x)))


def _rms_rows(x, g):
    return x * lax.rsqrt(jnp.mean(x * x, axis=-1, keepdims=True) + EPS) * g


def _seg_mean_sq(x, ones_bd):
    sq = x * x
    hi = sq.astype(BF16)
    lo = (sq - hi.astype(F32)).astype(BF16)
    return (_dot(hi, ones_bd) + _dot(lo, ones_bd)) * (1.0 / HEAD_DIM)


def _seg_rms(x, g, ones_bd):
    return x * lax.rsqrt(_seg_mean_sq(x, ones_bd) + EPS) * g


def _rope(x, c, sa, sb):
    w = x.shape[1]
    up = pltpu.roll(x, w - ROT_DIM // 2, 1)
    dn = pltpu.roll(x, ROT_DIM // 2, 1)
    return x * c + up * sa + dn * sb


def _lb_from_gamma(gamma, layer):
    mx = jnp.max(gamma, axis=0, keepdims=True)
    e = jnp.exp(gamma - mx)
    tot = jnp.sum(e, axis=0, keepdims=True)
    lb = jnp.zeros_like(tot)
    for i in range(1, layer + 1):
        lb = lb + e[i:i + 1, :] / tot
    return lb


def _iota(shape, axis):
    return lax.broadcasted_iota(jnp.int32, shape, axis)


def _swa_block(q, k, v, kst_ref, vst_ref, sink, first):
    kk = jnp.concatenate([kst_ref[...], k], axis=0)
    vv = jnp.concatenate([vst_ref[...], v], axis=0)
    qi = _iota((BLK, 2 * BLK), 0)
    kj = _iota((BLK, 2 * BLK), 1)
    valid = (kj > qi) & (kj <= qi + WINDOW) & ((kj >= BLK) | jnp.logical_not(first))
    outs = []
    for h in range(N_HEADS):
        kv = h // 2
        qh = q[:, h * HEAD_DIM:(h + 1) * HEAD_DIM].astype(BF16)
        kh = kk[:, kv * HEAD_DIM:(kv + 1) * HEAD_DIM].astype(BF16)
        vh = vv[:, kv * HEAD_DIM:(kv + 1) * HEAD_DIM].astype(BF16)
        s = _dot_nt(qh, kh) * (HEAD_DIM ** -0.5)
        s = jnp.where(valid, s, NEG_BIG)
        sk = sink[:, h:h + 1]
        mx = jnp.maximum(jnp.max(s, axis=-1, keepdims=True), sk)
        p = jnp.exp(s - mx)
        den = jnp.sum(p, axis=-1, keepdims=True) + jnp.exp(sk - mx)
        outs.append(_dot(p.astype(BF16), vh) / den)
    kst_ref[...] = k
    vst_ref[...] = v
    return jnp.concatenate(outs, axis=1)


def _hgrn_block(cq, cf, ci, cg, lb, hg, sst_ref, ones_bd):
    q = cq * _sigmoid(cq)
    f = lb + (1.0 - lb) * _sigmoid(cf)
    logf = jnp.log(f)
    k = 1.0 - f
    row = _iota((BLK, GROUP_W), 0)
    row_l = _iota((BLK, BLK), 0)
    col_l = _iota((BLK, BLK), 1)
    lane = _iota((BLK, LANES), 1)
    in_head = (lane < HEAD_DIM, lane >= HEAD_DIM)

    att = []
    kb = k.astype(BF16)
    for p in range(2):
        qp = q[:, p * LANES:(p + 1) * LANES]
        kp = kb[:, p * LANES:(p + 1) * LANES]
        for hh in range(2):
            qm = jnp.where(in_head[hh], qp, 0.0).astype(BF16)
            att.append(jnp.where(row_l == col_l, _dot_nt(qm, kp), 0.0))

    c = logf
    tot = logf
    lev = 1
    while (1 << lev) <= BLK:
        half = 1 << (lev - 1)
        right = ((row >> (lev - 1)) & 1) == 1
        tot_l = pltpu.roll(tot, half, 0)
        tot_r = pltpu.roll(tot, BLK - half, 0)
        x = jnp.where(right, q, k) * jnp.exp(jnp.where(right, c, tot - c))
        qx = jnp.where(right, x, 0.0)
        kx = jnp.where(right, 0.0, x).astype(BF16)
        same = (row_l >> lev) == (col_l >> lev)
        for p in range(2):
            qp = qx[:, p * LANES:(p + 1) * LANES]
            kp = kx[:, p * LANES:(p + 1) * LANES]
            for hh in range(2):
                qm = jnp.where(in_head[hh], qp, 0.0).astype(BF16)
                att[2 * p + hh] = att[2 * p + hh] + jnp.where(same, _dot_nt(qm, kp), 0.0)
        c = c + jnp.where(right, tot_l, 0.0)
        tot = tot + jnp.where(right, tot_l, tot_r)
        lev += 1

    qe = (q * jnp.exp(c)).astype(BF16)
    ke = (k * jnp.exp(tot - c)).astype(BF16)
    etot = jnp.exp(tot[0:1, :])
    same_head = (row_l >= HEAD_DIM) == (col_l >= HEAD_DIM)
    outs = []
    for p in range(2):
        sl = slice(p * LANES, (p + 1) * LANES)
        st = sst_ref[p]
        vp = ci[:, sl]
        o = _dot_nt(qe[:, sl], st.astype(BF16))
        for hh in range(2):
            vm = jnp.where(in_head[hh], vp, 0.0).astype(BF16)
            o = o + _dot(att[2 * p + hh].astype(BF16), vm)
        upd = _dot(vp.T.astype(BF16), ke[:, sl])
        sst_ref[p] = st * etot[:, sl] + jnp.where(same_head, upd, 0.0)
        outs.append(o)
    o = jnp.concatenate(outs, axis=1)
    return _seg_rms(o, hg, ones_bd) * (cg * _sigmoid(cg))


def _mlstm_block(dq, dk, dv, do, gi, gf, ib, fb, mg, cst_ref, mst_ref, ones_bd):
    k = dk * (HEAD_DIM ** -0.5)
    ig = gi + ib
    logf = _log_sigmoid(gf + fb)
    row = _iota((BLK, LANES), 0)
    lane = _iota((BLK, LANES), 1)
    row_l = _iota((BLK, BLK), 0)
    col_l = _iota((BLK, BLK), 1)
    causal = col_l <= row_l
    in_head = (lane < HEAD_DIM, lane >= HEAD_DIM)
    one_col = jnp.where(lane == HEAD_DIM, 1.0, 0.0)

    fc = logf
    s = 1
    while s < BLK:
        fc = fc + jnp.where(row >= s, pltpu.roll(fc, s, 0), 0.0)
        s *= 2
    g = ig - fc
    cm = g
    s = 1
    while s < BLK:
        cm = jnp.maximum(cm, jnp.where(row >= s, pltpu.roll(cm, s, 0), NEG_BIG))
        s *= 2
    m0 = mst_ref[...]
    mrun = jnp.maximum(cm, m0)
    inter = jnp.exp(m0 - mrun)
    eneg = jnp.exp(-(fc + mrun))
    m_last = mrun[BLK - 1:BLK, :]
    wend = jnp.exp(g - m_last)
    dec0 = jnp.exp(m0 - m_last)
    mst_ref[...] = fc[BLK - 1:BLK, :] + m_last
    g_t = g.T

    outs = []
    for p in range(2):
        sl = slice(p * LANES, (p + 1) * LANES)
        qp, kp, vp = dq[:, sl], k[:, sl], dv[:, sl]
        kpb = kp.astype(BF16)
        v_sw = pltpu.roll(vp, HEAD_DIM, 1)
        hv = []
        for hh in range(2):
            h = 2 * p + hh
            qm = jnp.where(in_head[hh], qp, 0.0).astype(BF16)
            sc = _dot_nt(qm, kpb)
            w = jnp.exp(jnp.where(causal, g_t[h:h + 1, :] - mrun[:, h:h + 1], NEG_BIG))
            sw = (sc * w).astype(BF16)
            vaug = jnp.where(in_head[0], vp if hh == 0 else v_sw, one_col).astype(BF16)
            cst = cst_ref[h]
            nd = inter[:, h:h + 1] * _dot(qm, cst.astype(BF16)) + _dot(sw, vaug)
            den = nd[:, HEAD_DIM:HEAD_DIM + 1]
            hv.append(nd / jnp.maximum(jnp.abs(den), eneg[:, h:h + 1]))
            kw = jnp.where(in_head[hh], kp, 0.0) * wend[:, h:h + 1]
            cst_ref[h] = dec0[:, h:h + 1] * cst + _dot(kw.T.astype(BF16), vaug)
        outs.append(jnp.where(in_head[0], hv[0], pltpu.roll(hv[1], HEAD_DIM, 1)))
    hcat = jnp.concatenate(outs, axis=1)
    return _seg_rms(hcat, mg, ones_bd) * _sigmoid(do)


def _mixer_kernel(h_ref, g1_ref, win_ref, convw_ref, convb_ref, wr_ref, br_ref, wi_ref, bi_ref,
                  lam_ref, qg_ref, kg_ref, sink_ref, rc_ref, ra_ref, rb_ref, gamma_ref, hg_ref,
                  ib_ref, fb_ref, mg_ref, ones_ref,
                  y_ref, hl_ref, conv_ref, kst_ref, vst_ref, sst_ref, cst_ref, mst_ref,
                  xbuf, *, layer, tm):
    t = pl.program_id(1)

    @pl.when(t == 0)
    def _init():
        hl_ref[...] = jnp.zeros_like(hl_ref)
        conv_ref[...] = jnp.zeros_like(conv_ref)
        kst_ref[...] = jnp.zeros_like(kst_ref)
        vst_ref[...] = jnp.zeros_like(vst_ref)
        sst_ref[...] = jnp.zeros_like(sst_ref)
        cst_ref[...] = jnp.zeros_like(cst_ref)
        mst_ref[...] = jnp.zeros_like(mst_ref)

    nb = _rms_rows(h_ref[...], g1_ref[...]).astype(BF16)
    ones_bd = ones_ref[...]

    def proj(c0, w):
        return _dot(nb, win_ref[:, c0:c0 + w])

    xa = proj(C_AX, GROUP_W)
    ga = proj(C_AG, GROUP_W)
    xbuf[0:8, :] = conv_ref[...]
    xbuf[8:8 + tm, :] = xa
    cw = convw_ref[...]
    xc = (convb_ref[...] + xbuf[5:5 + tm, :] * cw[0:1, :] + xbuf[6:6 + tm, :] * cw[1:2, :]
          + xbuf[7:7 + tm, :] * cw[2:3, :] + xa * cw[3:4, :])
    conv_ref[...] = xbuf[tm:tm + 8, :]
    xcb = xc.astype(BF16)
    r = _sigmoid(_dot(xcb, wr_ref[...]) + br_ref[...])
    ig = _sigmoid(_dot(xcb, wi_ref[...]) + bi_ref[...])
    log_a = (-LRU_C) * r * _softplus(-lam_ref[...])
    a = jnp.exp(log_a)
    bx = jnp.sqrt(1.0 - jnp.exp(2.0 * log_a)) * (ig * xc)
    row = _iota((tm, GROUP_W), 0)
    s = 1
    while s < tm:
        keep = row >= s
        a_s = pltpu.roll(a, s, 0)
        b_s = pltpu.roll(bx, s, 0)
        bx = jnp.where(keep, a * b_s + bx, bx)
        a = jnp.where(keep, a * a_s, a)
        s *= 2
    hseq = a * hl_ref[...] + bx
    hl_ref[...] = hseq[tm - 1:tm, :]
    y_ref[:, 0:GROUP_W] = (hseq * _gelu_tanh(ga)).astype(y_ref.dtype)

    rc, ra, rb = rc_ref[...], ra_ref[...], rb_ref[...]
    q = _seg_rms(proj(C_BQ, GROUP_W), qg_ref[...], ones_bd)
    q = _rope(q, jnp.concatenate([rc, rc], 1), jnp.concatenate([ra, ra], 1), jnp.concatenate([rb, rb], 1))
    k = _seg_rms(proj(C_BK, LANES), kg_ref[...], ones_bd[0:LANES, 0:LANES])
    k = _rope(k, rc, ra, rb)
    v = proj(C_BV, LANES)
    sink = sink_ref[...]
    for j in range(tm // BLK):
        rs = slice(j * BLK, (j + 1) * BLK)
        first = (t == 0) if j == 0 else False
        yb = _swa_block(q[rs], k[rs], v[rs], kst_ref, vst_ref, sink, first)
        y_ref[rs, GROUP_W:2 * GROUP_W] = yb.astype(y_ref.dtype)

    lb = _lb_from_gamma(gamma_ref[...], layer)
    cq, cf, ci, cg = (proj(C_CQ, GROUP_W), proj(C_CF, GROUP_W), proj(C_CI, GROUP_W), proj(C_CG, GROUP_W))
    hg = hg_ref[...]
    for j in range(tm // BLK):
        rs = slice(j * BLK, (j + 1) * BLK)
        yc = _hgrn_block(cq[rs], cf[rs], ci[rs], cg[rs], lb, hg, sst_ref, ones_bd)
        y_ref[rs, 2 * GROUP_W:3 * GROUP_W] = yc.astype(y_ref.dtype)

    dq, dk, dv, do = (proj(C_DQ, GROUP_W), proj(C_DK, GROUP_W), proj(C_DV, GROUP_W), proj(C_DO, GROUP_W))
    gi, gf = proj(C_DI, LANES), proj(C_DF, LANES)
    ib, fb, mg = ib_ref[...], fb_ref[...], mg_ref[...]
    for j in range(tm // BLK):
        rs = slice(j * BLK, (j + 1) * BLK)
        yd = _mlstm_block(dq[rs], dk[rs], dv[rs], do[rs], gi[rs], gf[rs], ib, fb, mg,
                          cst_ref, mst_ref, ones_bd)
        y_ref[rs, 3 * GROUP_W:4 * GROUP_W] = yd.astype(y_ref.dtype)


def _full(shape):
    nd = len(shape)
    return pl.BlockSpec(shape, lambda *_: (0,) * nd)


def _prompt_mixers(h, lw, layer, tm):
    bsz, t, _ = h.shape
    nt = t // tm
    kern = functools.partial(_mixer_kernel, layer=layer, tm=tm)
    small = [lw['conv_w'], lw['conv_b'], lw['wr_bd'], lw['lru_br'], lw['wi_bd'], lw['lru_bi'],
             lw['lru_lam'], lw['qg'], lw['kg'], lw['sinks']]
    tail = [lw['gamma'], lw['hg'], lw['ib'], lw['fb'], lw['mg'], lw['ones_bd']]
    rope_spec = pl.BlockSpec((tm, LANES), lambda b, i: (i, 0))
    in_specs = ([pl.BlockSpec((None, tm, D_MODEL), lambda b, i: (b, i, 0)), _full(lw['norm1_g'].shape),
                 _full(lw['w_in'].shape)] + [_full(a.shape) for a in small]
                + [rope_spec, rope_spec, rope_spec] + [_full(a.shape) for a in tail])

    def st_spec(*dims):
        return pl.BlockSpec((None,) + dims, lambda b, i: (b,) + (0,) * len(dims))

    out_shape = (jax.ShapeDtypeStruct((bsz, t, D_MODEL), BF16),
                 jax.ShapeDtypeStruct((bsz, 1, GROUP_W), F32),
                 jax.ShapeDtypeStruct((bsz, 8, GROUP_W), F32),
                 jax.ShapeDtypeStruct((bsz, BLK, LANES), F32),
                 jax.ShapeDtypeStruct((bsz, BLK, LANES), F32),
                 jax.ShapeDtypeStruct((bsz, 2, LANES, LANES), F32),
                 jax.ShapeDtypeStruct((bsz, N_HEADS, LANES, LANES), F32),
                 jax.ShapeDtypeStruct((bsz, 1, LANES), F32))
    out_specs = (pl.BlockSpec((None, tm, D_MODEL), lambda b, i: (b, i, 0)),
                 st_spec(1, GROUP_W), st_spec(8, GROUP_W), st_spec(BLK, LANES), st_spec(BLK, LANES),
                 st_spec(2, LANES, LANES), st_spec(N_HEADS, LANES, LANES), st_spec(1, LANES))
    return pl.pallas_call(
        kern, grid=(bsz, nt), in_specs=in_specs, out_specs=out_specs, out_shape=out_shape,
        scratch_shapes=[pltpu.VMEM((tm + 8, GROUP_W), F32)],
        compiler_params=pltpu.CompilerParams(dimension_semantics=("arbitrary", "arbitrary"),
                                             vmem_limit_bytes=VMEM_LIMIT),
        name=f"prompt_mixers_l{layer}",
    )(h, lw['norm1_g'], lw['w_in'], *small, lw['rope_c'], lw['rope_a'], lw['rope_b'], *tail)


def _ffn_kernel(h_ref, y_ref, p_ref, wout_ref, g2_ref, wup_ref, wdn_ref, wg_ref, wp_ref, o_ref):
    h = h_ref[...] + _dot(y_ref[...], wout_ref[...])
    nb = _rms_rows(h, g2_ref[...]).astype(BF16)
    acc = h
    step = 1024
    for c in range(0, D_FF, step):
        f = jnp.maximum(_dot(nb, wup_ref[:, c:c + step]), 0.0)
        acc = acc + _dot((f * f).astype(BF16), wdn_ref[c:c + step, :])
    gate = _sigmoid(_dot(acc.astype(BF16), wg_ref[...]))
    o_ref[...] = acc + gate * _dot(p_ref[...].astype(BF16), wp_ref[...])


def _ffn(h2, y2, p2, lw, tm, name):
    n = h2.shape[0]
    row = lambda w: pl.BlockSpec((tm, w), lambda i: (i, 0))
    ws = [lw['w_out'], lw['norm2_g'], lw['w_up'], lw['w_down'], lw['w_gate'], lw['w_proj']]
    return pl.pallas_call(
        _ffn_kernel, grid=(n // tm,),
        in_specs=[row(D_MODEL), row(D_MODEL), row(PLE_DIM)]
        + [pl.BlockSpec(a.shape, lambda i: (0, 0), pipeline_mode=pl.Buffered(1)) for a in ws],
        out_specs=row(D_MODEL), out_shape=jax.ShapeDtypeStruct((n, D_MODEL), F32),
        compiler_params=pltpu.CompilerParams(dimension_semantics=("arbitrary",),
                                             vmem_limit_bytes=VMEM_LIMIT),
        name=name,
    )(h2, y2, p2, *ws)


def _sample_in_kernel(x_ref, g1_ref, win_ref, wdi_ref, wdf_ref, convw_ref, convb_ref, wr_ref, br_ref,
                      wi_ref, bi_ref, lam_ref, qg_ref, kg_ref, rc_ref, ra_ref, rb_ref, gamma_ref,
                      ibe_ref, fbe_ref, ones_ref, h0_ref, conv0_ref, n0_ref, m0_ref,
                      ya_ref, hn_ref, convn_ref, q_ref, k_ref, v_ref,
                      hf_ref, hk_ref, hq_ref, hv_ref, hcg_ref,
                      mk_ref, mq_ref, mv_ref, mdec_ref, mw_ref, mn_ref, mden_ref, meneg_ref, mm_ref, mdo_ref,
                      *, layer):
    nb = _rms_rows(x_ref[...], g1_ref[...]).astype(BF16)
    ones_bd = ones_ref[...]
    nsm = x_ref.shape[0]

    def proj(c0, w):
        return _dot(nb, win_ref[:, c0:c0 + w])

    def per_head(ref, x):
        for h in range(N_HEADS):
            ref[h] = x[:, h * HEAD_DIM:(h + 1) * HEAD_DIM]

    xa = proj(C_AX, GROUP_W)
    ga = proj(C_AG, GROUP_W)
    cw = convw_ref[...]
    c0 = conv0_ref[...]
    xc = (convb_ref[...] + c0[:, 0:GROUP_W] * cw[0:1, :] + c0[:, GROUP_W:2 * GROUP_W] * cw[1:2, :]
          + c0[:, 2 * GROUP_W:3 * GROUP_W] * cw[2:3, :] + xa * cw[3:4, :])
    convn_ref[:, 0:2 * GROUP_W] = c0[:, GROUP_W:3 * GROUP_W]
    convn_ref[:, 2 * GROUP_W:3 * GROUP_W] = xa
    xcb = xc.astype(BF16)
    r = _sigmoid(_dot(xcb, wr_ref[...]) + br_ref[...])
    ig = _sigmoid(_dot(xcb, wi_ref[...]) + bi_ref[...])
    log_a = (-LRU_C) * r * _softplus(-lam_ref[...])
    hn = jnp.exp(log_a) * h0_ref[...] + jnp.sqrt(1.0 - jnp.exp(2.0 * log_a)) * (ig * xc)
    hn_ref[...] = hn
    ya_ref[...] = hn * _gelu_tanh(ga)

    rc, ra, rb = rc_ref[...], ra_ref[...], rb_ref[...]
    q = _seg_rms(proj(C_BQ, GROUP_W), qg_ref[...], ones_bd)
    q_ref[...] = _rope(q, jnp.concatenate([rc, rc], 1), jnp.concatenate([ra, ra], 1),
                       jnp.concatenate([rb, rb], 1))
    k = _seg_rms(proj(C_BK, LANES), kg_ref[...], ones_bd[0:LANES, 0:LANES])
    k_ref[...] = _rope(k, rc, ra, rb)
    v_ref[...] = proj(C_BV, LANES)

    lb = _lb_from_gamma(gamma_ref[...], layer)
    cq, cf, ci, cg = (proj(C_CQ, GROUP_W), proj(C_CF, GROUP_W), proj(C_CI, GROUP_W), proj(C_CG, GROUP_W))
    f = lb + (1.0 - lb) * _sigmoid(cf)
    per_head(hf_ref, f)
    per_head(hk_ref, 1.0 - f)
    per_head(hq_ref, cq * _sigmoid(cq))
    per_head(hv_ref, ci)
    hcg_ref[...] = cg * _sigmoid(cg)

    dq, dk, dv, do = (proj(C_DQ, GROUP_W), proj(C_DK, GROUP_W), proj(C_DV, GROUP_W), proj(C_DO, GROUP_W))
    k_m = dk * (HEAD_DIM ** -0.5)
    ig_e = _dot(nb, wdi_ref[...]) + ibe_ref[...]
    lf_e = _log_sigmoid(_dot(nb, wdf_ref[...]) + fbe_ref[...])
    lane = _iota((nsm, GROUP_W), 1)
    m0 = m0_ref[...]
    m0_e = jnp.zeros((nsm, GROUP_W), F32)
    for h in range(N_HEADS):
        m0_e = jnp.where((lane >= h * HEAD_DIM) & (lane < (h + 1) * HEAD_DIM), m0[:, h:h + 1], m0_e)
    a_int = lf_e + m0_e
    m_new = jnp.maximum(a_int, ig_e)
    dec = jnp.exp(a_int - m_new)
    w = jnp.exp(ig_e - m_new)
    n_new = dec * n0_ref[...] + w * k_m
    qn = dq * n_new
    hi = qn.astype(BF16)
    lo = (qn - hi.astype(F32)).astype(BF16)
    per_head(mk_ref, k_m)
    per_head(mq_ref, dq)
    per_head(mv_ref, dv)
    per_head(mdec_ref, dec)
    per_head(mw_ref, w)
    mn_ref[...] = n_new
    mden_ref[...] = _dot(hi, ones_bd) + _dot(lo, ones_bd)
    meneg_ref[...] = jnp.exp(-m_new)
    mm_ref[...] = m_new
    mdo_ref[...] = _sigmoid(do)


def _sample_attn_kernel(q_ref, kn_ref, vn_ref, kc_ref, vc_ref, sink_ref, ko_ref, vo_ref, o_ref):
    ko_ref[:, 0:WINDOW - 1, :] = kc_ref[:, 1:WINDOW, :]
    ko_ref[:, WINDOW - 1:WINDOW, :] = kn_ref[...]
    vo_ref[:, 0:WINDOW - 1, :] = vc_ref[:, 1:WINDOW, :]
    vo_ref[:, WINDOW - 1:WINDOW, :] = vn_ref[...]
    kk = ko_ref[...].astype(BF16)
    vv = vo_ref[...].astype(BF16)
    s = jnp.einsum('bqc,bkc->bqk', q_ref[...].astype(BF16), kk,
                   preferred_element_type=F32) * (HEAD_DIM ** -0.5)
    sk = sink_ref[...]
    mx = jnp.maximum(jnp.max(s, axis=-1, keepdims=True), sk)
    p = jnp.exp(s - mx)
    den = jnp.sum(p, axis=-1, keepdims=True) + jnp.exp(sk - mx)
    o = jnp.einsum('bqk,bkc->bqc', p.astype(BF16), vv, preferred_element_type=F32)
    o_ref[...] = o / den


def _sample_state_kernel(hf_ref, hk_ref, hq_ref, hv_ref, mk_ref, mq_ref, mv_ref, mdec_ref, mw_ref,
                         ed_ref, ee_ref, s_ref, c_ref, so_ref, co_ref, oh_ref, om_ref):
    nsm = s_ref.shape[0]
    width = HEAD_DIM * HEAD_DIM
    chunk = 512

    def expand(x, e):
        hi = x.astype(BF16)
        lo = (x - hi.astype(F32)).astype(BF16)
        return _dot(hi, e) + _dot(lo, e)

    hf, hk, hq, hv = hf_ref[...], hk_ref[...], hq_ref[...], hv_ref[...]
    mk, mq, mv = mk_ref[...], mq_ref[...], mv_ref[...]
    dec = mdec_ref[:, 0:1]
    w = mw_ref[:, 0:1]
    acc_h = jnp.zeros((nsm, LANES), F32)
    acc_m = jnp.zeros((nsm, LANES), F32)
    for c in range(0, width, chunk):
        ed = ed_ref[:, c:c + chunk]
        ee = ee_ref[:, c:c + chunk]
        s_new = expand(hf, ed) * s_ref[:, c:c + chunk] + expand(hk, ed) * expand(hv, ee)
        so_ref[:, c:c + chunk] = s_new
        oh = expand(hq, ed) * s_new
        c_new = dec * c_ref[:, c:c + chunk] + w * (expand(mk, ed) * expand(mv, ee))
        co_ref[:, c:c + chunk] = c_new
        om = expand(mq, ed) * c_new
        for i in range(0, chunk, LANES):
            acc_h = acc_h + oh[:, i:i + LANES]
            acc_m = acc_m + om[:, i:i + LANES]
    oh_ref[...] = (acc_h + pltpu.roll(acc_h, HEAD_DIM, 1))[:, 0:HEAD_DIM]
    om_ref[...] = (acc_m + pltpu.roll(acc_m, HEAD_DIM, 1))[:, 0:HEAD_DIM]


def _sample_out_kernel(ya_ref, yb_ref, oh_ref, hcg_ref, num_ref, den_ref, eneg_ref, do_ref,
                       hg_ref, mg_ref, ones_ref, y_ref):
    ones_bd = ones_ref[...]
    y_ref[:, 0:GROUP_W] = ya_ref[...].astype(y_ref.dtype)
    y_ref[:, GROUP_W:2 * GROUP_W] = yb_ref[...].astype(y_ref.dtype)
    yc = _seg_rms(oh_ref[...], hg_ref[...], ones_bd) * hcg_ref[...]
    y_ref[:, 2 * GROUP_W:3 * GROUP_W] = yc.astype(y_ref.dtype)
    hv = num_ref[...] / jnp.maximum(jnp.abs(den_ref[...]), eneg_ref[...])
    yd = _seg_rms(hv, mg_ref[...], ones_bd) * do_ref[...]
    y_ref[:, 3 * GROUP_W:4 * GROUP_W] = yd.astype(y_ref.dtype)


def _call_full(kern, args, out_shape, name):
    return pl.pallas_call(
        kern, grid=(1,), in_specs=[_full(a.shape) for a in args],
        out_specs=tuple(_full(s.shape) for s in out_shape), out_shape=tuple(out_shape),
        compiler_params=pltpu.CompilerParams(dimension_semantics=("arbitrary",),
                                             vmem_limit_bytes=VMEM_LIMIT),
        name=name,
    )(*args)


def _heads_to_lanes(x):
    return jnp.transpose(x, (1, 0, 2)).reshape(x.shape[1], GROUP_W)


def _sample_layer(h, p_l, st, lw, cw, layer):
    nsm = h.shape[0]
    h0, conv0, kc, vc, s0, c0, n0, m0 = st
    sd = lambda *shape: jax.ShapeDtypeStruct(shape, F32)
    g, l, hd = sd(nsm, GROUP_W), sd(nsm, LANES), sd(N_HEADS, nsm, HEAD_DIM)
    args = [h, lw['norm1_g'], lw['w_in'], lw['w_di_e'], lw['w_df_e'], lw['conv_w'], lw['conv_b'],
            lw['wr_bd'], lw['lru_br'], lw['wi_bd'], lw['lru_bi'], lw['lru_lam'], lw['qg'], lw['kg'],
            cw['rope_c_s'], cw['rope_a_s'], cw['rope_b_s'], lw['gamma'], lw['ib_e'], lw['fb_e'],
            lw['ones_bd'], h0, conv0.reshape(nsm, (CONV_W - 1) * GROUP_W), n0.reshape(nsm, GROUP_W), m0]
    outs = [g, g, sd(nsm, (CONV_W - 1) * GROUP_W), g, l, l,
            hd, hd, hd, hd, g,
            hd, hd, hd, hd, hd, g, g, g, g, g]
    (ya, hn, convn, q, k, v, hf, hk, hq, hv, hcg,
     mk, mq, mv, mdec, mw, mn, mden, meneg, mm, mdo) = _call_full(
        functools.partial(_sample_in_kernel, layer=layer), args, outs, f"sample_in_l{layer}")

    z = jnp.zeros((nsm, HEAD_DIM), F32)
    qrows = jnp.stack([jnp.concatenate([q[:, 0:64], z], 1), jnp.concatenate([q[:, 64:128], z], 1),
                       jnp.concatenate([z, q[:, 128:192]], 1), jnp.concatenate([z, q[:, 192:256]], 1)], 1)
    qrows = jnp.concatenate([qrows, jnp.zeros_like(qrows)], 1)
    sb = 16
    kc2 = kc.reshape(nsm, WINDOW, LANES)
    vc2 = vc.reshape(nsm, WINDOW, LANES)
    blk3 = lambda a, b: pl.BlockSpec((sb, a, b), lambda i: (i, 0, 0))
    ko, vo, o3 = pl.pallas_call(
        _sample_attn_kernel, grid=(nsm // sb,),
        in_specs=[blk3(8, LANES), blk3(1, LANES), blk3(1, LANES), blk3(WINDOW, LANES), blk3(WINDOW, LANES),
                  _full((8, 1))],
        out_specs=(blk3(WINDOW, LANES), blk3(WINDOW, LANES), blk3(8, LANES)),
        out_shape=(sd(nsm, WINDOW, LANES), sd(nsm, WINDOW, LANES), sd(nsm, 8, LANES)),
        compiler_params=pltpu.CompilerParams(dimension_semantics=("arbitrary",),
                                             vmem_limit_bytes=VMEM_LIMIT),
        name=f"sample_attn_l{layer}",
    )(qrows, k.reshape(nsm, 1, LANES), v.reshape(nsm, 1, LANES), kc2, vc2, lw['sinks8'])
    yb = jnp.concatenate([o3[:, 0, 0:64], o3[:, 1, 0:64], o3[:, 2, 64:128], o3[:, 3, 64:128]], 1)

    width = HEAD_DIM * HEAD_DIM
    hspec = pl.BlockSpec((None, nsm, HEAD_DIM), lambda i: (i, 0, 0))
    sspec = pl.BlockSpec((nsm, width), lambda i: (0, i))
    so, co, oh, om = pl.pallas_call(
        _sample_state_kernel, grid=(N_HEADS,),
        in_specs=[hspec] * 9 + [_full(cw['exp_d'].shape), _full(cw['exp_e'].shape), sspec, sspec],
        out_specs=(sspec, sspec, hspec, hspec),
        out_shape=(sd(nsm, N_HEADS * width), sd(nsm, N_HEADS * width), hd, hd),
        compiler_params=pltpu.CompilerParams(dimension_semantics=("arbitrary",),
                                             vmem_limit_bytes=VMEM_LIMIT),
        name=f"sample_state_l{layer}",
    )(hf, hk, hq, hv, mk, mq, mv, mdec, mw, cw['exp_d'], cw['exp_e'],
      s0.reshape(nsm, N_HEADS * width), c0.reshape(nsm, N_HEADS * width))

    (y,) = _call_full(
        _sample_out_kernel,
        [ya, yb, _heads_to_lanes(oh), hcg, _heads_to_lanes(om), mden, meneg, mdo, lw['hg'], lw['mg'],
         lw['ones_bd']],
        [jax.ShapeDtypeStruct((nsm, D_MODEL), BF16)], f"sample_out_l{layer}")
    h_new = _ffn(h, y, p_l, lw, nsm, f"sample_ffn_l{layer}")
    new_st = (hn, convn.reshape(nsm, CONV_W - 1, GROUP_W), ko.reshape(kc.shape), vo.reshape(vc.shape),
              so.reshape(s0.shape), co.reshape(c0.shape), mn.reshape(n0.shape), mm[:, ::HEAD_DIM])
    return h_new, new_st


def _block_diag(w):
    out = jnp.zeros((GROUP_W, GROUP_W), w.dtype)
    for i in range(w.shape[0]):
        out = lax.dynamic_update_slice(out, w[i], (i * HEAD_DIM, i * HEAD_DIM))
    return out


def _rope_tables(pos):
    half = ROT_DIM // 2
    inv = jnp.power(ROPE_THETA, -jnp.arange(half, dtype=F32) * (2.0 / ROT_DIM))
    ang = pos.astype(F32)[:, None] * inv[None, :]
    cos, sin = jnp.cos(ang), jnp.sin(ang)
    n = pos.shape[0]
    one = jnp.ones((n, HEAD_DIM - ROT_DIM), F32)
    zero8 = jnp.zeros((n, half), F32)
    zero = jnp.zeros((n, HEAD_DIM - ROT_DIM), F32)
    c = jnp.concatenate([cos, cos, one], 1)
    sa = jnp.concatenate([-sin, zero8, zero], 1)
    sb = jnp.concatenate([zero8, sin, zero], 1)
    tile2 = lambda a: jnp.concatenate([a, a], 1)
    return tile2(c), tile2(sa), tile2(sb)


def _pad_lanes(v, width=LANES):
    return jnp.zeros((1, width), F32).at[0, :v.shape[0]].set(v)


def _layer_weights(l, w):
    zpad = jnp.zeros((D_MODEL, LANES - N_HEADS), F32)
    w_in = w['w_in'][l]
    w_di, w_df = w_in[:, 3072:3076], w_in[:, 3076:3080]
    tile4 = lambda v: jnp.tile(v, N_HEADS)[None, :]
    lane_e = lambda v: jnp.repeat(v, HEAD_DIM)[None, :]
    sinks = w['attn_sinks'][l]
    return {
        'norm1_g': w['norm1_g'][l][None, :],
        'w_in': jnp.concatenate([w_in[:, :3072], w_di, zpad, w_df, zpad], 1).astype(BF16),
        'w_di_e': jnp.repeat(w_di, HEAD_DIM, axis=1).astype(BF16),
        'w_df_e': jnp.repeat(w_df, HEAD_DIM, axis=1).astype(BF16),
        'conv_w': w['conv_w'][l], 'conv_b': w['conv_b'][l][None, :],
        'wr_bd': _block_diag(w['lru_wr'][l]).astype(BF16), 'lru_br': w['lru_br'][l][None, :],
        'wi_bd': _block_diag(w['lru_wi'][l]).astype(BF16), 'lru_bi': w['lru_bi'][l][None, :],
        'lru_lam': w['lru_lam'][l][None, :],
        'qg': tile4(w['q_norm_g'][l]), 'kg': jnp.tile(w['k_norm_g'][l], 2)[None, :],
        'sinks': _pad_lanes(sinks),
        'sinks8': jnp.concatenate([sinks, jnp.zeros((4,), F32)])[:, None],
        'gamma': w['hgrn_gamma'], 'hg': tile4(w['hgrn_norm_g'][l]),
        'ib': _pad_lanes(w['mlstm_ib'][l]), 'fb': _pad_lanes(w['mlstm_fb'][l]),
        'ib_e': lane_e(w['mlstm_ib'][l]), 'fb_e': lane_e(w['mlstm_fb'][l]),
        'mg': tile4(w['mlstm_norm_g'][l]),
        'w_out': w['w_out'][l].astype(BF16), 'norm2_g': w['norm2_g'][l][None, :],
        'w_up': w['w_up'][l].astype(BF16), 'w_down': w['w_down'][l].astype(BF16),
        'w_gate': w['w_ple_gate'][l].astype(BF16), 'w_proj': w['w_ple_proj'][l].astype(BF16),
    }


def _run(x_prompt, x_sample, p_prompt, p_sample, sample_state, w, past_len, tm_mix=256, tm_ffn=512):
    depth = w['w_in'].shape[0]
    bsz, t, _ = x_prompt.shape
    nsm = x_sample.shape[0]
    idx = jnp.arange(GROUP_W)
    ones_bd = (idx[:, None] // HEAD_DIM == idx[None, :] // HEAD_DIM).astype(BF16)
    rc, ra, rb = _rope_tables(jnp.arange(t))
    rcs, ras, rbs = _rope_tables(past_len + jnp.arange(1))
    j = jnp.arange(HEAD_DIM * HEAD_DIM)
    d = jnp.arange(HEAD_DIM)
    cw = {'rope_c_s': rcs, 'rope_a_s': ras, 'rope_b_s': rbs,
          'exp_d': (d[:, None] == j[None, :] // HEAD_DIM).astype(BF16),
          'exp_e': (d[:, None] == j[None, :] % HEAD_DIM).astype(BF16)}

    hp = x_prompt
    hs = x_sample.reshape(nsm, D_MODEL)
    p_states, s_states = [], []
    tm_ffn = min(tm_ffn, bsz * t)
    for l in range(depth):
        lw = _layer_weights(l, w)
        lw.update(ones_bd=ones_bd, rope_c=rc, rope_a=ra, rope_b=rb)
        y, hl, conv, kst, vst, sst, cst, mst = _prompt_mixers(hp, lw, l, tm_mix)
        hp = _ffn(hp.reshape(bsz * t, D_MODEL), y.reshape(bsz * t, D_MODEL),
                  p_prompt[l].reshape(bsz * t, PLE_DIM), lw, tm_ffn, f"prompt_ffn_l{l}").reshape(bsz, t, D_MODEL)
        s_hgrn = jnp.swapaxes(sst, -1, -2)
        s_hgrn = jnp.stack([s_hgrn[:, 0, 0:64, 0:64], s_hgrn[:, 0, 64:128, 64:128],
                            s_hgrn[:, 1, 0:64, 0:64], s_hgrn[:, 1, 64:128, 64:128]], 1)
        c_rows = jnp.stack([cst[:, 0, 0:64], cst[:, 1, 64:128], cst[:, 2, 0:64], cst[:, 3, 64:128]], 1)
        p_states.append((hl[:, 0], conv[:, 8 - (CONV_W - 1):], kst.reshape(bsz, WINDOW, 2, HEAD_DIM),
                         vst.reshape(bsz, WINDOW, 2, HEAD_DIM), s_hgrn, c_rows[..., 0:HEAD_DIM],
                         c_rows[..., HEAD_DIM], mst[:, 0, 0:N_HEADS]))
        hs, st = _sample_layer(hs, p_sample[l].reshape(nsm, PLE_DIM), tuple(s[l] for s in sample_state),
                               lw, cw, l)
        s_states.append(st)
    stack = lambda sts: tuple(jnp.stack([s[i] for s in sts]) for i in range(8))
    return (hp, hs.reshape(x_sample.shape)) + stack(p_states) + stack(s_states)


def kernel(x_prompt, x_sample, p_prompt, p_sample, state_rglru_h, state_rglru_conv, cache_swa_k, cache_swa_v, state_hgrn_s, state_mlstm_c, state_mlstm_n, state_mlstm_m, norm1_g, w_in, conv_w, conv_b, lru_wr, lru_br, lru_wi, lru_bi, lru_lam, q_norm_g, k_norm_g, attn_sinks, hgrn_gamma, hgrn_norm_g, mlstm_ib, mlstm_fb, mlstm_norm_g, w_out, norm2_g, w_up, w_down, w_ple_gate, w_ple_proj):
    w = {'norm1_g': norm1_g, 'w_in': w_in, 'conv_w': conv_w, 'conv_b': conv_b, 'lru_wr': lru_wr,
         'lru_br': lru_br, 'lru_wi': lru_wi, 'lru_bi': lru_bi, 'lru_lam': lru_lam, 'q_norm_g': q_norm_g,
         'k_norm_g': k_norm_g, 'attn_sinks': attn_sinks, 'hgrn_gamma': hgrn_gamma,
         'hgrn_norm_g': hgrn_norm_g, 'mlstm_ib': mlstm_ib, 'mlstm_fb': mlstm_fb,
         'mlstm_norm_g': mlstm_norm_g, 'w_out': w_out, 'norm2_g': norm2_g, 'w_up': w_up,
         'w_down': w_down, 'w_ple_gate': w_ple_gate, 'w_ple_proj': w_ple_proj}
    st = (state_rglru_h, state_rglru_conv, cache_swa_k, cache_swa_v, state_hgrn_s, state_mlstm_c,
          state_mlstm_n, state_mlstm_m)
    past_len = 8192
    return _run(x_prompt, x_sample, p_prompt, p_sample, st, w, past_len)
```

```python
import functools

import jax
import jax.numpy as jnp
from jax import lax
from jax.experimental import pallas as pl
from jax.experimental.pallas import tpu as pltpu

F32 = jnp.float32
BF16 = jnp.bfloat16

D_MODEL = 1024
GROUP_W = 256
HEAD_DIM = 64
N_HEADS = 4
EPS = 1e-6
NEG_BIG = -1e30
LRU_C = 8.0
CONV_W = 4
ROT_DIM = 16
ROPE_THETA = 500000.0
WINDOW = 128
D_FF = 4096
PLE_DIM = 256
LANES = 128
SUBLANES = 8
BLK = 128

C_AX, C_AG, C_BQ, C_BK, C_BV = 0, 256, 512, 768, 896
C_CQ, C_CF, C_CI, C_CG = 1024, 1280, 1536, 1792
C_DQ, C_DK, C_DV, C_DO = 2048, 2304, 2560, 2816
C_GATES = 3072
D_IN = 3080
C_DI, C_DF = 3072, 3200
N_IN = 3328

V_G1, V_G2, V_CB, V_BR, V_BI, V_LAM = 0, 1024, 2048, 2304, 2560, 2816
V_QG, V_KG, V_SINK, V_HG, V_MG, V_IB, V_FB = 3072, 3328, 3456, 3584, 3840, 4096, 4224
N_VEC = 4352

R_HG, R_MG, R_GB = 0, 256, 512
N_COL = 520

FM_HF, FM_HK, FM_HQ, FM_HV, FM_CG = 0, 256, 512, 768, 1024
FM_MK, FM_MQ, FM_MV, FM_DO = 1280, 1536, 1792, 2048
FM_DEC, FM_W, FM_DEN, FM_ENEG = 2304, 2312, 2320, 2328
FM_ROWS = 2336

VMEM_LIMIT = 56 * 1024 * 1024


def _dot(a, b):
    return jnp.dot(a, b, preferred_element_type=F32)


def _dot_nt(a, b):
    return lax.dot_general(a, b, (((1,), (1,)), ((), ())), preferred_element_type=F32)


def _sigmoid(x):
    return jax.nn.sigmoid(x)


def _gelu_tanh(x):
    return 0.5 * x * (1.0 + jnp.tanh(0.7978845608028654 * (x + 0.044715 * (x * x * x))))


def _log_sigmoid(x):
    return jnp.minimum(x, 0.0) - jnp.log1p(jnp.exp(-jnp.abs(x)))


def _softplus(x):
    return jnp.maximum(x, 0.0) + jnp.log1p(jnp.exp(-jnp.abs(x)))


def _rms_rows(x, g):
    return x * lax.rsqrt(jnp.mean(x * x, axis=-1, keepdims=True) + EPS) * g


def _seg_mean_sq(x, ones_bd):
    sq = x * x
    hi = sq.astype(BF16)
    lo = (sq - hi.astype(F32)).astype(BF16)
    return (_dot(hi, ones_bd) + _dot(lo, ones_bd)) * (1.0 / HEAD_DIM)


def _seg_rms(x, g, ones_bd):
    return x * lax.rsqrt(_seg_mean_sq(x, ones_bd) + EPS) * g


def _rope(x, c, sa, sb):
    w = x.shape[1]
    up = pltpu.roll(x, w - ROT_DIM // 2, 1)
    dn = pltpu.roll(x, ROT_DIM // 2, 1)
    return x * c + up * sa + dn * sb


def _lb_from_gamma(gammas, layer):
    mx = functools.reduce(jnp.maximum, gammas)
    e = [jnp.exp(g - mx) for g in gammas]
    tot = functools.reduce(lambda a, b: a + b, e)
    lb = jnp.zeros_like(tot)
    for i in range(1, layer + 1):
        lb = lb + e[i] / tot
    return lb


def _iota(shape, axis):
    return lax.broadcasted_iota(jnp.int32, shape, axis)


def _rglru_gates(xc, wr, wi, br, bi, lam):
    xcb = xc.astype(BF16)
    r = _sigmoid(_dot(xcb, wr) + br)
    ig = _sigmoid(_dot(xcb, wi) + bi)
    log_a = (-LRU_C) * r * _softplus(-lam)
    return jnp.exp(log_a), jnp.sqrt(1.0 - jnp.exp(2.0 * log_a)) * (ig * xc)


def _swa_block(q, k, v, kst_ref, vst_ref, sink, first):
    kk = jnp.concatenate([kst_ref[...], k], axis=0)
    vv = jnp.concatenate([vst_ref[...], v], axis=0)
    qi = _iota((BLK, 2 * BLK), 0)
    kj = _iota((BLK, 2 * BLK), 1)
    valid = (kj > qi) & (kj <= qi + WINDOW) & ((kj >= BLK) | jnp.logical_not(first))
    outs = []
    for h in range(N_HEADS):
        kv = h // 2
        qh = q[:, h * HEAD_DIM:(h + 1) * HEAD_DIM].astype(BF16)
        kh = kk[:, kv * HEAD_DIM:(kv + 1) * HEAD_DIM].astype(BF16)
        vh = vv[:, kv * HEAD_DIM:(kv + 1) * HEAD_DIM].astype(BF16)
        s = _dot_nt(qh, kh) * (HEAD_DIM ** -0.5)
        s = jnp.where(valid, s, NEG_BIG)
        sk = sink[:, h:h + 1]
        mx = jnp.maximum(jnp.max(s, axis=-1, keepdims=True), sk)
        p = jnp.exp(s - mx)
        den = jnp.sum(p, axis=-1, keepdims=True) + jnp.exp(sk - mx)
        outs.append(_dot(p.astype(BF16), vh) / den)
    kst_ref[...] = k
    vst_ref[...] = v
    return jnp.concatenate(outs, axis=1)


def _hgrn_block(cq, cf, ci, cg, lb, hg, sst_ref, ones_bd):
    q = cq * _sigmoid(cq)
    f = lb + (1.0 - lb) * _sigmoid(cf)
    logf = jnp.log(f)
    k = 1.0 - f
    row = _iota((BLK, GROUP_W), 0)
    row_l = _iota((BLK, BLK), 0)
    col_l = _iota((BLK, BLK), 1)
    lane = _iota((BLK, LANES), 1)
    in_head = (lane < HEAD_DIM, lane >= HEAD_DIM)

    att = []
    kb = k.astype(BF16)
    for p in range(2):
        qp = q[:, p * LANES:(p + 1) * LANES]
        kp = kb[:, p * LANES:(p + 1) * LANES]
        for hh in range(2):
            qm = jnp.where(in_head[hh], qp, 0.0).astype(BF16)
            att.append(jnp.where(row_l == col_l, _dot_nt(qm, kp), 0.0))

    c = logf
    tot = logf
    lev = 1
    while (1 << lev) <= BLK:
        half = 1 << (lev - 1)
        right = ((row >> (lev - 1)) & 1) == 1
        tot_l = pltpu.roll(tot, half, 0)
        tot_r = pltpu.roll(tot, BLK - half, 0)
        x = jnp.where(right, q, k) * jnp.exp(jnp.where(right, c, tot - c))
        qx = jnp.where(right, x, 0.0)
        kx = jnp.where(right, 0.0, x).astype(BF16)
        same = (row_l >> lev) == (col_l >> lev)
        for p in range(2):
            qp = qx[:, p * LANES:(p + 1) * LANES]
            kp = kx[:, p * LANES:(p + 1) * LANES]
            for hh in range(2):
                qm = jnp.where(in_head[hh], qp, 0.0).astype(BF16)
                att[2 * p + hh] = att[2 * p + hh] + jnp.where(same, _dot_nt(qm, kp), 0.0)
        c = c + jnp.where(right, tot_l, 0.0)
        tot = tot + jnp.where(right, tot_l, tot_r)
        lev += 1

    qe = (q * jnp.exp(c)).astype(BF16)
    ke = (k * jnp.exp(tot - c)).astype(BF16)
    etot = jnp.exp(tot[0:1, :])
    same_head = (row_l >= HEAD_DIM) == (col_l >= HEAD_DIM)
    outs = []
    for p in range(2):
        sl = slice(p * LANES, (p + 1) * LANES)
        st = sst_ref[p]
        vp = ci[:, sl]
        o = _dot_nt(qe[:, sl], st.astype(BF16))
        for hh in range(2):
            vm = jnp.where(in_head[hh], vp, 0.0).astype(BF16)
            o = o + _dot(att[2 * p + hh].astype(BF16), vm)
        upd = _dot(vp.T.astype(BF16), ke[:, sl])
        sst_ref[p] = st * etot[:, sl] + jnp.where(same_head, upd, 0.0)
        outs.append(o)
    o = jnp.concatenate(outs, axis=1)
    return _seg_rms(o, hg, ones_bd) * (cg * _sigmoid(cg))


def _mlstm_block(dq, dk, dv, do, gi, gf, ib, fb, mg, cst_ref, mst_ref, ones_bd):
    k = dk * (HEAD_DIM ** -0.5)
    ig = gi + ib
    logf = _log_sigmoid(gf + fb)
    row = _iota((BLK, LANES), 0)
    lane = _iota((BLK, LANES), 1)
    row_l = _iota((BLK, BLK), 0)
    col_l = _iota((BLK, BLK), 1)
    causal = col_l <= row_l
    in_head = (lane < HEAD_DIM, lane >= HEAD_DIM)
    one_col = jnp.where(lane == HEAD_DIM, 1.0, 0.0)

    fc = logf
    s = 1
    while s < BLK:
        fc = fc + jnp.where(row >= s, pltpu.roll(fc, s, 0), 0.0)
        s *= 2
    g = ig - fc
    cm = g
    s = 1
    while s < BLK:
        cm = jnp.maximum(cm, jnp.where(row >= s, pltpu.roll(cm, s, 0), NEG_BIG))
        s *= 2
    m0 = mst_ref[...]
    mrun = jnp.maximum(cm, m0)
    inter = jnp.exp(m0 - mrun)
    eneg = jnp.exp(-(fc + mrun))
    m_last = mrun[BLK - 1:BLK, :]
    wend = jnp.exp(g - m_last)
    dec0 = jnp.exp(m0 - m_last)
    mst_ref[...] = fc[BLK - 1:BLK, :] + m_last
    g_t = g.T

    outs = []
    for p in range(2):
        sl = slice(p * LANES, (p + 1) * LANES)
        qp, kp, vp = dq[:, sl], k[:, sl], dv[:, sl]
        kpb = kp.astype(BF16)
        v_sw = pltpu.roll(vp, HEAD_DIM, 1)
        hv = []
        for hh in range(2):
            h = 2 * p + hh
            qm = jnp.where(in_head[hh], qp, 0.0).astype(BF16)
            sc = _dot_nt(qm, kpb)
            w = jnp.exp(jnp.where(causal, g_t[h:h + 1, :] - mrun[:, h:h + 1], NEG_BIG))
            sw = (sc * w).astype(BF16)
            vaug = jnp.where(in_head[0], vp if hh == 0 else v_sw, one_col).astype(BF16)
            cst = cst_ref[h]
            nd = inter[:, h:h + 1] * _dot(qm, cst.astype(BF16)) + _dot(sw, vaug)
            den = nd[:, HEAD_DIM:HEAD_DIM + 1]
            hv.append(nd / jnp.maximum(jnp.abs(den), eneg[:, h:h + 1]))
            kw = jnp.where(in_head[hh], kp, 0.0) * wend[:, h:h + 1]
            cst_ref[h] = dec0[:, h:h + 1] * cst + _dot(kw.T.astype(BF16), vaug)
        outs.append(jnp.where(in_head[0], hv[0], pltpu.roll(hv[1], HEAD_DIM, 1)))
    hcat = jnp.concatenate(outs, axis=1)
    return _seg_rms(hcat, mg, ones_bd) * _sigmoid(do)


def _mixer_kernel(h_ref, pv_ref, win_ref, convw_ref, wr_ref, wi_ref, rr_ref, rbase_ref, gamma_ref, ones_ref,
                  y_ref, hl_ref, conv_ref, kst_ref, vst_ref, sst_ref, cst_ref, mst_ref,
                  xbuf, *, layer, tm):
    t = pl.program_id(1)

    @pl.when(t == 0)
    def _init():
        hl_ref[...] = jnp.zeros_like(hl_ref)
        conv_ref[...] = jnp.zeros_like(conv_ref)
        kst_ref[...] = jnp.zeros_like(kst_ref)
        vst_ref[...] = jnp.zeros_like(vst_ref)
        sst_ref[...] = jnp.zeros_like(sst_ref)
        cst_ref[...] = jnp.zeros_like(cst_ref)
        mst_ref[...] = jnp.zeros_like(mst_ref)

    def vec(off, w):
        return pv_ref[layer:layer + 1, off:off + w]

    nb = _rms_rows(h_ref[...], vec(V_G1, D_MODEL)).astype(BF16)
    ones_bd = ones_ref[...]

    def proj(c0, w):
        return _dot(nb, win_ref[:, c0:c0 + w])

    xa = proj(C_AX, GROUP_W)
    ga = proj(C_AG, GROUP_W)
    xbuf[0:8, :] = conv_ref[...]
    xbuf[8:8 + tm, :] = xa
    cw = convw_ref[layer]
    xc = (vec(V_CB, GROUP_W) + xbuf[5:5 + tm, :] * cw[0:1, :] + xbuf[6:6 + tm, :] * cw[1:2, :]
          + xbuf[7:7 + tm, :] * cw[2:3, :] + xa * cw[3:4, :])
    conv_ref[...] = xbuf[tm:tm + 8, :]
    a, bx = _rglru_gates(xc, wr_ref[...], wi_ref[...], vec(V_BR, GROUP_W), vec(V_BI, GROUP_W),
                         vec(V_LAM, GROUP_W))
    row = _iota((tm, GROUP_W), 0)
    s = 1
    while s < tm:
        keep = row >= s
        a_s = pltpu.roll(a, s, 0)
        b_s = pltpu.roll(bx, s, 0)
        bx = jnp.where(keep, a * b_s + bx, bx)
        a = jnp.where(keep, a * a_s, a)
        s *= 2
    hseq = a * hl_ref[...] + bx
    hl_ref[...] = hseq[tm - 1:tm, :]
    y_ref[:, 0:GROUP_W] = (hseq * _gelu_tanh(ga)).astype(y_ref.dtype)

    cb = rbase_ref[:, 0:LANES]
    sb_ = rbase_ref[:, LANES:2 * LANES]
    rc = cb * rr_ref[0] - sb_ * rr_ref[1]
    ra = sb_ * rr_ref[2] + cb * rr_ref[3]
    rb = sb_ * rr_ref[4] + cb * rr_ref[5]
    q = _seg_rms(proj(C_BQ, GROUP_W), vec(V_QG, GROUP_W), ones_bd)
    q = _rope(q, jnp.concatenate([rc, rc], 1), jnp.concatenate([ra, ra], 1), jnp.concatenate([rb, rb], 1))
    k = _seg_rms(proj(C_BK, LANES), vec(V_KG, LANES), ones_bd[0:LANES, 0:LANES])
    k = _rope(k, rc, ra, rb)
    v = proj(C_BV, LANES)
    sink = vec(V_SINK, LANES)
    for j in range(tm // BLK):
        rs = slice(j * BLK, (j + 1) * BLK)
        first = (t == 0) if j == 0 else False
        yb = _swa_block(q[rs], k[rs], v[rs], kst_ref, vst_ref, sink, first)
        y_ref[rs, GROUP_W:2 * GROUP_W] = yb.astype(y_ref.dtype)

    lb = _lb_from_gamma([gamma_ref[i:i + 1, :] for i in range(gamma_ref.shape[0])], layer)
    cq, cf, ci, cg = (proj(C_CQ, GROUP_W), proj(C_CF, GROUP_W), proj(C_CI, GROUP_W), proj(C_CG, GROUP_W))
    hg = vec(V_HG, GROUP_W)
    for j in range(tm // BLK):
        rs = slice(j * BLK, (j + 1) * BLK)
        yc = _hgrn_block(cq[rs], cf[rs], ci[rs], cg[rs], lb, hg, sst_ref, ones_bd)
        y_ref[rs, 2 * GROUP_W:3 * GROUP_W] = yc.astype(y_ref.dtype)

    dq, dk, dv, do = (proj(C_DQ, GROUP_W), proj(C_DK, GROUP_W), proj(C_DV, GROUP_W), proj(C_DO, GROUP_W))
    gi, gf = proj(C_DI, LANES), proj(C_DF, LANES)
    ib, fb, mg = vec(V_IB, LANES), vec(V_FB, LANES), vec(V_MG, GROUP_W)
    for j in range(tm // BLK):
        rs = slice(j * BLK, (j + 1) * BLK)
        yd = _mlstm_block(dq[rs], dk[rs], dv[rs], do[rs], gi[rs], gf[rs], ib, fb, mg,
                          cst_ref, mst_ref, ones_bd)
        y_ref[rs, 3 * GROUP_W:4 * GROUP_W] = yd.astype(y_ref.dtype)


def _full(shape):
    nd = len(shape)
    return pl.BlockSpec(shape, lambda *_: (0,) * nd)


def _layer_block(shape, layer, single_buffer=False):
    nd = len(shape) - 1
    kw = {'pipeline_mode': pl.Buffered(1)} if single_buffer else {}
    return pl.BlockSpec((None,) + tuple(shape[1:]), lambda *_: (layer,) + (0,) * nd, **kw)


def _prompt_mixers(h, cw, layer, tm):
    bsz, t, _ = h.shape
    nt = t // tm
    kern = functools.partial(_mixer_kernel, layer=layer, tm=tm)
    in_specs = [pl.BlockSpec((None, tm, D_MODEL), lambda b, i: (b, i, 0)),
                _full(cw['vecs'].shape), _layer_block(cw['w_in_p'].shape, layer),
                _full(cw['conv_w'].shape), _layer_block(cw['wr_bd'].shape, layer),
                _layer_block(cw['wi_bd'].shape, layer), _full(cw['rope_r'].shape),
                pl.BlockSpec((None, 1, 2 * LANES), lambda b, i: (i, 0, 0)),
                _full(cw['gamma'].shape), _full(cw['ones_bd'].shape)]

    def st_spec(*dims):
        return pl.BlockSpec((None,) + dims, lambda b, i: (b,) + (0,) * len(dims))

    out_shape = (jax.ShapeDtypeStruct((bsz, t, D_MODEL), BF16),
                 jax.ShapeDtypeStruct((bsz, 1, GROUP_W), F32),
                 jax.ShapeDtypeStruct((bsz, 8, GROUP_W), F32),
                 jax.ShapeDtypeStruct((bsz, BLK, LANES), F32),
                 jax.ShapeDtypeStruct((bsz, BLK, LANES), F32),
                 jax.ShapeDtypeStruct((bsz, 2, LANES, LANES), F32),
                 jax.ShapeDtypeStruct((bsz, N_HEADS, LANES, LANES), F32),
                 jax.ShapeDtypeStruct((bsz, 1, LANES), F32))
    out_specs = (pl.BlockSpec((None, tm, D_MODEL), lambda b, i: (b, i, 0)),
                 st_spec(1, GROUP_W), st_spec(8, GROUP_W), st_spec(BLK, LANES), st_spec(BLK, LANES),
                 st_spec(2, LANES, LANES), st_spec(N_HEADS, LANES, LANES), st_spec(1, LANES))
    return pl.pallas_call(
        kern, grid=(bsz, nt), in_specs=in_specs, out_specs=out_specs, out_shape=out_shape,
        scratch_shapes=[pltpu.VMEM((tm + 8, GROUP_W), F32)],
        compiler_params=pltpu.CompilerParams(dimension_semantics=("arbitrary", "arbitrary"),
                                             vmem_limit_bytes=VMEM_LIMIT),
        name=f"prompt_mixers_l{layer}",
    )(h, cw['vecs'], cw['w_in_p'], cw['conv_w'], cw['wr_bd'], cw['wi_bd'], cw['rope_r'], cw['rope_base'],
      cw['gamma'], cw['ones_bd'])


def _ffn_math(h, yb, p, g2, wout_ref, wup_ref, wdn_ref, wg_ref, wp_ref):
    h = h + _dot(yb, wout_ref[...])
    nb = _rms_rows(h, g2).astype(BF16)
    acc = h
    step = 1024
    for c in range(0, D_FF, step):
        f = jnp.maximum(_dot(nb, wup_ref[:, c:c + step]), 0.0)
        acc = acc + _dot((f * f).astype(BF16), wdn_ref[c:c + step, :])
    gate = _sigmoid(_dot(acc.astype(BF16), wg_ref[...]))
    return acc + gate * _dot(p.astype(BF16), wp_ref[...])


def _ffn_kernel(h_ref, y_ref, p_ref, pv_ref, wout_ref, wup_ref, wdn_ref, wg_ref, wp_ref, o_ref, *, layer):
    g2 = pv_ref[layer:layer + 1, V_G2:V_G2 + D_MODEL]
    o_ref[...] = _ffn_math(h_ref[...], y_ref[...], p_ref[...], g2, wout_ref, wup_ref, wdn_ref, wg_ref, wp_ref)


def _ffn_weight_specs(cw, layer):
    names = ['w_out', 'w_up', 'w_down', 'w_gate', 'w_proj']
    return [cw[n] for n in names], [_layer_block(cw[n].shape, layer, single_buffer=True) for n in names]


def _prompt_ffn(h2, y2, p3, cw, layer, tm):
    n = h2.shape[0]
    row = lambda w: pl.BlockSpec((tm, w), lambda i: (i, 0))
    ws, wspecs = _ffn_weight_specs(cw, layer)
    return pl.pallas_call(
        functools.partial(_ffn_kernel, layer=layer), grid=(n // tm,),
        in_specs=[row(D_MODEL), row(D_MODEL), pl.BlockSpec((None, tm, PLE_DIM), lambda i: (layer, i, 0)),
                  _full(cw['vecs'].shape)] + wspecs,
        out_specs=row(D_MODEL), out_shape=jax.ShapeDtypeStruct((n, D_MODEL), F32),
        compiler_params=pltpu.CompilerParams(dimension_semantics=("arbitrary",),
                                             vmem_limit_bytes=VMEM_LIMIT),
        name=f"prompt_ffn_l{layer}",
    )(h2, y2, p3, cw['vecs'], *ws)


def _sample_pre_kernel(x_ref, pv_ref, wt_ref, convw_ref, wr_ref, wi_ref, ones_ref, rope_ref, gcol_ref,
                       cols_ref, h0_ref, conv0_ref, n0_ref, m0_ref,
                       ya_ref, hn_ref, convn_ref, q_ref, kt_ref, vt_ref, fm_ref, nn_ref, mn_ref, *, layer):
    def vec(off, w):
        return pv_ref[layer:layer + 1, off:off + w]

    n = _rms_rows(x_ref[...], vec(V_G1, D_MODEL))
    nb = n.astype(BF16)
    n_t = n.T.astype(BF16)
    ones_bd = ones_ref[...]

    def proj(r0, cnt):
        return _dot_nt(nb, wt_ref[r0:r0 + cnt, :].astype(BF16))

    def proj_t(r0, cnt):
        return _dot(wt_ref[r0:r0 + cnt, :].astype(BF16), n_t)

    xa = proj(C_AX, GROUP_W)
    ga = proj(C_AG, GROUP_W)
    cw = convw_ref[layer]
    xc = (vec(V_CB, GROUP_W) + conv0_ref[0] * cw[0:1, :] + conv0_ref[1] * cw[1:2, :]
          + conv0_ref[2] * cw[2:3, :] + xa * cw[3:4, :])
    convn_ref[0] = conv0_ref[1]
    convn_ref[1] = conv0_ref[2]
    convn_ref[2] = xa
    a, bx = _rglru_gates(xc, wr_ref[...], wi_ref[...], vec(V_BR, GROUP_W), vec(V_BI, GROUP_W),
                         vec(V_LAM, GROUP_W))
    hn = a * h0_ref[...] + bx
    hn_ref[...] = hn
    ya_ref[...] = hn * _gelu_tanh(ga)

    rc, ra, rb = rope_ref[0:1, :], rope_ref[1:2, :], rope_ref[2:3, :]
    q = _seg_rms(proj(C_BQ, GROUP_W), vec(V_QG, GROUP_W), ones_bd)
    q_ref[...] = _rope(q, jnp.concatenate([rc, rc], 1), jnp.concatenate([ra, ra], 1),
                       jnp.concatenate([rb, rb], 1))
    k = _seg_rms(proj(C_BK, LANES), vec(V_KG, LANES), ones_bd[0:LANES, 0:LANES])
    kt_ref[...] = _rope(k, rc, ra, rb).T
    vt_ref[...] = proj(C_BV, LANES).T

    lb = _lb_from_gamma([gcol_ref[i] for i in range(gcol_ref.shape[0])], layer)
    cq, cf, ci, cg = (proj_t(C_CQ, GROUP_W), proj_t(C_CF, GROUP_W), proj_t(C_CI, GROUP_W),
                      proj_t(C_CG, GROUP_W))
    f = lb + (1.0 - lb) * _sigmoid(cf)
    fm_ref[FM_HF:FM_HF + GROUP_W, :] = f
    fm_ref[FM_HK:FM_HK + GROUP_W, :] = 1.0 - f
    fm_ref[FM_HQ:FM_HQ + GROUP_W, :] = cq * _sigmoid(cq)
    fm_ref[FM_HV:FM_HV + GROUP_W, :] = ci
    fm_ref[FM_CG:FM_CG + GROUP_W, :] = cg * _sigmoid(cg)

    dq, dk, dv, do = (proj_t(C_DQ, GROUP_W), proj_t(C_DK, GROUP_W), proj_t(C_DV, GROUP_W),
                      proj_t(C_DO, GROUP_W))
    g8 = proj_t(C_GATES, 2 * N_HEADS) + cols_ref[layer, R_GB:R_GB + 2 * N_HEADS, :]
    ig = g8[0:N_HEADS, :]
    lf = _log_sigmoid(g8)[N_HEADS:2 * N_HEADS, :]
    a_int = lf + m0_ref[...]
    m_new = jnp.maximum(a_int, ig)
    dec = jnp.exp(a_int - m_new)
    w = jnp.exp(ig - m_new)
    mn_ref[...] = m_new
    km = dk * (HEAD_DIM ** -0.5)
    dens = []
    for h in range(N_HEADS):
        sl = slice(h * HEAD_DIM, (h + 1) * HEAD_DIM)
        nn_h = dec[h:h + 1, :] * n0_ref[sl, :] + w[h:h + 1, :] * km[sl, :]
        nn_ref[sl, :] = nn_h
        dens.append(jnp.sum(dq[sl, :] * nn_h, axis=0, keepdims=True))
    pad = jnp.zeros((SUBLANES - N_HEADS, x_ref.shape[0]), F32)
    fm_ref[FM_MK:FM_MK + GROUP_W, :] = km
    fm_ref[FM_MQ:FM_MQ + GROUP_W, :] = dq
    fm_ref[FM_MV:FM_MV + GROUP_W, :] = dv
    fm_ref[FM_DO:FM_DO + GROUP_W, :] = _sigmoid(do)
    fm_ref[FM_DEC:FM_DEC + SUBLANES, :] = jnp.concatenate([dec, pad], 0)
    fm_ref[FM_W:FM_W + SUBLANES, :] = jnp.concatenate([w, pad], 0)
    fm_ref[FM_DEN:FM_DEN + SUBLANES, :] = jnp.concatenate(dens + [pad], 0)
    fm_ref[FM_ENEG:FM_ENEG + SUBLANES, :] = jnp.concatenate([jnp.exp(-m_new), pad], 0)


def _sample_attn_kernel(q_ref, kn_ref, vn_ref, kc_ref, vc_ref, sink_ref, *rest):
    ko_ref, vo_ref, o_ref = rest[-3:]
    sb = q_ref.shape[0]
    rows = 2 * HEAD_DIM
    lane = _iota((rows, WINDOW), 1)
    kn = kn_ref[...].reshape(rows, sb)
    vn = vn_ref[...].reshape(rows, sb)
    for s in range(sb):
        kt = pltpu.roll(kc_ref[s].reshape(rows, WINDOW), WINDOW - 1, 1)
        ko_ref[s] = jnp.where(lane == WINDOW - 1, kn[:, s:s + 1], kt).reshape(2, HEAD_DIM, WINDOW)
        vt = pltpu.roll(vc_ref[s].reshape(rows, WINDOW), WINDOW - 1, 1)
        vo_ref[s] = jnp.where(lane == WINDOW - 1, vn[:, s:s + 1], vt).reshape(2, HEAD_DIM, WINDOW)
    for kv in range(2):
        kk = ko_ref[:, kv].astype(BF16)
        vv = vo_ref[:, kv].astype(BF16)
        s_ = jnp.einsum('bqc,bcj->bqj', q_ref[:, kv].astype(BF16), kk,
                        preferred_element_type=F32) * (HEAD_DIM ** -0.5)
        sk = sink_ref[kv]
        mx = jnp.maximum(jnp.max(s_, axis=-1, keepdims=True), sk)
        p = jnp.exp(s_ - mx)
        den = jnp.sum(p, axis=-1, keepdims=True) + jnp.exp(sk - mx)
        o = jnp.einsum('bqj,bcj->bqc', p.astype(BF16), vv, preferred_element_type=F32)
        o_ref[:, kv] = o / den


def _sample_state_kernel(fm_ref, s_ref, c_ref, *rest):
    so_ref, co_ref, oh_ref, om_ref = rest[-4:]
    h = pl.program_id(0)
    r0 = pl.multiple_of(h * HEAD_DIM, HEAD_DIM)
    hv = fm_ref[pl.ds(FM_HV + r0, HEAD_DIM), :]
    mv = fm_ref[pl.ds(FM_MV + r0, HEAD_DIM), :]
    dec = fm_ref[pl.ds(FM_DEC + h, 1), :]
    w = fm_ref[pl.ds(FM_W + h, 1), :]

    def body(d, carry):
        acc_h, acc_m = carry
        r = r0 + d
        s_new = fm_ref[pl.ds(FM_HF + r, 1), :] * s_ref[d] + fm_ref[pl.ds(FM_HK + r, 1), :] * hv
        so_ref[d] = s_new
        c_new = dec * c_ref[d] + (w * fm_ref[pl.ds(FM_MK + r, 1), :]) * mv
        co_ref[d] = c_new
        return (acc_h + fm_ref[pl.ds(FM_HQ + r, 1), :] * s_new,
                acc_m + fm_ref[pl.ds(FM_MQ + r, 1), :] * c_new)

    zero = jnp.zeros((HEAD_DIM, fm_ref.shape[1]), F32)
    acc_h, acc_m = lax.fori_loop(0, HEAD_DIM, body, (zero, zero), unroll=4)
    oh_ref[...] = acc_h
    om_ref[...] = acc_m


def _sample_post_kernel(h_ref, ya_ref, yb_ref, oh_ref, om_ref, fm_ref, cols_ref, p_ref, pv_ref,
                        wout_ref, wup_ref, wdn_ref, wg_ref, wp_ref, o_ref, *, layer):
    def head_rms(x):
        return x * lax.rsqrt(jnp.mean(x * x, axis=0, keepdims=True) + EPS)

    yc, yd = [], []
    for h in range(N_HEADS):
        sl = slice(h * HEAD_DIM, (h + 1) * HEAD_DIM)
        yc.append(head_rms(oh_ref[h]))
        den = fm_ref[FM_DEN + h:FM_DEN + h + 1, :]
        eneg = fm_ref[FM_ENEG + h:FM_ENEG + h + 1, :]
        yd.append(head_rms(om_ref[h] / jnp.maximum(jnp.abs(den), eneg)))
    yc = (jnp.concatenate(yc, 0) * cols_ref[layer, R_HG:R_HG + GROUP_W, :]
          * fm_ref[FM_CG:FM_CG + GROUP_W, :])
    yd = (jnp.concatenate(yd, 0) * cols_ref[layer, R_MG:R_MG + GROUP_W, :]
          * fm_ref[FM_DO:FM_DO + GROUP_W, :])
    y = jnp.concatenate([ya_ref[...], yb_ref[...], yc.T, yd.T], axis=1).astype(BF16)
    g2 = pv_ref[layer:layer + 1, V_G2:V_G2 + D_MODEL]
    o_ref[...] = _ffn_math(h_ref[...], y, p_ref[...], g2, wout_ref, wup_ref, wdn_ref, wg_ref, wp_ref)


def _call_full(kern, args, specs, out_shape, name):
    specs = [(_full(a.shape) if s is None else s) for a, s in zip(args, specs)]
    return pl.pallas_call(
        kern, grid=(1,), in_specs=specs,
        out_specs=tuple(_full(s.shape) for s in out_shape), out_shape=tuple(out_shape),
        compiler_params=pltpu.CompilerParams(dimension_semantics=("arbitrary",),
                                             vmem_limit_bytes=VMEM_LIMIT),
        name=name,
    )(*args)


def _sample_layer(h, sv, prev, cw, layer):
    nsm = h.shape[0]
    depth = cw['vecs'].shape[0]
    sd = lambda *shape: jax.ShapeDtypeStruct(shape, F32)
    g, l = sd(nsm, GROUP_W), sd(LANES, nsm)

    args = [h, cw['vecs'], cw['w_in_t'], cw['conv_w'], cw['wr_bd'], cw['wi_bd'], cw['ones_bd'],
            cw['rope_s'], cw['gamma_col'], cw['cols'], sv['h'], sv['conv'], sv['n'], sv['m']]
    specs = [None, None, _layer_block(cw['w_in_t'].shape, layer, single_buffer=True), None,
             _layer_block(cw['wr_bd'].shape, layer), _layer_block(cw['wi_bd'].shape, layer), None,
             None, None, None, _layer_block(sv['h'].shape, layer), _layer_block(sv['conv'].shape, layer),
             _layer_block(sv['n'].shape, layer), _layer_block(sv['m'].shape, layer)]
    outs = [g, g, sd(CONV_W - 1, nsm, GROUP_W), g, l, l, sd(FM_ROWS, nsm), sd(GROUP_W, nsm),
            sd(N_HEADS, nsm)]
    ya, hn, convn, q, kt, vt, fm, nn, mn = _call_full(
        functools.partial(_sample_pre_kernel, layer=layer), args, specs, outs, f"sample_pre_l{layer}")

    sb = 16
    nblk = nsm // sb
    q3 = jnp.pad(q.reshape(nsm, 2, 2, HEAD_DIM), ((0, 0), (0, 0), (0, SUBLANES - 2), (0, 0)))
    to_blocks = lambda a: a.reshape(2, HEAD_DIM, nblk, sb).transpose(2, 0, 1, 3)
    cshape = (depth, nsm, 2, HEAD_DIM, WINDOW)
    cspec = pl.BlockSpec((None, sb, 2, HEAD_DIM, WINDOW), lambda i: (layer, i, 0, 0, 0))
    nspec = pl.BlockSpec((None, 2, HEAD_DIM, sb), lambda i: (i, 0, 0, 0))
    qspec = pl.BlockSpec((sb, 2, SUBLANES, HEAD_DIM), lambda i: (i, 0, 0, 0))
    any_spec = pl.BlockSpec(memory_space=pl.ANY)
    chain = [] if prev is None else [prev['k'], prev['v']]
    ko, vo, o3 = pl.pallas_call(
        _sample_attn_kernel, grid=(nblk,),
        in_specs=[qspec, nspec, nspec, cspec, cspec, _layer_block(cw['sinks8'].shape, layer)]
        + [any_spec] * len(chain),
        out_specs=(cspec, cspec, qspec),
        out_shape=(sd(*cshape), sd(*cshape), sd(nsm, 2, SUBLANES, HEAD_DIM)),
        input_output_aliases={6 + i: i for i in range(len(chain))},
        compiler_params=pltpu.CompilerParams(dimension_semantics=("arbitrary",),
                                             vmem_limit_bytes=VMEM_LIMIT),
        name=f"sample_attn_l{layer}",
    )(q3, to_blocks(kt), to_blocks(vt), sv['k'], sv['v'], cw['sinks8'], *chain)
    yb = o3[:, :, 0:2, :].reshape(nsm, GROUP_W)

    sshape = (depth, N_HEADS, HEAD_DIM, HEAD_DIM, nsm)
    sspec = pl.BlockSpec((None, None, HEAD_DIM, HEAD_DIM, nsm), lambda i: (layer, i, 0, 0, 0))
    ospec = pl.BlockSpec((None, HEAD_DIM, nsm), lambda i: (i, 0, 0))
    chain = [] if prev is None else [prev['s'], prev['c']]
    so, co, oh, om = pl.pallas_call(
        _sample_state_kernel, grid=(N_HEADS,),
        in_specs=[_full(fm.shape), sspec, sspec] + [any_spec] * len(chain),
        out_specs=(sspec, sspec, ospec, ospec),
        out_shape=(sd(*sshape), sd(*sshape), sd(N_HEADS, HEAD_DIM, nsm), sd(N_HEADS, HEAD_DIM, nsm)),
        input_output_aliases={3 + i: i for i in range(len(chain))},
        compiler_params=pltpu.CompilerParams(dimension_semantics=("arbitrary",),
                                             vmem_limit_bytes=VMEM_LIMIT),
        name=f"sample_state_l{layer}",
    )(fm, sv['s'], sv['c'], *chain)

    ws, wspecs = _ffn_weight_specs(cw, layer)
    (h_new,) = _call_full(
        functools.partial(_sample_post_kernel, layer=layer),
        [h, ya, yb, oh, om, fm, cw['cols'], sv['p'], cw['vecs']] + ws,
        [None] * 7 + [_layer_block(sv['p'].shape, layer), None] + wspecs,
        [sd(nsm, D_MODEL)], f"sample_post_l{layer}")
    small = (hn, convn, nn, mn)
    big = {'k': ko, 'v': vo, 's': so, 'c': co}
    return h_new, small, big


def _block_diag_all(w):
    depth = w.shape[0]
    rows = w.reshape(depth, GROUP_W, HEAD_DIM)
    idx = jnp.arange(GROUP_W) // HEAD_DIM
    mask = idx[:, None] == idx[None, :]
    return jnp.where(mask[None], jnp.tile(rows, (1, 1, N_HEADS)), 0.0)


def _rope_lane_freq():
    half = ROT_DIM // 2
    inv = jnp.power(ROPE_THETA, -jnp.arange(half, dtype=F32) * (2.0 / ROT_DIM))
    dd = jnp.arange(LANES) % HEAD_DIM
    freq = jnp.where(dd < ROT_DIM, inv[dd % half], 0.0)
    m_a = (dd < half).astype(F32)
    m_b = ((dd >= half) & (dd < ROT_DIM)).astype(F32)
    return freq, m_a, m_b


def _rope_tables(pos):
    freq, m_a, m_b = _rope_lane_freq()
    ang = pos.astype(F32)[:, None] * freq[None, :]
    cos, sin = jnp.cos(ang), jnp.sin(ang)
    return cos, -sin * m_a, sin * m_b


def _rope_split_tables(t, tm):
    freq, m_a, m_b = _rope_lane_freq()
    ang_r = jnp.arange(tm, dtype=F32)[:, None] * freq[None, :]
    cr, sr = jnp.cos(ang_r), jnp.sin(ang_r)
    rope_r = jnp.stack([cr, sr, -cr * m_a, -sr * m_a, cr * m_b, sr * m_b])
    ang_b = (jnp.arange(t // tm) * tm).astype(F32)[:, None] * freq[None, :]
    rope_base = jnp.concatenate([jnp.cos(ang_b), jnp.sin(ang_b)], 1)[:, None, :]
    return rope_r, rope_base


def _pad_last(v, width):
    return jnp.pad(v, ((0, 0), (0, width - v.shape[-1])))


def _common(w, t, tm_mix, past_len, nsm):
    depth = w['w_in'].shape[0]
    tile = lambda v, n: jnp.tile(v, (1, n))
    vecs = jnp.concatenate([
        w['norm1_g'], w['norm2_g'], w['conv_b'], w['lru_br'], w['lru_bi'], w['lru_lam'],
        tile(w['q_norm_g'], N_HEADS), tile(w['k_norm_g'], 2), _pad_last(w['attn_sinks'], LANES),
        tile(w['hgrn_norm_g'], N_HEADS), tile(w['mlstm_norm_g'], N_HEADS),
        _pad_last(w['mlstm_ib'], LANES), _pad_last(w['mlstm_fb'], LANES)], axis=1)
    cols = jnp.concatenate([tile(w['hgrn_norm_g'], N_HEADS), tile(w['mlstm_norm_g'], N_HEADS),
                            w['mlstm_ib'], w['mlstm_fb']], axis=1)
    w_in = w['w_in']
    zpad = jnp.zeros((depth, D_MODEL, LANES - N_HEADS), BF16)
    w_in_b = w_in.astype(BF16)
    sinks = w['attn_sinks']
    z2 = jnp.zeros((depth, 2, SUBLANES - 2), F32)
    idx = jnp.arange(GROUP_W)
    rope_r, rope_base = _rope_split_tables(t, tm_mix)
    return {
        'vecs': vecs,
        'cols': jnp.broadcast_to(cols[:, :, None], cols.shape + (nsm,)),
        'gamma': w['hgrn_gamma'],
        'gamma_col': jnp.broadcast_to(w['hgrn_gamma'][:, :, None], w['hgrn_gamma'].shape + (nsm,)),
        'w_in_p': jnp.concatenate([w_in_b[:, :, :C_GATES], w_in_b[:, :, C_GATES:C_GATES + N_HEADS], zpad,
                                   w_in_b[:, :, C_GATES + N_HEADS:], zpad], axis=2),
        'w_in_t': jnp.swapaxes(w_in, 1, 2),
        'conv_w': w['conv_w'],
        'wr_bd': _block_diag_all(w['lru_wr']).astype(BF16),
        'wi_bd': _block_diag_all(w['lru_wi']).astype(BF16),
        'ones_bd': (idx[:, None] // HEAD_DIM == idx[None, :] // HEAD_DIM).astype(BF16),
        'sinks8': jnp.concatenate([sinks.reshape(depth, 2, 2), z2], axis=2)[..., None],
        'rope_r': rope_r, 'rope_base': rope_base,
        'rope_s': jnp.concatenate(_rope_tables(past_len + jnp.arange(1)), axis=0),
        'w_out': w['w_out'].astype(BF16), 'w_up': w['w_up'].astype(BF16),
        'w_down': w['w_down'].astype(BF16), 'w_gate': w['w_ple_gate'].astype(BF16),
        'w_proj': w['w_ple_proj'].astype(BF16),
    }


def _run(x_prompt, x_sample, p_prompt, p_sample, sample_state, w, past_len, tm_mix=256, tm_ffn=512):
    depth = w['w_in'].shape[0]
    bsz, t, _ = x_prompt.shape
    nsm = x_sample.shape[0]
    cw = _common(w, t, tm_mix, past_len, nsm)
    tm_ffn = min(tm_ffn, bsz * t)
    h0, conv0, kc, vc, s0, c0, n0, m0 = sample_state
    sv = {'h': h0, 'conv': jnp.transpose(conv0, (0, 2, 1, 3)),
          'k': jnp.transpose(kc, (0, 1, 3, 4, 2)), 'v': jnp.transpose(vc, (0, 1, 3, 4, 2)),
          's': jnp.transpose(s0, (0, 2, 3, 4, 1)), 'c': jnp.transpose(c0, (0, 2, 3, 4, 1)),
          'n': jnp.transpose(n0, (0, 2, 3, 1)).reshape(depth, GROUP_W, nsm),
          'm': jnp.transpose(m0, (0, 2, 1)), 'p': p_sample.reshape(depth, nsm, PLE_DIM)}
    p3 = p_prompt.reshape(depth, bsz * t, PLE_DIM)

    hp = x_prompt
    hs = x_sample.reshape(nsm, D_MODEL)
    p_states, s_small, big = [], [], None
    for l in range(depth):
        y, hl, conv, kst, vst, sst, cst, mst = _prompt_mixers(hp, cw, l, tm_mix)
        hp = _prompt_ffn(hp.reshape(bsz * t, D_MODEL), y.reshape(bsz * t, D_MODEL), p3, cw, l,
                         tm_ffn).reshape(bsz, t, D_MODEL)
        s_hgrn = jnp.swapaxes(sst, -1, -2)
        s_hgrn = jnp.stack([s_hgrn[:, 0, 0:64, 0:64], s_hgrn[:, 0, 64:128, 64:128],
                            s_hgrn[:, 1, 0:64, 0:64], s_hgrn[:, 1, 64:128, 64:128]], 1)
        c_rows = jnp.stack([cst[:, 0, 0:64], cst[:, 1, 64:128], cst[:, 2, 0:64], cst[:, 3, 64:128]], 1)
        p_states.append((hl[:, 0], conv[:, 8 - (CONV_W - 1):], kst.reshape(bsz, WINDOW, 2, HEAD_DIM),
                         vst.reshape(bsz, WINDOW, 2, HEAD_DIM), s_hgrn, c_rows[..., 0:HEAD_DIM],
                         c_rows[..., HEAD_DIM], mst[:, 0, 0:N_HEADS]))
        hs, small, big = _sample_layer(hs, sv, big, cw, l)
        s_small.append(small)
    stack = lambda sts, i: jnp.stack([s[i] for s in sts])
    prompt_out = tuple(stack(p_states, i) for i in range(8))
    hn, convn, nn, mn = (stack(s_small, i) for i in range(4))
    sample_out = (hn, jnp.transpose(convn, (0, 2, 1, 3)),
                  jnp.transpose(big['k'], (0, 1, 4, 2, 3)), jnp.transpose(big['v'], (0, 1, 4, 2, 3)),
                  jnp.transpose(big['s'], (0, 4, 1, 2, 3)), jnp.transpose(big['c'], (0, 4, 1, 2, 3)),
                  jnp.transpose(nn.reshape(depth, N_HEADS, HEAD_DIM, nsm), (0, 3, 1, 2)),
                  jnp.transpose(mn, (0, 2, 1)))
    return (hp, hs.reshape(x_sample.shape)) + prompt_out + sample_out


def kernel(x_prompt, x_sample, p_prompt, p_sample, state_rglru_h, state_rglru_conv, cache_swa_k, cache_swa_v, state_hgrn_s, state_mlstm_c, state_mlstm_n, state_mlstm_m, norm1_g, w_in, conv_w, conv_b, lru_wr, lru_br, lru_wi, lru_bi, lru_lam, q_norm_g, k_norm_g, attn_sinks, hgrn_gamma, hgrn_norm_g, mlstm_ib, mlstm_fb, mlstm_norm_g, w_out, norm2_g, w_up, w_down, w_ple_gate, w_ple_proj):
    w = {'norm1_g': norm1_g, 'w_in': w_in, 'conv_w': conv_w, 'conv_b': conv_b, 'lru_wr': lru_wr,
         'lru_br': lru_br, 'lru_wi': lru_wi, 'lru_bi': lru_bi, 'lru_lam': lru_lam, 'q_norm_g': q_norm_g,
         'k_norm_g': k_norm_g, 'attn_sinks': attn_sinks, 'hgrn_gamma': hgrn_gamma,
         'hgrn_norm_g': hgrn_norm_g, 'mlstm_ib': mlstm_ib, 'mlstm_fb': mlstm_fb,
         'mlstm_norm_g': mlstm_norm_g, 'w_out': w_out, 'norm2_g': norm2_g, 'w_up': w_up,
         'w_down': w_down, 'w_ple_gate': w_ple_gate, 'w_ple_proj': w_ple_proj}
    st = (state_rglru_h, state_rglru_conv, cache_swa_k, cache_swa_v, state_hgrn_s, state_mlstm_c,
          state_mlstm_n, state_mlstm_m)
    past_len = 8192
    return _run(x_prompt, x_sample, p_prompt, p_sample, st, w, past_len)
```

```python
import functools

import jax
import jax.numpy as jnp
from jax import lax
from jax.experimental import pallas as pl
from jax.experimental.pallas import tpu as pltpu

F32 = jnp.float32
BF16 = jnp.bfloat16

D_MODEL = 1024
GROUP_W = 256
HEAD_DIM = 64
N_HEADS = 4
EPS = 1e-6
NEG_BIG = -1e30
LRU_C = 8.0
CONV_W = 4
ROT_DIM = 16
ROPE_THETA = 500000.0
WINDOW = 128
D_FF = 4096
PLE_DIM = 256
LANES = 128
SUBLANES = 8
BLK = 128

C_AX, C_AG, C_BQ, C_BK, C_BV = 0, 256, 512, 768, 896
C_CQ, C_CF, C_CI, C_CG = 1024, 1280, 1536, 1792
C_DQ, C_DK, C_DV, C_DO = 2048, 2304, 2560, 2816
C_GATES = 3072
D_IN = 3080
N_IN = 3200

V_G1, V_G2, V_CB, V_BR, V_BI, V_LAM = 0, 1024, 2048, 2304, 2560, 2816
V_QG, V_KG, V_SINK, V_HG, V_MG, V_IB, V_FB = 3072, 3328, 3456, 3584, 3840, 4096, 4224
N_VEC = 4352

R_HG, R_MG, R_GB = 0, 256, 512
N_COL = 520

FM_HF, FM_HK, FM_HQ, FM_HV, FM_CG = 0, 256, 512, 768, 1024
FM_MK, FM_MQ, FM_MV, FM_DO = 1280, 1536, 1792, 2048
FM_DEC, FM_W, FM_DEN, FM_ENEG = 2304, 2312, 2320, 2328
FM_ROWS = 2336

VMEM_LIMIT = 56 * 1024 * 1024


def _dot(a, b):
    return jnp.dot(a, b, preferred_element_type=F32)


def _dot_nt(a, b):
    return lax.dot_general(a, b, (((1,), (1,)), ((), ())), preferred_element_type=F32)


def _sigmoid(x):
    return jax.nn.sigmoid(x)


def _gelu_tanh(x):
    return 0.5 * x * (1.0 + jnp.tanh(0.7978845608028654 * (x + 0.044715 * (x * x * x))))


def _log_sigmoid(x):
    return jnp.minimum(x, 0.0) - jnp.log1p(jnp.exp(-jnp.abs(x)))


def _softplus(x):
    return jnp.maximum(x, 0.0) + jnp.log1p(jnp.exp(-jnp.abs(x)))


def _rms_rows(x, g):
    return x * lax.rsqrt(jnp.mean(x * x, axis=-1, keepdims=True) + EPS) * g


def _seg_mean_sq(x, ones_bd):
    sq = x * x
    hi = sq.astype(BF16)
    lo = (sq - hi.astype(F32)).astype(BF16)
    return (_dot(hi, ones_bd) + _dot(lo, ones_bd)) * (1.0 / HEAD_DIM)


def _seg_rms(x, g, ones_bd):
    return x * lax.rsqrt(_seg_mean_sq(x, ones_bd) + EPS) * g


def _rope(x, c, sa, sb):
    w = x.shape[1]
    up = pltpu.roll(x, w - ROT_DIM // 2, 1)
    dn = pltpu.roll(x, ROT_DIM // 2, 1)
    return x * c + up * sa + dn * sb


def _lb_from_gamma(gammas, layer):
    mx = functools.reduce(jnp.maximum, gammas)
    e = [jnp.exp(g - mx) for g in gammas]
    tot = functools.reduce(lambda a, b: a + b, e)
    lb = jnp.zeros_like(tot)
    for i in range(1, layer + 1):
        lb = lb + e[i] / tot
    return lb


def _iota(shape, axis):
    return lax.broadcasted_iota(jnp.int32, shape, axis)


def _rglru_gates(xc, wr, wi, br, bi, lam):
    xcb = xc.astype(BF16)
    r = _sigmoid(_dot(xcb, wr) + br)
    ig = _sigmoid(_dot(xcb, wi) + bi)
    log_a = (-LRU_C) * r * _softplus(-lam)
    return jnp.exp(log_a), jnp.sqrt(1.0 - jnp.exp(2.0 * log_a)) * (ig * xc)


def _swa_block(q, k, v, k_prev, v_prev, sink, first):
    kk = jnp.concatenate([k_prev, k], axis=0)
    vv = jnp.concatenate([v_prev, v], axis=0)
    qi = _iota((BLK, 2 * BLK), 0)
    kj = _iota((BLK, 2 * BLK), 1)
    valid = (kj > qi) & (kj <= qi + WINDOW) & ((kj >= BLK) | jnp.logical_not(first))
    outs = []
    for h in range(N_HEADS):
        kv = h // 2
        qh = q[:, h * HEAD_DIM:(h + 1) * HEAD_DIM].astype(BF16)
        kh = kk[:, kv * HEAD_DIM:(kv + 1) * HEAD_DIM].astype(BF16)
        vh = vv[:, kv * HEAD_DIM:(kv + 1) * HEAD_DIM].astype(BF16)
        s = _dot_nt(qh, kh) * (HEAD_DIM ** -0.5)
        s = jnp.where(valid, s, NEG_BIG)
        sk = sink[:, h:h + 1]
        mx = jnp.maximum(jnp.max(s, axis=-1, keepdims=True), sk)
        p = jnp.exp(s - mx)
        den = jnp.sum(p, axis=-1, keepdims=True) + jnp.exp(sk - mx)
        outs.append(_dot(p.astype(BF16), vh) / den)
    return jnp.concatenate(outs, axis=1)


def _head_masks(rows, dtype):
    lane = _iota((rows, LANES), 1)
    return (jnp.where(lane < HEAD_DIM, 1.0, 0.0).astype(dtype),
            jnp.where(lane >= HEAD_DIM, 1.0, 0.0).astype(dtype))


def _pair_scores(qe, ke_b, hm):
    n = qe.shape[0]
    if n != hm[0].shape[0]:
        hm = _head_masks(n, BF16)
    res = []
    for p in range(2):
        sl = slice(p * LANES, (p + 1) * LANES)
        qb = qe[:, sl].astype(BF16)
        lhs = jnp.concatenate([qb * hm[0], qb * hm[1]], axis=0)
        pr = _dot_nt(lhs, ke_b[:, sl])
        res += [pr[:n], pr[n:]]
    return res


def _hgrn_block(cq, cf, ci, cg, lb, hg, sst_ref, ones_bd, msmall_ref, mbig_ref, hm, sub):
    q = cq * _sigmoid(cq)
    f = lb + (1.0 - lb) * _sigmoid(cf)
    logf = jnp.log(f)
    k = 1.0 - f

    att = [s_ * msmall_ref[0] for s_ in _pair_scores(q, k.astype(BF16), hm)]

    c = logf
    tot = logf
    for lev in range(1, 4):
        half = 1 << (lev - 1)
        right = (sub & half) != 0
        tot_l = pltpu.roll(tot, half, 0)
        tot_r = pltpu.roll(tot, BLK - half, 0)
        e = jnp.exp(jnp.where(right, c, tot - c))
        sc = _pair_scores(q * e, (k * e).astype(BF16), hm)
        m = msmall_ref[lev]
        att = [a_ + s_ * m for a_, s_ in zip(att, sc)]
        c = c + jnp.where(right, tot_l, 0.0)
        tot = tot + jnp.where(right, tot_l, tot_r)

    pieces, carry = [], None
    for g in range(BLK // SUBLANES):
        rows = slice(g * SUBLANES, (g + 1) * SUBLANES)
        pieces.append(c[rows] if carry is None else c[rows] + carry)
        t_g = tot[g * SUBLANES:g * SUBLANES + 1]
        carry = t_g if carry is None else carry + t_g
    b = jnp.concatenate(pieces, axis=0)
    btot = carry

    for lev in range(4, 8):
        half = 1 << (lev - 1)
        nblk = BLK // (2 * half)
        qr, kf = [], []
        for i in range(nblk):
            lo = i * 2 * half
            mid = lo + half
            bref = b[mid - 1:mid]
            qr.append(q[mid:mid + half] * jnp.exp(b[mid:mid + half] - bref))
            kf.append(k[lo:mid] * jnp.exp(bref - b[lo:mid]))
            kf.append(jnp.zeros((half, GROUP_W), F32))
        sc = _pair_scores(jnp.concatenate(qr, axis=0), jnp.concatenate(kf, axis=0).astype(BF16), hm)
        m = mbig_ref[lev - 4]
        zero = jnp.zeros((half, BLK), F32)
        new = []
        for a_, s_ in zip(att, sc):
            u = s_ * m
            parts = []
            for i in range(nblk):
                parts += [zero, u[i * half:(i + 1) * half]]
            new.append(a_ + jnp.concatenate(parts, axis=0))
        att = new

    qe = (q * jnp.exp(b)).astype(BF16)
    ke = (k * jnp.exp(btot - b)).astype(BF16)
    etot = jnp.exp(btot)
    row_l = _iota((BLK, BLK), 0)
    col_l = _iota((BLK, BLK), 1)
    same_head = (row_l >= HEAD_DIM) == (col_l >= HEAD_DIM)
    outs = []
    for p in range(2):
        sl = slice(p * LANES, (p + 1) * LANES)
        st = sst_ref[p]
        vp = ci[:, sl]
        vb = vp.astype(BF16)
        a2 = jnp.concatenate([att[2 * p], att[2 * p + 1]], axis=1).astype(BF16)
        v2 = jnp.concatenate([vb * hm[0], vb * hm[1]], axis=0)
        o = _dot_nt(qe[:, sl], st.astype(BF16)) + _dot(a2, v2)
        upd = _dot(vp.T.astype(BF16), ke[:, sl])
        sst_ref[p] = st * etot[:, sl] + jnp.where(same_head, upd, 0.0)
        outs.append(o)
    o = jnp.concatenate(outs, axis=1)
    return _seg_rms(o, hg, ones_bd) * (cg * _sigmoid(cg))


def _mlstm_tile_gates(gcols, gb, m0, triu):
    tm = gcols.shape[0]
    gt = gcols.T[0:SUBLANES, :] + gb
    lf = _log_sigmoid(gt)
    hi = lf.astype(BF16)
    r1 = lf - hi.astype(F32)
    mid = r1.astype(BF16)
    lo = (r1 - mid.astype(F32)).astype(BF16)
    parts = jnp.concatenate([hi.astype(F32), mid.astype(F32), lo.astype(F32)], axis=0)
    cs = _dot(parts, triu)
    fcum = cs[0:SUBLANES] + cs[SUBLANES:2 * SUBLANES] + cs[2 * SUBLANES:3 * SUBLANES]
    fcum = pltpu.roll(fcum, N_HEADS, 0)
    g = gt - fcum
    pad = jnp.zeros((LANES - 2 * SUBLANES, tm), F32)
    cols = jnp.concatenate([g, fcum, pad], axis=0).T
    f_c = pltpu.roll(cols, LANES - SUBLANES, 1)
    sub = _iota((tm, LANES), 0) & (SUBLANES - 1)
    cm = cols
    s = 1
    while s < SUBLANES:
        cm = jnp.maximum(cm, jnp.where(sub >= s, pltpu.roll(cm, s, 0), NEG_BIG))
        s *= 2
    carry = m0
    ms = []
    for grp in range(tm // SUBLANES):
        m_g = jnp.maximum(cm[grp * SUBLANES:(grp + 1) * SUBLANES], carry)
        ms.append(m_g)
        carry = m_g[SUBLANES - 1:SUBLANES]
    m_c = jnp.concatenate(ms, axis=0)
    eneg_c = jnp.exp(-(f_c + m_c))
    m_new = f_c[tm - 1:tm] + carry
    return g, cols, m_c, eneg_c, m_new


def _mlstm_block(dq, dk, dv, do, g_rows, g_c, m_c, eneg_c, m_prev, mg, cst_ref, ones_bd, hm, hmf):
    k = dk * (HEAD_DIM ** -0.5)
    m_end = m_c[BLK - 1:BLK]
    inter = jnp.exp(m_prev - m_c)
    wend = jnp.exp(g_c - m_end)
    dec0 = jnp.exp(m_prev - m_end)
    mrun_c = lambda h: m_c[:, h:h + 1]
    inter_c = lambda h: inter[:, h:h + 1]
    eneg_c_ = lambda h: eneg_c[:, h:h + 1]
    wend_c = lambda h: wend[:, h:h + 1]
    g = g_rows
    lane = _iota((BLK, LANES), 1)
    low = lane < HEAD_DIM
    one_col = jnp.where(lane == HEAD_DIM, 1.0, 0.0)
    causal = _iota((BLK, BLK), 1) <= _iota((BLK, BLK), 0)

    outs = []
    for p in range(2):
        sl = slice(p * LANES, (p + 1) * LANES)
        kp, vp = k[:, sl], dv[:, sl]
        qb = dq[:, sl].astype(BF16)
        qm = [qb * hm[0], qb * hm[1]]
        sc2 = _dot_nt(jnp.concatenate(qm, axis=0), kp.astype(BF16))
        v_sw = pltpu.roll(vp, HEAD_DIM, 1)
        hv = []
        for hh in range(2):
            h = 2 * p + hh
            sc = sc2[hh * BLK:(hh + 1) * BLK]
            w = jnp.exp(jnp.where(causal, g[h:h + 1, :] - mrun_c(h), NEG_BIG))
            sw = (sc * w).astype(BF16)
            vaug = jnp.where(low, vp if hh == 0 else v_sw, one_col).astype(BF16)
            cst = cst_ref[h]
            nd = inter_c(h) * _dot(qm[hh], cst.astype(BF16)) + _dot(sw, vaug)
            den = nd[:, HEAD_DIM:HEAD_DIM + 1]
            hv.append(nd / jnp.maximum(jnp.abs(den), eneg_c_(h)))
            kw = kp * (wend_c(h) * hmf[hh])
            cst_ref[h] = dec0[:, h:h + 1] * cst + _dot(kw.T.astype(BF16), vaug)
        outs.append(jnp.where(low, hv[0], pltpu.roll(hv[1], HEAD_DIM, 1)))
    hcat = jnp.concatenate(outs, axis=1)
    return _seg_rms(hcat, mg, ones_bd) * _sigmoid(do)


def _mixer_kernel(h_ref, pv_ref, win_ref, convw_ref, wr_ref, wi_ref, rr_ref, rbase_ref, gamma_ref, ones_ref,
                  msmall_ref, mbig_ref, gb_ref, triu_ref, y_ref, hl_ref, conv_ref, kst_ref, vst_ref, sst_ref,
                  cst_ref, mst_ref, xbuf, *, layer, tm):
    t = pl.program_id(0)
    states = (hl_ref, conv_ref, kst_ref, vst_ref, sst_ref, cst_ref, mst_ref)

    @pl.when(t == 0)
    def _init():
        for ref in states:
            ref[...] = jnp.zeros_like(ref)

    for b in range(h_ref.shape[0]):
        _mixer_tile(t, h_ref.at[b], pv_ref, win_ref, convw_ref, wr_ref, wi_ref, rr_ref, rbase_ref, gamma_ref,
                    ones_ref, msmall_ref, mbig_ref, gb_ref, triu_ref, y_ref.at[b],
                    *[ref.at[b] for ref in states], xbuf.at[b], layer=layer, tm=tm)


def _mixer_tile(t, h_ref, pv_ref, win_ref, convw_ref, wr_ref, wi_ref, rr_ref, rbase_ref, gamma_ref, ones_ref,
                msmall_ref, mbig_ref, gb_ref, triu_ref, y_ref, hl_ref, conv_ref, kst_ref, vst_ref, sst_ref,
                cst_ref, mst_ref, xbuf, *, layer, tm):
    def vec(off, w):
        return pv_ref[layer:layer + 1, off:off + w]

    nb = _rms_rows(h_ref[...], vec(V_G1, D_MODEL)).astype(BF16)
    ones_bd = ones_ref[...]

    def proj(c0, w):
        return _dot(nb, win_ref[:, c0:c0 + w])

    xa = proj(C_AX, GROUP_W)
    ga = proj(C_AG, GROUP_W)
    xbuf[0:8, :] = conv_ref[...]
    xbuf[8:8 + tm, :] = xa
    cw = convw_ref[layer]
    xc = (vec(V_CB, GROUP_W) + xbuf[5:5 + tm, :] * cw[0:1, :] + xbuf[6:6 + tm, :] * cw[1:2, :]
          + xbuf[7:7 + tm, :] * cw[2:3, :] + xa * cw[3:4, :])
    conv_ref[...] = xbuf[tm:tm + 8, :]
    a, bx = _rglru_gates(xc, wr_ref[...], wi_ref[...], vec(V_BR, GROUP_W), vec(V_BI, GROUP_W),
                         vec(V_LAM, GROUP_W))
    sub_t = _iota((tm, GROUP_W), 0) & (SUBLANES - 1)
    s = 1
    while s < SUBLANES:
        keep = sub_t >= s
        a_s = pltpu.roll(a, s, 0)
        b_s = pltpu.roll(bx, s, 0)
        bx = jnp.where(keep, a * b_s + bx, bx)
        a = jnp.where(keep, a * a_s, a)
        s *= 2
    carry = hl_ref[...]
    hs = []
    for g in range(tm // SUBLANES):
        rows = slice(g * SUBLANES, (g + 1) * SUBLANES)
        hg_ = a[rows] * carry + bx[rows]
        hs.append(hg_)
        carry = hg_[SUBLANES - 1:SUBLANES]
    hseq = jnp.concatenate(hs, axis=0)
    hl_ref[...] = carry
    y_ref[:, 0:GROUP_W] = (hseq * _gelu_tanh(ga)).astype(y_ref.dtype)

    cb = rbase_ref[:, 0:LANES]
    sb_ = rbase_ref[:, LANES:2 * LANES]
    rc = cb * rr_ref[0] - sb_ * rr_ref[1]
    ra = sb_ * rr_ref[2] + cb * rr_ref[3]
    rb = sb_ * rr_ref[4] + cb * rr_ref[5]
    q = _seg_rms(proj(C_BQ, GROUP_W), vec(V_QG, GROUP_W), ones_bd)
    q = _rope(q, jnp.concatenate([rc, rc], 1), jnp.concatenate([ra, ra], 1), jnp.concatenate([rb, rb], 1))
    k = _seg_rms(proj(C_BK, LANES), vec(V_KG, LANES), ones_bd[0:LANES, 0:LANES])
    k = _rope(k, rc, ra, rb)
    v = proj(C_BV, LANES)
    sink = vec(V_SINK, LANES)
    k_prev, v_prev = kst_ref[...], vst_ref[...]
    for j in range(tm // BLK):
        rs = slice(j * BLK, (j + 1) * BLK)
        first = (t == 0) if j == 0 else False
        yb = _swa_block(q[rs], k[rs], v[rs], k_prev, v_prev, sink, first)
        y_ref[rs, GROUP_W:2 * GROUP_W] = yb.astype(y_ref.dtype)
        k_prev, v_prev = k[rs], v[rs]
    kst_ref[...] = k_prev
    vst_ref[...] = v_prev

    lb = _lb_from_gamma([gamma_ref[i:i + 1, :] for i in range(gamma_ref.shape[0])], layer)
    cq, cf, ci, cg = (proj(C_CQ, GROUP_W), proj(C_CF, GROUP_W), proj(C_CI, GROUP_W), proj(C_CG, GROUP_W))
    hg = vec(V_HG, GROUP_W)
    hm = _head_masks(BLK, BF16)
    hmf = _head_masks(1, F32)
    sub = _iota((BLK, GROUP_W), 0) & (SUBLANES - 1)
    for j in range(tm // BLK):
        rs = slice(j * BLK, (j + 1) * BLK)
        yc = _hgrn_block(cq[rs], cf[rs], ci[rs], cg[rs], lb, hg, sst_ref, ones_bd, msmall_ref, mbig_ref,
                         hm, sub)
        y_ref[rs, 2 * GROUP_W:3 * GROUP_W] = yc.astype(y_ref.dtype)

    dq, dk, dv, do = (proj(C_DQ, GROUP_W), proj(C_DK, GROUP_W), proj(C_DV, GROUP_W), proj(C_DO, GROUP_W))
    mg = vec(V_MG, GROUP_W)
    m_prev = mst_ref[...]
    g_rows, g_c, m_c, eneg_c, m_new = _mlstm_tile_gates(proj(C_GATES, LANES), gb_ref[layer], m_prev,
                                                       triu_ref[...])
    mst_ref[...] = m_new
    for j in range(tm // BLK):
        rs = slice(j * BLK, (j + 1) * BLK)
        yd = _mlstm_block(dq[rs], dk[rs], dv[rs], do[rs], g_rows[:, rs], g_c[rs], m_c[rs], eneg_c[rs],
                          m_prev, mg, cst_ref, ones_bd, hm, hmf)
        y_ref[rs, 3 * GROUP_W:4 * GROUP_W] = yd.astype(y_ref.dtype)
        m_prev = m_c[(j + 1) * BLK - 1:(j + 1) * BLK]


def _full(shape):
    nd = len(shape)
    return pl.BlockSpec(shape, lambda *_: (0,) * nd)


def _layer_block(shape, layer, single_buffer=False):
    nd = len(shape) - 1
    kw = {'pipeline_mode': pl.Buffered(1)} if single_buffer else {}
    return pl.BlockSpec((None,) + tuple(shape[1:]), lambda *_: (layer,) + (0,) * nd, **kw)


def _prompt_mixers(h, cw, layer, tm):
    bsz, t, _ = h.shape
    nt = t // tm
    kern = functools.partial(_mixer_kernel, layer=layer, tm=tm)
    in_specs = [pl.BlockSpec((bsz, tm, D_MODEL), lambda i: (0, i, 0)),
                _full(cw['vecs'].shape), _layer_block(cw['w_in_p'].shape, layer),
                _full(cw['conv_w'].shape), _layer_block(cw['wr_bd'].shape, layer),
                _layer_block(cw['wi_bd'].shape, layer), _full(cw['rope_r'].shape),
                pl.BlockSpec((None, 1, 2 * LANES), lambda i: (i, 0, 0)),
                _full(cw['gamma'].shape), _full(cw['ones_bd'].shape), _full(cw['lvl_small'].shape),
                _full(cw['lvl_big'].shape), _full(cw['gate_bias'].shape), _full(cw['triu'].shape)]
    st_shapes = [(bsz, 1, GROUP_W), (bsz, 8, GROUP_W), (bsz, BLK, LANES), (bsz, BLK, LANES),
                 (bsz, 2, LANES, LANES), (bsz, N_HEADS, LANES, LANES), (bsz, 1, LANES)]
    out_shape = ([jax.ShapeDtypeStruct((bsz, t, D_MODEL), BF16)]
                 + [jax.ShapeDtypeStruct(s, F32) for s in st_shapes])
    out_specs = [pl.BlockSpec((bsz, tm, D_MODEL), lambda i: (0, i, 0))] + [_full(s) for s in st_shapes]
    return pl.pallas_call(
        kern, grid=(nt,), in_specs=in_specs, out_specs=out_specs, out_shape=out_shape,
        scratch_shapes=[pltpu.VMEM((bsz, tm + 8, GROUP_W), F32)],
        compiler_params=pltpu.CompilerParams(dimension_semantics=("arbitrary",),
                                             vmem_limit_bytes=VMEM_LIMIT),
        name=f"prompt_mixers_l{layer}",
    )(h, cw['vecs'], cw['w_in_p'], cw['conv_w'], cw['wr_bd'], cw['wi_bd'], cw['rope_r'], cw['rope_base'],
      cw['gamma'], cw['ones_bd'], cw['lvl_small'], cw['lvl_big'], cw['gate_bias'], cw['triu'])


def _ffn_math(h, yb, p, g2, wout_ref, wup_ref, wdn_ref, wg_ref, wp_ref):
    h = h + _dot(yb, wout_ref[...])
    nb = _rms_rows(h, g2).astype(BF16)
    acc = h
    step = 1024
    for c in range(0, D_FF, step):
        f = jnp.maximum(_dot(nb, wup_ref[:, c:c + step]), 0.0)
        acc = acc + _dot((f * f).astype(BF16), wdn_ref[c:c + step, :])
    gate = _sigmoid(_dot(acc.astype(BF16), wg_ref[...]))
    return acc + gate * _dot(p.astype(BF16), wp_ref[...])


def _ffn_kernel(h_ref, y_ref, p_ref, pv_ref, wout_ref, wup_ref, wdn_ref, wg_ref, wp_ref, o_ref, *, layer):
    g2 = pv_ref[layer:layer + 1, V_G2:V_G2 + D_MODEL]
    o_ref[...] = _ffn_math(h_ref[...], y_ref[...], p_ref[...], g2, wout_ref, wup_ref, wdn_ref, wg_ref, wp_ref)


def _ffn_weight_specs(cw, layer):
    names = ['w_out', 'w_up', 'w_down', 'w_gate', 'w_proj']
    return [cw[n] for n in names], [_layer_block(cw[n].shape, layer, single_buffer=True) for n in names]


def _prompt_ffn(h2, y2, p3, cw, layer, tm):
    n = h2.shape[0]
    row = lambda w: pl.BlockSpec((tm, w), lambda i: (i, 0))
    ws, wspecs = _ffn_weight_specs(cw, layer)
    return pl.pallas_call(
        functools.partial(_ffn_kernel, layer=layer), grid=(n // tm,),
        in_specs=[row(D_MODEL), row(D_MODEL), pl.BlockSpec((None, tm, PLE_DIM), lambda i: (layer, i, 0)),
                  _full(cw['vecs'].shape)] + wspecs,
        out_specs=row(D_MODEL), out_shape=jax.ShapeDtypeStruct((n, D_MODEL), F32),
        compiler_params=pltpu.CompilerParams(dimension_semantics=("arbitrary",),
                                             vmem_limit_bytes=VMEM_LIMIT),
        name=f"prompt_ffn_l{layer}",
    )(h2, y2, p3, cw['vecs'], *ws)


def _sample_pre_kernel(x_ref, pv_ref, wt_ref, convw_ref, wr_ref, wi_ref, ones_ref, rope_ref, gcol_ref,
                       cols_ref, h0_ref, conv0_ref, n0_ref, m0_ref,
                       ya_ref, hn_ref, convn_ref, q_ref, kt_ref, vt_ref, fm_ref, nn_ref, mn_ref, *, layer):
    def vec(off, w):
        return pv_ref[layer:layer + 1, off:off + w]

    n = _rms_rows(x_ref[...], vec(V_G1, D_MODEL))
    nb = n.astype(BF16)
    n_t = n.T.astype(BF16)
    ones_bd = ones_ref[...]

    def proj(r0, cnt):
        return _dot_nt(nb, wt_ref[r0:r0 + cnt, :].astype(BF16))

    def proj_t(r0, cnt):
        return _dot(wt_ref[r0:r0 + cnt, :].astype(BF16), n_t)

    xa = proj(C_AX, GROUP_W)
    ga = proj(C_AG, GROUP_W)
    cw = convw_ref[layer]
    xc = (vec(V_CB, GROUP_W) + conv0_ref[0] * cw[0:1, :] + conv0_ref[1] * cw[1:2, :]
          + conv0_ref[2] * cw[2:3, :] + xa * cw[3:4, :])
    convn_ref[0] = conv0_ref[1]
    convn_ref[1] = conv0_ref[2]
    convn_ref[2] = xa
    a, bx = _rglru_gates(xc, wr_ref[...], wi_ref[...], vec(V_BR, GROUP_W), vec(V_BI, GROUP_W),
                         vec(V_LAM, GROUP_W))
    hn = a * h0_ref[...] + bx
    hn_ref[...] = hn
    ya_ref[...] = hn * _gelu_tanh(ga)

    rc, ra, rb = rope_ref[0:1, :], rope_ref[1:2, :], rope_ref[2:3, :]
    q = _seg_rms(proj(C_BQ, GROUP_W), vec(V_QG, GROUP_W), ones_bd)
    q_ref[...] = _rope(q, jnp.concatenate([rc, rc], 1), jnp.concatenate([ra, ra], 1),
                       jnp.concatenate([rb, rb], 1))
    k = _seg_rms(proj(C_BK, LANES), vec(V_KG, LANES), ones_bd[0:LANES, 0:LANES])
    kt_ref[...] = _rope(k, rc, ra, rb).T
    vt_ref[...] = proj(C_BV, LANES).T

    lb = _lb_from_gamma([gcol_ref[i] for i in range(gcol_ref.shape[0])], layer)
    cq, cf, ci, cg = (proj_t(C_CQ, GROUP_W), proj_t(C_CF, GROUP_W), proj_t(C_CI, GROUP_W),
                      proj_t(C_CG, GROUP_W))
    f = lb + (1.0 - lb) * _sigmoid(cf)
    fm_ref[FM_HF:FM_HF + GROUP_W, :] = f
    fm_ref[FM_HK:FM_HK + GROUP_W, :] = 1.0 - f
    fm_ref[FM_HQ:FM_HQ + GROUP_W, :] = cq * _sigmoid(cq)
    fm_ref[FM_HV:FM_HV + GROUP_W, :] = ci
    fm_ref[FM_CG:FM_CG + GROUP_W, :] = cg * _sigmoid(cg)

    dq, dk, dv, do = (proj_t(C_DQ, GROUP_W), proj_t(C_DK, GROUP_W), proj_t(C_DV, GROUP_W),
                      proj_t(C_DO, GROUP_W))
    g8 = proj_t(C_GATES, 2 * N_HEADS) + cols_ref[layer, R_GB:R_GB + 2 * N_HEADS, :]
    ig = g8[0:N_HEADS, :]
    lf = _log_sigmoid(g8)[N_HEADS:2 * N_HEADS, :]
    a_int = lf + m0_ref[...]
    m_new = jnp.maximum(a_int, ig)
    dec = jnp.exp(a_int - m_new)
    w = jnp.exp(ig - m_new)
    mn_ref[...] = m_new
    km = dk * (HEAD_DIM ** -0.5)
    dens = []
    for h in range(N_HEADS):
        sl = slice(h * HEAD_DIM, (h + 1) * HEAD_DIM)
        nn_h = dec[h:h + 1, :] * n0_ref[sl, :] + w[h:h + 1, :] * km[sl, :]
        nn_ref[sl, :] = nn_h
        dens.append(jnp.sum(dq[sl, :] * nn_h, axis=0, keepdims=True))
    pad = jnp.zeros((SUBLANES - N_HEADS, x_ref.shape[0]), F32)
    fm_ref[FM_MK:FM_MK + GROUP_W, :] = km
    fm_ref[FM_MQ:FM_MQ + GROUP_W, :] = dq
    fm_ref[FM_MV:FM_MV + GROUP_W, :] = dv
    fm_ref[FM_DO:FM_DO + GROUP_W, :] = _sigmoid(do)
    fm_ref[FM_DEC:FM_DEC + SUBLANES, :] = jnp.concatenate([dec, pad], 0)
    fm_ref[FM_W:FM_W + SUBLANES, :] = jnp.concatenate([w, pad], 0)
    fm_ref[FM_DEN:FM_DEN + SUBLANES, :] = jnp.concatenate(dens + [pad], 0)
    fm_ref[FM_ENEG:FM_ENEG + SUBLANES, :] = jnp.concatenate([jnp.exp(-m_new), pad], 0)


def _own_slab(ref, first_layer):
    if not first_layer:
        return ref
    ref[1:] = jnp.zeros((ref.shape[0] - 1,) + ref.shape[1:], ref.dtype)
    return ref.at[0]


def _sample_attn_kernel(q_ref, kn_ref, vn_ref, kc_ref, vc_ref, sink_ref, *rest, first_layer):
    ko_ref, vo_ref, o_ref = rest[-3:]
    ko_ref, vo_ref = _own_slab(ko_ref, first_layer), _own_slab(vo_ref, first_layer)
    sb = q_ref.shape[0]
    rows = 2 * HEAD_DIM
    lane = _iota((rows, WINDOW), 1)
    kn = kn_ref[...].reshape(rows, sb)
    vn = vn_ref[...].reshape(rows, sb)
    for s in range(sb):
        kt = pltpu.roll(kc_ref[s].reshape(rows, WINDOW), WINDOW - 1, 1)
        ko_ref[s] = jnp.where(lane == WINDOW - 1, kn[:, s:s + 1], kt).reshape(2, HEAD_DIM, WINDOW)
        vt = pltpu.roll(vc_ref[s].reshape(rows, WINDOW), WINDOW - 1, 1)
        vo_ref[s] = jnp.where(lane == WINDOW - 1, vn[:, s:s + 1], vt).reshape(2, HEAD_DIM, WINDOW)
    for kv in range(2):
        kk = ko_ref[:, kv].astype(BF16)
        vv = vo_ref[:, kv].astype(BF16)
        s_ = jnp.einsum('bqc,bcj->bqj', q_ref[:, kv].astype(BF16), kk,
                        preferred_element_type=F32) * (HEAD_DIM ** -0.5)
        sk = sink_ref[kv]
        mx = jnp.maximum(jnp.max(s_, axis=-1, keepdims=True), sk)
        p = jnp.exp(s_ - mx)
        den = jnp.sum(p, axis=-1, keepdims=True) + jnp.exp(sk - mx)
        o = jnp.einsum('bqj,bcj->bqc', p.astype(BF16), vv, preferred_element_type=F32)
        o_ref[:, kv] = o / den


def _sample_state_kernel(fm_ref, s_ref, c_ref, *rest, first_layer):
    so_ref, co_ref, oh_ref, om_ref = rest[-4:]
    so_ref, co_ref = _own_slab(so_ref, first_layer), _own_slab(co_ref, first_layer)
    h = pl.program_id(0)
    r0 = pl.multiple_of(h * HEAD_DIM, HEAD_DIM)
    hv = fm_ref[pl.ds(FM_HV + r0, HEAD_DIM), :]
    mv = fm_ref[pl.ds(FM_MV + r0, HEAD_DIM), :]
    dec = fm_ref[pl.ds(FM_DEC + h, 1), :]
    w = fm_ref[pl.ds(FM_W + h, 1), :]

    def body(d, carry):
        acc_h, acc_m = carry
        r = r0 + d
        s_new = fm_ref[pl.ds(FM_HF + r, 1), :] * s_ref[d] + fm_ref[pl.ds(FM_HK + r, 1), :] * hv
        so_ref[d] = s_new
        c_new = dec * c_ref[d] + (w * fm_ref[pl.ds(FM_MK + r, 1), :]) * mv
        co_ref[d] = c_new
        return (acc_h + fm_ref[pl.ds(FM_HQ + r, 1), :] * s_new,
                acc_m + fm_ref[pl.ds(FM_MQ + r, 1), :] * c_new)

    zero = jnp.zeros((HEAD_DIM, fm_ref.shape[1]), F32)
    acc_h, acc_m = lax.fori_loop(0, HEAD_DIM, body, (zero, zero), unroll=4)
    oh_ref[...] = acc_h
    om_ref[...] = acc_m


def _sample_post_kernel(h_ref, ya_ref, yb_ref, oh_ref, om_ref, fm_ref, cols_ref, p_ref, pv_ref,
                        wout_ref, wup_ref, wdn_ref, wg_ref, wp_ref, o_ref, *, layer):
    def head_rms(x):
        return x * lax.rsqrt(jnp.mean(x * x, axis=0, keepdims=True) + EPS)

    yc, yd = [], []
    for h in range(N_HEADS):
        sl = slice(h * HEAD_DIM, (h + 1) * HEAD_DIM)
        yc.append(head_rms(oh_ref[h]))
        den = fm_ref[FM_DEN + h:FM_DEN + h + 1, :]
        eneg = fm_ref[FM_ENEG + h:FM_ENEG + h + 1, :]
        yd.append(head_rms(om_ref[h] / jnp.maximum(jnp.abs(den), eneg)))
    yc = (jnp.concatenate(yc, 0) * cols_ref[layer, R_HG:R_HG + GROUP_W, :]
          * fm_ref[FM_CG:FM_CG + GROUP_W, :])
    yd = (jnp.concatenate(yd, 0) * cols_ref[layer, R_MG:R_MG + GROUP_W, :]
          * fm_ref[FM_DO:FM_DO + GROUP_W, :])
    y = jnp.concatenate([ya_ref[...], yb_ref[...], yc.T, yd.T], axis=1).astype(BF16)
    g2 = pv_ref[layer:layer + 1, V_G2:V_G2 + D_MODEL]
    o_ref[...] = _ffn_math(h_ref[...], y, p_ref[...], g2, wout_ref, wup_ref, wdn_ref, wg_ref, wp_ref)


def _call_full(kern, args, specs, out_shape, name):
    specs = [(_full(a.shape) if s is None else s) for a, s in zip(args, specs)]
    return pl.pallas_call(
        kern, grid=(1,), in_specs=specs,
        out_specs=tuple(_full(s.shape) for s in out_shape), out_shape=tuple(out_shape),
        compiler_params=pltpu.CompilerParams(dimension_semantics=("arbitrary",),
                                             vmem_limit_bytes=VMEM_LIMIT),
        name=name,
    )(*args)


def _sample_layer(h, sv, prev, cw, layer):
    nsm = h.shape[0]
    depth = cw['vecs'].shape[0]
    sd = lambda *shape: jax.ShapeDtypeStruct(shape, F32)
    g, l = sd(nsm, GROUP_W), sd(LANES, nsm)

    args = [h, cw['vecs'], cw['w_in_t'], cw['conv_w'], cw['wr_bd'], cw['wi_bd'], cw['ones_bd'],
            cw['rope_s'], cw['gamma_col'], cw['cols'], sv['h'], sv['conv'], sv['n'], sv['m']]
    specs = [None, None, _layer_block(cw['w_in_t'].shape, layer, single_buffer=True), None,
             _layer_block(cw['wr_bd'].shape, layer), _layer_block(cw['wi_bd'].shape, layer), None,
             None, None, None, _layer_block(sv['h'].shape, layer), _layer_block(sv['conv'].shape, layer),
             _layer_block(sv['n'].shape, layer), _layer_block(sv['m'].shape, layer)]
    outs = [g, g, sd(CONV_W - 1, nsm, GROUP_W), g, l, l, sd(FM_ROWS, nsm), sd(GROUP_W, nsm),
            sd(N_HEADS, nsm)]
    ya, hn, convn, q, kt, vt, fm, nn, mn = _call_full(
        functools.partial(_sample_pre_kernel, layer=layer), args, specs, outs, f"sample_pre_l{layer}")

    sb = 16
    nblk = nsm // sb
    q3 = jnp.pad(q.reshape(nsm, 2, 2, HEAD_DIM), ((0, 0), (0, 0), (0, SUBLANES - 2), (0, 0)))
    to_blocks = lambda a: a.reshape(2, HEAD_DIM, nblk, sb).transpose(2, 0, 1, 3)
    cshape = (depth, nsm, 2, HEAD_DIM, WINDOW)
    cspec = pl.BlockSpec((None, sb, 2, HEAD_DIM, WINDOW), lambda i: (layer, i, 0, 0, 0))
    nspec = pl.BlockSpec((None, 2, HEAD_DIM, sb), lambda i: (i, 0, 0, 0))
    qspec = pl.BlockSpec((sb, 2, SUBLANES, HEAD_DIM), lambda i: (i, 0, 0, 0))
    any_spec = pl.BlockSpec(memory_space=pl.ANY)
    cout = (pl.BlockSpec((depth, sb, 2, HEAD_DIM, WINDOW), lambda i: (0, i, 0, 0, 0)) if prev is None
            else cspec)
    chain = [] if prev is None else [prev['k'], prev['v']]
    ko, vo, o3 = pl.pallas_call(
        functools.partial(_sample_attn_kernel, first_layer=prev is None), grid=(nblk,),
        in_specs=[qspec, nspec, nspec, cspec, cspec, _layer_block(cw['sinks8'].shape, layer)]
        + [any_spec] * len(chain),
        out_specs=(cout, cout, qspec),
        out_shape=(sd(*cshape), sd(*cshape), sd(nsm, 2, SUBLANES, HEAD_DIM)),
        input_output_aliases={6 + i: i for i in range(len(chain))},
        compiler_params=pltpu.CompilerParams(dimension_semantics=("arbitrary",),
                                             vmem_limit_bytes=VMEM_LIMIT),
        name=f"sample_attn_l{layer}",
    )(q3, to_blocks(kt), to_blocks(vt), sv['k'], sv['v'], cw['sinks8'], *chain)
    yb = o3[:, :, 0:2, :].reshape(nsm, GROUP_W)

    sshape = (depth, N_HEADS, HEAD_DIM, HEAD_DIM, nsm)
    sspec = pl.BlockSpec((None, None, HEAD_DIM, HEAD_DIM, nsm), lambda i: (layer, i, 0, 0, 0))
    ospec = pl.BlockSpec((None, HEAD_DIM, nsm), lambda i: (i, 0, 0))
    sout = (pl.BlockSpec((depth, None, HEAD_DIM, HEAD_DIM, nsm), lambda i: (0, i, 0, 0, 0))
            if prev is None else sspec)
    chain = [] if prev is None else [prev['s'], prev['c']]
    so, co, oh, om = pl.pallas_call(
        functools.partial(_sample_state_kernel, first_layer=prev is None), grid=(N_HEADS,),
        in_specs=[_full(fm.shape), sspec, sspec] + [any_spec] * len(chain),
        out_specs=(sout, sout, ospec, ospec),
        out_shape=(sd(*sshape), sd(*sshape), sd(N_HEADS, HEAD_DIM, nsm), sd(N_HEADS, HEAD_DIM, nsm)),
        input_output_aliases={3 + i: i for i in range(len(chain))},
        compiler_params=pltpu.CompilerParams(dimension_semantics=("arbitrary",),
                                             vmem_limit_bytes=VMEM_LIMIT),
        name=f"sample_state_l{layer}",
    )(fm, sv['s'], sv['c'], *chain)

    ws, wspecs = _ffn_weight_specs(cw, layer)
    (h_new,) = _call_full(
        functools.partial(_sample_post_kernel, layer=layer),
        [h, ya, yb, oh, om, fm, cw['cols'], sv['p'], cw['vecs']] + ws,
        [None] * 7 + [_layer_block(sv['p'].shape, layer), None] + wspecs,
        [sd(nsm, D_MODEL)], f"sample_post_l{layer}")
    small = (hn, convn, nn, mn)
    big = {'k': ko, 'v': vo, 's': so, 'c': co}
    return h_new, small, big


def _block_diag_all(w):
    depth = w.shape[0]
    rows = w.reshape(depth, GROUP_W, HEAD_DIM)
    idx = jnp.arange(GROUP_W) // HEAD_DIM
    mask = idx[:, None] == idx[None, :]
    return jnp.where(mask[None], jnp.tile(rows, (1, 1, N_HEADS)), 0.0)


def _rope_lane_freq():
    half = ROT_DIM // 2
    inv = jnp.power(ROPE_THETA, -jnp.arange(half, dtype=F32) * (2.0 / ROT_DIM))
    dd = jnp.arange(LANES) % HEAD_DIM
    freq = jnp.where(dd < ROT_DIM, inv[dd % half], 0.0)
    m_a = (dd < half).astype(F32)
    m_b = ((dd >= half) & (dd < ROT_DIM)).astype(F32)
    return freq, m_a, m_b


def _rope_tables(pos):
    freq, m_a, m_b = _rope_lane_freq()
    ang = pos.astype(F32)[:, None] * freq[None, :]
    cos, sin = jnp.cos(ang), jnp.sin(ang)
    return cos, -sin * m_a, sin * m_b


def _rope_split_tables(t, tm):
    freq, m_a, m_b = _rope_lane_freq()
    ang_r = jnp.arange(tm, dtype=F32)[:, None] * freq[None, :]
    cr, sr = jnp.cos(ang_r), jnp.sin(ang_r)
    rope_r = jnp.stack([cr, sr, -cr * m_a, -sr * m_a, cr * m_b, sr * m_b])
    ang_b = (jnp.arange(t // tm) * tm).astype(F32)[:, None] * freq[None, :]
    rope_base = jnp.concatenate([jnp.cos(ang_b), jnp.sin(ang_b)], 1)[:, None, :]
    return rope_r, rope_base


def _hgrn_level_masks():
    t = jnp.arange(BLK)[:, None]
    s = jnp.arange(BLK)[None, :]
    small = [t == s]
    big = []
    for lev in range(1, 8):
        half = 1 << (lev - 1)
        own = ((t >> lev) == (s >> lev)) & ((t & half) != 0) & ((s & half) == 0)
        if lev < 4:
            small.append(own)
        else:
            rows = jnp.concatenate([jnp.arange(m, m + half) for m in range(half, BLK, 2 * half)])
            big.append(own[rows])
    return jnp.stack(small).astype(F32), jnp.stack(big).astype(F32)


def _pad_last(v, width):
    return jnp.pad(v, ((0, 0), (0, width - v.shape[-1])))


def _common(w, t, tm_mix, past_len, nsm):
    depth = w['w_in'].shape[0]
    tile = lambda v, n: jnp.tile(v, (1, n))
    vecs = jnp.concatenate([
        w['norm1_g'], w['norm2_g'], w['conv_b'], w['lru_br'], w['lru_bi'], w['lru_lam'],
        tile(w['q_norm_g'], N_HEADS), tile(w['k_norm_g'], 2), _pad_last(w['attn_sinks'], LANES),
        tile(w['hgrn_norm_g'], N_HEADS), tile(w['mlstm_norm_g'], N_HEADS),
        _pad_last(w['mlstm_ib'], LANES), _pad_last(w['mlstm_fb'], LANES)], axis=1)
    cols = jnp.concatenate([tile(w['hgrn_norm_g'], N_HEADS), tile(w['mlstm_norm_g'], N_HEADS),
                            w['mlstm_ib'], w['mlstm_fb']], axis=1)
    w_in = w['w_in']
    gate_bias = jnp.concatenate([w['mlstm_ib'], w['mlstm_fb']], axis=1)
    lvl_small, lvl_big = _hgrn_level_masks()
    sinks = w['attn_sinks']
    z2 = jnp.zeros((depth, 2, SUBLANES - 2), F32)
    idx = jnp.arange(GROUP_W)
    rope_r, rope_base = _rope_split_tables(t, tm_mix)
    return {
        'vecs': vecs,
        'cols': jnp.broadcast_to(cols[:, :, None], cols.shape + (nsm,)),
        'gamma': w['hgrn_gamma'],
        'gamma_col': jnp.broadcast_to(w['hgrn_gamma'][:, :, None], w['hgrn_gamma'].shape + (nsm,)),
        'w_in_p': jnp.pad(w_in, ((0, 0), (0, 0), (0, N_IN - D_IN))).astype(BF16),
        'lvl_small': lvl_small, 'lvl_big': lvl_big,
        'gate_bias': jnp.broadcast_to(gate_bias[:, :, None], gate_bias.shape + (tm_mix,)),
        'triu': (jnp.arange(tm_mix)[:, None] <= jnp.arange(tm_mix)[None, :]).astype(F32),
        'w_in_t': jnp.swapaxes(w_in, 1, 2),
        'conv_w': w['conv_w'],
        'wr_bd': _block_diag_all(w['lru_wr']).astype(BF16),
        'wi_bd': _block_diag_all(w['lru_wi']).astype(BF16),
        'ones_bd': (idx[:, None] // HEAD_DIM == idx[None, :] // HEAD_DIM).astype(BF16),
        'sinks8': jnp.concatenate([sinks.reshape(depth, 2, 2), z2], axis=2)[..., None],
        'rope_r': rope_r, 'rope_base': rope_base,
        'rope_s': jnp.concatenate(_rope_tables(past_len + jnp.arange(1)), axis=0),
        'w_out': w['w_out'].astype(BF16), 'w_up': w['w_up'].astype(BF16),
        'w_down': w['w_down'].astype(BF16), 'w_gate': w['w_ple_gate'].astype(BF16),
        'w_proj': w['w_ple_proj'].astype(BF16),
    }


def _run(x_prompt, x_sample, p_prompt, p_sample, sample_state, w, past_len, tm_mix=256, tm_ffn=512):
    depth = w['w_in'].shape[0]
    bsz, t, _ = x_prompt.shape
    nsm = x_sample.shape[0]
    cw = _common(w, t, tm_mix, past_len, nsm)
    tm_ffn = min(tm_ffn, bsz * t)
    h0, conv0, kc, vc, s0, c0, n0, m0 = sample_state
    sv = {'h': h0, 'conv': jnp.transpose(conv0, (0, 2, 1, 3)),
          'k': jnp.transpose(kc, (0, 1, 3, 4, 2)), 'v': jnp.transpose(vc, (0, 1, 3, 4, 2)),
          's': jnp.transpose(s0, (0, 2, 3, 4, 1)), 'c': jnp.transpose(c0, (0, 2, 3, 4, 1)),
          'n': jnp.transpose(n0, (0, 2, 3, 1)).reshape(depth, GROUP_W, nsm),
          'm': jnp.transpose(m0, (0, 2, 1)), 'p': p_sample.reshape(depth, nsm, PLE_DIM)}
    p3 = p_prompt.reshape(depth, bsz * t, PLE_DIM)

    hp = x_prompt
    hs = x_sample.reshape(nsm, D_MODEL)
    p_states, s_small, big = [], [], None
    for l in range(depth):
        y, hl, conv, kst, vst, sst, cst, mst = _prompt_mixers(hp, cw, l, tm_mix)
        hp = _prompt_ffn(hp.reshape(bsz * t, D_MODEL), y.reshape(bsz * t, D_MODEL), p3, cw, l,
                         tm_ffn).reshape(bsz, t, D_MODEL)
        s_hgrn = jnp.swapaxes(sst, -1, -2)
        s_hgrn = jnp.stack([s_hgrn[:, 0, 0:64, 0:64], s_hgrn[:, 0, 64:128, 64:128],
                            s_hgrn[:, 1, 0:64, 0:64], s_hgrn[:, 1, 64:128, 64:128]], 1)
        c_rows = jnp.stack([cst[:, 0, 0:64], cst[:, 1, 64:128], cst[:, 2, 0:64], cst[:, 3, 64:128]], 1)
        p_states.append((hl[:, 0], conv[:, 8 - (CONV_W - 1):], kst.reshape(bsz, WINDOW, 2, HEAD_DIM),
                         vst.reshape(bsz, WINDOW, 2, HEAD_DIM), s_hgrn, c_rows[..., 0:HEAD_DIM],
                         c_rows[..., HEAD_DIM], mst[:, 0, 0:N_HEADS]))
        hs, small, big = _sample_layer(hs, sv, big, cw, l)
        s_small.append(small)
    stack = lambda sts, i: jnp.stack([s[i] for s in sts])
    prompt_out = tuple(stack(p_states, i) for i in range(8))
    hn, convn, nn, mn = (stack(s_small, i) for i in range(4))
    sample_out = (hn, jnp.transpose(convn, (0, 2, 1, 3)),
                  jnp.transpose(big['k'], (0, 1, 4, 2, 3)), jnp.transpose(big['v'], (0, 1, 4, 2, 3)),
                  jnp.transpose(big['s'], (0, 4, 1, 2, 3)), jnp.transpose(big['c'], (0, 4, 1, 2, 3)),
                  jnp.transpose(nn.reshape(depth, N_HEADS, HEAD_DIM, nsm), (0, 3, 1, 2)),
                  jnp.transpose(mn, (0, 2, 1)))
    return (hp, hs.reshape(x_sample.shape)) + prompt_out + sample_out


def kernel(x_prompt, x_sample, p_prompt, p_sample, state_rglru_h, state_rglru_conv, cache_swa_k, cache_swa_v, state_hgrn_s, state_mlstm_c, state_mlstm_n, state_mlstm_m, norm1_g, w_in, conv_w, conv_b, lru_wr, lru_br, lru_wi, lru_bi, lru_lam, q_norm_g, k_norm_g, attn_sinks, hgrn_gamma, hgrn_norm_g, mlstm_ib, mlstm_fb, mlstm_norm_g, w_out, norm2_g, w_up, w_down, w_ple_gate, w_ple_proj):
    w = {'norm1_g': norm1_g, 'w_in': w_in, 'conv_w': conv_w, 'conv_b': conv_b, 'lru_wr': lru_wr,
         'lru_br': lru_br, 'lru_wi': lru_wi, 'lru_bi': lru_bi, 'lru_lam': lru_lam, 'q_norm_g': q_norm_g,
         'k_norm_g': k_norm_g, 'attn_sinks': attn_sinks, 'hgrn_gamma': hgrn_gamma,
         'hgrn_norm_g': hgrn_norm_g, 'mlstm_ib': mlstm_ib, 'mlstm_fb': mlstm_fb,
         'mlstm_norm_g': mlstm_norm_g, 'w_out': w_out, 'norm2_g': norm2_g, 'w_up': w_up,
         'w_down': w_down, 'w_ple_gate': w_ple_gate, 'w_ple_proj': w_ple_proj}
    st = (state_rglru_h, state_rglru_conv, cache_swa_k, cache_swa_v, state_hgrn_s, state_mlstm_c,
          state_mlstm_n, state_mlstm_m)
    past_len = 8192
    return _run(x_prompt, x_sample, p_prompt, p_sample, st, w, past_len)
```

```python
import functools
import types

import jax
import jax.numpy as jnp
from jax import lax
from jax.experimental import pallas as pl
from jax.experimental.pallas import tpu as pltpu

F32 = jnp.float32
BF16 = jnp.bfloat16

D_MODEL = 1024
GROUP_W = 256
HEAD_DIM = 64
N_HEADS = 4
EPS = 1e-6
NEG_BIG = -1e30
LRU_C = 8.0
CONV_W = 4
ROT_DIM = 16
ROPE_THETA = 500000.0
WINDOW = 128
D_FF = 4096
PLE_DIM = 256
LANES = 128
SUBLANES = 8
BLK = 128

C_AX, C_AG, C_BQ, C_BK, C_BV = 0, 256, 512, 768, 896
C_CQ, C_CF, C_CI, C_CG = 1024, 1280, 1536, 1792
C_DQ, C_DK, C_DV, C_DO = 2048, 2304, 2560, 2816
C_GATES = 3072
D_IN = 3080
N_IN = 3200

V_G1, V_G2, V_CB, V_BR, V_BI, V_LAM = 0, 1024, 2048, 2304, 2560, 2816
V_QG, V_KG, V_SINK, V_HG, V_MG, V_IB, V_FB = 3072, 3328, 3456, 3584, 3840, 4096, 4224
N_VEC = 4352

R_HG, R_MG, R_GB = 0, 256, 512
N_COL = 520

FM_HF, FM_HK, FM_HQ, FM_HV, FM_CG = 0, 256, 512, 768, 1024
FM_MK, FM_MQ, FM_MV, FM_DO = 1280, 1536, 1792, 2048
FM_DEC, FM_W, FM_DEN, FM_ENEG = 2304, 2312, 2320, 2328
FM_ROWS = 2336

VMEM_LIMIT = 56 * 1024 * 1024


def _dot(a, b):
    return jnp.dot(a, b, preferred_element_type=F32)


def _dot_nt(a, b):
    return lax.dot_general(a, b, (((1,), (1,)), ((), ())), preferred_element_type=F32)


def _sigmoid(x):
    return jax.nn.sigmoid(x)


def _gelu_tanh(x):
    return 0.5 * x * (1.0 + jnp.tanh(0.7978845608028654 * (x + 0.044715 * (x * x * x))))


def _log_sigmoid(x):
    return jnp.minimum(x, 0.0) - jnp.log1p(jnp.exp(-jnp.abs(x)))


def _softplus(x):
    return jnp.maximum(x, 0.0) + jnp.log1p(jnp.exp(-jnp.abs(x)))


def _rms_rows(x, g):
    return x * lax.rsqrt(jnp.mean(x * x, axis=-1, keepdims=True) + EPS) * g


def _seg_mean_sq(x, ones_bd):
    sq = x * x
    hi = sq.astype(BF16)
    lo = (sq - hi.astype(F32)).astype(BF16)
    return (_dot(hi, ones_bd) + _dot(lo, ones_bd)) * (1.0 / HEAD_DIM)


def _seg_rms(x, g, ones_bd):
    return x * lax.rsqrt(_seg_mean_sq(x, ones_bd) + EPS) * g


def _rope(x, c, sa, sb):
    w = x.shape[1]
    up = pltpu.roll(x, w - ROT_DIM // 2, 1)
    dn = pltpu.roll(x, ROT_DIM // 2, 1)
    return x * c + up * sa + dn * sb


def _lb_from_gamma(gammas, layer):
    mx = functools.reduce(jnp.maximum, gammas)
    e = [jnp.exp(g - mx) for g in gammas]
    tot = functools.reduce(lambda a, b: a + b, e)
    lb = jnp.zeros_like(tot)
    for i in range(1, layer + 1):
        lb = lb + e[i] / tot
    return lb


def _iota(shape, axis):
    return lax.broadcasted_iota(jnp.int32, shape, axis)


def _rglru_gates(xc, wr, wi, br, bi, lam):
    xcb = xc.astype(BF16)
    r = _sigmoid(_dot(xcb, wr) + br)
    ig = _sigmoid(_dot(xcb, wi) + bi)
    log_a = (-LRU_C) * r * _softplus(-lam)
    return jnp.exp(log_a), jnp.sqrt(1.0 - jnp.exp(2.0 * log_a)) * (ig * xc)


def _swa_block(q, k, v, k_prev, v_prev, sink, first):
    kk = jnp.concatenate([k_prev, k], axis=0)
    vv = jnp.concatenate([v_prev, v], axis=0)
    qi = _iota((BLK, 2 * BLK), 0)
    kj = _iota((BLK, 2 * BLK), 1)
    valid = (kj > qi) & (kj <= qi + WINDOW) & ((kj >= BLK) | jnp.logical_not(first))
    outs = []
    for h in range(N_HEADS):
        kv = h // 2
        qh = q[:, h * HEAD_DIM:(h + 1) * HEAD_DIM].astype(BF16)
        kh = kk[:, kv * HEAD_DIM:(kv + 1) * HEAD_DIM].astype(BF16)
        vh = vv[:, kv * HEAD_DIM:(kv + 1) * HEAD_DIM].astype(BF16)
        s = _dot_nt(qh, kh) * (HEAD_DIM ** -0.5)
        s = jnp.where(valid, s, NEG_BIG)
        sk = sink[:, h:h + 1]
        mx = jnp.maximum(jnp.max(s, axis=-1, keepdims=True), sk)
        p = jnp.exp(s - mx)
        den = jnp.sum(p, axis=-1, keepdims=True) + jnp.exp(sk - mx)
        outs.append(_dot(p.astype(BF16), vh) / den)
    return jnp.concatenate(outs, axis=1)


def _head_masks(rows, dtype):
    lane = _iota((rows, LANES), 1)
    return (jnp.where(lane < HEAD_DIM, 1.0, 0.0).astype(dtype),
            jnp.where(lane >= HEAD_DIM, 1.0, 0.0).astype(dtype))


def _pair_scores(qe, ke_b, hm):
    n = qe.shape[0]
    if n != hm[0].shape[0]:
        hm = _head_masks(n, BF16)
    res = []
    for p in range(2):
        sl = slice(p * LANES, (p + 1) * LANES)
        qb = qe[:, sl].astype(BF16)
        lhs = jnp.concatenate([qb * hm[0], qb * hm[1]], axis=0)
        pr = _dot_nt(lhs, ke_b[:, sl])
        res += [pr[:n], pr[n:]]
    return res


def _hgrn_block(cq, cf, ci, cg, lb, hg, sst_ref, ones_bd, msmall_ref, mbig_ref, hm, sub):
    q = cq * _sigmoid(cq)
    f = lb + (1.0 - lb) * _sigmoid(cf)
    logf = jnp.log(f)
    k = 1.0 - f

    att = [s_ * msmall_ref[0] for s_ in _pair_scores(q, k.astype(BF16), hm)]

    c = logf
    tot = logf
    for lev in range(1, 4):
        half = 1 << (lev - 1)
        right = (sub & half) != 0
        tot_l = pltpu.roll(tot, half, 0)
        tot_r = pltpu.roll(tot, BLK - half, 0)
        e = jnp.exp(jnp.where(right, c, tot - c))
        sc = _pair_scores(q * e, (k * e).astype(BF16), hm)
        m = msmall_ref[lev]
        att = [a_ + s_ * m for a_, s_ in zip(att, sc)]
        c = c + jnp.where(right, tot_l, 0.0)
        tot = tot + jnp.where(right, tot_l, tot_r)

    pieces, carry = [], None
    for g in range(BLK // SUBLANES):
        rows = slice(g * SUBLANES, (g + 1) * SUBLANES)
        pieces.append(c[rows] if carry is None else c[rows] + carry)
        t_g = tot[g * SUBLANES:g * SUBLANES + 1]
        carry = t_g if carry is None else carry + t_g
    b = jnp.concatenate(pieces, axis=0)
    btot = carry

    for lev in range(4, 8):
        half = 1 << (lev - 1)
        nblk = BLK // (2 * half)
        qr, kf = [], []
        for i in range(nblk):
            lo = i * 2 * half
            mid = lo + half
            bref = b[mid - 1:mid]
            qr.append(q[mid:mid + half] * jnp.exp(b[mid:mid + half] - bref))
            kf.append(k[lo:mid] * jnp.exp(bref - b[lo:mid]))
            kf.append(jnp.zeros((half, GROUP_W), F32))
        sc = _pair_scores(jnp.concatenate(qr, axis=0), jnp.concatenate(kf, axis=0).astype(BF16), hm)
        m = mbig_ref[lev - 4]
        zero = jnp.zeros((half, BLK), F32)
        new = []
        for a_, s_ in zip(att, sc):
            u = s_ * m
            parts = []
            for i in range(nblk):
                parts += [zero, u[i * half:(i + 1) * half]]
            new.append(a_ + jnp.concatenate(parts, axis=0))
        att = new

    qe = (q * jnp.exp(b)).astype(BF16)
    ke = (k * jnp.exp(btot - b)).astype(BF16)
    etot = jnp.exp(btot)
    row_l = _iota((BLK, BLK), 0)
    col_l = _iota((BLK, BLK), 1)
    same_head = (row_l >= HEAD_DIM) == (col_l >= HEAD_DIM)
    outs = []
    for p in range(2):
        sl = slice(p * LANES, (p + 1) * LANES)
        st = sst_ref[p]
        vp = ci[:, sl]
        vb = vp.astype(BF16)
        a2 = jnp.concatenate([att[2 * p], att[2 * p + 1]], axis=1).astype(BF16)
        v2 = jnp.concatenate([vb * hm[0], vb * hm[1]], axis=0)
        o = _dot_nt(qe[:, sl], st.astype(BF16)) + _dot(a2, v2)
        upd = _dot(vp.T.astype(BF16), ke[:, sl])
        sst_ref[p] = st * etot[:, sl] + jnp.where(same_head, upd, 0.0)
        outs.append(o)
    o = jnp.concatenate(outs, axis=1)
    return _seg_rms(o, hg, ones_bd) * (cg * _sigmoid(cg))


def _mlstm_tile_gates(gcols, gb, m0, triu):
    tm = gcols.shape[0]
    gt = gcols.T[0:SUBLANES, :] + gb
    lf = _log_sigmoid(gt)
    hi = lf.astype(BF16)
    r1 = lf - hi.astype(F32)
    mid = r1.astype(BF16)
    lo = (r1 - mid.astype(F32)).astype(BF16)
    parts = jnp.concatenate([hi.astype(F32), mid.astype(F32), lo.astype(F32)], axis=0)
    cs = _dot(parts, triu)
    fcum = cs[0:SUBLANES] + cs[SUBLANES:2 * SUBLANES] + cs[2 * SUBLANES:3 * SUBLANES]
    fcum = pltpu.roll(fcum, N_HEADS, 0)
    g = gt - fcum
    pad = jnp.zeros((LANES - 2 * SUBLANES, tm), F32)
    cols = jnp.concatenate([g, fcum, pad], axis=0).T
    f_c = pltpu.roll(cols, LANES - SUBLANES, 1)
    sub = _iota((tm, LANES), 0) & (SUBLANES - 1)
    cm = cols
    s = 1
    while s < SUBLANES:
        cm = jnp.maximum(cm, jnp.where(sub >= s, pltpu.roll(cm, s, 0), NEG_BIG))
        s *= 2
    carry = m0
    ms = []
    for grp in range(tm // SUBLANES):
        m_g = jnp.maximum(cm[grp * SUBLANES:(grp + 1) * SUBLANES], carry)
        ms.append(m_g)
        carry = m_g[SUBLANES - 1:SUBLANES]
    m_c = jnp.concatenate(ms, axis=0)
    eneg_c = jnp.exp(-(f_c + m_c))
    m_new = f_c[tm - 1:tm] + carry
    return g, cols, m_c, eneg_c, m_new


def _mlstm_block(dq, dk, dv, do, g_rows, g_c, m_c, eneg_c, m_prev, mg, cst_ref, ones_bd, hm, hmf):
    k = dk * (HEAD_DIM ** -0.5)
    m_end = m_c[BLK - 1:BLK]
    inter = jnp.exp(m_prev - m_c)
    wend = jnp.exp(g_c - m_end)
    dec0 = jnp.exp(m_prev - m_end)
    mrun_c = lambda h: m_c[:, h:h + 1]
    inter_c = lambda h: inter[:, h:h + 1]
    eneg_c_ = lambda h: eneg_c[:, h:h + 1]
    wend_c = lambda h: wend[:, h:h + 1]
    g = g_rows
    lane = _iota((BLK, LANES), 1)
    low = lane < HEAD_DIM
    one_col = jnp.where(lane == HEAD_DIM, 1.0, 0.0)
    causal = _iota((BLK, BLK), 1) <= _iota((BLK, BLK), 0)

    outs = []
    for p in range(2):
        sl = slice(p * LANES, (p + 1) * LANES)
        kp, vp = k[:, sl], dv[:, sl]
        qb = dq[:, sl].astype(BF16)
        qm = [qb * hm[0], qb * hm[1]]
        sc2 = _dot_nt(jnp.concatenate(qm, axis=0), kp.astype(BF16))
        v_sw = pltpu.roll(vp, HEAD_DIM, 1)
        hv = []
        for hh in range(2):
            h = 2 * p + hh
            sc = sc2[hh * BLK:(hh + 1) * BLK]
            w = jnp.exp(jnp.where(causal, g[h:h + 1, :] - mrun_c(h), NEG_BIG))
            sw = (sc * w).astype(BF16)
            vaug = jnp.where(low, vp if hh == 0 else v_sw, one_col).astype(BF16)
            cst = cst_ref[h]
            nd = inter_c(h) * _dot(qm[hh], cst.astype(BF16)) + _dot(sw, vaug)
            den = nd[:, HEAD_DIM:HEAD_DIM + 1]
            hv.append(nd / jnp.maximum(jnp.abs(den), eneg_c_(h)))
            kw = kp * (wend_c(h) * hmf[hh])
            cst_ref[h] = dec0[:, h:h + 1] * cst + _dot(kw.T.astype(BF16), vaug)
        outs.append(jnp.where(low, hv[0], pltpu.roll(hv[1], HEAD_DIM, 1)))
    hcat = jnp.concatenate(outs, axis=1)
    return _seg_rms(hcat, mg, ones_bd) * _sigmoid(do)


def _mixer_kernel(h_ref, pv_ref, win_ref, convw_ref, wr_ref, wi_ref, rr_ref, rbase_ref, gamma_ref, ones_ref,
                  msmall_ref, mbig_ref, gb_ref, triu_ref, y_ref, hl_ref, conv_ref, kst_ref, vst_ref, sst_ref,
                  cst_ref, mst_ref, xbuf, *, layer, tm):
    t = pl.program_id(0)
    states = (hl_ref, conv_ref, kst_ref, vst_ref, sst_ref, cst_ref, mst_ref)

    @pl.when(t == 0)
    def _init():
        for ref in states:
            ref[...] = jnp.zeros_like(ref)

    bsz = h_ref.shape[0]
    g1 = pv_ref[layer:layer + 1, V_G1:V_G1 + D_MODEL]
    nb = jnp.concatenate([_rms_rows(h_ref[b], g1).astype(BF16) for b in range(bsz)], axis=0)
    done = {}

    def proj_all(c0, w):
        if (c0, w) not in done:
            done[(c0, w)] = _dot(nb, win_ref[:, c0:c0 + w])
        return done[(c0, w)]

    def vec(off, w):
        return pv_ref[layer:layer + 1, off:off + w]

    ctxs = []
    for b in range(bsz):
        proj = functools.partial(lambda c0, w, b: proj_all(c0, w)[b * tm:(b + 1) * tm], b=b)
        ctxs.append(types.SimpleNamespace(
            t=t, proj=proj, vec=vec, layer=layer, tm=tm, ones_bd=ones_ref[...], convw_ref=convw_ref,
            wr_ref=wr_ref, wi_ref=wi_ref, rr_ref=rr_ref, rbase_ref=rbase_ref, gamma_ref=gamma_ref,
            msmall_ref=msmall_ref, mbig_ref=mbig_ref, gb_ref=gb_ref, triu_ref=triu_ref, y_ref=y_ref.at[b],
            hl_ref=hl_ref.at[b], conv_ref=conv_ref.at[b], kst_ref=kst_ref.at[b], vst_ref=vst_ref.at[b],
            sst_ref=sst_ref.at[b], cst_ref=cst_ref.at[b], mst_ref=mst_ref.at[b], xbuf=xbuf.at[b],
            hm=_head_masks(BLK, BF16), hmf=_head_masks(1, F32)))
    cols = (((C_AX, GROUP_W), (C_AG, GROUP_W)),
            ((C_BQ, GROUP_W), (C_BK, LANES), (C_BV, LANES)),
            ((C_CQ, GROUP_W), (C_CF, GROUP_W), (C_CI, GROUP_W), (C_CG, GROUP_W)),
            ((C_DQ, GROUP_W), (C_DK, GROUP_W), (C_DV, GROUP_W), (C_DO, GROUP_W), (C_GATES, LANES)))
    groups = (_group_a, _group_b, _group_c, _group_d)
    for cw_ in cols[0]:
        proj_all(*cw_)
    def prefetch(chunks):
        for cw_ in chunks:
            proj_all(*cw_)
            yield

    for gi, group in enumerate(groups):
        nxt = cols[gi + 1] if gi + 1 < len(groups) else ()
        per = -(-len(nxt) // len(ctxs))
        for ci, c in enumerate(ctxs):
            _round_robin([prefetch(nxt[ci * per:(ci + 1) * per])])
            _round_robin([group(c)])


def _round_robin(tasks):
    tasks = list(tasks)
    while tasks:
        for task in list(tasks):
            try:
                next(task)
            except StopIteration:
                tasks.remove(task)


def _group_a(c):
    proj, vec, tm, layer = c.proj, c.vec, c.tm, c.layer
    xbuf, conv_ref, convw_ref, wr_ref, wi_ref, hl_ref, y_ref = (c.xbuf, c.conv_ref, c.convw_ref, c.wr_ref,
                                                                c.wi_ref, c.hl_ref, c.y_ref)
    xa = proj(C_AX, GROUP_W)
    ga = proj(C_AG, GROUP_W)
    xbuf[0:8, :] = conv_ref[...]
    xbuf[8:8 + tm, :] = xa
    cw = convw_ref[layer]
    xc = (vec(V_CB, GROUP_W) + xbuf[5:5 + tm, :] * cw[0:1, :] + xbuf[6:6 + tm, :] * cw[1:2, :]
          + xbuf[7:7 + tm, :] * cw[2:3, :] + xa * cw[3:4, :])
    conv_ref[...] = xbuf[tm:tm + 8, :]
    yield
    a, bx = _rglru_gates(xc, wr_ref[...], wi_ref[...], vec(V_BR, GROUP_W), vec(V_BI, GROUP_W),
                         vec(V_LAM, GROUP_W))
    sub_t = _iota((tm, GROUP_W), 0) & (SUBLANES - 1)
    s = 1
    while s < SUBLANES:
        keep = sub_t >= s
        a_s = pltpu.roll(a, s, 0)
        b_s = pltpu.roll(bx, s, 0)
        bx = jnp.where(keep, a * b_s + bx, bx)
        a = jnp.where(keep, a * a_s, a)
        s *= 2
    carry = hl_ref[...]
    hs = []
    for g in range(tm // SUBLANES):
        rows = slice(g * SUBLANES, (g + 1) * SUBLANES)
        hg_ = a[rows] * carry + bx[rows]
        hs.append(hg_)
        carry = hg_[SUBLANES - 1:SUBLANES]
    hseq = jnp.concatenate(hs, axis=0)
    hl_ref[...] = carry
    yield
    y_ref[:, 0:GROUP_W] = (hseq * _gelu_tanh(ga)).astype(y_ref.dtype)


def _group_b(c):
    proj, vec, tm, t, ones_bd = c.proj, c.vec, c.tm, c.t, c.ones_bd
    rr_ref, rbase_ref, kst_ref, vst_ref, y_ref = c.rr_ref, c.rbase_ref, c.kst_ref, c.vst_ref, c.y_ref
    cb = rbase_ref[:, 0:LANES]
    sb_ = rbase_ref[:, LANES:2 * LANES]
    rc = cb * rr_ref[0] - sb_ * rr_ref[1]
    ra = sb_ * rr_ref[2] + cb * rr_ref[3]
    rb = sb_ * rr_ref[4] + cb * rr_ref[5]
    q = _seg_rms(proj(C_BQ, GROUP_W), vec(V_QG, GROUP_W), ones_bd)
    q = _rope(q, jnp.concatenate([rc, rc], 1), jnp.concatenate([ra, ra], 1), jnp.concatenate([rb, rb], 1))
    k = _seg_rms(proj(C_BK, LANES), vec(V_KG, LANES), ones_bd[0:LANES, 0:LANES])
    k = _rope(k, rc, ra, rb)
    v = proj(C_BV, LANES)
    sink = vec(V_SINK, LANES)
    k_prev, v_prev = kst_ref[...], vst_ref[...]
    for j in range(tm // BLK):
        rs = slice(j * BLK, (j + 1) * BLK)
        first = (t == 0) if j == 0 else False
        yb = _swa_block(q[rs], k[rs], v[rs], k_prev, v_prev, sink, first)
        y_ref[rs, GROUP_W:2 * GROUP_W] = yb.astype(y_ref.dtype)
        k_prev, v_prev = k[rs], v[rs]
        yield
    kst_ref[...] = k_prev
    vst_ref[...] = v_prev


def _group_c(c):
    proj, vec, tm, layer, ones_bd, hm = c.proj, c.vec, c.tm, c.layer, c.ones_bd, c.hm
    gamma_ref, sst_ref, msmall_ref, mbig_ref, y_ref = c.gamma_ref, c.sst_ref, c.msmall_ref, c.mbig_ref, c.y_ref
    lb = _lb_from_gamma([gamma_ref[i:i + 1, :] for i in range(gamma_ref.shape[0])], layer)
    cq, cf, ci, cg = (proj(C_CQ, GROUP_W), proj(C_CF, GROUP_W), proj(C_CI, GROUP_W), proj(C_CG, GROUP_W))
    hg = vec(V_HG, GROUP_W)
    sub = _iota((BLK, GROUP_W), 0) & (SUBLANES - 1)
    for j in range(tm // BLK):
        rs = slice(j * BLK, (j + 1) * BLK)
        yc = _hgrn_block(cq[rs], cf[rs], ci[rs], cg[rs], lb, hg, sst_ref, ones_bd, msmall_ref, mbig_ref,
                         hm, sub)
        y_ref[rs, 2 * GROUP_W:3 * GROUP_W] = yc.astype(y_ref.dtype)
        yield


def _group_d(c):
    proj, vec, tm, layer, ones_bd, hm, hmf = c.proj, c.vec, c.tm, c.layer, c.ones_bd, c.hm, c.hmf
    gb_ref, triu_ref, mst_ref, cst_ref, y_ref = c.gb_ref, c.triu_ref, c.mst_ref, c.cst_ref, c.y_ref
    dq, dk, dv, do = (proj(C_DQ, GROUP_W), proj(C_DK, GROUP_W), proj(C_DV, GROUP_W), proj(C_DO, GROUP_W))
    mg = vec(V_MG, GROUP_W)
    m_prev = mst_ref[...]
    g_rows, g_c, m_c, eneg_c, m_new = _mlstm_tile_gates(proj(C_GATES, LANES), gb_ref[layer], m_prev,
                                                       triu_ref[...])
    mst_ref[...] = m_new
    yield
    for j in range(tm // BLK):
        rs = slice(j * BLK, (j + 1) * BLK)
        yd = _mlstm_block(dq[rs], dk[rs], dv[rs], do[rs], g_rows[:, rs], g_c[rs], m_c[rs], eneg_c[rs],
                          m_prev, mg, cst_ref, ones_bd, hm, hmf)
        y_ref[rs, 3 * GROUP_W:4 * GROUP_W] = yd.astype(y_ref.dtype)
        m_prev = m_c[(j + 1) * BLK - 1:(j + 1) * BLK]
        yield


def _full(shape):
    nd = len(shape)
    return pl.BlockSpec(shape, lambda *_: (0,) * nd)


def _layer_block(shape, layer, single_buffer=False):
    nd = len(shape) - 1
    kw = {'pipeline_mode': pl.Buffered(1)} if single_buffer else {}
    return pl.BlockSpec((None,) + tuple(shape[1:]), lambda *_: (layer,) + (0,) * nd, **kw)


def _prompt_mixers(h, cw, layer, tm):
    bsz, t, _ = h.shape
    nt = t // tm
    kern = functools.partial(_mixer_kernel, layer=layer, tm=tm)
    in_specs = [pl.BlockSpec((bsz, tm, D_MODEL), lambda i: (0, i, 0)),
                _full(cw['vecs'].shape), _layer_block(cw['w_in_p'].shape, layer),
                _full(cw['conv_w'].shape), _layer_block(cw['wr_bd'].shape, layer),
                _layer_block(cw['wi_bd'].shape, layer), _full(cw['rope_r'].shape),
                pl.BlockSpec((None, 1, 2 * LANES), lambda i: (i, 0, 0)),
                _full(cw['gamma'].shape), _full(cw['ones_bd'].shape), _full(cw['lvl_small'].shape),
                _full(cw['lvl_big'].shape), _full(cw['gate_bias'].shape), _full(cw['triu'].shape)]
    st_shapes = [(bsz, 1, GROUP_W), (bsz, 8, GROUP_W), (bsz, BLK, LANES), (bsz, BLK, LANES),
                 (bsz, 2, LANES, LANES), (bsz, N_HEADS, LANES, LANES), (bsz, 1, LANES)]
    out_shape = ([jax.ShapeDtypeStruct((bsz, t, D_MODEL), BF16)]
                 + [jax.ShapeDtypeStruct(s, F32) for s in st_shapes])
    out_specs = [pl.BlockSpec((bsz, tm, D_MODEL), lambda i: (0, i, 0))] + [_full(s) for s in st_shapes]
    return pl.pallas_call(
        kern, grid=(nt,), in_specs=in_specs, out_specs=out_specs, out_shape=out_shape,
        scratch_shapes=[pltpu.VMEM((bsz, tm + 8, GROUP_W), F32)],
        compiler_params=pltpu.CompilerParams(dimension_semantics=("arbitrary",),
                                             vmem_limit_bytes=VMEM_LIMIT),
        name=f"prompt_mixers_l{layer}",
    )(h, cw['vecs'], cw['w_in_p'], cw['conv_w'], cw['wr_bd'], cw['wi_bd'], cw['rope_r'], cw['rope_base'],
      cw['gamma'], cw['ones_bd'], cw['lvl_small'], cw['lvl_big'], cw['gate_bias'], cw['triu'])


def _ffn_math(h, yb, p, g2, wout_ref, wup_ref, wdn_ref, wg_ref, wp_ref):
    h = h + _dot(yb, wout_ref[...])
    nb = _rms_rows(h, g2).astype(BF16)
    acc = h
    step = 1024
    for c in range(0, D_FF, step):
        f = jnp.maximum(_dot(nb, wup_ref[:, c:c + step]), 0.0)
        acc = acc + _dot((f * f).astype(BF16), wdn_ref[c:c + step, :])
    gate = _sigmoid(_dot(acc.astype(BF16), wg_ref[...]))
    return acc + gate * _dot(p.astype(BF16), wp_ref[...])


def _ffn_kernel(h_ref, y_ref, p_ref, pv_ref, wout_ref, wup_ref, wdn_ref, wg_ref, wp_ref, o_ref, *, layer):
    g2 = pv_ref[layer:layer + 1, V_G2:V_G2 + D_MODEL]
    o_ref[...] = _ffn_math(h_ref[...], y_ref[...], p_ref[...], g2, wout_ref, wup_ref, wdn_ref, wg_ref, wp_ref)


def _ffn_weight_specs(cw, layer):
    names = ['w_out', 'w_up', 'w_down', 'w_gate', 'w_proj']
    return [cw[n] for n in names], [_layer_block(cw[n].shape, layer, single_buffer=True) for n in names]


def _prompt_ffn(h2, y2, p3, cw, layer, tm):
    n = h2.shape[0]
    row = lambda w: pl.BlockSpec((tm, w), lambda i: (i, 0))
    ws, wspecs = _ffn_weight_specs(cw, layer)
    return pl.pallas_call(
        functools.partial(_ffn_kernel, layer=layer), grid=(n // tm,),
        in_specs=[row(D_MODEL), row(D_MODEL), pl.BlockSpec((None, tm, PLE_DIM), lambda i: (layer, i, 0)),
                  _full(cw['vecs'].shape)] + wspecs,
        out_specs=row(D_MODEL), out_shape=jax.ShapeDtypeStruct((n, D_MODEL), F32),
        compiler_params=pltpu.CompilerParams(dimension_semantics=("arbitrary",),
                                             vmem_limit_bytes=VMEM_LIMIT),
        name=f"prompt_ffn_l{layer}",
    )(h2, y2, p3, cw['vecs'], *ws)


def _sample_pre_kernel(x_ref, pv_ref, wt_ref, convw_ref, wr_ref, wi_ref, ones_ref, rope_ref, gcol_ref,
                       cols_ref, h0_ref, conv0_ref, n0_ref, m0_ref,
                       ya_ref, hn_ref, convn_ref, q_ref, kt_ref, vt_ref, fm_ref, nn_ref, mn_ref, *, layer):
    def vec(off, w):
        return pv_ref[layer:layer + 1, off:off + w]

    n = _rms_rows(x_ref[...], vec(V_G1, D_MODEL))
    nb = n.astype(BF16)
    n_t = n.T.astype(BF16)
    ones_bd = ones_ref[...]

    def proj(r0, cnt):
        return _dot_nt(nb, wt_ref[r0:r0 + cnt, :].astype(BF16))

    def proj_t(r0, cnt):
        return _dot(wt_ref[r0:r0 + cnt, :].astype(BF16), n_t)

    xa = proj(C_AX, GROUP_W)
    ga = proj(C_AG, GROUP_W)
    cw = convw_ref[layer]
    xc = (vec(V_CB, GROUP_W) + conv0_ref[0] * cw[0:1, :] + conv0_ref[1] * cw[1:2, :]
          + conv0_ref[2] * cw[2:3, :] + xa * cw[3:4, :])
    convn_ref[0] = conv0_ref[1]
    convn_ref[1] = conv0_ref[2]
    convn_ref[2] = xa
    a, bx = _rglru_gates(xc, wr_ref[...], wi_ref[...], vec(V_BR, GROUP_W), vec(V_BI, GROUP_W),
                         vec(V_LAM, GROUP_W))
    hn = a * h0_ref[...] + bx
    hn_ref[...] = hn
    ya_ref[...] = hn * _gelu_tanh(ga)

    rc, ra, rb = rope_ref[0:1, :], rope_ref[1:2, :], rope_ref[2:3, :]
    q = _seg_rms(proj(C_BQ, GROUP_W), vec(V_QG, GROUP_W), ones_bd)
    q_ref[...] = _rope(q, jnp.concatenate([rc, rc], 1), jnp.concatenate([ra, ra], 1),
                       jnp.concatenate([rb, rb], 1))
    k = _seg_rms(proj(C_BK, LANES), vec(V_KG, LANES), ones_bd[0:LANES, 0:LANES])
    kt_ref[...] = _rope(k, rc, ra, rb).T
    vt_ref[...] = proj(C_BV, LANES).T

    lb = _lb_from_gamma([gcol_ref[i] for i in range(gcol_ref.shape[0])], layer)
    cq, cf, ci, cg = (proj_t(C_CQ, GROUP_W), proj_t(C_CF, GROUP_W), proj_t(C_CI, GROUP_W),
                      proj_t(C_CG, GROUP_W))
    f = lb + (1.0 - lb) * _sigmoid(cf)
    fm_ref[FM_HF:FM_HF + GROUP_W, :] = f
    fm_ref[FM_HK:FM_HK + GROUP_W, :] = 1.0 - f
    fm_ref[FM_HQ:FM_HQ + GROUP_W, :] = cq * _sigmoid(cq)
    fm_ref[FM_HV:FM_HV + GROUP_W, :] = ci
    fm_ref[FM_CG:FM_CG + GROUP_W, :] = cg * _sigmoid(cg)

    dq, dk, dv, do = (proj_t(C_DQ, GROUP_W), proj_t(C_DK, GROUP_W), proj_t(C_DV, GROUP_W),
                      proj_t(C_DO, GROUP_W))
    g8 = proj_t(C_GATES, 2 * N_HEADS) + cols_ref[layer, R_GB:R_GB + 2 * N_HEADS, :]
    ig = g8[0:N_HEADS, :]
    lf = _log_sigmoid(g8)[N_HEADS:2 * N_HEADS, :]
    a_int = lf + m0_ref[...]
    m_new = jnp.maximum(a_int, ig)
    dec = jnp.exp(a_int - m_new)
    w = jnp.exp(ig - m_new)
    mn_ref[...] = m_new
    km = dk * (HEAD_DIM ** -0.5)
    dens = []
    for h in range(N_HEADS):
        sl = slice(h * HEAD_DIM, (h + 1) * HEAD_DIM)
        nn_h = dec[h:h + 1, :] * n0_ref[sl, :] + w[h:h + 1, :] * km[sl, :]
        nn_ref[sl, :] = nn_h
        dens.append(jnp.sum(dq[sl, :] * nn_h, axis=0, keepdims=True))
    pad = jnp.zeros((SUBLANES - N_HEADS, x_ref.shape[0]), F32)
    fm_ref[FM_MK:FM_MK + GROUP_W, :] = km
    fm_ref[FM_MQ:FM_MQ + GROUP_W, :] = dq
    fm_ref[FM_MV:FM_MV + GROUP_W, :] = dv
    fm_ref[FM_DO:FM_DO + GROUP_W, :] = _sigmoid(do)
    fm_ref[FM_DEC:FM_DEC + SUBLANES, :] = jnp.concatenate([dec, pad], 0)
    fm_ref[FM_W:FM_W + SUBLANES, :] = jnp.concatenate([w, pad], 0)
    fm_ref[FM_DEN:FM_DEN + SUBLANES, :] = jnp.concatenate(dens + [pad], 0)
    fm_ref[FM_ENEG:FM_ENEG + SUBLANES, :] = jnp.concatenate([jnp.exp(-m_new), pad], 0)


def _own_slab(ref, first_layer):
    if not first_layer:
        return ref
    ref[1:] = jnp.zeros((ref.shape[0] - 1,) + ref.shape[1:], ref.dtype)
    return ref.at[0]


def _sample_attn_kernel(q_ref, kn_ref, vn_ref, kc_ref, vc_ref, sink_ref, *rest, first_layer):
    ko_ref, vo_ref, o_ref = rest[-3:]
    ko_ref, vo_ref = _own_slab(ko_ref, first_layer), _own_slab(vo_ref, first_layer)
    sb = q_ref.shape[0]
    rows = 2 * HEAD_DIM
    lane = _iota((rows, WINDOW), 1)
    kn = kn_ref[...].reshape(rows, sb)
    vn = vn_ref[...].reshape(rows, sb)
    for s in range(sb):
        kt = pltpu.roll(kc_ref[s].reshape(rows, WINDOW), WINDOW - 1, 1)
        ko_ref[s] = jnp.where(lane == WINDOW - 1, kn[:, s:s + 1], kt).reshape(2, HEAD_DIM, WINDOW)
        vt = pltpu.roll(vc_ref[s].reshape(rows, WINDOW), WINDOW - 1, 1)
        vo_ref[s] = jnp.where(lane == WINDOW - 1, vn[:, s:s + 1], vt).reshape(2, HEAD_DIM, WINDOW)
    for kv in range(2):
        kk = ko_ref[:, kv].astype(BF16)
        vv = vo_ref[:, kv].astype(BF16)
        s_ = jnp.einsum('bqc,bcj->bqj', q_ref[:, kv].astype(BF16), kk,
                        preferred_element_type=F32) * (HEAD_DIM ** -0.5)
        sk = sink_ref[kv]
        mx = jnp.maximum(jnp.max(s_, axis=-1, keepdims=True), sk)
        p = jnp.exp(s_ - mx)
        den = jnp.sum(p, axis=-1, keepdims=True) + jnp.exp(sk - mx)
        o = jnp.einsum('bqj,bcj->bqc', p.astype(BF16), vv, preferred_element_type=F32)
        o_ref[:, kv] = o / den


def _sample_state_kernel(fm_ref, s_ref, c_ref, *rest, first_layer):
    so_ref, co_ref, oh_ref, om_ref = rest[-4:]
    so_ref, co_ref = _own_slab(so_ref, first_layer), _own_slab(co_ref, first_layer)
    h = pl.program_id(0)
    r0 = pl.multiple_of(h * HEAD_DIM, HEAD_DIM)
    hv = fm_ref[pl.ds(FM_HV + r0, HEAD_DIM), :]
    mv = fm_ref[pl.ds(FM_MV + r0, HEAD_DIM), :]
    dec = fm_ref[pl.ds(FM_DEC + h, 1), :]
    w = fm_ref[pl.ds(FM_W + h, 1), :]

    def body(d, carry):
        acc_h, acc_m = carry
        r = r0 + d
        s_new = fm_ref[pl.ds(FM_HF + r, 1), :] * s_ref[d] + fm_ref[pl.ds(FM_HK + r, 1), :] * hv
        so_ref[d] = s_new
        c_new = dec * c_ref[d] + (w * fm_ref[pl.ds(FM_MK + r, 1), :]) * mv
        co_ref[d] = c_new
        return (acc_h + fm_ref[pl.ds(FM_HQ + r, 1), :] * s_new,
                acc_m + fm_ref[pl.ds(FM_MQ + r, 1), :] * c_new)

    zero = jnp.zeros((HEAD_DIM, fm_ref.shape[1]), F32)
    acc_h, acc_m = lax.fori_loop(0, HEAD_DIM, body, (zero, zero), unroll=4)
    oh_ref[...] = acc_h
    om_ref[...] = acc_m


def _sample_post_kernel(h_ref, ya_ref, yb_ref, oh_ref, om_ref, fm_ref, cols_ref, p_ref, pv_ref,
                        wout_ref, wup_ref, wdn_ref, wg_ref, wp_ref, o_ref, *, layer):
    def head_rms(x):
        return x * lax.rsqrt(jnp.mean(x * x, axis=0, keepdims=True) + EPS)

    yc, yd = [], []
    for h in range(N_HEADS):
        sl = slice(h * HEAD_DIM, (h + 1) * HEAD_DIM)
        yc.append(head_rms(oh_ref[h]))
        den = fm_ref[FM_DEN + h:FM_DEN + h + 1, :]
        eneg = fm_ref[FM_ENEG + h:FM_ENEG + h + 1, :]
        yd.append(head_rms(om_ref[h] / jnp.maximum(jnp.abs(den), eneg)))
    yc = (jnp.concatenate(yc, 0) * cols_ref[layer, R_HG:R_HG + GROUP_W, :]
          * fm_ref[FM_CG:FM_CG + GROUP_W, :])
    yd = (jnp.concatenate(yd, 0) * cols_ref[layer, R_MG:R_MG + GROUP_W, :]
          * fm_ref[FM_DO:FM_DO + GROUP_W, :])
    y = jnp.concatenate([ya_ref[...], yb_ref[...], yc.T, yd.T], axis=1).astype(BF16)
    g2 = pv_ref[layer:layer + 1, V_G2:V_G2 + D_MODEL]
    o_ref[...] = _ffn_math(h_ref[...], y, p_ref[...], g2, wout_ref, wup_ref, wdn_ref, wg_ref, wp_ref)


def _call_full(kern, args, specs, out_shape, name):
    specs = [(_full(a.shape) if s is None else s) for a, s in zip(args, specs)]
    return pl.pallas_call(
        kern, grid=(1,), in_specs=specs,
        out_specs=tuple(_full(s.shape) for s in out_shape), out_shape=tuple(out_shape),
        compiler_params=pltpu.CompilerParams(dimension_semantics=("arbitrary",),
                                             vmem_limit_bytes=VMEM_LIMIT),
        name=name,
    )(*args)


def _sample_layer(h, sv, prev, cw, layer):
    nsm = h.shape[0]
    depth = cw['vecs'].shape[0]
    sd = lambda *shape: jax.ShapeDtypeStruct(shape, F32)
    g, l = sd(nsm, GROUP_W), sd(LANES, nsm)

    args = [h, cw['vecs'], cw['w_in_t'], cw['conv_w'], cw['wr_bd'], cw['wi_bd'], cw['ones_bd'],
            cw['rope_s'], cw['gamma_col'], cw['cols'], sv['h'], sv['conv'], sv['n'], sv['m']]
    specs = [None, None, _layer_block(cw['w_in_t'].shape, layer, single_buffer=True), None,
             _layer_block(cw['wr_bd'].shape, layer), _layer_block(cw['wi_bd'].shape, layer), None,
             None, None, None, _layer_block(sv['h'].shape, layer), _layer_block(sv['conv'].shape, layer),
             _layer_block(sv['n'].shape, layer), _layer_block(sv['m'].shape, layer)]
    outs = [g, g, sd(CONV_W - 1, nsm, GROUP_W), g, l, l, sd(FM_ROWS, nsm), sd(GROUP_W, nsm),
            sd(N_HEADS, nsm)]
    ya, hn, convn, q, kt, vt, fm, nn, mn = _call_full(
        functools.partial(_sample_pre_kernel, layer=layer), args, specs, outs, f"sample_pre_l{layer}")

    sb = 16
    nblk = nsm // sb
    q3 = jnp.pad(q.reshape(nsm, 2, 2, HEAD_DIM), ((0, 0), (0, 0), (0, SUBLANES - 2), (0, 0)))
    to_blocks = lambda a: a.reshape(2, HEAD_DIM, nblk, sb).transpose(2, 0, 1, 3)
    cshape = (depth, nsm, 2, HEAD_DIM, WINDOW)
    cspec = pl.BlockSpec((None, sb, 2, HEAD_DIM, WINDOW), lambda i: (layer, i, 0, 0, 0))
    nspec = pl.BlockSpec((None, 2, HEAD_DIM, sb), lambda i: (i, 0, 0, 0))
    qspec = pl.BlockSpec((sb, 2, SUBLANES, HEAD_DIM), lambda i: (i, 0, 0, 0))
    any_spec = pl.BlockSpec(memory_space=pl.ANY)
    cout = (pl.BlockSpec((depth, sb, 2, HEAD_DIM, WINDOW), lambda i: (0, i, 0, 0, 0)) if prev is None
            else cspec)
    chain = [] if prev is None else [prev['k'], prev['v']]
    ko, vo, o3 = pl.pallas_call(
        functools.partial(_sample_attn_kernel, first_layer=prev is None), grid=(nblk,),
        in_specs=[qspec, nspec, nspec, cspec, cspec, _layer_block(cw['sinks8'].shape, layer)]
        + [any_spec] * len(chain),
        out_specs=(cout, cout, qspec),
        out_shape=(sd(*cshape), sd(*cshape), sd(nsm, 2, SUBLANES, HEAD_DIM)),
        input_output_aliases={6 + i: i for i in range(len(chain))},
        compiler_params=pltpu.CompilerParams(dimension_semantics=("arbitrary",),
                                             vmem_limit_bytes=VMEM_LIMIT),
        name=f"sample_attn_l{layer}",
    )(q3, to_blocks(kt), to_blocks(vt), sv['k'], sv['v'], cw['sinks8'], *chain)
    yb = o3[:, :, 0:2, :].reshape(nsm, GROUP_W)

    sshape = (depth, N_HEADS, HEAD_DIM, HEAD_DIM, nsm)
    sspec = pl.BlockSpec((None, None, HEAD_DIM, HEAD_DIM, nsm), lambda i: (layer, i, 0, 0, 0))
    ospec = pl.BlockSpec((None, HEAD_DIM, nsm), lambda i: (i, 0, 0))
    sout = (pl.BlockSpec((depth, None, HEAD_DIM, HEAD_DIM, nsm), lambda i: (0, i, 0, 0, 0))
            if prev is None else sspec)
    chain = [] if prev is None else [prev['s'], prev['c']]
    so, co, oh, om = pl.pallas_call(
        functools.partial(_sample_state_kernel, first_layer=prev is None), grid=(N_HEADS,),
        in_specs=[_full(fm.shape), sspec, sspec] + [any_spec] * len(chain),
        out_specs=(sout, sout, ospec, ospec),
        out_shape=(sd(*sshape), sd(*sshape), sd(N_HEADS, HEAD_DIM, nsm), sd(N_HEADS, HEAD_DIM, nsm)),
        input_output_aliases={3 + i: i for i in range(len(chain))},
        compiler_params=pltpu.CompilerParams(dimension_semantics=("arbitrary",),
                                             vmem_limit_bytes=VMEM_LIMIT),
        name=f"sample_state_l{layer}",
    )(fm, sv['s'], sv['c'], *chain)

    ws, wspecs = _ffn_weight_specs(cw, layer)
    (h_new,) = _call_full(
        functools.partial(_sample_post_kernel, layer=layer),
        [h, ya, yb, oh, om, fm, cw['cols'], sv['p'], cw['vecs']] + ws,
        [None] * 7 + [_layer_block(sv['p'].shape, layer), None] + wspecs,
        [sd(nsm, D_MODEL)], f"sample_post_l{layer}")
    small = (hn, convn, nn, mn)
    big = {'k': ko, 'v': vo, 's': so, 'c': co}
    return h_new, small, big


def _block_diag_all(w):
    depth = w.shape[0]
    rows = w.reshape(depth, GROUP_W, HEAD_DIM)
    idx = jnp.arange(GROUP_W) // HEAD_DIM
    mask = idx[:, None] == idx[None, :]
    return jnp.where(mask[None], jnp.tile(rows, (1, 1, N_HEADS)), 0.0)


def _rope_lane_freq():
    half = ROT_DIM // 2
    inv = jnp.power(ROPE_THETA, -jnp.arange(half, dtype=F32) * (2.0 / ROT_DIM))
    dd = jnp.arange(LANES) % HEAD_DIM
    freq = jnp.where(dd < ROT_DIM, inv[dd % half], 0.0)
    m_a = (dd < half).astype(F32)
    m_b = ((dd >= half) & (dd < ROT_DIM)).astype(F32)
    return freq, m_a, m_b


def _rope_tables(pos):
    freq, m_a, m_b = _rope_lane_freq()
    ang = pos.astype(F32)[:, None] * freq[None, :]
    cos, sin = jnp.cos(ang), jnp.sin(ang)
    return cos, -sin * m_a, sin * m_b


def _rope_split_tables(t, tm):
    freq, m_a, m_b = _rope_lane_freq()
    ang_r = jnp.arange(tm, dtype=F32)[:, None] * freq[None, :]
    cr, sr = jnp.cos(ang_r), jnp.sin(ang_r)
    rope_r = jnp.stack([cr, sr, -cr * m_a, -sr * m_a, cr * m_b, sr * m_b])
    ang_b = (jnp.arange(t // tm) * tm).astype(F32)[:, None] * freq[None, :]
    rope_base = jnp.concatenate([jnp.cos(ang_b), jnp.sin(ang_b)], 1)[:, None, :]
    return rope_r, rope_base


def _hgrn_level_masks():
    t = jnp.arange(BLK)[:, None]
    s = jnp.arange(BLK)[None, :]
    small = [t == s]
    big = []
    for lev in range(1, 8):
        half = 1 << (lev - 1)
        own = ((t >> lev) == (s >> lev)) & ((t & half) != 0) & ((s & half) == 0)
        if lev < 4:
            small.append(own)
        else:
            rows = jnp.concatenate([jnp.arange(m, m + half) for m in range(half, BLK, 2 * half)])
            big.append(own[rows])
    return jnp.stack(small).astype(F32), jnp.stack(big).astype(F32)


def _pad_last(v, width):
    return jnp.pad(v, ((0, 0), (0, width - v.shape[-1])))


def _common(w, t, tm_mix, past_len, nsm):
    depth = w['w_in'].shape[0]
    tile = lambda v, n: jnp.tile(v, (1, n))
    vecs = jnp.concatenate([
        w['norm1_g'], w['norm2_g'], w['conv_b'], w['lru_br'], w['lru_bi'], w['lru_lam'],
        tile(w['q_norm_g'], N_HEADS), tile(w['k_norm_g'], 2), _pad_last(w['attn_sinks'], LANES),
        tile(w['hgrn_norm_g'], N_HEADS), tile(w['mlstm_norm_g'], N_HEADS),
        _pad_last(w['mlstm_ib'], LANES), _pad_last(w['mlstm_fb'], LANES)], axis=1)
    cols = jnp.concatenate([tile(w['hgrn_norm_g'], N_HEADS), tile(w['mlstm_norm_g'], N_HEADS),
                            w['mlstm_ib'], w['mlstm_fb']], axis=1)
    w_in = w['w_in']
    gate_bias = jnp.concatenate([w['mlstm_ib'], w['mlstm_fb']], axis=1)
    lvl_small, lvl_big = _hgrn_level_masks()
    sinks = w['attn_sinks']
    z2 = jnp.zeros((depth, 2, SUBLANES - 2), F32)
    idx = jnp.arange(GROUP_W)
    rope_r, rope_base = _rope_split_tables(t, tm_mix)
    return {
        'vecs': vecs,
        'cols': jnp.broadcast_to(cols[:, :, None], cols.shape + (nsm,)),
        'gamma': w['hgrn_gamma'],
        'gamma_col': jnp.broadcast_to(w['hgrn_gamma'][:, :, None], w['hgrn_gamma'].shape + (nsm,)),
        'w_in_p': jnp.pad(w_in, ((0, 0), (0, 0), (0, N_IN - D_IN))).astype(BF16),
        'lvl_small': lvl_small, 'lvl_big': lvl_big,
        'gate_bias': jnp.broadcast_to(gate_bias[:, :, None], gate_bias.shape + (tm_mix,)),
        'triu': (jnp.arange(tm_mix)[:, None] <= jnp.arange(tm_mix)[None, :]).astype(F32),
        'w_in_t': jnp.swapaxes(w_in, 1, 2),
        'conv_w': w['conv_w'],
        'wr_bd': _block_diag_all(w['lru_wr']).astype(BF16),
        'wi_bd': _block_diag_all(w['lru_wi']).astype(BF16),
        'ones_bd': (idx[:, None] // HEAD_DIM == idx[None, :] // HEAD_DIM).astype(BF16),
        'sinks8': jnp.concatenate([sinks.reshape(depth, 2, 2), z2], axis=2)[..., None],
        'rope_r': rope_r, 'rope_base': rope_base,
        'rope_s': jnp.concatenate(_rope_tables(past_len + jnp.arange(1)), axis=0),
        'w_out': w['w_out'].astype(BF16), 'w_up': w['w_up'].astype(BF16),
        'w_down': w['w_down'].astype(BF16), 'w_gate': w['w_ple_gate'].astype(BF16),
        'w_proj': w['w_ple_proj'].astype(BF16),
    }


def _run(x_prompt, x_sample, p_prompt, p_sample, sample_state, w, past_len, tm_mix=256, tm_ffn=512):
    depth = w['w_in'].shape[0]
    bsz, t, _ = x_prompt.shape
    nsm = x_sample.shape[0]
    cw = _common(w, t, tm_mix, past_len, nsm)
    tm_ffn = min(tm_ffn, bsz * t)
    h0, conv0, kc, vc, s0, c0, n0, m0 = sample_state
    sv = {'h': h0, 'conv': jnp.transpose(conv0, (0, 2, 1, 3)),
          'k': jnp.transpose(kc, (0, 1, 3, 4, 2)), 'v': jnp.transpose(vc, (0, 1, 3, 4, 2)),
          's': jnp.transpose(s0, (0, 2, 3, 4, 1)), 'c': jnp.transpose(c0, (0, 2, 3, 4, 1)),
          'n': jnp.transpose(n0, (0, 2, 3, 1)).reshape(depth, GROUP_W, nsm),
          'm': jnp.transpose(m0, (0, 2, 1)), 'p': p_sample.reshape(depth, nsm, PLE_DIM)}
    p3 = p_prompt.reshape(depth, bsz * t, PLE_DIM)

    hp = x_prompt
    hs = x_sample.reshape(nsm, D_MODEL)
    p_states, s_small, big = [], [], None
    for l in range(depth):
        y, hl, conv, kst, vst, sst, cst, mst = _prompt_mixers(hp, cw, l, tm_mix)
        hp = _prompt_ffn(hp.reshape(bsz * t, D_MODEL), y.reshape(bsz * t, D_MODEL), p3, cw, l,
                         tm_ffn).reshape(bsz, t, D_MODEL)
        s_hgrn = jnp.swapaxes(sst, -1, -2)
        s_hgrn = jnp.stack([s_hgrn[:, 0, 0:64, 0:64], s_hgrn[:, 0, 64:128, 64:128],
                            s_hgrn[:, 1, 0:64, 0:64], s_hgrn[:, 1, 64:128, 64:128]], 1)
        c_rows = jnp.stack([cst[:, 0, 0:64], cst[:, 1, 64:128], cst[:, 2, 0:64], cst[:, 3, 64:128]], 1)
        p_states.append((hl[:, 0], conv[:, 8 - (CONV_W - 1):], kst.reshape(bsz, WINDOW, 2, HEAD_DIM),
                         vst.reshape(bsz, WINDOW, 2, HEAD_DIM), s_hgrn, c_rows[..., 0:HEAD_DIM],
                         c_rows[..., HEAD_DIM], mst[:, 0, 0:N_HEADS]))
        hs, small, big = _sample_layer(hs, sv, big, cw, l)
        s_small.append(small)
    stack = lambda sts, i: jnp.stack([s[i] for s in sts])
    prompt_out = tuple(stack(p_states, i) for i in range(8))
    hn, convn, nn, mn = (stack(s_small, i) for i in range(4))
    sample_out = (hn, jnp.transpose(convn, (0, 2, 1, 3)),
                  jnp.transpose(big['k'], (0, 1, 4, 2, 3)), jnp.transpose(big['v'], (0, 1, 4, 2, 3)),
                  jnp.transpose(big['s'], (0, 4, 1, 2, 3)), jnp.transpose(big['c'], (0, 4, 1, 2, 3)),
                  jnp.transpose(nn.reshape(depth, N_HEADS, HEAD_DIM, nsm), (0, 3, 1, 2)),
                  jnp.transpose(mn, (0, 2, 1)))
    return (hp, hs.reshape(x_sample.shape)) + prompt_out + sample_out


def kernel(x_prompt, x_sample, p_prompt, p_sample, state_rglru_h, state_rglru_conv, cache_swa_k, cache_swa_v, state_hgrn_s, state_mlstm_c, state_mlstm_n, state_mlstm_m, norm1_g, w_in, conv_w, conv_b, lru_wr, lru_br, lru_wi, lru_bi, lru_lam, q_norm_g, k_norm_g, attn_sinks, hgrn_gamma, hgrn_norm_g, mlstm_ib, mlstm_fb, mlstm_norm_g, w_out, norm2_g, w_up, w_down, w_ple_gate, w_ple_proj):
    w = {'norm1_g': norm1_g, 'w_in': w_in, 'conv_w': conv_w, 'conv_b': conv_b, 'lru_wr': lru_wr,
         'lru_br': lru_br, 'lru_wi': lru_wi, 'lru_bi': lru_bi, 'lru_lam': lru_lam, 'q_norm_g': q_norm_g,
         'k_norm_g': k_norm_g, 'attn_sinks': attn_sinks, 'hgrn_gamma': hgrn_gamma,
         'hgrn_norm_g': hgrn_norm_g, 'mlstm_ib': mlstm_ib, 'mlstm_fb': mlstm_fb,
         'mlstm_norm_g': mlstm_norm_g, 'w_out': w_out, 'norm2_g': norm2_g, 'w_up': w_up,
         'w_down': w_down, 'w_ple_gate': w_ple_gate, 'w_ple_proj': w_ple_proj}
    st = (state_rglru_h, state_rglru_conv, cache_swa_k, cache_swa_v, state_hgrn_s, state_mlstm_c,
          state_mlstm_n, state_mlstm_m)
    past_len = 8192
    return _run(x_prompt, x_sample, p_prompt, p_sample, st, w, past_len)
```

```python
import functools
import types

import jax
import jax.numpy as jnp
from jax import lax
from jax.experimental import pallas as pl
from jax.experimental.pallas import tpu as pltpu

F32 = jnp.float32
BF16 = jnp.bfloat16

D_MODEL = 1024
GROUP_W = 256
HEAD_DIM = 64
N_HEADS = 4
EPS = 1e-6
NEG_BIG = -1e30
LRU_C = 8.0
CONV_W = 4
ROT_DIM = 16
ROPE_THETA = 500000.0
WINDOW = 128
D_FF = 4096
PLE_DIM = 256
LANES = 128
SUBLANES = 8
BLK = 128

C_AX, C_AG, C_BQ, C_BK, C_BV = 0, 256, 512, 768, 896
C_CQ, C_CF, C_CI, C_CG = 1024, 1280, 1536, 1792
C_DQ, C_DK, C_DV, C_DO = 2048, 2304, 2560, 2816
C_GATES = 3072
D_IN = 3080
N_IN = 3200

V_G1, V_G2, V_CB, V_BR, V_BI, V_LAM = 0, 1024, 2048, 2304, 2560, 2816
V_QG, V_KG, V_SINK, V_HG, V_MG, V_IB, V_FB = 3072, 3328, 3456, 3584, 3840, 4096, 4224
N_VEC = 4352

R_HG, R_MG, R_GB = 0, 256, 512
N_COL = 520

FM_HF, FM_HK, FM_HQ, FM_HV, FM_CG = 0, 256, 512, 768, 1024
FM_MK, FM_MQ, FM_MV, FM_DO = 1280, 1536, 1792, 2048
FM_DEC, FM_W, FM_DEN, FM_ENEG = 2304, 2312, 2320, 2328
FM_ROWS = 2336

VMEM_LIMIT = 56 * 1024 * 1024


def _dot(a, b):
    return jnp.dot(a, b, preferred_element_type=F32)


def _dot_nt(a, b):
    return lax.dot_general(a, b, (((1,), (1,)), ((), ())), preferred_element_type=F32)


def _sigmoid(x):
    return jax.nn.sigmoid(x)


def _gelu_tanh(x):
    return 0.5 * x * (1.0 + jnp.tanh(0.7978845608028654 * (x + 0.044715 * (x * x * x))))


def _log_sigmoid(x):
    return jnp.minimum(x, 0.0) - jnp.log1p(jnp.exp(-jnp.abs(x)))


def _softplus(x):
    return jnp.maximum(x, 0.0) + jnp.log1p(jnp.exp(-jnp.abs(x)))


def _rms_rows(x, g):
    return x * lax.rsqrt(jnp.mean(x * x, axis=-1, keepdims=True) + EPS) * g


def _seg_mean_sq(x, ones_bd):
    sq = x * x
    hi = sq.astype(BF16)
    lo = (sq - hi.astype(F32)).astype(BF16)
    return (_dot(hi, ones_bd) + _dot(lo, ones_bd)) * (1.0 / HEAD_DIM)


def _seg_rms(x, g, ones_bd):
    return x * lax.rsqrt(_seg_mean_sq(x, ones_bd) + EPS) * g


def _rope(x, c, sa, sb):
    w = x.shape[1]
    up = pltpu.roll(x, w - ROT_DIM // 2, 1)
    dn = pltpu.roll(x, ROT_DIM // 2, 1)
    return x * c + up * sa + dn * sb


def _lb_from_gamma(gammas, layer):
    mx = functools.reduce(jnp.maximum, gammas)
    e = [jnp.exp(g - mx) for g in gammas]
    tot = functools.reduce(lambda a, b: a + b, e)
    lb = jnp.zeros_like(tot)
    for i in range(1, layer + 1):
        lb = lb + e[i] / tot
    return lb


def _iota(shape, axis):
    return lax.broadcasted_iota(jnp.int32, shape, axis)


def _rglru_gates(xc, wr, wi, br, bi, lam):
    xcb = xc.astype(BF16)
    r = _sigmoid(_dot(xcb, wr) + br)
    ig = _sigmoid(_dot(xcb, wi) + bi)
    log_a = (-LRU_C) * r * _softplus(-lam)
    return jnp.exp(log_a), jnp.sqrt(1.0 - jnp.exp(2.0 * log_a)) * (ig * xc)


def _swa_block(q, k, v, k_prev, v_prev, sink, first):
    kk = jnp.concatenate([k_prev, k], axis=0)
    vv = jnp.concatenate([v_prev, v], axis=0)
    qi = _iota((BLK, 2 * BLK), 0)
    kj = _iota((BLK, 2 * BLK), 1)
    valid = (kj > qi) & (kj <= qi + WINDOW) & ((kj >= BLK) | jnp.logical_not(first))
    outs = []
    for h in range(N_HEADS):
        kv = h // 2
        qh = q[:, h * HEAD_DIM:(h + 1) * HEAD_DIM].astype(BF16)
        kh = kk[:, kv * HEAD_DIM:(kv + 1) * HEAD_DIM].astype(BF16)
        vh = vv[:, kv * HEAD_DIM:(kv + 1) * HEAD_DIM].astype(BF16)
        s = _dot_nt(qh, kh) * (HEAD_DIM ** -0.5)
        s = jnp.where(valid, s, NEG_BIG)
        sk = sink[:, h:h + 1]
        mx = jnp.maximum(jnp.max(s, axis=-1, keepdims=True), sk)
        p = jnp.exp(s - mx)
        den = jnp.sum(p, axis=-1, keepdims=True) + jnp.exp(sk - mx)
        outs.append(_dot(p.astype(BF16), vh) / den)
    return jnp.concatenate(outs, axis=1)


def _head_masks(rows, dtype):
    lane = _iota((rows, LANES), 1)
    return (jnp.where(lane < HEAD_DIM, 1.0, 0.0).astype(dtype),
            jnp.where(lane >= HEAD_DIM, 1.0, 0.0).astype(dtype))


def _pair_scores(qe, ke_b, hm):
    n = qe.shape[0]
    if n != hm[0].shape[0]:
        hm = _head_masks(n, BF16)
    res = []
    for p in range(2):
        sl = slice(p * LANES, (p + 1) * LANES)
        qb = qe[:, sl].astype(BF16)
        lhs = jnp.concatenate([qb * hm[0], qb * hm[1]], axis=0)
        pr = _dot_nt(lhs, ke_b[:, sl])
        res += [pr[:n], pr[n:]]
    return res


def _hgrn_block(cq, cf, ci, cg, lb, hg, sst_ref, ones_bd, msmall_ref, mbig_ref, hm, sub):
    q = cq * _sigmoid(cq)
    f = lb + (1.0 - lb) * _sigmoid(cf)
    logf = jnp.log(f)
    k = 1.0 - f

    att = [s_ * msmall_ref[0] for s_ in _pair_scores(q, k.astype(BF16), hm)]

    c = logf
    tot = logf
    for lev in range(1, 4):
        half = 1 << (lev - 1)
        right = (sub & half) != 0
        tot_l = pltpu.roll(tot, half, 0)
        tot_r = pltpu.roll(tot, BLK - half, 0)
        e = jnp.exp(jnp.where(right, c, tot - c))
        sc = _pair_scores(q * e, (k * e).astype(BF16), hm)
        m = msmall_ref[lev]
        att = [a_ + s_ * m for a_, s_ in zip(att, sc)]
        c = c + jnp.where(right, tot_l, 0.0)
        tot = tot + jnp.where(right, tot_l, tot_r)

    pieces, carry = [], None
    for g in range(BLK // SUBLANES):
        rows = slice(g * SUBLANES, (g + 1) * SUBLANES)
        pieces.append(c[rows] if carry is None else c[rows] + carry)
        t_g = tot[g * SUBLANES:g * SUBLANES + 1]
        carry = t_g if carry is None else carry + t_g
    yield
    b = jnp.concatenate(pieces, axis=0)
    btot = carry

    for lev in range(4, 8):
        half = 1 << (lev - 1)
        nblk = BLK // (2 * half)
        qr, kf = [], []
        for i in range(nblk):
            lo = i * 2 * half
            mid = lo + half
            bref = b[mid - 1:mid]
            qr.append(q[mid:mid + half] * jnp.exp(b[mid:mid + half] - bref))
            kf.append(k[lo:mid] * jnp.exp(bref - b[lo:mid]))
            kf.append(jnp.zeros((half, GROUP_W), F32))
        sc = _pair_scores(jnp.concatenate(qr, axis=0), jnp.concatenate(kf, axis=0).astype(BF16), hm)
        m = mbig_ref[lev - 4]
        zero = jnp.zeros((half, BLK), F32)
        new = []
        for a_, s_ in zip(att, sc):
            u = s_ * m
            parts = []
            for i in range(nblk):
                parts += [zero, u[i * half:(i + 1) * half]]
            new.append(a_ + jnp.concatenate(parts, axis=0))
        att = new

    yield
    qe = (q * jnp.exp(b)).astype(BF16)
    ke = (k * jnp.exp(btot - b)).astype(BF16)
    etot = jnp.exp(btot)
    row_l = _iota((BLK, BLK), 0)
    col_l = _iota((BLK, BLK), 1)
    same_head = (row_l >= HEAD_DIM) == (col_l >= HEAD_DIM)
    outs = []
    for p in range(2):
        sl = slice(p * LANES, (p + 1) * LANES)
        st = sst_ref[p]
        vp = ci[:, sl]
        vb = vp.astype(BF16)
        a2 = jnp.concatenate([att[2 * p], att[2 * p + 1]], axis=1).astype(BF16)
        v2 = jnp.concatenate([vb * hm[0], vb * hm[1]], axis=0)
        o = _dot_nt(qe[:, sl], st.astype(BF16)) + _dot(a2, v2)
        upd = _dot(vp.T.astype(BF16), ke[:, sl])
        sst_ref[p] = st * etot[:, sl] + jnp.where(same_head, upd, 0.0)
        outs.append(o)
    o = jnp.concatenate(outs, axis=1)
    return _seg_rms(o, hg, ones_bd) * (cg * _sigmoid(cg))


def _mlstm_tile_gates(gcols, gb, m0, triu):
    tm = gcols.shape[0]
    gt = gcols.T[0:SUBLANES, :] + gb
    lf = _log_sigmoid(gt)
    hi = lf.astype(BF16)
    r1 = lf - hi.astype(F32)
    mid = r1.astype(BF16)
    lo = (r1 - mid.astype(F32)).astype(BF16)
    parts = jnp.concatenate([hi.astype(F32), mid.astype(F32), lo.astype(F32)], axis=0)
    cs = _dot(parts, triu)
    fcum = cs[0:SUBLANES] + cs[SUBLANES:2 * SUBLANES] + cs[2 * SUBLANES:3 * SUBLANES]
    fcum = pltpu.roll(fcum, N_HEADS, 0)
    g = gt - fcum
    pad = jnp.zeros((LANES - 2 * SUBLANES, tm), F32)
    cols = jnp.concatenate([g, fcum, pad], axis=0).T
    f_c = pltpu.roll(cols, LANES - SUBLANES, 1)
    sub = _iota((tm, LANES), 0) & (SUBLANES - 1)
    cm = cols
    s = 1
    while s < SUBLANES:
        cm = jnp.maximum(cm, jnp.where(sub >= s, pltpu.roll(cm, s, 0), NEG_BIG))
        s *= 2
    carry = m0
    ms = []
    for grp in range(tm // SUBLANES):
        m_g = jnp.maximum(cm[grp * SUBLANES:(grp + 1) * SUBLANES], carry)
        ms.append(m_g)
        carry = m_g[SUBLANES - 1:SUBLANES]
    m_c = jnp.concatenate(ms, axis=0)
    eneg_c = jnp.exp(-(f_c + m_c))
    m_new = f_c[tm - 1:tm] + carry
    return g, cols, m_c, eneg_c, m_new


def _mlstm_block(dq, dk, dv, do, g_rows, g_c, m_c, eneg_c, m_prev, mg, cst_ref, ones_bd, hm, hmf):
    k = dk * (HEAD_DIM ** -0.5)
    m_end = m_c[BLK - 1:BLK]
    inter = jnp.exp(m_prev - m_c)
    wend = jnp.exp(g_c - m_end)
    dec0 = jnp.exp(m_prev - m_end)
    mrun_c = lambda h: m_c[:, h:h + 1]
    inter_c = lambda h: inter[:, h:h + 1]
    eneg_c_ = lambda h: eneg_c[:, h:h + 1]
    wend_c = lambda h: wend[:, h:h + 1]
    g = g_rows
    lane = _iota((BLK, LANES), 1)
    low = lane < HEAD_DIM
    one_col = jnp.where(lane == HEAD_DIM, 1.0, 0.0)
    causal = _iota((BLK, BLK), 1) <= _iota((BLK, BLK), 0)

    outs = []
    for p in range(2):
        sl = slice(p * LANES, (p + 1) * LANES)
        kp, vp = k[:, sl], dv[:, sl]
        qb = dq[:, sl].astype(BF16)
        qm = [qb * hm[0], qb * hm[1]]
        sc2 = _dot_nt(jnp.concatenate(qm, axis=0), kp.astype(BF16))
        v_sw = pltpu.roll(vp, HEAD_DIM, 1)
        hv = []
        for hh in range(2):
            h = 2 * p + hh
            sc = sc2[hh * BLK:(hh + 1) * BLK]
            w = jnp.exp(jnp.where(causal, g[h:h + 1, :] - mrun_c(h), NEG_BIG))
            sw = (sc * w).astype(BF16)
            vaug = jnp.where(low, vp if hh == 0 else v_sw, one_col).astype(BF16)
            cst = cst_ref[h]
            nd = inter_c(h) * _dot(qm[hh], cst.astype(BF16)) + _dot(sw, vaug)
            den = nd[:, HEAD_DIM:HEAD_DIM + 1]
            hv.append(nd / jnp.maximum(jnp.abs(den), eneg_c_(h)))
            kw = kp * (wend_c(h) * hmf[hh])
            cst_ref[h] = dec0[:, h:h + 1] * cst + _dot(kw.T.astype(BF16), vaug)
        outs.append(jnp.where(low, hv[0], pltpu.roll(hv[1], HEAD_DIM, 1)))
        yield
    hcat = jnp.concatenate(outs, axis=1)
    return _seg_rms(hcat, mg, ones_bd) * _sigmoid(do)


def _mixer_kernel(hc_ref, pv_ref, win_ref, convw_ref, wr_ref, wi_ref, rr_ref, rbase_ref, gamma_ref,
                  ones_ref, msmall_ref, mbig_ref, gb_ref, triu_ref, y_ref, hl_ref, conv_ref, kst_ref, vst_ref,
                  sst_ref, cst_ref, mst_ref, xbuf, u_scr, nb_scr, *, layer, tm):
    t = pl.program_id(0)
    states = (hl_ref, conv_ref, kst_ref, vst_ref, sst_ref, cst_ref, mst_ref)

    @pl.when(t == 0)
    def _init():
        for ref in states:
            ref[...] = jnp.zeros_like(ref)

    bsz = hc_ref.shape[0]
    g1 = pv_ref[layer:layer + 1, V_G1:V_G1 + D_MODEL]
    cols = (((C_AX, C_BQ - C_AX),), ((C_BQ, C_CQ - C_BQ),),
            ((C_CQ, 2 * GROUP_W), (C_CI, 2 * GROUP_W)),
            ((C_DQ, 2 * GROUP_W), (C_DV, N_IN - C_DV)))

    def normalise(src_ref):
        for b in range(bsz):
            nb_scr[b * tm:(b + 1) * tm, :] = _rms_rows(src_ref[b], g1).astype(BF16)

    projected = set()

    def project(group_ids):
        for gid in group_ids:
            for c0, w in cols[gid]:
                u_scr[:, c0:c0 + w] = _dot(nb_scr[...], win_ref[:, c0:c0 + w])
                yield
            projected.add(gid)

    normalise(hc_ref)
    _round_robin([(lambda: True, project([0]))])

    def vec(off, w):
        return pv_ref[layer:layer + 1, off:off + w]

    ctxs = []
    for b in range(bsz):
        proj = functools.partial(lambda c0, w, b: u_scr[b * tm:(b + 1) * tm, c0:c0 + w], b=b)
        ctxs.append(types.SimpleNamespace(
            t=t, proj=proj, vec=vec, layer=layer, tm=tm, ones_bd=ones_ref[...], convw_ref=convw_ref,
            wr_ref=wr_ref, wi_ref=wi_ref, rr_ref=rr_ref, rbase_ref=rbase_ref, gamma_ref=gamma_ref,
            msmall_ref=msmall_ref, mbig_ref=mbig_ref, gb_ref=gb_ref, triu_ref=triu_ref, y_ref=y_ref.at[b],
            hl_ref=hl_ref.at[b], conv_ref=conv_ref.at[b], kst_ref=kst_ref.at[b], vst_ref=vst_ref.at[b],
            sst_ref=sst_ref.at[b], cst_ref=cst_ref.at[b], mst_ref=mst_ref.at[b], xbuf=xbuf.at[b],
            hm=_head_masks(BLK, BF16), hmf=_head_masks(1, F32)))
    groups = (_group_a, _group_b, _group_c, _group_d)
    phases = ((0, 1), (2, 3))
    _round_robin([(lambda: True, project(phases[0][1:]))])
    for pi, phase in enumerate(phases):
        later = [gid for ph in phases[pi + 1:pi + 2] for gid in ph]
        tasks = [(lambda: True, project(later))]
        for c in ctxs:
            for gid in phase:
                tasks.append((lambda: True, groups[gid](c)))
        _round_robin(tasks)


def _round_robin(tasks):
    tasks = list(tasks)
    while tasks:
        for task in list(tasks):
            ready, gen = task
            if not ready():
                continue
            try:
                next(gen)
            except StopIteration:
                tasks.remove(task)


def _group_a(c):
    proj, vec, tm, layer = c.proj, c.vec, c.tm, c.layer
    xbuf, conv_ref, convw_ref, wr_ref, wi_ref, hl_ref, y_ref = (c.xbuf, c.conv_ref, c.convw_ref, c.wr_ref,
                                                                c.wi_ref, c.hl_ref, c.y_ref)
    xa = proj(C_AX, GROUP_W)
    ga = proj(C_AG, GROUP_W)
    xbuf[0:8, :] = conv_ref[...]
    xbuf[8:8 + tm, :] = xa
    cw = convw_ref[layer]
    xc = (vec(V_CB, GROUP_W) + xbuf[5:5 + tm, :] * cw[0:1, :] + xbuf[6:6 + tm, :] * cw[1:2, :]
          + xbuf[7:7 + tm, :] * cw[2:3, :] + xa * cw[3:4, :])
    conv_ref[...] = xbuf[tm:tm + 8, :]
    yield
    a, bx = _rglru_gates(xc, wr_ref[...], wi_ref[...], vec(V_BR, GROUP_W), vec(V_BI, GROUP_W),
                         vec(V_LAM, GROUP_W))
    sub_t = _iota((tm, GROUP_W), 0) & (SUBLANES - 1)
    s = 1
    while s < SUBLANES:
        keep = sub_t >= s
        a_s = pltpu.roll(a, s, 0)
        b_s = pltpu.roll(bx, s, 0)
        bx = jnp.where(keep, a * b_s + bx, bx)
        a = jnp.where(keep, a * a_s, a)
        s *= 2
    carry = hl_ref[...]
    hs = []
    for g in range(tm // SUBLANES):
        rows = slice(g * SUBLANES, (g + 1) * SUBLANES)
        hg_ = a[rows] * carry + bx[rows]
        hs.append(hg_)
        carry = hg_[SUBLANES - 1:SUBLANES]
    hseq = jnp.concatenate(hs, axis=0)
    hl_ref[...] = carry
    yield
    y_ref[:, 0:GROUP_W] = (hseq * _gelu_tanh(ga)).astype(y_ref.dtype)


def _group_b(c):
    proj, vec, tm, t, ones_bd = c.proj, c.vec, c.tm, c.t, c.ones_bd
    rr_ref, rbase_ref, kst_ref, vst_ref, y_ref = c.rr_ref, c.rbase_ref, c.kst_ref, c.vst_ref, c.y_ref
    cb = rbase_ref[:, 0:LANES]
    sb_ = rbase_ref[:, LANES:2 * LANES]
    rc = cb * rr_ref[0] - sb_ * rr_ref[1]
    ra = sb_ * rr_ref[2] + cb * rr_ref[3]
    rb = sb_ * rr_ref[4] + cb * rr_ref[5]
    q = _seg_rms(proj(C_BQ, GROUP_W), vec(V_QG, GROUP_W), ones_bd)
    q = _rope(q, jnp.concatenate([rc, rc], 1), jnp.concatenate([ra, ra], 1), jnp.concatenate([rb, rb], 1))
    k = _seg_rms(proj(C_BK, LANES), vec(V_KG, LANES), ones_bd[0:LANES, 0:LANES])
    k = _rope(k, rc, ra, rb)
    v = proj(C_BV, LANES)
    sink = vec(V_SINK, LANES)
    k_prev, v_prev = kst_ref[...], vst_ref[...]
    for j in range(tm // BLK):
        rs = slice(j * BLK, (j + 1) * BLK)
        first = (t == 0) if j == 0 else False
        yb = _swa_block(q[rs], k[rs], v[rs], k_prev, v_prev, sink, first)
        y_ref[rs, GROUP_W:2 * GROUP_W] = yb.astype(y_ref.dtype)
        k_prev, v_prev = k[rs], v[rs]
        yield
    kst_ref[...] = k_prev
    vst_ref[...] = v_prev


def _group_c(c):
    proj, vec, tm, layer, ones_bd, hm = c.proj, c.vec, c.tm, c.layer, c.ones_bd, c.hm
    gamma_ref, sst_ref, msmall_ref, mbig_ref, y_ref = c.gamma_ref, c.sst_ref, c.msmall_ref, c.mbig_ref, c.y_ref
    lb = _lb_from_gamma([gamma_ref[i:i + 1, :] for i in range(gamma_ref.shape[0])], layer)
    cq, cf, ci, cg = (proj(C_CQ, GROUP_W), proj(C_CF, GROUP_W), proj(C_CI, GROUP_W), proj(C_CG, GROUP_W))
    hg = vec(V_HG, GROUP_W)
    sub = _iota((BLK, GROUP_W), 0) & (SUBLANES - 1)
    for j in range(tm // BLK):
        rs = slice(j * BLK, (j + 1) * BLK)
        yc = yield from _hgrn_block(cq[rs], cf[rs], ci[rs], cg[rs], lb, hg, sst_ref, ones_bd, msmall_ref, mbig_ref,
                         hm, sub)
        y_ref[rs, 2 * GROUP_W:3 * GROUP_W] = yc.astype(y_ref.dtype)
        yield


def _group_d(c):
    proj, vec, tm, layer, ones_bd, hm, hmf = c.proj, c.vec, c.tm, c.layer, c.ones_bd, c.hm, c.hmf
    gb_ref, triu_ref, mst_ref, cst_ref, y_ref = c.gb_ref, c.triu_ref, c.mst_ref, c.cst_ref, c.y_ref
    dq, dk, dv, do = (proj(C_DQ, GROUP_W), proj(C_DK, GROUP_W), proj(C_DV, GROUP_W), proj(C_DO, GROUP_W))
    mg = vec(V_MG, GROUP_W)
    m_prev = mst_ref[...]
    g_rows, g_c, m_c, eneg_c, m_new = _mlstm_tile_gates(proj(C_GATES, LANES), gb_ref[layer], m_prev,
                                                       triu_ref[...])
    mst_ref[...] = m_new
    yield
    for j in range(tm // BLK):
        rs = slice(j * BLK, (j + 1) * BLK)
        yd = yield from _mlstm_block(dq[rs], dk[rs], dv[rs], do[rs], g_rows[:, rs], g_c[rs], m_c[rs], eneg_c[rs],
                          m_prev, mg, cst_ref, ones_bd, hm, hmf)
        y_ref[rs, 3 * GROUP_W:4 * GROUP_W] = yd.astype(y_ref.dtype)
        m_prev = m_c[(j + 1) * BLK - 1:(j + 1) * BLK]
        yield


def _full(shape):
    nd = len(shape)
    return pl.BlockSpec(shape, lambda *_: (0,) * nd)


def _layer_block(shape, layer, single_buffer=False):
    nd = len(shape) - 1
    kw = {'pipeline_mode': pl.Buffered(1)} if single_buffer else {}
    return pl.BlockSpec((None,) + tuple(shape[1:]), lambda *_: (layer,) + (0,) * nd, **kw)


def _prompt_mixers(h, cw, layer, tm):
    bsz, t, _ = h.shape
    nt = t // tm
    kern = functools.partial(_mixer_kernel, layer=layer, tm=tm)
    in_specs = [pl.BlockSpec((bsz, tm, D_MODEL), lambda i: (0, i, 0)),
                _full(cw['vecs'].shape), _layer_block(cw['w_in_p'].shape, layer, single_buffer=True),
                _full(cw['conv_w'].shape), _layer_block(cw['wr_bd'].shape, layer),
                _layer_block(cw['wi_bd'].shape, layer), _full(cw['rope_r'].shape),
                pl.BlockSpec((None, 1, 2 * LANES), lambda i: (i, 0, 0)),
                _full(cw['gamma'].shape), _full(cw['ones_bd'].shape), _full(cw['lvl_small'].shape),
                _full(cw['lvl_big'].shape), _full(cw['gate_bias'].shape), _full(cw['triu'].shape)]
    st_shapes = [(bsz, 1, GROUP_W), (bsz, 8, GROUP_W), (bsz, BLK, LANES), (bsz, BLK, LANES),
                 (bsz, 2, LANES, LANES), (bsz, N_HEADS, LANES, LANES), (bsz, 1, LANES)]
    out_shape = ([jax.ShapeDtypeStruct((bsz, t, D_MODEL), BF16)]
                 + [jax.ShapeDtypeStruct(s, F32) for s in st_shapes])
    out_specs = [pl.BlockSpec((bsz, tm, D_MODEL), lambda i: (0, i, 0))] + [_full(s) for s in st_shapes]
    return pl.pallas_call(
        kern, grid=(nt,), in_specs=in_specs, out_specs=out_specs, out_shape=out_shape,
        scratch_shapes=[pltpu.VMEM((bsz, tm + 8, GROUP_W), F32), pltpu.VMEM((bsz * tm, N_IN), F32),
                        pltpu.VMEM((bsz * tm, D_MODEL), BF16)],
        compiler_params=pltpu.CompilerParams(dimension_semantics=("arbitrary",),
                                             vmem_limit_bytes=VMEM_LIMIT),
        name=f"prompt_mixers_l{layer}",
    )(h, cw['vecs'], cw['w_in_p'], cw['conv_w'], cw['wr_bd'], cw['wi_bd'], cw['rope_r'], cw['rope_base'],
      cw['gamma'], cw['ones_bd'], cw['lvl_small'], cw['lvl_big'], cw['gate_bias'], cw['triu'])


def _ffn_math(h, yb, p, g2, wout_ref, wup_ref, wdn_ref, wg_ref, wp_ref):
    h = h + _dot(yb, wout_ref[...])
    nb = _rms_rows(h, g2).astype(BF16)
    acc = h
    step = 1024
    for c in range(0, D_FF, step):
        f = jnp.maximum(_dot(nb, wup_ref[:, c:c + step]), 0.0)
        acc = acc + _dot((f * f).astype(BF16), wdn_ref[c:c + step, :])
    gate = _sigmoid(_dot(acc.astype(BF16), wg_ref[...]))
    return acc + gate * _dot(p.astype(BF16), wp_ref[...])


def _ffn_kernel(h_ref, y_ref, p_ref, pv_ref, wout_ref, wup_ref, wdn_ref, wg_ref, wp_ref, o_ref, *, layer):
    g2 = pv_ref[layer:layer + 1, V_G2:V_G2 + D_MODEL]
    o_ref[...] = _ffn_math(h_ref[...], y_ref[...], p_ref[...], g2, wout_ref, wup_ref, wdn_ref, wg_ref, wp_ref)


def _ffn_weight_specs(cw, layer):
    names = ['w_out', 'w_up', 'w_down', 'w_gate', 'w_proj']
    return [cw[n] for n in names], [_layer_block(cw[n].shape, layer, single_buffer=True) for n in names]


def _prompt_ffn(h2, y2, p3, cw, layer, tm):
    n = h2.shape[0]
    row = lambda w: pl.BlockSpec((tm, w), lambda i: (i, 0))
    ws, wspecs = _ffn_weight_specs(cw, layer)
    return pl.pallas_call(
        functools.partial(_ffn_kernel, layer=layer), grid=(n // tm,),
        in_specs=[row(D_MODEL), row(D_MODEL), pl.BlockSpec((None, tm, PLE_DIM), lambda i: (layer, i, 0)),
                  _full(cw['vecs'].shape)] + wspecs,
        out_specs=row(D_MODEL), out_shape=jax.ShapeDtypeStruct((n, D_MODEL), F32),
        compiler_params=pltpu.CompilerParams(dimension_semantics=("arbitrary",),
                                             vmem_limit_bytes=VMEM_LIMIT),
        name=f"prompt_ffn_l{layer}",
    )(h2, y2, p3, cw['vecs'], *ws)


def _sample_pre_kernel(x_ref, pv_ref, wt_ref, convw_ref, wr_ref, wi_ref, ones_ref, rope_ref, gcol_ref,
                       cols_ref, h0_ref, conv0_ref, n0_ref, m0_ref,
                       ya_ref, hn_ref, convn_ref, q_ref, kt_ref, vt_ref, fm_ref, nn_ref, mn_ref, *, layer):
    def vec(off, w):
        return pv_ref[layer:layer + 1, off:off + w]

    n = _rms_rows(x_ref[...], vec(V_G1, D_MODEL))
    nb = n.astype(BF16)
    n_t = n.T.astype(BF16)
    ones_bd = ones_ref[...]

    def proj(r0, cnt):
        return _dot_nt(nb, wt_ref[r0:r0 + cnt, :].astype(BF16))

    def proj_t(r0, cnt):
        return _dot(wt_ref[r0:r0 + cnt, :].astype(BF16), n_t)

    xa = proj(C_AX, GROUP_W)
    ga = proj(C_AG, GROUP_W)
    cw = convw_ref[layer]
    xc = (vec(V_CB, GROUP_W) + conv0_ref[0] * cw[0:1, :] + conv0_ref[1] * cw[1:2, :]
          + conv0_ref[2] * cw[2:3, :] + xa * cw[3:4, :])
    convn_ref[0] = conv0_ref[1]
    convn_ref[1] = conv0_ref[2]
    convn_ref[2] = xa
    a, bx = _rglru_gates(xc, wr_ref[...], wi_ref[...], vec(V_BR, GROUP_W), vec(V_BI, GROUP_W),
                         vec(V_LAM, GROUP_W))
    hn = a * h0_ref[...] + bx
    hn_ref[...] = hn
    ya_ref[...] = hn * _gelu_tanh(ga)

    rc, ra, rb = rope_ref[0:1, :], rope_ref[1:2, :], rope_ref[2:3, :]
    q = _seg_rms(proj(C_BQ, GROUP_W), vec(V_QG, GROUP_W), ones_bd)
    q_ref[...] = _rope(q, jnp.concatenate([rc, rc], 1), jnp.concatenate([ra, ra], 1),
                       jnp.concatenate([rb, rb], 1))
    k = _seg_rms(proj(C_BK, LANES), vec(V_KG, LANES), ones_bd[0:LANES, 0:LANES])
    kt_ref[...] = _rope(k, rc, ra, rb).T
    vt_ref[...] = proj(C_BV, LANES).T

    lb = _lb_from_gamma([gcol_ref[i] for i in range(gcol_ref.shape[0])], layer)
    cq, cf, ci, cg = (proj_t(C_CQ, GROUP_W), proj_t(C_CF, GROUP_W), proj_t(C_CI, GROUP_W),
                      proj_t(C_CG, GROUP_W))
    f = lb + (1.0 - lb) * _sigmoid(cf)
    fm_ref[FM_HF:FM_HF + GROUP_W, :] = f
    fm_ref[FM_HK:FM_HK + GROUP_W, :] = 1.0 - f
    fm_ref[FM_HQ:FM_HQ + GROUP_W, :] = cq * _sigmoid(cq)
    fm_ref[FM_HV:FM_HV + GROUP_W, :] = ci
    fm_ref[FM_CG:FM_CG + GROUP_W, :] = cg * _sigmoid(cg)

    dq, dk, dv, do = (proj_t(C_DQ, GROUP_W), proj_t(C_DK, GROUP_W), proj_t(C_DV, GROUP_W),
                      proj_t(C_DO, GROUP_W))
    g8 = proj_t(C_GATES, 2 * N_HEADS) + cols_ref[layer, R_GB:R_GB + 2 * N_HEADS, :]
    ig = g8[0:N_HEADS, :]
    lf = _log_sigmoid(g8)[N_HEADS:2 * N_HEADS, :]
    a_int = lf + m0_ref[...]
    m_new = jnp.maximum(a_int, ig)
    dec = jnp.exp(a_int - m_new)
    w = jnp.exp(ig - m_new)
    mn_ref[...] = m_new
    km = dk * (HEAD_DIM ** -0.5)
    dens = []
    for h in range(N_HEADS):
        sl = slice(h * HEAD_DIM, (h + 1) * HEAD_DIM)
        nn_h = dec[h:h + 1, :] * n0_ref[sl, :] + w[h:h + 1, :] * km[sl, :]
        nn_ref[sl, :] = nn_h
        dens.append(jnp.sum(dq[sl, :] * nn_h, axis=0, keepdims=True))
    pad = jnp.zeros((SUBLANES - N_HEADS, x_ref.shape[0]), F32)
    fm_ref[FM_MK:FM_MK + GROUP_W, :] = km
    fm_ref[FM_MQ:FM_MQ + GROUP_W, :] = dq
    fm_ref[FM_MV:FM_MV + GROUP_W, :] = dv
    fm_ref[FM_DO:FM_DO + GROUP_W, :] = _sigmoid(do)
    fm_ref[FM_DEC:FM_DEC + SUBLANES, :] = jnp.concatenate([dec, pad], 0)
    fm_ref[FM_W:FM_W + SUBLANES, :] = jnp.concatenate([w, pad], 0)
    fm_ref[FM_DEN:FM_DEN + SUBLANES, :] = jnp.concatenate(dens + [pad], 0)
    fm_ref[FM_ENEG:FM_ENEG + SUBLANES, :] = jnp.concatenate([jnp.exp(-m_new), pad], 0)


def _own_slab(ref, first_layer):
    if not first_layer:
        return ref
    ref[1:] = jnp.zeros((ref.shape[0] - 1,) + ref.shape[1:], ref.dtype)
    return ref.at[0]


def _sample_attn_kernel(q_ref, kn_ref, vn_ref, kc_ref, vc_ref, sink_ref, *rest, first_layer):
    ko_ref, vo_ref, o_ref = rest[-3:]
    ko_ref, vo_ref = _own_slab(ko_ref, first_layer), _own_slab(vo_ref, first_layer)
    sb = q_ref.shape[0]
    rows = 2 * HEAD_DIM
    lane = _iota((rows, WINDOW), 1)
    kn = kn_ref[...].reshape(rows, sb)
    vn = vn_ref[...].reshape(rows, sb)
    for s in range(sb):
        kt = pltpu.roll(kc_ref[s].reshape(rows, WINDOW), WINDOW - 1, 1)
        ko_ref[s] = jnp.where(lane == WINDOW - 1, kn[:, s:s + 1], kt).reshape(2, HEAD_DIM, WINDOW)
        vt = pltpu.roll(vc_ref[s].reshape(rows, WINDOW), WINDOW - 1, 1)
        vo_ref[s] = jnp.where(lane == WINDOW - 1, vn[:, s:s + 1], vt).reshape(2, HEAD_DIM, WINDOW)
    for kv in range(2):
        kk = ko_ref[:, kv].astype(BF16)
        vv = vo_ref[:, kv].astype(BF16)
        s_ = jnp.einsum('bqc,bcj->bqj', q_ref[:, kv].astype(BF16), kk,
                        preferred_element_type=F32) * (HEAD_DIM ** -0.5)
        sk = sink_ref[kv]
        mx = jnp.maximum(jnp.max(s_, axis=-1, keepdims=True), sk)
        p = jnp.exp(s_ - mx)
        den = jnp.sum(p, axis=-1, keepdims=True) + jnp.exp(sk - mx)
        o = jnp.einsum('bqj,bcj->bqc', p.astype(BF16), vv, preferred_element_type=F32)
        o_ref[:, kv] = o / den


def _sample_state_kernel(fm_ref, s_ref, c_ref, *rest, first_layer):
    so_ref, co_ref, oh_ref, om_ref = rest[-4:]
    so_ref, co_ref = _own_slab(so_ref, first_layer), _own_slab(co_ref, first_layer)
    h = pl.program_id(0)
    r0 = pl.multiple_of(h * HEAD_DIM, HEAD_DIM)
    hv = fm_ref[pl.ds(FM_HV + r0, HEAD_DIM), :]
    mv = fm_ref[pl.ds(FM_MV + r0, HEAD_DIM), :]
    dec = fm_ref[pl.ds(FM_DEC + h, 1), :]
    w = fm_ref[pl.ds(FM_W + h, 1), :]

    def body(d, carry):
        acc_h, acc_m = carry
        r = r0 + d
        s_new = fm_ref[pl.ds(FM_HF + r, 1), :] * s_ref[d] + fm_ref[pl.ds(FM_HK + r, 1), :] * hv
        so_ref[d] = s_new
        c_new = dec * c_ref[d] + (w * fm_ref[pl.ds(FM_MK + r, 1), :]) * mv
        co_ref[d] = c_new
        return (acc_h + fm_ref[pl.ds(FM_HQ + r, 1), :] * s_new,
                acc_m + fm_ref[pl.ds(FM_MQ + r, 1), :] * c_new)

    zero = jnp.zeros((HEAD_DIM, fm_ref.shape[1]), F32)
    acc_h, acc_m = lax.fori_loop(0, HEAD_DIM, body, (zero, zero), unroll=4)
    oh_ref[...] = acc_h
    om_ref[...] = acc_m


def _sample_post_kernel(h_ref, ya_ref, yb_ref, oh_ref, om_ref, fm_ref, cols_ref, p_ref, pv_ref,
                        wout_ref, wup_ref, wdn_ref, wg_ref, wp_ref, o_ref, *, layer):
    def head_rms(x):
        return x * lax.rsqrt(jnp.mean(x * x, axis=0, keepdims=True) + EPS)

    yc, yd = [], []
    for h in range(N_HEADS):
        sl = slice(h * HEAD_DIM, (h + 1) * HEAD_DIM)
        yc.append(head_rms(oh_ref[h]))
        den = fm_ref[FM_DEN + h:FM_DEN + h + 1, :]
        eneg = fm_ref[FM_ENEG + h:FM_ENEG + h + 1, :]
        yd.append(head_rms(om_ref[h] / jnp.maximum(jnp.abs(den), eneg)))
    yc = (jnp.concatenate(yc, 0) * cols_ref[layer, R_HG:R_HG + GROUP_W, :]
          * fm_ref[FM_CG:FM_CG + GROUP_W, :])
    yd = (jnp.concatenate(yd, 0) * cols_ref[layer, R_MG:R_MG + GROUP_W, :]
          * fm_ref[FM_DO:FM_DO + GROUP_W, :])
    y = jnp.concatenate([ya_ref[...], yb_ref[...], yc.T, yd.T], axis=1).astype(BF16)
    g2 = pv_ref[layer:layer + 1, V_G2:V_G2 + D_MODEL]
    o_ref[...] = _ffn_math(h_ref[...], y, p_ref[...], g2, wout_ref, wup_ref, wdn_ref, wg_ref, wp_ref)


def _call_full(kern, args, specs, out_shape, name):
    specs = [(_full(a.shape) if s is None else s) for a, s in zip(args, specs)]
    return pl.pallas_call(
        kern, grid=(1,), in_specs=specs,
        out_specs=tuple(_full(s.shape) for s in out_shape), out_shape=tuple(out_shape),
        compiler_params=pltpu.CompilerParams(dimension_semantics=("arbitrary",),
                                             vmem_limit_bytes=VMEM_LIMIT),
        name=name,
    )(*args)


def _sample_layer(h, sv, prev, cw, layer):
    nsm = h.shape[0]
    depth = cw['vecs'].shape[0]
    sd = lambda *shape: jax.ShapeDtypeStruct(shape, F32)
    g, l = sd(nsm, GROUP_W), sd(LANES, nsm)

    args = [h, cw['vecs'], cw['w_in_t'], cw['conv_w'], cw['wr_bd'], cw['wi_bd'], cw['ones_bd'],
            cw['rope_s'], cw['gamma_col'], cw['cols'], sv['h'], sv['conv'], sv['n'], sv['m']]
    specs = [None, None, _layer_block(cw['w_in_t'].shape, layer, single_buffer=True), None,
             _layer_block(cw['wr_bd'].shape, layer), _layer_block(cw['wi_bd'].shape, layer), None,
             None, None, None, _layer_block(sv['h'].shape, layer), _layer_block(sv['conv'].shape, layer),
             _layer_block(sv['n'].shape, layer), _layer_block(sv['m'].shape, layer)]
    outs = [g, g, sd(CONV_W - 1, nsm, GROUP_W), g, l, l, sd(FM_ROWS, nsm), sd(GROUP_W, nsm),
            sd(N_HEADS, nsm)]
    ya, hn, convn, q, kt, vt, fm, nn, mn = _call_full(
        functools.partial(_sample_pre_kernel, layer=layer), args, specs, outs, f"sample_pre_l{layer}")

    sb = 16
    nblk = nsm // sb
    q3 = jnp.pad(q.reshape(nsm, 2, 2, HEAD_DIM), ((0, 0), (0, 0), (0, SUBLANES - 2), (0, 0)))
    to_blocks = lambda a: a.reshape(2, HEAD_DIM, nblk, sb).transpose(2, 0, 1, 3)
    cshape = (depth, nsm, 2, HEAD_DIM, WINDOW)
    cspec = pl.BlockSpec((None, sb, 2, HEAD_DIM, WINDOW), lambda i: (layer, i, 0, 0, 0))
    nspec = pl.BlockSpec((None, 2, HEAD_DIM, sb), lambda i: (i, 0, 0, 0))
    qspec = pl.BlockSpec((sb, 2, SUBLANES, HEAD_DIM), lambda i: (i, 0, 0, 0))
    any_spec = pl.BlockSpec(memory_space=pl.ANY)
    cout = (pl.BlockSpec((depth, sb, 2, HEAD_DIM, WINDOW), lambda i: (0, i, 0, 0, 0)) if prev is None
            else cspec)
    chain = [] if prev is None else [prev['k'], prev['v']]
    ko, vo, o3 = pl.pallas_call(
        functools.partial(_sample_attn_kernel, first_layer=prev is None), grid=(nblk,),
        in_specs=[qspec, nspec, nspec, cspec, cspec, _layer_block(cw['sinks8'].shape, layer)]
        + [any_spec] * len(chain),
        out_specs=(cout, cout, qspec),
        out_shape=(sd(*cshape), sd(*cshape), sd(nsm, 2, SUBLANES, HEAD_DIM)),
        input_output_aliases={6 + i: i for i in range(len(chain))},
        compiler_params=pltpu.CompilerParams(dimension_semantics=("arbitrary",),
                                             vmem_limit_bytes=VMEM_LIMIT),
        name=f"sample_attn_l{layer}",
    )(q3, to_blocks(kt), to_blocks(vt), sv['k'], sv['v'], cw['sinks8'], *chain)
    yb = o3[:, :, 0:2, :].reshape(nsm, GROUP_W)

    sshape = (depth, N_HEADS, HEAD_DIM, HEAD_DIM, nsm)
    sspec = pl.BlockSpec((None, None, HEAD_DIM, HEAD_DIM, nsm), lambda i: (layer, i, 0, 0, 0))
    ospec = pl.BlockSpec((None, HEAD_DIM, nsm), lambda i: (i, 0, 0))
    sout = (pl.BlockSpec((depth, None, HEAD_DIM, HEAD_DIM, nsm), lambda i: (0, i, 0, 0, 0))
            if prev is None else sspec)
    chain = [] if prev is None else [prev['s'], prev['c']]
    so, co, oh, om = pl.pallas_call(
        functools.partial(_sample_state_kernel, first_layer=prev is None), grid=(N_HEADS,),
        in_specs=[_full(fm.shape), sspec, sspec] + [any_spec] * len(chain),
        out_specs=(sout, sout, ospec, ospec),
        out_shape=(sd(*sshape), sd(*sshape), sd(N_HEADS, HEAD_DIM, nsm), sd(N_HEADS, HEAD_DIM, nsm)),
        input_output_aliases={3 + i: i for i in range(len(chain))},
        compiler_params=pltpu.CompilerParams(dimension_semantics=("arbitrary",),
                                             vmem_limit_bytes=VMEM_LIMIT),
        name=f"sample_state_l{layer}",
    )(fm, sv['s'], sv['c'], *chain)

    ws, wspecs = _ffn_weight_specs(cw, layer)
    (h_new,) = _call_full(
        functools.partial(_sample_post_kernel, layer=layer),
        [h, ya, yb, oh, om, fm, cw['cols'], sv['p'], cw['vecs']] + ws,
        [None] * 7 + [_layer_block(sv['p'].shape, layer), None] + wspecs,
        [sd(nsm, D_MODEL)], f"sample_post_l{layer}")
    small = (hn, convn, nn, mn)
    big = {'k': ko, 'v': vo, 's': so, 'c': co}
    return h_new, small, big


def _block_diag_all(w):
    depth = w.shape[0]
    rows = w.reshape(depth, GROUP_W, HEAD_DIM)
    idx = jnp.arange(GROUP_W) // HEAD_DIM
    mask = idx[:, None] == idx[None, :]
    return jnp.where(mask[None], jnp.tile(rows, (1, 1, N_HEADS)), 0.0)


def _rope_lane_freq():
    half = ROT_DIM // 2
    inv = jnp.power(ROPE_THETA, -jnp.arange(half, dtype=F32) * (2.0 / ROT_DIM))
    dd = jnp.arange(LANES) % HEAD_DIM
    freq = jnp.where(dd < ROT_DIM, inv[dd % half], 0.0)
    m_a = (dd < half).astype(F32)
    m_b = ((dd >= half) & (dd < ROT_DIM)).astype(F32)
    return freq, m_a, m_b


def _rope_tables(pos):
    freq, m_a, m_b = _rope_lane_freq()
    ang = pos.astype(F32)[:, None] * freq[None, :]
    cos, sin = jnp.cos(ang), jnp.sin(ang)
    return cos, -sin * m_a, sin * m_b


def _rope_split_tables(t, tm):
    freq, m_a, m_b = _rope_lane_freq()
    ang_r = jnp.arange(tm, dtype=F32)[:, None] * freq[None, :]
    cr, sr = jnp.cos(ang_r), jnp.sin(ang_r)
    rope_r = jnp.stack([cr, sr, -cr * m_a, -sr * m_a, cr * m_b, sr * m_b])
    ang_b = (jnp.arange(t // tm) * tm).astype(F32)[:, None] * freq[None, :]
    rope_base = jnp.concatenate([jnp.cos(ang_b), jnp.sin(ang_b)], 1)[:, None, :]
    return rope_r, rope_base


def _hgrn_level_masks():
    t = jnp.arange(BLK)[:, None]
    s = jnp.arange(BLK)[None, :]
    small = [t == s]
    big = []
    for lev in range(1, 8):
        half = 1 << (lev - 1)
        own = ((t >> lev) == (s >> lev)) & ((t & half) != 0) & ((s & half) == 0)
        if lev < 4:
            small.append(own)
        else:
            rows = jnp.concatenate([jnp.arange(m, m + half) for m in range(half, BLK, 2 * half)])
            big.append(own[rows])
    return jnp.stack(small).astype(F32), jnp.stack(big).astype(F32)


def _pad_last(v, width):
    return jnp.pad(v, ((0, 0), (0, width - v.shape[-1])))


def _common(w, t, tm_mix, past_len, nsm):
    depth = w['w_in'].shape[0]
    tile = lambda v, n: jnp.tile(v, (1, n))
    vecs = jnp.concatenate([
        w['norm1_g'], w['norm2_g'], w['conv_b'], w['lru_br'], w['lru_bi'], w['lru_lam'],
        tile(w['q_norm_g'], N_HEADS), tile(w['k_norm_g'], 2), _pad_last(w['attn_sinks'], LANES),
        tile(w['hgrn_norm_g'], N_HEADS), tile(w['mlstm_norm_g'], N_HEADS),
        _pad_last(w['mlstm_ib'], LANES), _pad_last(w['mlstm_fb'], LANES)], axis=1)
    cols = jnp.concatenate([tile(w['hgrn_norm_g'], N_HEADS), tile(w['mlstm_norm_g'], N_HEADS),
                            w['mlstm_ib'], w['mlstm_fb']], axis=1)
    w_in = w['w_in']
    gate_bias = jnp.concatenate([w['mlstm_ib'], w['mlstm_fb']], axis=1)
    lvl_small, lvl_big = _hgrn_level_masks()
    sinks = w['attn_sinks']
    z2 = jnp.zeros((depth, 2, SUBLANES - 2), F32)
    idx = jnp.arange(GROUP_W)
    rope_r, rope_base = _rope_split_tables(t, tm_mix)
    return {
        'vecs': vecs,
        'cols': jnp.broadcast_to(cols[:, :, None], cols.shape + (nsm,)),
        'gamma': w['hgrn_gamma'],
        'gamma_col': jnp.broadcast_to(w['hgrn_gamma'][:, :, None], w['hgrn_gamma'].shape + (nsm,)),
        'w_in_p': jnp.pad(w_in, ((0, 0), (0, 0), (0, N_IN - D_IN))).astype(BF16),
        'lvl_small': lvl_small, 'lvl_big': lvl_big,
        'gate_bias': jnp.broadcast_to(gate_bias[:, :, None], gate_bias.shape + (tm_mix,)),
        'triu': (jnp.arange(tm_mix)[:, None] <= jnp.arange(tm_mix)[None, :]).astype(F32),
        'w_in_t': jnp.swapaxes(w_in, 1, 2),
        'conv_w': w['conv_w'],
        'wr_bd': _block_diag_all(w['lru_wr']).astype(BF16),
        'wi_bd': _block_diag_all(w['lru_wi']).astype(BF16),
        'ones_bd': (idx[:, None] // HEAD_DIM == idx[None, :] // HEAD_DIM).astype(BF16),
        'sinks8': jnp.concatenate([sinks.reshape(depth, 2, 2), z2], axis=2)[..., None],
        'rope_r': rope_r, 'rope_base': rope_base,
        'rope_s': jnp.concatenate(_rope_tables(past_len + jnp.arange(1)), axis=0),
        'w_out': w['w_out'].astype(BF16), 'w_up': w['w_up'].astype(BF16),
        'w_down': w['w_down'].astype(BF16), 'w_gate': w['w_ple_gate'].astype(BF16),
        'w_proj': w['w_ple_proj'].astype(BF16),
    }


def _run(x_prompt, x_sample, p_prompt, p_sample, sample_state, w, past_len, tm_mix=256, tm_ffn=512):
    depth = w['w_in'].shape[0]
    bsz, t, _ = x_prompt.shape
    nsm = x_sample.shape[0]
    cw = _common(w, t, tm_mix, past_len, nsm)
    tm_ffn = min(tm_ffn, bsz * t)
    h0, conv0, kc, vc, s0, c0, n0, m0 = sample_state
    sv = {'h': h0, 'conv': jnp.transpose(conv0, (0, 2, 1, 3)),
          'k': jnp.transpose(kc, (0, 1, 3, 4, 2)), 'v': jnp.transpose(vc, (0, 1, 3, 4, 2)),
          's': jnp.transpose(s0, (0, 2, 3, 4, 1)), 'c': jnp.transpose(c0, (0, 2, 3, 4, 1)),
          'n': jnp.transpose(n0, (0, 2, 3, 1)).reshape(depth, GROUP_W, nsm),
          'm': jnp.transpose(m0, (0, 2, 1)), 'p': p_sample.reshape(depth, nsm, PLE_DIM)}
    p3 = p_prompt.reshape(depth, bsz * t, PLE_DIM)

    hp = x_prompt
    hs = x_sample.reshape(nsm, D_MODEL)
    p_states, s_small, big = [], [], None
    for l in range(depth):
        y, hl, conv, kst, vst, sst, cst, mst = _prompt_mixers(hp, cw, l, tm_mix)
        hp = _prompt_ffn(hp.reshape(bsz * t, D_MODEL), y.reshape(bsz * t, D_MODEL), p3, cw, l,
                         tm_ffn).reshape(bsz, t, D_MODEL)
        s_hgrn = jnp.swapaxes(sst, -1, -2)
        s_hgrn = jnp.stack([s_hgrn[:, 0, 0:64, 0:64], s_hgrn[:, 0, 64:128, 64:128],
                            s_hgrn[:, 1, 0:64, 0:64], s_hgrn[:, 1, 64:128, 64:128]], 1)
        c_rows = jnp.stack([cst[:, 0, 0:64], cst[:, 1, 64:128], cst[:, 2, 0:64], cst[:, 3, 64:128]], 1)
        p_states.append((hl[:, 0], conv[:, 8 - (CONV_W - 1):], kst.reshape(bsz, WINDOW, 2, HEAD_DIM),
                         vst.reshape(bsz, WINDOW, 2, HEAD_DIM), s_hgrn, c_rows[..., 0:HEAD_DIM],
                         c_rows[..., HEAD_DIM], mst[:, 0, 0:N_HEADS]))
        hs, small, big = _sample_layer(hs, sv, big, cw, l)
        s_small.append(small)
    stack = lambda sts, i: jnp.stack([s[i] for s in sts])
    prompt_out = tuple(stack(p_states, i) for i in range(8))
    hn, convn, nn, mn = (stack(s_small, i) for i in range(4))
    sample_out = (hn, jnp.transpose(convn, (0, 2, 1, 3)),
                  jnp.transpose(big['k'], (0, 1, 4, 2, 3)), jnp.transpose(big['v'], (0, 1, 4, 2, 3)),
                  jnp.transpose(big['s'], (0, 4, 1, 2, 3)), jnp.transpose(big['c'], (0, 4, 1, 2, 3)),
                  jnp.transpose(nn.reshape(depth, N_HEADS, HEAD_DIM, nsm), (0, 3, 1, 2)),
                  jnp.transpose(mn, (0, 2, 1)))
    return (hp, hs.reshape(x_sample.shape)) + prompt_out + sample_out


def kernel(x_prompt, x_sample, p_prompt, p_sample, state_rglru_h, state_rglru_conv, cache_swa_k, cache_swa_v, state_hgrn_s, state_mlstm_c, state_mlstm_n, state_mlstm_m, norm1_g, w_in, conv_w, conv_b, lru_wr, lru_br, lru_wi, lru_bi, lru_lam, q_norm_g, k_norm_g, attn_sinks, hgrn_gamma, hgrn_norm_g, mlstm_ib, mlstm_fb, mlstm_norm_g, w_out, norm2_g, w_up, w_down, w_ple_gate, w_ple_proj):
    w = {'norm1_g': norm1_g, 'w_in': w_in, 'conv_w': conv_w, 'conv_b': conv_b, 'lru_wr': lru_wr,
         'lru_br': lru_br, 'lru_wi': lru_wi, 'lru_bi': lru_bi, 'lru_lam': lru_lam, 'q_norm_g': q_norm_g,
         'k_norm_g': k_norm_g, 'attn_sinks': attn_sinks, 'hgrn_gamma': hgrn_gamma,
         'hgrn_norm_g': hgrn_norm_g, 'mlstm_ib': mlstm_ib, 'mlstm_fb': mlstm_fb,
         'mlstm_norm_g': mlstm_norm_g, 'w_out': w_out, 'norm2_g': norm2_g, 'w_up': w_up,
         'w_down': w_down, 'w_ple_gate': w_ple_gate, 'w_ple_proj': w_ple_proj}
    st = (state_rglru_h, state_rglru_conv, cache_swa_k, cache_swa_v, state_hgrn_s, state_mlstm_c,
          state_mlstm_n, state_mlstm_m)
    past_len = 8192
    return _run(x_prompt, x_sample, p_prompt, p_sample, st, w, past_len)
```

```python
import functools
import types

import jax
import jax.numpy as jnp
from jax import lax
from jax.experimental import pallas as pl
from jax.experimental.pallas import tpu as pltpu

F32 = jnp.float32
BF16 = jnp.bfloat16

D_MODEL = 1024
GROUP_W = 256
HEAD_DIM = 64
N_HEADS = 4
EPS = 1e-6
NEG_BIG = -1e30
LRU_C = 8.0
CONV_W = 4
ROT_DIM = 16
ROPE_THETA = 500000.0
WINDOW = 128
D_FF = 4096
PLE_DIM = 256
LANES = 128
SUBLANES = 8
BLK = 128

C_AX, C_AG, C_BQ, C_BK, C_BV = 0, 256, 512, 768, 896
C_CQ, C_CF, C_CI, C_CG = 1024, 1280, 1536, 1792
C_DQ, C_DK, C_DV, C_DO = 2048, 2304, 2560, 2816
C_GATES = 3072
D_IN = 3080
N_IN = 3200

V_G1, V_G2, V_CB, V_BR, V_BI, V_LAM = 0, 1024, 2048, 2304, 2560, 2816
V_QG, V_KG, V_SINK, V_HG, V_MG, V_IB, V_FB = 3072, 3328, 3456, 3584, 3840, 4096, 4224
N_VEC = 4352

R_HG, R_MG, R_GB = 0, 256, 512
N_COL = 520

FM_HF, FM_HK, FM_HQ, FM_HV, FM_CG = 0, 256, 512, 768, 1024
FM_MK, FM_MQ, FM_MV, FM_DO = 1280, 1536, 1792, 2048
FM_DEC, FM_W, FM_DEN, FM_ENEG = 2304, 2312, 2320, 2328
FM_ROWS = 2336

VMEM_LIMIT = 56 * 1024 * 1024


def _dot(a, b):
    return jnp.dot(a, b, preferred_element_type=F32)


def _dot_nt(a, b):
    return lax.dot_general(a, b, (((1,), (1,)), ((), ())), preferred_element_type=F32)


def _sigmoid(x):
    return jax.nn.sigmoid(x)


def _gelu_tanh(x):
    return 0.5 * x * (1.0 + jnp.tanh(0.7978845608028654 * (x + 0.044715 * (x * x * x))))


def _log_sigmoid(x):
    return jnp.minimum(x, 0.0) - jnp.log1p(jnp.exp(-jnp.abs(x)))


def _softplus(x):
    return jnp.maximum(x, 0.0) + jnp.log1p(jnp.exp(-jnp.abs(x)))


def _rms_rows(x, g):
    return x * lax.rsqrt(jnp.mean(x * x, axis=-1, keepdims=True) + EPS) * g


def _seg_mean_sq(x, ones_bd):
    sq = x * x
    hi = sq.astype(BF16)
    lo = (sq - hi.astype(F32)).astype(BF16)
    return (_dot(hi, ones_bd) + _dot(lo, ones_bd)) * (1.0 / HEAD_DIM)


def _seg_rms(x, g, ones_bd):
    return x * lax.rsqrt(_seg_mean_sq(x, ones_bd) + EPS) * g


def _rope(x, c, sa, sb):
    w = x.shape[1]
    up = pltpu.roll(x, w - ROT_DIM // 2, 1)
    dn = pltpu.roll(x, ROT_DIM // 2, 1)
    return x * c + up * sa + dn * sb


def _lb_from_gamma(gammas, layer):
    mx = functools.reduce(jnp.maximum, gammas)
    e = [jnp.exp(g - mx) for g in gammas]
    tot = functools.reduce(lambda a, b: a + b, e)
    lb = jnp.zeros_like(tot)
    for i in range(1, layer + 1):
        lb = lb + e[i] / tot
    return lb


def _iota(shape, axis):
    return lax.broadcasted_iota(jnp.int32, shape, axis)


def _rglru_gates(xc, wr, wi, br, bi, lam):
    xcb = xc.astype(BF16)
    r = _sigmoid(_dot(xcb, wr) + br)
    ig = _sigmoid(_dot(xcb, wi) + bi)
    log_a = (-LRU_C) * r * _softplus(-lam)
    a = jnp.exp(log_a)
    y = 1.0 - a * a
    root = jnp.where(y > 0.0, y * lax.rsqrt(y), 0.0)
    return a, root * (ig * xc)


def _swa_block(q, k, v, k_prev, v_prev, sink, first):
    kk = jnp.concatenate([k_prev, k], axis=0)
    vv = jnp.concatenate([v_prev, v], axis=0)
    qi = _iota((BLK, 2 * BLK), 0)
    kj = _iota((BLK, 2 * BLK), 1)
    valid = (kj > qi) & (kj <= qi + WINDOW) & ((kj >= BLK) | jnp.logical_not(first))
    lane = _iota((2 * BLK, LANES), 1)
    one_col = jnp.where(lane == HEAD_DIM, 1.0, 0.0)
    vaug = (jnp.where(lane < HEAD_DIM, vv, one_col).astype(BF16),
            jnp.where(lane < HEAD_DIM, pltpu.roll(vv, HEAD_DIM, 1), one_col).astype(BF16))
    qs = q * (HEAD_DIM ** -0.5)
    outs = []
    for h in range(N_HEADS):
        kv = h // 2
        qh = qs[:, h * HEAD_DIM:(h + 1) * HEAD_DIM].astype(BF16)
        kh = kk[:, kv * HEAD_DIM:(kv + 1) * HEAD_DIM].astype(BF16)
        s = jnp.where(valid, _dot_nt(qh, kh), NEG_BIG)
        sk = sink[:, h:h + 1]
        mx = jnp.maximum(jnp.max(s, axis=-1, keepdims=True), sk)
        p = jnp.exp(s - mx)
        o = _dot(p.astype(BF16), vaug[kv])
        den = o[:, HEAD_DIM:HEAD_DIM + 1] + jnp.exp(sk - mx)
        outs.append(o[:, 0:HEAD_DIM] / den)
    return jnp.concatenate(outs, axis=1)


def _head_masks(rows, dtype):
    lane = _iota((rows, LANES), 1)
    return (jnp.where(lane < HEAD_DIM, 1.0, 0.0).astype(dtype),
            jnp.where(lane >= HEAD_DIM, 1.0, 0.0).astype(dtype))


def _pair_scores(qe, ke_b, hm):
    n = qe.shape[0]
    if n != hm[0].shape[0]:
        hm = _head_masks(n, BF16)
    res = []
    for p in range(2):
        sl = slice(p * LANES, (p + 1) * LANES)
        qb = qe[:, sl].astype(BF16)
        lhs = jnp.concatenate([qb * hm[0], qb * hm[1]], axis=0)
        pr = _dot_nt(lhs, ke_b[:, sl])
        res += [pr[:n], pr[n:]]
    return res


def _hgrn_block(cq, cf, ci, cg, lb, hg, sst_ref, ones_bd, msmall_ref, mbig_ref, hm, sub):
    q = cq * _sigmoid(cq)
    f = lb + (1.0 - lb) * _sigmoid(cf)
    logf = jnp.log(f)
    k = 1.0 - f

    att = [s_ * msmall_ref[0] for s_ in _pair_scores(q, k.astype(BF16), hm)]

    c = logf
    tot = logf
    for lev in range(1, 4):
        half = 1 << (lev - 1)
        right = (sub & half) != 0
        tot_l = pltpu.roll(tot, half, 0)
        tot_r = pltpu.roll(tot, BLK - half, 0)
        e = jnp.exp(jnp.where(right, c, tot - c))
        sc = _pair_scores(q * e, (k * e).astype(BF16), hm)
        m = msmall_ref[lev]
        att = [a_ + s_ * m for a_, s_ in zip(att, sc)]
        c = c + jnp.where(right, tot_l, 0.0)
        tot = tot + jnp.where(right, tot_l, tot_r)

    pieces, carry = [], None
    for g in range(BLK // SUBLANES):
        rows = slice(g * SUBLANES, (g + 1) * SUBLANES)
        pieces.append(c[rows] if carry is None else c[rows] + carry)
        t_g = tot[g * SUBLANES:g * SUBLANES + 1]
        carry = t_g if carry is None else carry + t_g
    yield
    b = jnp.concatenate(pieces, axis=0)
    btot = carry

    for lev in range(4, 8):
        half = 1 << (lev - 1)
        nblk = BLK // (2 * half)
        qr, kf = [], []
        for i in range(nblk):
            lo = i * 2 * half
            mid = lo + half
            bref = b[mid - 1:mid]
            qr.append(q[mid:mid + half] * jnp.exp(b[mid:mid + half] - bref))
            kf.append(k[lo:mid] * jnp.exp(bref - b[lo:mid]))
            kf.append(jnp.zeros((half, GROUP_W), F32))
        sc = _pair_scores(jnp.concatenate(qr, axis=0), jnp.concatenate(kf, axis=0).astype(BF16), hm)
        m = mbig_ref[lev - 4]
        zero = jnp.zeros((half, BLK), F32)
        new = []
        for a_, s_ in zip(att, sc):
            u = s_ * m
            parts = []
            for i in range(nblk):
                parts += [zero, u[i * half:(i + 1) * half]]
            new.append(a_ + jnp.concatenate(parts, axis=0))
        att = new

    yield
    qe = (q * jnp.exp(b)).astype(BF16)
    ke = (k * jnp.exp(btot - b)).astype(BF16)
    etot = jnp.exp(btot)
    row_l = _iota((BLK, BLK), 0)
    col_l = _iota((BLK, BLK), 1)
    same_head = (row_l >= HEAD_DIM) == (col_l >= HEAD_DIM)
    outs = []
    for p in range(2):
        sl = slice(p * LANES, (p + 1) * LANES)
        st = sst_ref[p]
        vp = ci[:, sl]
        vb = vp.astype(BF16)
        a2 = jnp.concatenate([att[2 * p], att[2 * p + 1]], axis=1).astype(BF16)
        v2 = jnp.concatenate([vb * hm[0], vb * hm[1]], axis=0)
        o = _dot_nt(qe[:, sl], st.astype(BF16)) + _dot(a2, v2)
        upd = _dot(vp.T.astype(BF16), ke[:, sl])
        sst_ref[p] = st * etot[:, sl] + jnp.where(same_head, upd, 0.0)
        outs.append(o)
    o = jnp.concatenate(outs, axis=1)
    return _seg_rms(o, hg, ones_bd) * (cg * _sigmoid(cg))


def _mlstm_tile_gates(gcols, gb, m0, triu):
    tm = gcols.shape[0]
    gt = gcols.T[0:SUBLANES, :] + gb
    lf = _log_sigmoid(gt)
    hi = lf.astype(BF16)
    r1 = lf - hi.astype(F32)
    mid = r1.astype(BF16)
    lo = (r1 - mid.astype(F32)).astype(BF16)
    parts = jnp.concatenate([hi.astype(F32), mid.astype(F32), lo.astype(F32)], axis=0)
    cs = _dot(parts, triu)
    fcum = cs[0:SUBLANES] + cs[SUBLANES:2 * SUBLANES] + cs[2 * SUBLANES:3 * SUBLANES]
    fcum = pltpu.roll(fcum, N_HEADS, 0)
    g = gt - fcum
    pad = jnp.zeros((LANES - 2 * SUBLANES, tm), F32)
    cols = jnp.concatenate([g, fcum, pad], axis=0).T
    f_c = pltpu.roll(cols, LANES - SUBLANES, 1)
    sub = _iota((tm, LANES), 0) & (SUBLANES - 1)
    cm = cols
    s = 1
    while s < SUBLANES:
        cm = jnp.maximum(cm, jnp.where(sub >= s, pltpu.roll(cm, s, 0), NEG_BIG))
        s *= 2
    carry = m0
    ms = []
    for grp in range(tm // SUBLANES):
        m_g = jnp.maximum(cm[grp * SUBLANES:(grp + 1) * SUBLANES], carry)
        ms.append(m_g)
        carry = m_g[SUBLANES - 1:SUBLANES]
    m_c = jnp.concatenate(ms, axis=0)
    eneg_c = jnp.exp(-(f_c + m_c))
    m_new = f_c[tm - 1:tm] + carry
    return g, cols, m_c, eneg_c, m_new


def _mlstm_block(dq, dk, dv, do, g_rows, g_c, m_c, eneg_c, m_prev, mg, cst_ref, ones_bd, hm, hmf):
    k = dk * (HEAD_DIM ** -0.5)
    m_end = m_c[BLK - 1:BLK]
    inter = jnp.exp(m_prev - m_c)
    wend = jnp.exp(g_c - m_end)
    dec0 = jnp.exp(m_prev - m_end)
    mrun_c = lambda h: m_c[:, h:h + 1]
    inter_c = lambda h: inter[:, h:h + 1]
    eneg_c_ = lambda h: eneg_c[:, h:h + 1]
    wend_c = lambda h: wend[:, h:h + 1]
    g = g_rows
    lane = _iota((BLK, LANES), 1)
    low = lane < HEAD_DIM
    one_col = jnp.where(lane == HEAD_DIM, 1.0, 0.0)
    causal = _iota((BLK, BLK), 1) <= _iota((BLK, BLK), 0)

    outs = []
    for p in range(2):
        sl = slice(p * LANES, (p + 1) * LANES)
        kp, vp = k[:, sl], dv[:, sl]
        qb = dq[:, sl].astype(BF16)
        qm = [qb * hm[0], qb * hm[1]]
        sc2 = _dot_nt(jnp.concatenate(qm, axis=0), kp.astype(BF16))
        v_sw = pltpu.roll(vp, HEAD_DIM, 1)
        hv = []
        for hh in range(2):
            h = 2 * p + hh
            sc = sc2[hh * BLK:(hh + 1) * BLK]
            w = jnp.exp(jnp.where(causal, g[h:h + 1, :] - mrun_c(h), NEG_BIG))
            sw = (sc * w).astype(BF16)
            vaug = jnp.where(low, vp if hh == 0 else v_sw, one_col).astype(BF16)
            cst = cst_ref[h]
            nd = inter_c(h) * _dot(qm[hh], cst.astype(BF16)) + _dot(sw, vaug)
            den = nd[:, HEAD_DIM:HEAD_DIM + 1]
            hv.append(nd / jnp.maximum(jnp.abs(den), eneg_c_(h)))
            kw = kp * (wend_c(h) * hmf[hh])
            cst_ref[h] = dec0[:, h:h + 1] * cst + _dot(kw.T.astype(BF16), vaug)
        outs.append(jnp.where(low, hv[0], pltpu.roll(hv[1], HEAD_DIM, 1)))
        yield
    hcat = jnp.concatenate(outs, axis=1)
    return _seg_rms(hcat, mg, ones_bd) * _sigmoid(do)


def _mixer_kernel(hc_ref, hn_ref, pv_ref, win_ref, convw_ref, wr_ref, wi_ref, rr_ref, rbase_ref, gamma_ref,
                  ones_ref, msmall_ref, mbig_ref, gb_ref, triu_ref, y_ref, hl_ref, conv_ref, kst_ref, vst_ref,
                  sst_ref, cst_ref, mst_ref, xbuf, u_scr, nb_scr, *, layer, tm):
    t = pl.program_id(0)
    states = (hl_ref, conv_ref, kst_ref, vst_ref, sst_ref, cst_ref, mst_ref)

    @pl.when(t == 0)
    def _init():
        for ref in states:
            ref[...] = jnp.zeros_like(ref)

    bsz = hc_ref.shape[0]
    g1 = pv_ref[layer:layer + 1, V_G1:V_G1 + D_MODEL]
    cols = (((C_AX, C_BQ - C_AX),), ((C_BQ, C_CQ - C_BQ),),
            ((C_CQ, 2 * GROUP_W), (C_CI, 2 * GROUP_W)),
            ((C_DQ, 2 * GROUP_W), (C_DV, N_IN - C_DV)))

    def normalise(src_ref):
        for b in range(bsz):
            nb_scr[b * tm:(b + 1) * tm, :] = _rms_rows(src_ref[b], g1).astype(BF16)

    def project(group_ids):
        for gid in group_ids:
            for c0, w in cols[gid]:
                u_scr[:, c0:c0 + w] = _dot(nb_scr[...], win_ref[:, c0:c0 + w])
                yield

    phases = ((0, 1), (2, 3))

    @pl.when(t == 0)
    def _prologue():
        normalise(hc_ref)
        _round_robin([(lambda: True, project(phases[0]))])

    def vec(off, w):
        return pv_ref[layer:layer + 1, off:off + w]

    ctxs = []
    for b in range(bsz):
        proj = functools.partial(lambda c0, w, b: u_scr[b * tm:(b + 1) * tm, c0:c0 + w], b=b)
        ctxs.append(types.SimpleNamespace(
            t=t, proj=proj, vec=vec, layer=layer, tm=tm, ones_bd=ones_ref[...], convw_ref=convw_ref,
            wr_ref=wr_ref, wi_ref=wi_ref, rr_ref=rr_ref, rbase_ref=rbase_ref, gamma_ref=gamma_ref,
            msmall_ref=msmall_ref, mbig_ref=mbig_ref, gb_ref=gb_ref, triu_ref=triu_ref, y_ref=y_ref.at[b],
            hl_ref=hl_ref.at[b], conv_ref=conv_ref.at[b], kst_ref=kst_ref.at[b], vst_ref=vst_ref.at[b],
            sst_ref=sst_ref.at[b], cst_ref=cst_ref.at[b], mst_ref=mst_ref.at[b], xbuf=xbuf.at[b],
            hm=_head_masks(BLK, BF16), hmf=_head_masks(1, F32)))
    groups = (_group_a, _group_b, _group_c, _group_d)
    for pi, phase in enumerate(phases):
        if pi == 1:
            normalise(hn_ref)
        tasks = [(lambda: True, project(phases[1 - pi]))]
        for c in ctxs:
            for gid in phase:
                tasks.append((lambda: True, groups[gid](c)))
        _round_robin(tasks)


def _round_robin(tasks):
    tasks = list(tasks)
    while tasks:
        for task in list(tasks):
            ready, gen = task
            if not ready():
                continue
            try:
                next(gen)
            except StopIteration:
                tasks.remove(task)


def _group_a(c):
    proj, vec, tm, layer = c.proj, c.vec, c.tm, c.layer
    xbuf, conv_ref, convw_ref, wr_ref, wi_ref, hl_ref, y_ref = (c.xbuf, c.conv_ref, c.convw_ref, c.wr_ref,
                                                                c.wi_ref, c.hl_ref, c.y_ref)
    xa = proj(C_AX, GROUP_W)
    ga = proj(C_AG, GROUP_W)
    cw = convw_ref[layer]
    tail = conv_ref[...]
    sub8 = _iota((SUBLANES, GROUP_W), 0)

    def shifted(j):
        r = pltpu.roll(xa, j, 0)
        head = jnp.where(sub8 < j, pltpu.roll(tail, j, 0), r[0:SUBLANES])
        return jnp.concatenate([head, r[SUBLANES:]], axis=0)

    xc = (vec(V_CB, GROUP_W) + shifted(3) * cw[0:1, :] + shifted(2) * cw[1:2, :]
          + shifted(1) * cw[2:3, :] + xa * cw[3:4, :])
    conv_ref[...] = xa[tm - SUBLANES:tm]
    yield
    a, bx = _rglru_gates(xc, wr_ref[...], wi_ref[...], vec(V_BR, GROUP_W), vec(V_BI, GROUP_W),
                         vec(V_LAM, GROUP_W))
    sub_t = _iota((tm, GROUP_W), 0) & (SUBLANES - 1)
    s = 1
    while s < SUBLANES:
        keep = sub_t >= s
        a_s = pltpu.roll(a, s, 0)
        b_s = pltpu.roll(bx, s, 0)
        bx = jnp.where(keep, a * b_s + bx, bx)
        a = jnp.where(keep, a * a_s, a)
        s *= 2
    carry = hl_ref[...]
    hs = []
    for g in range(tm // SUBLANES):
        rows = slice(g * SUBLANES, (g + 1) * SUBLANES)
        hg_ = a[rows] * carry + bx[rows]
        hs.append(hg_)
        carry = hg_[SUBLANES - 1:SUBLANES]
    hseq = jnp.concatenate(hs, axis=0)
    hl_ref[...] = carry
    yield
    y_ref[:, 0:GROUP_W] = (hseq * _gelu_tanh(ga)).astype(y_ref.dtype)


def _group_b(c):
    proj, vec, tm, t, ones_bd = c.proj, c.vec, c.tm, c.t, c.ones_bd
    rr_ref, rbase_ref, kst_ref, vst_ref, y_ref = c.rr_ref, c.rbase_ref, c.kst_ref, c.vst_ref, c.y_ref
    cb = rbase_ref[:, 0:LANES]
    sb_ = rbase_ref[:, LANES:2 * LANES]
    rc = cb * rr_ref[0] - sb_ * rr_ref[1]
    ra = sb_ * rr_ref[2] + cb * rr_ref[3]
    rb = sb_ * rr_ref[4] + cb * rr_ref[5]
    q = _seg_rms(proj(C_BQ, GROUP_W), vec(V_QG, GROUP_W), ones_bd)
    q = _rope(q, jnp.concatenate([rc, rc], 1), jnp.concatenate([ra, ra], 1), jnp.concatenate([rb, rb], 1))
    k = _seg_rms(proj(C_BK, LANES), vec(V_KG, LANES), ones_bd[0:LANES, 0:LANES])
    k = _rope(k, rc, ra, rb)
    v = proj(C_BV, LANES)
    sink = vec(V_SINK, LANES)
    k_prev, v_prev = kst_ref[...], vst_ref[...]
    for j in range(tm // BLK):
        rs = slice(j * BLK, (j + 1) * BLK)
        first = (t == 0) if j == 0 else False
        yb = _swa_block(q[rs], k[rs], v[rs], k_prev, v_prev, sink, first)
        y_ref[rs, GROUP_W:2 * GROUP_W] = yb.astype(y_ref.dtype)
        k_prev, v_prev = k[rs], v[rs]
        yield
    kst_ref[...] = k_prev
    vst_ref[...] = v_prev


def _group_c(c):
    proj, vec, tm, layer, ones_bd, hm = c.proj, c.vec, c.tm, c.layer, c.ones_bd, c.hm
    gamma_ref, sst_ref, msmall_ref, mbig_ref, y_ref = c.gamma_ref, c.sst_ref, c.msmall_ref, c.mbig_ref, c.y_ref
    lb = _lb_from_gamma([gamma_ref[i:i + 1, :] for i in range(gamma_ref.shape[0])], layer)
    cq, cf, ci, cg = (proj(C_CQ, GROUP_W), proj(C_CF, GROUP_W), proj(C_CI, GROUP_W), proj(C_CG, GROUP_W))
    hg = vec(V_HG, GROUP_W)
    sub = _iota((BLK, GROUP_W), 0) & (SUBLANES - 1)
    for j in range(tm // BLK):
        rs = slice(j * BLK, (j + 1) * BLK)
        yc = yield from _hgrn_block(cq[rs], cf[rs], ci[rs], cg[rs], lb, hg, sst_ref, ones_bd, msmall_ref, mbig_ref,
                         hm, sub)
        y_ref[rs, 2 * GROUP_W:3 * GROUP_W] = yc.astype(y_ref.dtype)
        yield


def _group_d(c):
    proj, vec, tm, layer, ones_bd, hm, hmf = c.proj, c.vec, c.tm, c.layer, c.ones_bd, c.hm, c.hmf
    gb_ref, triu_ref, mst_ref, cst_ref, y_ref = c.gb_ref, c.triu_ref, c.mst_ref, c.cst_ref, c.y_ref
    dq, dk, dv, do = (proj(C_DQ, GROUP_W), proj(C_DK, GROUP_W), proj(C_DV, GROUP_W), proj(C_DO, GROUP_W))
    mg = vec(V_MG, GROUP_W)
    m_prev = mst_ref[...]
    g_rows, g_c, m_c, eneg_c, m_new = _mlstm_tile_gates(proj(C_GATES, LANES), gb_ref[layer], m_prev,
                                                       triu_ref[...])
    mst_ref[...] = m_new
    yield
    for j in range(tm // BLK):
        rs = slice(j * BLK, (j + 1) * BLK)
        yd = yield from _mlstm_block(dq[rs], dk[rs], dv[rs], do[rs], g_rows[:, rs], g_c[rs], m_c[rs], eneg_c[rs],
                          m_prev, mg, cst_ref, ones_bd, hm, hmf)
        y_ref[rs, 3 * GROUP_W:4 * GROUP_W] = yd.astype(y_ref.dtype)
        m_prev = m_c[(j + 1) * BLK - 1:(j + 1) * BLK]
        yield


def _full(shape):
    nd = len(shape)
    return pl.BlockSpec(shape, lambda *_: (0,) * nd)


def _layer_block(shape, layer, single_buffer=False):
    nd = len(shape) - 1
    kw = {'pipeline_mode': pl.Buffered(1)} if single_buffer else {}
    return pl.BlockSpec((None,) + tuple(shape[1:]), lambda *_: (layer,) + (0,) * nd, **kw)


def _prompt_mixers(h, cw, layer, tm):
    bsz, t, _ = h.shape
    nt = t // tm
    kern = functools.partial(_mixer_kernel, layer=layer, tm=tm)
    in_specs = [pl.BlockSpec((bsz, tm, D_MODEL), lambda i: (0, i, 0)),
                pl.BlockSpec((bsz, tm, D_MODEL), lambda i: (0, jnp.minimum(i + 1, nt - 1), 0)),
                _full(cw['vecs'].shape), _layer_block(cw['w_in_p'].shape, layer, single_buffer=True),
                _full(cw['conv_w'].shape), _layer_block(cw['wr_bd'].shape, layer),
                _layer_block(cw['wi_bd'].shape, layer), _full(cw['rope_r'].shape),
                pl.BlockSpec((None, 1, 2 * LANES), lambda i: (i, 0, 0)),
                _full(cw['gamma'].shape), _full(cw['ones_bd'].shape), _full(cw['lvl_small'].shape),
                _full(cw['lvl_big'].shape), _full(cw['gate_bias'].shape), _full(cw['triu'].shape)]
    st_shapes = [(bsz, 1, GROUP_W), (bsz, 8, GROUP_W), (bsz, BLK, LANES), (bsz, BLK, LANES),
                 (bsz, 2, LANES, LANES), (bsz, N_HEADS, LANES, LANES), (bsz, 1, LANES)]
    out_shape = ([jax.ShapeDtypeStruct((bsz, t, D_MODEL), BF16)]
                 + [jax.ShapeDtypeStruct(s, F32) for s in st_shapes])
    out_specs = [pl.BlockSpec((bsz, tm, D_MODEL), lambda i: (0, i, 0))] + [_full(s) for s in st_shapes]
    return pl.pallas_call(
        kern, grid=(nt,), in_specs=in_specs, out_specs=out_specs, out_shape=out_shape,
        scratch_shapes=[pltpu.VMEM((bsz, tm + 8, GROUP_W), F32), pltpu.VMEM((bsz * tm, N_IN), F32),
                        pltpu.VMEM((bsz * tm, D_MODEL), BF16)],
        compiler_params=pltpu.CompilerParams(dimension_semantics=("arbitrary",),
                                             vmem_limit_bytes=VMEM_LIMIT),
        name=f"prompt_mixers_l{layer}",
    )(h, h, cw['vecs'], cw['w_in_p'], cw['conv_w'], cw['wr_bd'], cw['wi_bd'], cw['rope_r'], cw['rope_base'],
      cw['gamma'], cw['ones_bd'], cw['lvl_small'], cw['lvl_big'], cw['gate_bias'], cw['triu'])


def _ffn_math(h, yb, p, g2, wout_ref, wup_ref, wdn_ref, wg_ref, wp_ref):
    h = h + _dot(yb, wout_ref[...])
    nb = _rms_rows(h, g2).astype(BF16)
    acc = h
    step = 1024
    for c in range(0, D_FF, step):
        f = jnp.maximum(_dot(nb, wup_ref[:, c:c + step]), 0.0)
        acc = acc + _dot((f * f).astype(BF16), wdn_ref[c:c + step, :])
    gate = _sigmoid(_dot(acc.astype(BF16), wg_ref[...]))
    return acc + gate * _dot(p.astype(BF16), wp_ref[...])


def _ffn_kernel(h_ref, y_ref, p_ref, pv_ref, wout_ref, wup_ref, wdn_ref, wg_ref, wp_ref, o_ref, *, layer):
    g2 = pv_ref[layer:layer + 1, V_G2:V_G2 + D_MODEL]
    o_ref[...] = _ffn_math(h_ref[...], y_ref[...], p_ref[...], g2, wout_ref, wup_ref, wdn_ref, wg_ref, wp_ref)


def _ffn_weight_specs(cw, layer):
    names = ['w_out', 'w_up', 'w_down', 'w_gate', 'w_proj']
    return [cw[n] for n in names], [_layer_block(cw[n].shape, layer, single_buffer=True) for n in names]


def _prompt_ffn(h2, y2, p3, cw, layer, tm):
    n = h2.shape[0]
    row = lambda w: pl.BlockSpec((tm, w), lambda i: (i, 0))
    ws, wspecs = _ffn_weight_specs(cw, layer)
    return pl.pallas_call(
        functools.partial(_ffn_kernel, layer=layer), grid=(n // tm,),
        in_specs=[row(D_MODEL), row(D_MODEL), pl.BlockSpec((None, tm, PLE_DIM), lambda i: (layer, i, 0)),
                  _full(cw['vecs'].shape)] + wspecs,
        out_specs=row(D_MODEL), out_shape=jax.ShapeDtypeStruct((n, D_MODEL), F32),
        compiler_params=pltpu.CompilerParams(dimension_semantics=("arbitrary",),
                                             vmem_limit_bytes=VMEM_LIMIT),
        name=f"prompt_ffn_l{layer}",
    )(h2, y2, p3, cw['vecs'], *ws)


def _sample_pre_kernel(x_ref, pv_ref, wt_ref, convw_ref, wr_ref, wi_ref, ones_ref, rope_ref, gcol_ref,
                       cols_ref, h0_ref, conv0_ref, n0_ref, m0_ref,
                       ya_ref, hn_ref, convn_ref, q_ref, kt_ref, vt_ref, fm_ref, nn_ref, mn_ref, *, layer):
    def vec(off, w):
        return pv_ref[layer:layer + 1, off:off + w]

    n = _rms_rows(x_ref[...], vec(V_G1, D_MODEL))
    nb = n.astype(BF16)
    n_t = n.T.astype(BF16)
    ones_bd = ones_ref[...]

    def proj(r0, cnt):
        return _dot_nt(nb, wt_ref[r0:r0 + cnt, :].astype(BF16))

    def proj_t(r0, cnt):
        return _dot(wt_ref[r0:r0 + cnt, :].astype(BF16), n_t)

    xa = proj(C_AX, GROUP_W)
    ga = proj(C_AG, GROUP_W)
    cw = convw_ref[layer]
    xc = (vec(V_CB, GROUP_W) + conv0_ref[0] * cw[0:1, :] + conv0_ref[1] * cw[1:2, :]
          + conv0_ref[2] * cw[2:3, :] + xa * cw[3:4, :])
    convn_ref[0] = conv0_ref[1]
    convn_ref[1] = conv0_ref[2]
    convn_ref[2] = xa
    a, bx = _rglru_gates(xc, wr_ref[...], wi_ref[...], vec(V_BR, GROUP_W), vec(V_BI, GROUP_W),
                         vec(V_LAM, GROUP_W))
    hn = a * h0_ref[...] + bx
    hn_ref[...] = hn
    ya_ref[...] = hn * _gelu_tanh(ga)

    rc, ra, rb = rope_ref[0:1, :], rope_ref[1:2, :], rope_ref[2:3, :]
    q = _seg_rms(proj(C_BQ, GROUP_W), vec(V_QG, GROUP_W), ones_bd)
    q_ref[...] = _rope(q, jnp.concatenate([rc, rc], 1), jnp.concatenate([ra, ra], 1),
                       jnp.concatenate([rb, rb], 1))
    k = _seg_rms(proj(C_BK, LANES), vec(V_KG, LANES), ones_bd[0:LANES, 0:LANES])
    kt_ref[...] = _rope(k, rc, ra, rb).T
    vt_ref[...] = proj(C_BV, LANES).T

    lb = _lb_from_gamma([gcol_ref[i] for i in range(gcol_ref.shape[0])], layer)
    cq, cf, ci, cg = (proj_t(C_CQ, GROUP_W), proj_t(C_CF, GROUP_W), proj_t(C_CI, GROUP_W),
                      proj_t(C_CG, GROUP_W))
    f = lb + (1.0 - lb) * _sigmoid(cf)
    fm_ref[FM_HF:FM_HF + GROUP_W, :] = f
    fm_ref[FM_HK:FM_HK + GROUP_W, :] = 1.0 - f
    fm_ref[FM_HQ:FM_HQ + GROUP_W, :] = cq * _sigmoid(cq)
    fm_ref[FM_HV:FM_HV + GROUP_W, :] = ci
    fm_ref[FM_CG:FM_CG + GROUP_W, :] = cg * _sigmoid(cg)

    dq, dk, dv, do = (proj_t(C_DQ, GROUP_W), proj_t(C_DK, GROUP_W), proj_t(C_DV, GROUP_W),
                      proj_t(C_DO, GROUP_W))
    g8 = proj_t(C_GATES, 2 * N_HEADS) + cols_ref[layer, R_GB:R_GB + 2 * N_HEADS, :]
    ig = g8[0:N_HEADS, :]
    lf = _log_sigmoid(g8)[N_HEADS:2 * N_HEADS, :]
    a_int = lf + m0_ref[...]
    m_new = jnp.maximum(a_int, ig)
    dec = jnp.exp(a_int - m_new)
    w = jnp.exp(ig - m_new)
    mn_ref[...] = m_new
    km = dk * (HEAD_DIM ** -0.5)
    dens = []
    for h in range(N_HEADS):
        sl = slice(h * HEAD_DIM, (h + 1) * HEAD_DIM)
        nn_h = dec[h:h + 1, :] * n0_ref[sl, :] + w[h:h + 1, :] * km[sl, :]
        nn_ref[sl, :] = nn_h
        dens.append(jnp.sum(dq[sl, :] * nn_h, axis=0, keepdims=True))
    pad = jnp.zeros((SUBLANES - N_HEADS, x_ref.shape[0]), F32)
    fm_ref[FM_MK:FM_MK + GROUP_W, :] = km
    fm_ref[FM_MQ:FM_MQ + GROUP_W, :] = dq
    fm_ref[FM_MV:FM_MV + GROUP_W, :] = dv
    fm_ref[FM_DO:FM_DO + GROUP_W, :] = _sigmoid(do)
    fm_ref[FM_DEC:FM_DEC + SUBLANES, :] = jnp.concatenate([dec, pad], 0)
    fm_ref[FM_W:FM_W + SUBLANES, :] = jnp.concatenate([w, pad], 0)
    fm_ref[FM_DEN:FM_DEN + SUBLANES, :] = jnp.concatenate(dens + [pad], 0)
    fm_ref[FM_ENEG:FM_ENEG + SUBLANES, :] = jnp.concatenate([jnp.exp(-m_new), pad], 0)


def _own_slab(ref, first_layer):
    if not first_layer:
        return ref
    ref[1:] = jnp.zeros((ref.shape[0] - 1,) + ref.shape[1:], ref.dtype)
    return ref.at[0]


def _sample_attn_kernel(q_ref, kn_ref, vn_ref, kc_ref, vc_ref, sink_ref, *rest, first_layer):
    ko_ref, vo_ref, o_ref = rest[-3:]
    ko_ref, vo_ref = _own_slab(ko_ref, first_layer), _own_slab(vo_ref, first_layer)
    sb = q_ref.shape[0]
    rows = 2 * HEAD_DIM
    lane = _iota((rows, WINDOW), 1)
    kn = kn_ref[...].reshape(rows, sb)
    vn = vn_ref[...].reshape(rows, sb)
    for s in range(sb):
        kt = pltpu.roll(kc_ref[s].reshape(rows, WINDOW), WINDOW - 1, 1)
        ko_ref[s] = jnp.where(lane == WINDOW - 1, kn[:, s:s + 1], kt).reshape(2, HEAD_DIM, WINDOW)
        vt = pltpu.roll(vc_ref[s].reshape(rows, WINDOW), WINDOW - 1, 1)
        vo_ref[s] = jnp.where(lane == WINDOW - 1, vn[:, s:s + 1], vt).reshape(2, HEAD_DIM, WINDOW)
    for kv in range(2):
        kk = ko_ref[:, kv].astype(BF16)
        vv = vo_ref[:, kv].astype(BF16)
        s_ = jnp.einsum('bqc,bcj->bqj', q_ref[:, kv].astype(BF16), kk,
                        preferred_element_type=F32) * (HEAD_DIM ** -0.5)
        sk = sink_ref[kv]
        mx = jnp.maximum(jnp.max(s_, axis=-1, keepdims=True), sk)
        p = jnp.exp(s_ - mx)
        den = jnp.sum(p, axis=-1, keepdims=True) + jnp.exp(sk - mx)
        o = jnp.einsum('bqj,bcj->bqc', p.astype(BF16), vv, preferred_element_type=F32)
        o_ref[:, kv] = o / den


def _sample_state_kernel(fm_ref, s_ref, c_ref, *rest, first_layer):
    so_ref, co_ref, oh_ref, om_ref = rest[-4:]
    so_ref, co_ref = _own_slab(so_ref, first_layer), _own_slab(co_ref, first_layer)
    h = pl.program_id(0)
    r0 = pl.multiple_of(h * HEAD_DIM, HEAD_DIM)
    hv = fm_ref[pl.ds(FM_HV + r0, HEAD_DIM), :]
    mv = fm_ref[pl.ds(FM_MV + r0, HEAD_DIM), :]
    dec = fm_ref[pl.ds(FM_DEC + h, 1), :]
    w = fm_ref[pl.ds(FM_W + h, 1), :]

    def body(d, carry):
        acc_h, acc_m = carry
        r = r0 + d
        s_new = fm_ref[pl.ds(FM_HF + r, 1), :] * s_ref[d] + fm_ref[pl.ds(FM_HK + r, 1), :] * hv
        so_ref[d] = s_new
        c_new = dec * c_ref[d] + (w * fm_ref[pl.ds(FM_MK + r, 1), :]) * mv
        co_ref[d] = c_new
        return (acc_h + fm_ref[pl.ds(FM_HQ + r, 1), :] * s_new,
                acc_m + fm_ref[pl.ds(FM_MQ + r, 1), :] * c_new)

    zero = jnp.zeros((HEAD_DIM, fm_ref.shape[1]), F32)
    acc_h, acc_m = lax.fori_loop(0, HEAD_DIM, body, (zero, zero), unroll=4)
    oh_ref[...] = acc_h
    om_ref[...] = acc_m


def _sample_post_kernel(h_ref, ya_ref, yb_ref, oh_ref, om_ref, fm_ref, cols_ref, p_ref, pv_ref,
                        wout_ref, wup_ref, wdn_ref, wg_ref, wp_ref, o_ref, *, layer):
    def head_rms(x):
        return x * lax.rsqrt(jnp.mean(x * x, axis=0, keepdims=True) + EPS)

    yc, yd = [], []
    for h in range(N_HEADS):
        sl = slice(h * HEAD_DIM, (h + 1) * HEAD_DIM)
        yc.append(head_rms(oh_ref[h]))
        den = fm_ref[FM_DEN + h:FM_DEN + h + 1, :]
        eneg = fm_ref[FM_ENEG + h:FM_ENEG + h + 1, :]
        yd.append(head_rms(om_ref[h] / jnp.maximum(jnp.abs(den), eneg)))
    yc = (jnp.concatenate(yc, 0) * cols_ref[layer, R_HG:R_HG + GROUP_W, :]
          * fm_ref[FM_CG:FM_CG + GROUP_W, :])
    yd = (jnp.concatenate(yd, 0) * cols_ref[layer, R_MG:R_MG + GROUP_W, :]
          * fm_ref[FM_DO:FM_DO + GROUP_W, :])
    y = jnp.concatenate([ya_ref[...], yb_ref[...], yc.T, yd.T], axis=1).astype(BF16)
    g2 = pv_ref[layer:layer + 1, V_G2:V_G2 + D_MODEL]
    o_ref[...] = _ffn_math(h_ref[...], y, p_ref[...], g2, wout_ref, wup_ref, wdn_ref, wg_ref, wp_ref)


def _call_full(kern, args, specs, out_shape, name):
    specs = [(_full(a.shape) if s is None else s) for a, s in zip(args, specs)]
    return pl.pallas_call(
        kern, grid=(1,), in_specs=specs,
        out_specs=tuple(_full(s.shape) for s in out_shape), out_shape=tuple(out_shape),
        compiler_params=pltpu.CompilerParams(dimension_semantics=("arbitrary",),
                                             vmem_limit_bytes=VMEM_LIMIT),
        name=name,
    )(*args)


def _sample_layer(h, sv, prev, cw, layer):
    nsm = h.shape[0]
    depth = cw['vecs'].shape[0]
    sd = lambda *shape: jax.ShapeDtypeStruct(shape, F32)
    g, l = sd(nsm, GROUP_W), sd(LANES, nsm)

    args = [h, cw['vecs'], cw['w_in_t'], cw['conv_w'], cw['wr_bd'], cw['wi_bd'], cw['ones_bd'],
            cw['rope_s'], cw['gamma_col'], cw['cols'], sv['h'], sv['conv'], sv['n'], sv['m']]
    specs = [None, None, _layer_block(cw['w_in_t'].shape, layer, single_buffer=True), None,
             _layer_block(cw['wr_bd'].shape, layer), _layer_block(cw['wi_bd'].shape, layer), None,
             None, None, None, _layer_block(sv['h'].shape, layer), _layer_block(sv['conv'].shape, layer),
             _layer_block(sv['n'].shape, layer), _layer_block(sv['m'].shape, layer)]
    outs = [g, g, sd(CONV_W - 1, nsm, GROUP_W), g, l, l, sd(FM_ROWS, nsm), sd(GROUP_W, nsm),
            sd(N_HEADS, nsm)]
    ya, hn, convn, q, kt, vt, fm, nn, mn = _call_full(
        functools.partial(_sample_pre_kernel, layer=layer), args, specs, outs, f"sample_pre_l{layer}")

    sb = 16
    nblk = nsm // sb
    q3 = jnp.pad(q.reshape(nsm, 2, 2, HEAD_DIM), ((0, 0), (0, 0), (0, SUBLANES - 2), (0, 0)))
    to_blocks = lambda a: a.reshape(2, HEAD_DIM, nblk, sb).transpose(2, 0, 1, 3)
    cshape = (depth, nsm, 2, HEAD_DIM, WINDOW)
    cspec = pl.BlockSpec((None, sb, 2, HEAD_DIM, WINDOW), lambda i: (layer, i, 0, 0, 0))
    nspec = pl.BlockSpec((None, 2, HEAD_DIM, sb), lambda i: (i, 0, 0, 0))
    qspec = pl.BlockSpec((sb, 2, SUBLANES, HEAD_DIM), lambda i: (i, 0, 0, 0))
    any_spec = pl.BlockSpec(memory_space=pl.ANY)
    cout = (pl.BlockSpec((depth, sb, 2, HEAD_DIM, WINDOW), lambda i: (0, i, 0, 0, 0)) if prev is None
            else cspec)
    chain = [] if prev is None else [prev['k'], prev['v']]
    ko, vo, o3 = pl.pallas_call(
        functools.partial(_sample_attn_kernel, first_layer=prev is None), grid=(nblk,),
        in_specs=[qspec, nspec, nspec, cspec, cspec, _layer_block(cw['sinks8'].shape, layer)]
        + [any_spec] * len(chain),
        out_specs=(cout, cout, qspec),
        out_shape=(sd(*cshape), sd(*cshape), sd(nsm, 2, SUBLANES, HEAD_DIM)),
        input_output_aliases={6 + i: i for i in range(len(chain))},
        compiler_params=pltpu.CompilerParams(dimension_semantics=("arbitrary",),
                                             vmem_limit_bytes=VMEM_LIMIT),
        name=f"sample_attn_l{layer}",
    )(q3, to_blocks(kt), to_blocks(vt), sv['k'], sv['v'], cw['sinks8'], *chain)
    yb = o3[:, :, 0:2, :].reshape(nsm, GROUP_W)

    sshape = (depth, N_HEADS, HEAD_DIM, HEAD_DIM, nsm)
    sspec = pl.BlockSpec((None, None, HEAD_DIM, HEAD_DIM, nsm), lambda i: (layer, i, 0, 0, 0))
    ospec = pl.BlockSpec((None, HEAD_DIM, nsm), lambda i: (i, 0, 0))
    sout = (pl.BlockSpec((depth, None, HEAD_DIM, HEAD_DIM, nsm), lambda i: (0, i, 0, 0, 0))
            if prev is None else sspec)
    chain = [] if prev is None else [prev['s'], prev['c']]
    so, co, oh, om = pl.pallas_call(
        functools.partial(_sample_state_kernel, first_layer=prev is None), grid=(N_HEADS,),
        in_specs=[_full(fm.shape), sspec, sspec] + [any_spec] * len(chain),
        out_specs=(sout, sout, ospec, ospec),
        out_shape=(sd(*sshape), sd(*sshape), sd(N_HEADS, HEAD_DIM, nsm), sd(N_HEADS, HEAD_DIM, nsm)),
        input_output_aliases={3 + i: i for i in range(len(chain))},
        compiler_params=pltpu.CompilerParams(dimension_semantics=("arbitrary",),
                                             vmem_limit_bytes=VMEM_LIMIT),
        name=f"sample_state_l{layer}",
    )(fm, sv['s'], sv['c'], *chain)

    ws, wspecs = _ffn_weight_specs(cw, layer)
    (h_new,) = _call_full(
        functools.partial(_sample_post_kernel, layer=layer),
        [h, ya, yb, oh, om, fm, cw['cols'], sv['p'], cw['vecs']] + ws,
        [None] * 7 + [_layer_block(sv['p'].shape, layer), None] + wspecs,
        [sd(nsm, D_MODEL)], f"sample_post_l{layer}")
    small = (hn, convn, nn, mn)
    big = {'k': ko, 'v': vo, 's': so, 'c': co}
    return h_new, small, big


def _block_diag_all(w):
    depth = w.shape[0]
    rows = w.reshape(depth, GROUP_W, HEAD_DIM)
    idx = jnp.arange(GROUP_W) // HEAD_DIM
    mask = idx[:, None] == idx[None, :]
    return jnp.where(mask[None], jnp.tile(rows, (1, 1, N_HEADS)), 0.0)


def _rope_lane_freq():
    half = ROT_DIM // 2
    inv = jnp.power(ROPE_THETA, -jnp.arange(half, dtype=F32) * (2.0 / ROT_DIM))
    dd = jnp.arange(LANES) % HEAD_DIM
    freq = jnp.where(dd < ROT_DIM, inv[dd % half], 0.0)
    m_a = (dd < half).astype(F32)
    m_b = ((dd >= half) & (dd < ROT_DIM)).astype(F32)
    return freq, m_a, m_b


def _rope_tables(pos):
    freq, m_a, m_b = _rope_lane_freq()
    ang = pos.astype(F32)[:, None] * freq[None, :]
    cos, sin = jnp.cos(ang), jnp.sin(ang)
    return cos, -sin * m_a, sin * m_b


def _rope_split_tables(t, tm):
    freq, m_a, m_b = _rope_lane_freq()
    ang_r = jnp.arange(tm, dtype=F32)[:, None] * freq[None, :]
    cr, sr = jnp.cos(ang_r), jnp.sin(ang_r)
    rope_r = jnp.stack([cr, sr, -cr * m_a, -sr * m_a, cr * m_b, sr * m_b])
    ang_b = (jnp.arange(t // tm) * tm).astype(F32)[:, None] * freq[None, :]
    rope_base = jnp.concatenate([jnp.cos(ang_b), jnp.sin(ang_b)], 1)[:, None, :]
    return rope_r, rope_base


def _hgrn_level_masks():
    t = jnp.arange(BLK)[:, None]
    s = jnp.arange(BLK)[None, :]
    small = [t == s]
    big = []
    for lev in range(1, 8):
        half = 1 << (lev - 1)
        own = ((t >> lev) == (s >> lev)) & ((t & half) != 0) & ((s & half) == 0)
        if lev < 4:
            small.append(own)
        else:
            rows = jnp.concatenate([jnp.arange(m, m + half) for m in range(half, BLK, 2 * half)])
            big.append(own[rows])
    return jnp.stack(small).astype(F32), jnp.stack(big).astype(F32)


def _pad_last(v, width):
    return jnp.pad(v, ((0, 0), (0, width - v.shape[-1])))


def _common(w, t, tm_mix, past_len, nsm):
    depth = w['w_in'].shape[0]
    tile = lambda v, n: jnp.tile(v, (1, n))
    vecs = jnp.concatenate([
        w['norm1_g'], w['norm2_g'], w['conv_b'], w['lru_br'], w['lru_bi'], w['lru_lam'],
        tile(w['q_norm_g'], N_HEADS), tile(w['k_norm_g'], 2), _pad_last(w['attn_sinks'], LANES),
        tile(w['hgrn_norm_g'], N_HEADS), tile(w['mlstm_norm_g'], N_HEADS),
        _pad_last(w['mlstm_ib'], LANES), _pad_last(w['mlstm_fb'], LANES)], axis=1)
    cols = jnp.concatenate([tile(w['hgrn_norm_g'], N_HEADS), tile(w['mlstm_norm_g'], N_HEADS),
                            w['mlstm_ib'], w['mlstm_fb']], axis=1)
    w_in = w['w_in']
    gate_bias = jnp.concatenate([w['mlstm_ib'], w['mlstm_fb']], axis=1)
    lvl_small, lvl_big = _hgrn_level_masks()
    sinks = w['attn_sinks']
    z2 = jnp.zeros((depth, 2, SUBLANES - 2), F32)
    idx = jnp.arange(GROUP_W)
    rope_r, rope_base = _rope_split_tables(t, tm_mix)
    return {
        'vecs': vecs,
        'cols': jnp.broadcast_to(cols[:, :, None], cols.shape + (nsm,)),
        'gamma': w['hgrn_gamma'],
        'gamma_col': jnp.broadcast_to(w['hgrn_gamma'][:, :, None], w['hgrn_gamma'].shape + (nsm,)),
        'w_in_p': jnp.pad(w_in, ((0, 0), (0, 0), (0, N_IN - D_IN))).astype(BF16),
        'lvl_small': lvl_small, 'lvl_big': lvl_big,
        'gate_bias': jnp.broadcast_to(gate_bias[:, :, None], gate_bias.shape + (tm_mix,)),
        'triu': (jnp.arange(tm_mix)[:, None] <= jnp.arange(tm_mix)[None, :]).astype(F32),
        'w_in_t': jnp.swapaxes(w_in, 1, 2),
        'conv_w': w['conv_w'],
        'wr_bd': _block_diag_all(w['lru_wr']).astype(BF16),
        'wi_bd': _block_diag_all(w['lru_wi']).astype(BF16),
        'ones_bd': (idx[:, None] // HEAD_DIM == idx[None, :] // HEAD_DIM).astype(BF16),
        'sinks8': jnp.concatenate([sinks.reshape(depth, 2, 2), z2], axis=2)[..., None],
        'rope_r': rope_r, 'rope_base': rope_base,
        'rope_s': jnp.concatenate(_rope_tables(past_len + jnp.arange(1)), axis=0),
        'w_out': w['w_out'].astype(BF16), 'w_up': w['w_up'].astype(BF16),
        'w_down': w['w_down'].astype(BF16), 'w_gate': w['w_ple_gate'].astype(BF16),
        'w_proj': w['w_ple_proj'].astype(BF16),
    }


def _run(x_prompt, x_sample, p_prompt, p_sample, sample_state, w, past_len, tm_mix=256, tm_ffn=512):
    depth = w['w_in'].shape[0]
    bsz, t, _ = x_prompt.shape
    nsm = x_sample.shape[0]
    cw = _common(w, t, tm_mix, past_len, nsm)
    tm_ffn = min(tm_ffn, bsz * t)
    h0, conv0, kc, vc, s0, c0, n0, m0 = sample_state
    sv = {'h': h0, 'conv': jnp.transpose(conv0, (0, 2, 1, 3)),
          'k': jnp.transpose(kc, (0, 1, 3, 4, 2)), 'v': jnp.transpose(vc, (0, 1, 3, 4, 2)),
          's': jnp.transpose(s0, (0, 2, 3, 4, 1)), 'c': jnp.transpose(c0, (0, 2, 3, 4, 1)),
          'n': jnp.transpose(n0, (0, 2, 3, 1)).reshape(depth, GROUP_W, nsm),
          'm': jnp.transpose(m0, (0, 2, 1)), 'p': p_sample.reshape(depth, nsm, PLE_DIM)}
    p3 = p_prompt.reshape(depth, bsz * t, PLE_DIM)

    hp = x_prompt
    hs = x_sample.reshape(nsm, D_MODEL)
    p_states, s_small, big = [], [], None
    for l in range(depth):
        y, hl, conv, kst, vst, sst, cst, mst = _prompt_mixers(hp, cw, l, tm_mix)
        hp = _prompt_ffn(hp.reshape(bsz * t, D_MODEL), y.reshape(bsz * t, D_MODEL), p3, cw, l,
                         tm_ffn).reshape(bsz, t, D_MODEL)
        s_hgrn = jnp.swapaxes(sst, -1, -2)
        s_hgrn = jnp.stack([s_hgrn[:, 0, 0:64, 0:64], s_hgrn[:, 0, 64:128, 64:128],
                            s_hgrn[:, 1, 0:64, 0:64], s_hgrn[:, 1, 64:128, 64:128]], 1)
        c_rows = jnp.stack([cst[:, 0, 0:64], cst[:, 1, 64:128], cst[:, 2, 0:64], cst[:, 3, 64:128]], 1)
        p_states.append((hl[:, 0], conv[:, 8 - (CONV_W - 1):], kst.reshape(bsz, WINDOW, 2, HEAD_DIM),
                         vst.reshape(bsz, WINDOW, 2, HEAD_DIM), s_hgrn, c_rows[..., 0:HEAD_DIM],
                         c_rows[..., HEAD_DIM], mst[:, 0, 0:N_HEADS]))
        hs, small, big = _sample_layer(hs, sv, big, cw, l)
        s_small.append(small)
    stack = lambda sts, i: jnp.stack([s[i] for s in sts])
    prompt_out = tuple(stack(p_states, i) for i in range(8))
    hn, convn, nn, mn = (stack(s_small, i) for i in range(4))
    sample_out = (hn, jnp.transpose(convn, (0, 2, 1, 3)),
                  jnp.transpose(big['k'], (0, 1, 4, 2, 3)), jnp.transpose(big['v'], (0, 1, 4, 2, 3)),
                  jnp.transpose(big['s'], (0, 4, 1, 2, 3)), jnp.transpose(big['c'], (0, 4, 1, 2, 3)),
                  jnp.transpose(nn.reshape(depth, N_HEADS, HEAD_DIM, nsm), (0, 3, 1, 2)),
                  jnp.transpose(mn, (0, 2, 1)))
    return (hp, hs.reshape(x_sample.shape)) + prompt_out + sample_out


def kernel(x_prompt, x_sample, p_prompt, p_sample, state_rglru_h, state_rglru_conv, cache_swa_k, cache_swa_v, state_hgrn_s, state_mlstm_c, state_mlstm_n, state_mlstm_m, norm1_g, w_in, conv_w, conv_b, lru_wr, lru_br, lru_wi, lru_bi, lru_lam, q_norm_g, k_norm_g, attn_sinks, hgrn_gamma, hgrn_norm_g, mlstm_ib, mlstm_fb, mlstm_norm_g, w_out, norm2_g, w_up, w_down, w_ple_gate, w_ple_proj):
    w = {'norm1_g': norm1_g, 'w_in': w_in, 'conv_w': conv_w, 'conv_b': conv_b, 'lru_wr': lru_wr,
         'lru_br': lru_br, 'lru_wi': lru_wi, 'lru_bi': lru_bi, 'lru_lam': lru_lam, 'q_norm_g': q_norm_g,
         'k_norm_g': k_norm_g, 'attn_sinks': attn_sinks, 'hgrn_gamma': hgrn_gamma,
         'hgrn_norm_g': hgrn_norm_g, 'mlstm_ib': mlstm_ib, 'mlstm_fb': mlstm_fb,
         'mlstm_norm_g': mlstm_norm_g, 'w_out': w_out, 'norm2_g': norm2_g, 'w_up': w_up,
         'w_down': w_down, 'w_ple_gate': w_ple_gate, 'w_ple_proj': w_ple_proj}
    st = (state_rglru_h, state_rglru_conv, cache_swa_k, cache_swa_v, state_hgrn_s, state_mlstm_c,
          state_mlstm_n, state_mlstm_m)
    past_len = 8192
    return _run(x_prompt, x_sample, p_prompt, p_sample, st, w, past_len)
```

```python
import functools
import types

import jax
import jax.numpy as jnp
import numpy as np
from jax import lax
from jax.experimental import pallas as pl
from jax.experimental.pallas import tpu as pltpu

F32 = jnp.float32
BF16 = jnp.bfloat16

D_MODEL = 1024
GROUP_W = 256
HEAD_DIM = 64
N_HEADS = 4
EPS = 1e-6
NEG_BIG = -1e30
LRU_C = 8.0
CONV_W = 4
ROT_DIM = 16
ROPE_THETA = 500000.0
WINDOW = 128
D_FF = 4096
PLE_DIM = 256
LANES = 128
SUBLANES = 8
BLK = 128

C_AX, C_AG, C_BQ, C_BK, C_BV = 0, 256, 512, 768, 896
C_CQ, C_CF, C_CI, C_CG = 1024, 1280, 1536, 1792
C_DQ, C_DK, C_DV, C_DO = 2048, 2304, 2560, 2816
C_GATES = 3072
D_IN = 3080
N_IN = 3200

V_G1, V_G2, V_CB, V_BR, V_BI, V_LAM = 0, 1024, 2048, 2304, 2560, 2816
V_QG, V_KG, V_SINK, V_HG, V_MG, V_IB, V_FB = 3072, 3328, 3456, 3584, 3840, 4096, 4224
N_VEC = 4352

R_HG, R_MG, R_GB = 0, 256, 512
N_COL = 520

FM_HF, FM_HK, FM_HQ, FM_HV, FM_CG = 0, 256, 512, 768, 1024
FM_MK, FM_MQ, FM_MV, FM_DO = 1280, 1536, 1792, 2048
FM_DEC, FM_W, FM_DEN, FM_ENEG = 2304, 2312, 2320, 2328
FM_ROWS = 2336

VMEM_LIMIT = 56 * 1024 * 1024


def _dot(a, b):
    return jnp.dot(a, b, preferred_element_type=F32)


def _dot_nt(a, b):
    return lax.dot_general(a, b, (((1,), (1,)), ((), ())), preferred_element_type=F32)


def _sigmoid(x):
    return jax.nn.sigmoid(x)


def _gelu_tanh(x):
    return 0.5 * x * (1.0 + jnp.tanh(0.7978845608028654 * (x + 0.044715 * (x * x * x))))


def _log_sigmoid(x):
    return jnp.minimum(x, 0.0) - jnp.log1p(jnp.exp(-jnp.abs(x)))


def _softplus(x):
    return jnp.maximum(x, 0.0) + jnp.log1p(jnp.exp(-jnp.abs(x)))


def _rms_rows(x, g):
    return x * lax.rsqrt(jnp.mean(x * x, axis=-1, keepdims=True) + EPS) * g


def _seg_mean_sq(x, ones_bd):
    sq = x * x
    hi = sq.astype(BF16)
    lo = (sq - hi.astype(F32)).astype(BF16)
    return (_dot(hi, ones_bd) + _dot(lo, ones_bd)) * (1.0 / HEAD_DIM)


def _seg_rms(x, g, ones_bd):
    return x * lax.rsqrt(_seg_mean_sq(x, ones_bd) + EPS) * g


def _rope(x, c, sa, sb):
    w = x.shape[1]
    up = pltpu.roll(x, w - ROT_DIM // 2, 1)
    dn = pltpu.roll(x, ROT_DIM // 2, 1)
    return x * c + up * sa + dn * sb


def _lb_from_gamma(gammas, layer):
    mx = functools.reduce(jnp.maximum, gammas)
    e = [jnp.exp(g - mx) for g in gammas]
    tot = functools.reduce(lambda a, b: a + b, e)
    lb = jnp.zeros_like(tot)
    for i in range(1, layer + 1):
        lb = lb + e[i] / tot
    return lb


def _iota(shape, axis):
    return lax.broadcasted_iota(jnp.int32, shape, axis)


def _rglru_gates(xc, wr, wi, br, bi, lam):
    xcb = xc.astype(BF16)
    r = _sigmoid(_dot(xcb, wr) + br)
    ig = _sigmoid(_dot(xcb, wi) + bi)
    log_a = (-LRU_C) * r * _softplus(-lam)
    a = jnp.exp(log_a)
    y = 1.0 - a * a
    root = jnp.where(y > 0.0, y * lax.rsqrt(y), 0.0)
    return a, root * (ig * xc)


def _swa_block(q, k, v, k_prev, v_prev, sink, first):
    kk = jnp.concatenate([k_prev, k], axis=0)
    vv = jnp.concatenate([v_prev, v], axis=0)
    qi = _iota((BLK, 2 * BLK), 0)
    kj = _iota((BLK, 2 * BLK), 1)
    valid = (kj > qi) & (kj <= qi + WINDOW) & ((kj >= BLK) | jnp.logical_not(first))
    lane = _iota((2 * BLK, LANES), 1)
    one_col = jnp.where(lane == HEAD_DIM, 1.0, 0.0)
    vaug = (jnp.where(lane < HEAD_DIM, vv, one_col).astype(BF16),
            jnp.where(lane < HEAD_DIM, pltpu.roll(vv, HEAD_DIM, 1), one_col).astype(BF16))
    qs = q * (HEAD_DIM ** -0.5)
    outs = []
    for h in range(N_HEADS):
        kv = h // 2
        qh = qs[:, h * HEAD_DIM:(h + 1) * HEAD_DIM].astype(BF16)
        kh = kk[:, kv * HEAD_DIM:(kv + 1) * HEAD_DIM].astype(BF16)
        s = jnp.where(valid, _dot_nt(qh, kh), NEG_BIG)
        sk = sink[:, h:h + 1]
        mx = jnp.maximum(jnp.max(s, axis=-1, keepdims=True), sk)
        p = jnp.exp(s - mx)
        o = _dot(p.astype(BF16), vaug[kv])
        den = o[:, HEAD_DIM:HEAD_DIM + 1] + jnp.exp(sk - mx)
        outs.append(o[:, 0:HEAD_DIM] / den)
    return jnp.concatenate(outs, axis=1)


def _head_masks(rows, dtype):
    lane = _iota((rows, LANES), 1)
    return (jnp.where(lane < HEAD_DIM, 1.0, 0.0).astype(dtype),
            jnp.where(lane >= HEAD_DIM, 1.0, 0.0).astype(dtype))


def _pair_scores(qe, ke_b, hm):
    n = qe.shape[0]
    if n != hm[0].shape[0]:
        hm = _head_masks(n, BF16)
    res = []
    for p in range(2):
        sl = slice(p * LANES, (p + 1) * LANES)
        qb = qe[:, sl].astype(BF16)
        lhs = jnp.concatenate([qb * hm[0], qb * hm[1]], axis=0)
        pr = _dot_nt(lhs, ke_b[:, sl])
        res += [pr[:n], pr[n:]]
    return res


def _hgrn_block(cq, cf, ci, cg, lb, hg, sst_ref, ones_bd, msmall_ref, mbig_ref, hm, sub):
    q = cq * _sigmoid(cq)
    f = lb + (1.0 - lb) * _sigmoid(cf)
    logf = jnp.log(f)
    k = 1.0 - f

    att = [s_ * msmall_ref[0] for s_ in _pair_scores(q, k.astype(BF16), hm)]

    c = logf
    tot = logf
    for lev in range(1, 4):
        half = 1 << (lev - 1)
        right = (sub & half) != 0
        tot_l = pltpu.roll(tot, half, 0)
        tot_r = pltpu.roll(tot, BLK - half, 0)
        e = jnp.exp(jnp.where(right, c, tot - c))
        sc = _pair_scores(q * e, (k * e).astype(BF16), hm)
        m = msmall_ref[lev]
        att = [a_ + s_ * m for a_, s_ in zip(att, sc)]
        c = c + jnp.where(right, tot_l, 0.0)
        tot = tot + jnp.where(right, tot_l, tot_r)

    pieces, carry = [], None
    for g in range(BLK // SUBLANES):
        rows = slice(g * SUBLANES, (g + 1) * SUBLANES)
        pieces.append(c[rows] if carry is None else c[rows] + carry)
        t_g = tot[g * SUBLANES:g * SUBLANES + 1]
        carry = t_g if carry is None else carry + t_g
    yield
    b = jnp.concatenate(pieces, axis=0)
    btot = carry

    for lev in range(4, 8):
        half = 1 << (lev - 1)
        nblk = BLK // (2 * half)
        qr, kf = [], []
        for i in range(nblk):
            lo = i * 2 * half
            mid = lo + half
            bref = b[mid - 1:mid]
            qr.append(q[mid:mid + half] * jnp.exp(b[mid:mid + half] - bref))
            kf.append(k[lo:mid] * jnp.exp(bref - b[lo:mid]))
            kf.append(jnp.zeros((half, GROUP_W), F32))
        sc = _pair_scores(jnp.concatenate(qr, axis=0), jnp.concatenate(kf, axis=0).astype(BF16), hm)
        m = mbig_ref[lev - 4]
        zero = jnp.zeros((half, BLK), F32)
        new = []
        for a_, s_ in zip(att, sc):
            u = s_ * m
            parts = []
            for i in range(nblk):
                parts += [zero, u[i * half:(i + 1) * half]]
            new.append(a_ + jnp.concatenate(parts, axis=0))
        att = new

    yield
    qe = (q * jnp.exp(b)).astype(BF16)
    ke = (k * jnp.exp(btot - b)).astype(BF16)
    etot = jnp.exp(btot)
    row_l = _iota((BLK, BLK), 0)
    col_l = _iota((BLK, BLK), 1)
    same_head = (row_l >= HEAD_DIM) == (col_l >= HEAD_DIM)
    outs = []
    for p in range(2):
        sl = slice(p * LANES, (p + 1) * LANES)
        st = sst_ref[p]
        vp = ci[:, sl]
        vb = vp.astype(BF16)
        a2 = jnp.concatenate([att[2 * p], att[2 * p + 1]], axis=1).astype(BF16)
        v2 = jnp.concatenate([vb * hm[0], vb * hm[1]], axis=0)
        o = _dot_nt(qe[:, sl], st.astype(BF16)) + _dot(a2, v2)
        upd = _dot(vp.T.astype(BF16), ke[:, sl])
        sst_ref[p] = st * etot[:, sl] + jnp.where(same_head, upd, 0.0)
        outs.append(o)
    o = jnp.concatenate(outs, axis=1)
    return _seg_rms(o, hg, ones_bd) * (cg * _sigmoid(cg))


def _mlstm_tile_gates(gcols, gb, m0, triu):
    tm = gcols.shape[0]
    gt = gcols.T[0:SUBLANES, :] + gb
    lf = _log_sigmoid(gt)
    hi = lf.astype(BF16)
    r1 = lf - hi.astype(F32)
    mid = r1.astype(BF16)
    lo = (r1 - mid.astype(F32)).astype(BF16)
    parts = jnp.concatenate([hi.astype(F32), mid.astype(F32), lo.astype(F32)], axis=0)
    cs = _dot(parts, triu)
    fcum = cs[0:SUBLANES] + cs[SUBLANES:2 * SUBLANES] + cs[2 * SUBLANES:3 * SUBLANES]
    fcum = pltpu.roll(fcum, N_HEADS, 0)
    g = gt - fcum
    pad = jnp.zeros((LANES - 2 * SUBLANES, tm), F32)
    cols = jnp.concatenate([g, fcum, pad], axis=0).T
    f_c = pltpu.roll(cols, LANES - SUBLANES, 1)
    sub = _iota((tm, LANES), 0) & (SUBLANES - 1)
    cm = cols
    s = 1
    while s < SUBLANES:
        cm = jnp.maximum(cm, jnp.where(sub >= s, pltpu.roll(cm, s, 0), NEG_BIG))
        s *= 2
    carry = m0
    ms = []
    for grp in range(tm // SUBLANES):
        m_g = jnp.maximum(cm[grp * SUBLANES:(grp + 1) * SUBLANES], carry)
        ms.append(m_g)
        carry = m_g[SUBLANES - 1:SUBLANES]
    m_c = jnp.concatenate(ms, axis=0)
    eneg_c = jnp.exp(-(f_c + m_c))
    m_new = f_c[tm - 1:tm] + carry
    return g, cols, m_c, eneg_c, m_new


def _mlstm_block(dq, dk, dv, do, g_rows, g_c, m_c, eneg_c, m_prev, mg, cst_ref, ones_bd, hm, hmf):
    k = dk * (HEAD_DIM ** -0.5)
    m_end = m_c[BLK - 1:BLK]
    inter = jnp.exp(m_prev - m_c)
    wend = jnp.exp(g_c - m_end)
    dec0 = jnp.exp(m_prev - m_end)
    mrun_c = lambda h: m_c[:, h:h + 1]
    inter_c = lambda h: inter[:, h:h + 1]
    eneg_c_ = lambda h: eneg_c[:, h:h + 1]
    wend_c = lambda h: wend[:, h:h + 1]
    g = g_rows
    lane = _iota((BLK, LANES), 1)
    low = lane < HEAD_DIM
    one_col = jnp.where(lane == HEAD_DIM, 1.0, 0.0)
    causal = _iota((BLK, BLK), 1) <= _iota((BLK, BLK), 0)

    outs = []
    for p in range(2):
        sl = slice(p * LANES, (p + 1) * LANES)
        kp, vp = k[:, sl], dv[:, sl]
        qb = dq[:, sl].astype(BF16)
        qm = [qb * hm[0], qb * hm[1]]
        sc2 = _dot_nt(jnp.concatenate(qm, axis=0), kp.astype(BF16))
        v_sw = pltpu.roll(vp, HEAD_DIM, 1)
        hv = []
        for hh in range(2):
            h = 2 * p + hh
            sc = sc2[hh * BLK:(hh + 1) * BLK]
            w = jnp.exp(jnp.where(causal, g[h:h + 1, :] - mrun_c(h), NEG_BIG))
            sw = (sc * w).astype(BF16)
            vaug = jnp.where(low, vp if hh == 0 else v_sw, one_col).astype(BF16)
            cst = cst_ref[h]
            nd = inter_c(h) * _dot(qm[hh], cst.astype(BF16)) + _dot(sw, vaug)
            den = nd[:, HEAD_DIM:HEAD_DIM + 1]
            hv.append(nd / jnp.maximum(jnp.abs(den), eneg_c_(h)))
            kw = kp * (wend_c(h) * hmf[hh])
            cst_ref[h] = dec0[:, h:h + 1] * cst + _dot(kw.T.astype(BF16), vaug)
        outs.append(jnp.where(low, hv[0], pltpu.roll(hv[1], HEAD_DIM, 1)))
        yield
    hcat = jnp.concatenate(outs, axis=1)
    return _seg_rms(hcat, mg, ones_bd) * _sigmoid(do)


def _mixer_kernel(hc_ref, hn_ref, pv_ref, win_ref, convw_ref, wr_ref, wi_ref, rr_ref, rbase_ref, gamma_ref,
                  ones_ref, msmall_ref, mbig_ref, gb_ref, triu_ref, *rest, layer, tm):
    y_ref, hl_ref, conv_ref, kst_ref, vst_ref, sst_ref, cst_ref, mst_ref, u_scr, nb_scr = rest[-10:]
    t = pl.program_id(0)
    states = (hl_ref, conv_ref, kst_ref, vst_ref, sst_ref, cst_ref, mst_ref)

    @pl.when(t == 0)
    def _init():
        for ref in states:
            ref[...] = jnp.zeros_like(ref)

    if layer == 0:
        hl_ref, conv_ref, kst_ref, vst_ref, sst_ref, cst_ref, mst_ref = (ref.at[0] for ref in states)

    bsz = hc_ref.shape[0]
    g1 = pv_ref[layer:layer + 1, V_G1:V_G1 + D_MODEL]
    cols = (((C_AX, C_BQ - C_AX),), ((C_BQ, C_CQ - C_BQ),),
            ((C_CQ, 2 * GROUP_W), (C_CI, 2 * GROUP_W)),
            ((C_DQ, 2 * GROUP_W), (C_DV, N_IN - C_DV)))

    def normalise(src_ref):
        for b in range(bsz):
            nb_scr[b * tm:(b + 1) * tm, :] = _rms_rows(src_ref[b], g1).astype(BF16)

    def project(group_ids):
        for gid in group_ids:
            for c0, w in cols[gid]:
                u_scr[:, c0:c0 + w] = _dot(nb_scr[...], win_ref[:, c0:c0 + w])
                yield

    phases = ((0, 1), (2, 3))

    @pl.when(t == 0)
    def _prologue():
        normalise(hc_ref)
        _round_robin([(lambda: True, project(phases[0]))])

    def vec(off, w):
        return pv_ref[layer:layer + 1, off:off + w]

    ctxs = []
    for b in range(bsz):
        proj = functools.partial(lambda c0, w, b: u_scr[b * tm:(b + 1) * tm, c0:c0 + w], b=b)
        ctxs.append(types.SimpleNamespace(
            t=t, proj=proj, vec=vec, layer=layer, tm=tm, ones_bd=ones_ref[...], convw_ref=convw_ref,
            wr_ref=wr_ref, wi_ref=wi_ref, rr_ref=rr_ref, rbase_ref=rbase_ref, gamma_ref=gamma_ref,
            msmall_ref=msmall_ref, mbig_ref=mbig_ref, gb_ref=gb_ref, triu_ref=triu_ref, y_ref=y_ref.at[b],
            hl_ref=hl_ref.at[b], conv_ref=conv_ref.at[b], kst_ref=kst_ref.at[b], vst_ref=vst_ref.at[b],
            sst_ref=sst_ref.at[b], cst_ref=cst_ref.at[b], mst_ref=mst_ref.at[b],
            hm=_head_masks(BLK, BF16), hmf=_head_masks(1, F32)))
    groups = (_group_a, _group_b, _group_c, _group_d)
    for pi, phase in enumerate(phases):
        if pi == 1:
            normalise(hn_ref)
        tasks = [(lambda: True, project(phases[1 - pi]))]
        for c in ctxs:
            for gid in phase:
                tasks.append((lambda: True, groups[gid](c)))
        _round_robin(tasks)


def _round_robin(tasks):
    tasks = list(tasks)
    while tasks:
        for task in list(tasks):
            ready, gen = task
            if not ready():
                continue
            try:
                next(gen)
            except StopIteration:
                tasks.remove(task)


def _group_a(c):
    proj, vec, tm, layer = c.proj, c.vec, c.tm, c.layer
    conv_ref, convw_ref, wr_ref, wi_ref, hl_ref, y_ref = (c.conv_ref, c.convw_ref, c.wr_ref, c.wi_ref,
                                                          c.hl_ref, c.y_ref)
    xa = proj(C_AX, GROUP_W)
    ga = proj(C_AG, GROUP_W)
    cw = convw_ref[layer]
    tail = conv_ref[...]
    sub8 = _iota((SUBLANES, GROUP_W), 0)

    def shifted(j):
        r = pltpu.roll(xa, j, 0)
        head = jnp.where(sub8 < j, pltpu.roll(tail, j, 0), r[0:SUBLANES])
        return jnp.concatenate([head, r[SUBLANES:]], axis=0)

    xc = (vec(V_CB, GROUP_W) + shifted(3) * cw[0:1, :] + shifted(2) * cw[1:2, :]
          + shifted(1) * cw[2:3, :] + xa * cw[3:4, :])
    conv_ref[...] = xa[tm - SUBLANES:tm]
    yield
    a, bx = _rglru_gates(xc, wr_ref[...], wi_ref[...], vec(V_BR, GROUP_W), vec(V_BI, GROUP_W),
                         vec(V_LAM, GROUP_W))
    sub_t = _iota((tm, GROUP_W), 0) & (SUBLANES - 1)
    s = 1
    while s < SUBLANES:
        keep = sub_t >= s
        a_s = pltpu.roll(a, s, 0)
        b_s = pltpu.roll(bx, s, 0)
        bx = jnp.where(keep, a * b_s + bx, bx)
        a = jnp.where(keep, a * a_s, a)
        s *= 2
    carry = hl_ref[...]
    hs = []
    for g in range(tm // SUBLANES):
        rows = slice(g * SUBLANES, (g + 1) * SUBLANES)
        hg_ = a[rows] * carry + bx[rows]
        hs.append(hg_)
        carry = hg_[SUBLANES - 1:SUBLANES]
    hseq = jnp.concatenate(hs, axis=0)
    hl_ref[...] = carry
    yield
    y_ref[:, 0:GROUP_W] = (hseq * _gelu_tanh(ga)).astype(y_ref.dtype)


def _group_b(c):
    proj, vec, tm, t, ones_bd = c.proj, c.vec, c.tm, c.t, c.ones_bd
    rr_ref, rbase_ref, kst_ref, vst_ref, y_ref = c.rr_ref, c.rbase_ref, c.kst_ref, c.vst_ref, c.y_ref
    cb = rbase_ref[:, 0:LANES]
    sb_ = rbase_ref[:, LANES:2 * LANES]
    rc = cb * rr_ref[0] - sb_ * rr_ref[1]
    ra = sb_ * rr_ref[2] + cb * rr_ref[3]
    rb = sb_ * rr_ref[4] + cb * rr_ref[5]
    q = _seg_rms(proj(C_BQ, GROUP_W), vec(V_QG, GROUP_W), ones_bd)
    q = _rope(q, jnp.concatenate([rc, rc], 1), jnp.concatenate([ra, ra], 1), jnp.concatenate([rb, rb], 1))
    k = _seg_rms(proj(C_BK, LANES), vec(V_KG, LANES), ones_bd[0:LANES, 0:LANES])
    k = _rope(k, rc, ra, rb)
    v = proj(C_BV, LANES)
    sink = vec(V_SINK, LANES)
    k_prev, v_prev = kst_ref[...], vst_ref[...]
    for j in range(tm // BLK):
        rs = slice(j * BLK, (j + 1) * BLK)
        first = (t == 0) if j == 0 else False
        yb = _swa_block(q[rs], k[rs], v[rs], k_prev, v_prev, sink, first)
        y_ref[rs, GROUP_W:2 * GROUP_W] = yb.astype(y_ref.dtype)
        k_prev, v_prev = k[rs], v[rs]
        yield
    kst_ref[...] = k_prev
    vst_ref[...] = v_prev


def _group_c(c):
    proj, vec, tm, layer, ones_bd, hm = c.proj, c.vec, c.tm, c.layer, c.ones_bd, c.hm
    gamma_ref, sst_ref, msmall_ref, mbig_ref, y_ref = c.gamma_ref, c.sst_ref, c.msmall_ref, c.mbig_ref, c.y_ref
    lb = _lb_from_gamma([gamma_ref[i:i + 1, :] for i in range(gamma_ref.shape[0])], layer)
    cq, cf, ci, cg = (proj(C_CQ, GROUP_W), proj(C_CF, GROUP_W), proj(C_CI, GROUP_W), proj(C_CG, GROUP_W))
    hg = vec(V_HG, GROUP_W)
    sub = _iota((BLK, GROUP_W), 0) & (SUBLANES - 1)
    for j in range(tm // BLK):
        rs = slice(j * BLK, (j + 1) * BLK)
        yc = yield from _hgrn_block(cq[rs], cf[rs], ci[rs], cg[rs], lb, hg, sst_ref, ones_bd, msmall_ref, mbig_ref,
                         hm, sub)
        y_ref[rs, 2 * GROUP_W:3 * GROUP_W] = yc.astype(y_ref.dtype)
        yield


def _group_d(c):
    proj, vec, tm, layer, ones_bd, hm, hmf = c.proj, c.vec, c.tm, c.layer, c.ones_bd, c.hm, c.hmf
    gb_ref, triu_ref, mst_ref, cst_ref, y_ref = c.gb_ref, c.triu_ref, c.mst_ref, c.cst_ref, c.y_ref
    dq, dk, dv, do = (proj(C_DQ, GROUP_W), proj(C_DK, GROUP_W), proj(C_DV, GROUP_W), proj(C_DO, GROUP_W))
    mg = vec(V_MG, GROUP_W)
    m_prev = mst_ref[...]
    g_rows, g_c, m_c, eneg_c, m_new = _mlstm_tile_gates(proj(C_GATES, LANES), gb_ref[layer], m_prev,
                                                       triu_ref[...])
    mst_ref[...] = m_new
    yield
    for j in range(tm // BLK):
        rs = slice(j * BLK, (j + 1) * BLK)
        yd = yield from _mlstm_block(dq[rs], dk[rs], dv[rs], do[rs], g_rows[:, rs], g_c[rs], m_c[rs], eneg_c[rs],
                          m_prev, mg, cst_ref, ones_bd, hm, hmf)
        y_ref[rs, 3 * GROUP_W:4 * GROUP_W] = yd.astype(y_ref.dtype)
        m_prev = m_c[(j + 1) * BLK - 1:(j + 1) * BLK]
        yield


def _full(shape):
    nd = len(shape)
    return pl.BlockSpec(shape, lambda *_: (0,) * nd)


def _layer_block(shape, layer, single_buffer=False):
    nd = len(shape) - 1
    kw = {'pipeline_mode': pl.Buffered(1)} if single_buffer else {}
    return pl.BlockSpec((None,) + tuple(shape[1:]), lambda *_: (layer,) + (0,) * nd, **kw)


def _prompt_mixers(h, cw, layer, tm, prev):
    bsz, t, _ = h.shape
    nt = t // tm
    kern = functools.partial(_mixer_kernel, layer=layer, tm=tm)
    in_specs = [pl.BlockSpec((bsz, tm, D_MODEL), lambda i: (0, i, 0)),
                pl.BlockSpec((bsz, tm, D_MODEL), lambda i: (0, jnp.minimum(i + 1, nt - 1), 0)),
                _full(cw['vecs'].shape), _layer_block(cw['w_in_p'].shape, layer, single_buffer=True),
                _full(cw['conv_w'].shape), _layer_block(cw['wr_bd'].shape, layer),
                _layer_block(cw['wi_bd'].shape, layer), _full(cw['rope_r'].shape),
                pl.BlockSpec((None, 1, 2 * LANES), lambda i: (i, 0, 0)),
                _full(cw['gamma'].shape), _full(cw['ones_bd'].shape), _full(cw['lvl_small'].shape),
                _full(cw['lvl_big'].shape), _full(cw['gate_bias'].shape), _full(cw['triu'].shape)]
    depth = cw['vecs'].shape[0]
    st_shapes = [(depth, bsz) + s for s in ((1, GROUP_W), (8, GROUP_W), (BLK, LANES), (BLK, LANES),
                                            (2, LANES, LANES), (N_HEADS, LANES, LANES), (1, LANES))]
    out_shape = ([jax.ShapeDtypeStruct((bsz, t, D_MODEL), BF16)]
                 + [jax.ShapeDtypeStruct(s, F32) for s in st_shapes])
    st_specs = [_full(s) if prev is None else _layer_block(s, layer) for s in st_shapes]
    chain = [] if prev is None else list(prev)
    n_in = len(in_specs)
    return pl.pallas_call(
        kern, grid=(nt,), in_specs=in_specs + [pl.BlockSpec(memory_space=pl.ANY)] * len(chain),
        out_specs=[pl.BlockSpec((bsz, tm, D_MODEL), lambda i: (0, i, 0))] + st_specs, out_shape=out_shape,
        input_output_aliases={n_in + i: 1 + i for i in range(len(chain))},
        scratch_shapes=[pltpu.VMEM((bsz * tm, N_IN), F32), pltpu.VMEM((bsz * tm, D_MODEL), BF16)],
        compiler_params=pltpu.CompilerParams(dimension_semantics=("arbitrary",),
                                             vmem_limit_bytes=VMEM_LIMIT),
        name=f"prompt_mixers_l{layer}",
    )(h, h, cw['vecs'], cw['w_in_p'], cw['conv_w'], cw['wr_bd'], cw['wi_bd'], cw['rope_r'], cw['rope_base'],
      cw['gamma'], cw['ones_bd'], cw['lvl_small'], cw['lvl_big'], cw['gate_bias'], cw['triu'], *chain)


def _ffn_math(h, yb, p, g2, wout_ref, wup_ref, wdn_ref, wg_ref, wp_ref):
    h = h + _dot(yb, wout_ref[...])
    nb = _rms_rows(h, g2).astype(BF16)
    acc = h
    step = 1024
    for c in range(0, D_FF, step):
        f = jnp.maximum(_dot(nb, wup_ref[:, c:c + step]), 0.0)
        acc = acc + _dot((f * f).astype(BF16), wdn_ref[c:c + step, :])
    gate = _sigmoid(_dot(acc.astype(BF16), wg_ref[...]))
    return acc + gate * _dot(p.astype(BF16), wp_ref[...])


def _ffn_kernel(h_ref, y_ref, p_ref, pv_ref, wout_ref, wup_ref, wdn_ref, wg_ref, wp_ref, o_ref, *, layer):
    g2 = pv_ref[layer:layer + 1, V_G2:V_G2 + D_MODEL]
    o_ref[...] = _ffn_math(h_ref[...], y_ref[...], p_ref[...], g2, wout_ref, wup_ref, wdn_ref, wg_ref, wp_ref)


def _ffn_weight_specs(cw, layer):
    names = ['w_out', 'w_up', 'w_down', 'w_gate', 'w_proj']
    return [cw[n] for n in names], [_layer_block(cw[n].shape, layer, single_buffer=True) for n in names]


def _prompt_ffn(h2, y2, p3, cw, layer, tm):
    n = h2.shape[0]
    row = lambda w: pl.BlockSpec((tm, w), lambda i: (i, 0))
    ws, wspecs = _ffn_weight_specs(cw, layer)
    return pl.pallas_call(
        functools.partial(_ffn_kernel, layer=layer), grid=(n // tm,),
        in_specs=[row(D_MODEL), row(D_MODEL), pl.BlockSpec((None, tm, PLE_DIM), lambda i: (layer, i, 0)),
                  _full(cw['vecs'].shape)] + wspecs,
        out_specs=row(D_MODEL), out_shape=jax.ShapeDtypeStruct((n, D_MODEL), F32),
        compiler_params=pltpu.CompilerParams(dimension_semantics=("arbitrary",),
                                             vmem_limit_bytes=VMEM_LIMIT),
        name=f"prompt_ffn_l{layer}",
    )(h2, y2, p3, cw['vecs'], *ws)


def _sample_pre_kernel(x_ref, pv_ref, wt_ref, convw_ref, wr_ref, wi_ref, ones_ref, rope_ref, gcol_ref,
                       cols_ref, h0_ref, conv0_ref, n0_ref, m0_ref,
                       ya_ref, hn_ref, convn_ref, q_ref, kt_ref, vt_ref, fm_ref, nn_ref, mn_ref, *, layer):
    def vec(off, w):
        return pv_ref[layer:layer + 1, off:off + w]

    n = _rms_rows(x_ref[...], vec(V_G1, D_MODEL))
    nb = n.astype(BF16)
    n_t = n.T.astype(BF16)
    ones_bd = ones_ref[...]

    def proj(r0, cnt):
        return _dot_nt(nb, wt_ref[r0:r0 + cnt, :].astype(BF16))

    def proj_t(r0, cnt):
        return _dot(wt_ref[r0:r0 + cnt, :].astype(BF16), n_t)

    xa = proj(C_AX, GROUP_W)
    ga = proj(C_AG, GROUP_W)
    cw = convw_ref[layer]
    xc = (vec(V_CB, GROUP_W) + conv0_ref[0] * cw[0:1, :] + conv0_ref[1] * cw[1:2, :]
          + conv0_ref[2] * cw[2:3, :] + xa * cw[3:4, :])
    convn_ref[0] = conv0_ref[1]
    convn_ref[1] = conv0_ref[2]
    convn_ref[2] = xa
    a, bx = _rglru_gates(xc, wr_ref[...], wi_ref[...], vec(V_BR, GROUP_W), vec(V_BI, GROUP_W),
                         vec(V_LAM, GROUP_W))
    hn = a * h0_ref[...] + bx
    hn_ref[...] = hn
    ya_ref[...] = hn * _gelu_tanh(ga)

    rc, ra, rb = rope_ref[0:1, :], rope_ref[1:2, :], rope_ref[2:3, :]
    q = _seg_rms(proj(C_BQ, GROUP_W), vec(V_QG, GROUP_W), ones_bd)
    q_ref[...] = _rope(q, jnp.concatenate([rc, rc], 1), jnp.concatenate([ra, ra], 1),
                       jnp.concatenate([rb, rb], 1))
    k = _seg_rms(proj(C_BK, LANES), vec(V_KG, LANES), ones_bd[0:LANES, 0:LANES])
    kt_ref[...] = _rope(k, rc, ra, rb).T
    vt_ref[...] = proj(C_BV, LANES).T

    lb = _lb_from_gamma([gcol_ref[i] for i in range(gcol_ref.shape[0])], layer)
    cq, cf, ci, cg = (proj_t(C_CQ, GROUP_W), proj_t(C_CF, GROUP_W), proj_t(C_CI, GROUP_W),
                      proj_t(C_CG, GROUP_W))
    f = lb + (1.0 - lb) * _sigmoid(cf)
    fm_ref[FM_HF:FM_HF + GROUP_W, :] = f
    fm_ref[FM_HK:FM_HK + GROUP_W, :] = 1.0 - f
    fm_ref[FM_HQ:FM_HQ + GROUP_W, :] = cq * _sigmoid(cq)
    fm_ref[FM_HV:FM_HV + GROUP_W, :] = ci
    fm_ref[FM_CG:FM_CG + GROUP_W, :] = cg * _sigmoid(cg)

    dq, dk, dv, do = (proj_t(C_DQ, GROUP_W), proj_t(C_DK, GROUP_W), proj_t(C_DV, GROUP_W),
                      proj_t(C_DO, GROUP_W))
    g8 = proj_t(C_GATES, 2 * N_HEADS) + cols_ref[layer, R_GB:R_GB + 2 * N_HEADS, :]
    ig = g8[0:N_HEADS, :]
    lf = _log_sigmoid(g8)[N_HEADS:2 * N_HEADS, :]
    a_int = lf + m0_ref[...]
    m_new = jnp.maximum(a_int, ig)
    dec = jnp.exp(a_int - m_new)
    w = jnp.exp(ig - m_new)
    mn_ref[...] = m_new
    km = dk * (HEAD_DIM ** -0.5)
    dens = []
    for h in range(N_HEADS):
        sl = slice(h * HEAD_DIM, (h + 1) * HEAD_DIM)
        nn_h = dec[h:h + 1, :] * n0_ref[sl, :] + w[h:h + 1, :] * km[sl, :]
        nn_ref[sl, :] = nn_h
        dens.append(jnp.sum(dq[sl, :] * nn_h, axis=0, keepdims=True))
    pad = jnp.zeros((SUBLANES - N_HEADS, x_ref.shape[0]), F32)
    fm_ref[FM_MK:FM_MK + GROUP_W, :] = km
    fm_ref[FM_MQ:FM_MQ + GROUP_W, :] = dq
    fm_ref[FM_MV:FM_MV + GROUP_W, :] = dv
    fm_ref[FM_DO:FM_DO + GROUP_W, :] = _sigmoid(do)
    fm_ref[FM_DEC:FM_DEC + SUBLANES, :] = jnp.concatenate([dec, pad], 0)
    fm_ref[FM_W:FM_W + SUBLANES, :] = jnp.concatenate([w, pad], 0)
    fm_ref[FM_DEN:FM_DEN + SUBLANES, :] = jnp.concatenate(dens + [pad], 0)
    fm_ref[FM_ENEG:FM_ENEG + SUBLANES, :] = jnp.concatenate([jnp.exp(-m_new), pad], 0)


def _own_slab(ref, first_layer):
    if not first_layer:
        return ref
    ref[1:] = jnp.zeros((ref.shape[0] - 1,) + ref.shape[1:], ref.dtype)
    return ref.at[0]


def _sample_attn_kernel(q_ref, kn_ref, vn_ref, kc_ref, vc_ref, sink_ref, *rest, first_layer):
    ko_ref, vo_ref, o_ref = rest[-3:]
    ko_ref, vo_ref = _own_slab(ko_ref, first_layer), _own_slab(vo_ref, first_layer)
    sb = q_ref.shape[0]
    rows = 2 * HEAD_DIM
    lane = _iota((rows, WINDOW), 1)
    kn = kn_ref[...].reshape(rows, sb)
    vn = vn_ref[...].reshape(rows, sb)
    for s in range(sb):
        kt = pltpu.roll(kc_ref[s].reshape(rows, WINDOW), WINDOW - 1, 1)
        ko_ref[s] = jnp.where(lane == WINDOW - 1, kn[:, s:s + 1], kt).reshape(2, HEAD_DIM, WINDOW)
        vt = pltpu.roll(vc_ref[s].reshape(rows, WINDOW), WINDOW - 1, 1)
        vo_ref[s] = jnp.where(lane == WINDOW - 1, vn[:, s:s + 1], vt).reshape(2, HEAD_DIM, WINDOW)
    for kv in range(2):
        kk = ko_ref[:, kv].astype(BF16)
        vv = vo_ref[:, kv].astype(BF16)
        s_ = jnp.einsum('bqc,bcj->bqj', q_ref[:, kv].astype(BF16), kk,
                        preferred_element_type=F32) * (HEAD_DIM ** -0.5)
        sk = sink_ref[kv]
        mx = jnp.maximum(jnp.max(s_, axis=-1, keepdims=True), sk)
        p = jnp.exp(s_ - mx)
        den = jnp.sum(p, axis=-1, keepdims=True) + jnp.exp(sk - mx)
        o = jnp.einsum('bqj,bcj->bqc', p.astype(BF16), vv, preferred_element_type=F32)
        o_ref[:, kv] = o / den


def _sample_state_kernel(fm_ref, s_ref, c_ref, *rest, first_layer):
    so_ref, co_ref, oh_ref, om_ref = rest[-4:]
    so_ref, co_ref = _own_slab(so_ref, first_layer), _own_slab(co_ref, first_layer)
    h = pl.program_id(0)
    r0 = pl.multiple_of(h * HEAD_DIM, HEAD_DIM)
    hv = fm_ref[pl.ds(FM_HV + r0, HEAD_DIM), :]
    mv = fm_ref[pl.ds(FM_MV + r0, HEAD_DIM), :]
    dec = fm_ref[pl.ds(FM_DEC + h, 1), :]
    w = fm_ref[pl.ds(FM_W + h, 1), :]

    def body(d, carry):
        acc_h, acc_m = carry
        r = r0 + d
        s_new = fm_ref[pl.ds(FM_HF + r, 1), :] * s_ref[d] + fm_ref[pl.ds(FM_HK + r, 1), :] * hv
        so_ref[d] = s_new
        c_new = dec * c_ref[d] + (w * fm_ref[pl.ds(FM_MK + r, 1), :]) * mv
        co_ref[d] = c_new
        return (acc_h + fm_ref[pl.ds(FM_HQ + r, 1), :] * s_new,
                acc_m + fm_ref[pl.ds(FM_MQ + r, 1), :] * c_new)

    zero = jnp.zeros((HEAD_DIM, fm_ref.shape[1]), F32)
    acc_h, acc_m = lax.fori_loop(0, HEAD_DIM, body, (zero, zero), unroll=4)
    oh_ref[...] = acc_h
    om_ref[...] = acc_m


def _sample_post_kernel(h_ref, ya_ref, yb_ref, oh_ref, om_ref, fm_ref, cols_ref, p_ref, pv_ref,
                        wout_ref, wup_ref, wdn_ref, wg_ref, wp_ref, o_ref, *, layer):
    def head_rms(x):
        return x * lax.rsqrt(jnp.mean(x * x, axis=0, keepdims=True) + EPS)

    yc, yd = [], []
    for h in range(N_HEADS):
        sl = slice(h * HEAD_DIM, (h + 1) * HEAD_DIM)
        yc.append(head_rms(oh_ref[h]))
        den = fm_ref[FM_DEN + h:FM_DEN + h + 1, :]
        eneg = fm_ref[FM_ENEG + h:FM_ENEG + h + 1, :]
        yd.append(head_rms(om_ref[h] / jnp.maximum(jnp.abs(den), eneg)))
    yc = (jnp.concatenate(yc, 0) * cols_ref[layer, R_HG:R_HG + GROUP_W, :]
          * fm_ref[FM_CG:FM_CG + GROUP_W, :])
    yd = (jnp.concatenate(yd, 0) * cols_ref[layer, R_MG:R_MG + GROUP_W, :]
          * fm_ref[FM_DO:FM_DO + GROUP_W, :])
    y = jnp.concatenate([ya_ref[...], yb_ref[...], yc.T, yd.T], axis=1).astype(BF16)
    g2 = pv_ref[layer:layer + 1, V_G2:V_G2 + D_MODEL]
    o_ref[...] = _ffn_math(h_ref[...], y, p_ref[...], g2, wout_ref, wup_ref, wdn_ref, wg_ref, wp_ref)


def _call_full(kern, args, specs, out_shape, name):
    specs = [(_full(a.shape) if s is None else s) for a, s in zip(args, specs)]
    return pl.pallas_call(
        kern, grid=(1,), in_specs=specs,
        out_specs=tuple(_full(s.shape) for s in out_shape), out_shape=tuple(out_shape),
        compiler_params=pltpu.CompilerParams(dimension_semantics=("arbitrary",),
                                             vmem_limit_bytes=VMEM_LIMIT),
        name=name,
    )(*args)


def _sample_layer(h, sv, prev, cw, layer):
    nsm = h.shape[0]
    depth = cw['vecs'].shape[0]
    sd = lambda *shape: jax.ShapeDtypeStruct(shape, F32)
    g, l = sd(nsm, GROUP_W), sd(LANES, nsm)

    args = [h, cw['vecs'], cw['w_in_t'], cw['conv_w'], cw['wr_bd'], cw['wi_bd'], cw['ones_bd'],
            cw['rope_s'], cw['gamma_col'], cw['cols'], sv['h'], sv['conv'], sv['n'], sv['m']]
    specs = [None, None, _layer_block(cw['w_in_t'].shape, layer, single_buffer=True), None,
             _layer_block(cw['wr_bd'].shape, layer), _layer_block(cw['wi_bd'].shape, layer), None,
             None, None, None, _layer_block(sv['h'].shape, layer), _layer_block(sv['conv'].shape, layer),
             _layer_block(sv['n'].shape, layer), _layer_block(sv['m'].shape, layer)]
    outs = [g, g, sd(CONV_W - 1, nsm, GROUP_W), g, l, l, sd(FM_ROWS, nsm), sd(GROUP_W, nsm),
            sd(N_HEADS, nsm)]
    ya, hn, convn, q, kt, vt, fm, nn, mn = _call_full(
        functools.partial(_sample_pre_kernel, layer=layer), args, specs, outs, f"sample_pre_l{layer}")

    sb = 16
    nblk = nsm // sb
    q3 = jnp.pad(q.reshape(nsm, 2, 2, HEAD_DIM), ((0, 0), (0, 0), (0, SUBLANES - 2), (0, 0)))
    to_blocks = lambda a: a.reshape(2, HEAD_DIM, nblk, sb).transpose(2, 0, 1, 3)
    cshape = (depth, nsm, 2, HEAD_DIM, WINDOW)
    cspec = pl.BlockSpec((None, sb, 2, HEAD_DIM, WINDOW), lambda i: (layer, i, 0, 0, 0))
    nspec = pl.BlockSpec((None, 2, HEAD_DIM, sb), lambda i: (i, 0, 0, 0))
    qspec = pl.BlockSpec((sb, 2, SUBLANES, HEAD_DIM), lambda i: (i, 0, 0, 0))
    any_spec = pl.BlockSpec(memory_space=pl.ANY)
    cout = (pl.BlockSpec((depth, sb, 2, HEAD_DIM, WINDOW), lambda i: (0, i, 0, 0, 0)) if prev is None
            else cspec)
    chain = [] if prev is None else [prev['k'], prev['v']]
    ko, vo, o3 = pl.pallas_call(
        functools.partial(_sample_attn_kernel, first_layer=prev is None), grid=(nblk,),
        in_specs=[qspec, nspec, nspec, cspec, cspec, _layer_block(cw['sinks8'].shape, layer)]
        + [any_spec] * len(chain),
        out_specs=(cout, cout, qspec),
        out_shape=(sd(*cshape), sd(*cshape), sd(nsm, 2, SUBLANES, HEAD_DIM)),
        input_output_aliases={6 + i: i for i in range(len(chain))},
        compiler_params=pltpu.CompilerParams(dimension_semantics=("arbitrary",),
                                             vmem_limit_bytes=VMEM_LIMIT),
        name=f"sample_attn_l{layer}",
    )(q3, to_blocks(kt), to_blocks(vt), sv['k'], sv['v'], cw['sinks8'], *chain)
    yb = o3[:, :, 0:2, :].reshape(nsm, GROUP_W)

    sshape = (depth, N_HEADS, HEAD_DIM, HEAD_DIM, nsm)
    sspec = pl.BlockSpec((None, None, HEAD_DIM, HEAD_DIM, nsm), lambda i: (layer, i, 0, 0, 0))
    ospec = pl.BlockSpec((None, HEAD_DIM, nsm), lambda i: (i, 0, 0))
    sout = (pl.BlockSpec((depth, None, HEAD_DIM, HEAD_DIM, nsm), lambda i: (0, i, 0, 0, 0))
            if prev is None else sspec)
    chain = [] if prev is None else [prev['s'], prev['c']]
    so, co, oh, om = pl.pallas_call(
        functools.partial(_sample_state_kernel, first_layer=prev is None), grid=(N_HEADS,),
        in_specs=[_full(fm.shape), sspec, sspec] + [any_spec] * len(chain),
        out_specs=(sout, sout, ospec, ospec),
        out_shape=(sd(*sshape), sd(*sshape), sd(N_HEADS, HEAD_DIM, nsm), sd(N_HEADS, HEAD_DIM, nsm)),
        input_output_aliases={3 + i: i for i in range(len(chain))},
        compiler_params=pltpu.CompilerParams(dimension_semantics=("arbitrary",),
                                             vmem_limit_bytes=VMEM_LIMIT),
        name=f"sample_state_l{layer}",
    )(fm, sv['s'], sv['c'], *chain)

    ws, wspecs = _ffn_weight_specs(cw, layer)
    (h_new,) = _call_full(
        functools.partial(_sample_post_kernel, layer=layer),
        [h, ya, yb, oh, om, fm, cw['cols'], sv['p'], cw['vecs']] + ws,
        [None] * 7 + [_layer_block(sv['p'].shape, layer), None] + wspecs,
        [sd(nsm, D_MODEL)], f"sample_post_l{layer}")
    small = (hn, convn, nn, mn)
    big = {'k': ko, 'v': vo, 's': so, 'c': co}
    return h_new, small, big


def _block_diag_all(w):
    depth = w.shape[0]
    rows = w.reshape(depth, GROUP_W, HEAD_DIM)
    idx = np.arange(GROUP_W) // HEAD_DIM
    mask = idx[:, None] == idx[None, :]
    return jnp.where(mask[None], jnp.tile(rows, (1, 1, N_HEADS)), 0.0)


def _rope_lane_freq():
    half = ROT_DIM // 2
    inv = np.power(np.float32(ROPE_THETA), -np.arange(half, dtype=np.float32) * np.float32(2.0 / ROT_DIM))
    dd = np.arange(LANES) % HEAD_DIM
    freq = np.where(dd < ROT_DIM, inv[dd % half], np.float32(0.0))
    m_a = (dd < half).astype(np.float32)
    m_b = ((dd >= half) & (dd < ROT_DIM)).astype(np.float32)
    return freq.astype(np.float32), m_a, m_b


def _rope_tables(pos):
    freq, m_a, m_b = _rope_lane_freq()
    ang = np.asarray(pos, np.float32)[:, None] * freq[None, :]
    cos, sin = np.cos(ang), np.sin(ang)
    return cos, -sin * m_a, sin * m_b


def _rope_split_tables(t, tm):
    freq, m_a, m_b = _rope_lane_freq()
    ang_r = np.arange(tm, dtype=np.float32)[:, None] * freq[None, :]
    cr, sr = np.cos(ang_r), np.sin(ang_r)
    rope_r = np.stack([cr, sr, -cr * m_a, -sr * m_a, cr * m_b, sr * m_b])
    ang_b = (np.arange(t // tm) * tm).astype(np.float32)[:, None] * freq[None, :]
    rope_base = np.concatenate([np.cos(ang_b), np.sin(ang_b)], 1)[:, None, :]
    return rope_r, rope_base


def _hgrn_level_masks():
    t = np.arange(BLK)[:, None]
    s = np.arange(BLK)[None, :]
    small = [t == s]
    big = []
    for lev in range(1, 8):
        half = 1 << (lev - 1)
        own = ((t >> lev) == (s >> lev)) & ((t & half) != 0) & ((s & half) == 0)
        if lev < 4:
            small.append(own)
        else:
            rows = np.concatenate([np.arange(m, m + half) for m in range(half, BLK, 2 * half)])
            big.append(own[rows])
    return np.stack(small).astype(np.float32), np.stack(big).astype(np.float32)


def _pad_last(v, width):
    return jnp.pad(v, ((0, 0), (0, width - v.shape[-1])))


def _common(w, t, tm_mix, past_len, nsm):
    depth = w['w_in'].shape[0]
    tile = lambda v, n: jnp.tile(v, (1, n))
    vecs = jnp.concatenate([
        w['norm1_g'], w['norm2_g'], w['conv_b'], w['lru_br'], w['lru_bi'], w['lru_lam'],
        tile(w['q_norm_g'], N_HEADS), tile(w['k_norm_g'], 2), _pad_last(w['attn_sinks'], LANES),
        tile(w['hgrn_norm_g'], N_HEADS), tile(w['mlstm_norm_g'], N_HEADS),
        _pad_last(w['mlstm_ib'], LANES), _pad_last(w['mlstm_fb'], LANES)], axis=1)
    cols = jnp.concatenate([tile(w['hgrn_norm_g'], N_HEADS), tile(w['mlstm_norm_g'], N_HEADS),
                            w['mlstm_ib'], w['mlstm_fb']], axis=1)
    w_in = w['w_in']
    gate_bias = jnp.concatenate([w['mlstm_ib'], w['mlstm_fb']], axis=1)
    lvl_small, lvl_big = _hgrn_level_masks()
    sinks = w['attn_sinks']
    z2 = jnp.zeros((depth, 2, SUBLANES - 2), F32)
    idx = np.arange(GROUP_W)
    rope_r, rope_base = _rope_split_tables(t, tm_mix)
    return {
        'vecs': vecs,
        'cols': jnp.broadcast_to(cols[:, :, None], cols.shape + (nsm,)),
        'gamma': w['hgrn_gamma'],
        'gamma_col': jnp.broadcast_to(w['hgrn_gamma'][:, :, None], w['hgrn_gamma'].shape + (nsm,)),
        'w_in_p': jnp.pad(w_in, ((0, 0), (0, 0), (0, N_IN - D_IN))).astype(BF16),
        'lvl_small': lvl_small, 'lvl_big': lvl_big,
        'gate_bias': jnp.broadcast_to(gate_bias[:, :, None], gate_bias.shape + (tm_mix,)),
        'triu': (np.arange(tm_mix)[:, None] <= np.arange(tm_mix)[None, :]).astype(np.float32),
        'w_in_t': jnp.swapaxes(w_in, 1, 2),
        'conv_w': w['conv_w'],
        'wr_bd': _block_diag_all(w['lru_wr']).astype(BF16),
        'wi_bd': _block_diag_all(w['lru_wi']).astype(BF16),
        'ones_bd': jnp.asarray(idx[:, None] // HEAD_DIM == idx[None, :] // HEAD_DIM, BF16),
        'sinks8': jnp.concatenate([sinks.reshape(depth, 2, 2), z2], axis=2)[..., None],
        'rope_r': rope_r, 'rope_base': rope_base,
        'rope_s': np.concatenate(_rope_tables(past_len + np.arange(1)), axis=0),
        'w_out': w['w_out'].astype(BF16), 'w_up': w['w_up'].astype(BF16),
        'w_down': w['w_down'].astype(BF16), 'w_gate': w['w_ple_gate'].astype(BF16),
        'w_proj': w['w_ple_proj'].astype(BF16),
    }


def _run(x_prompt, x_sample, p_prompt, p_sample, sample_state, w, past_len, tm_mix=256, tm_ffn=512):
    depth = w['w_in'].shape[0]
    bsz, t, _ = x_prompt.shape
    nsm = x_sample.shape[0]
    cw = _common(w, t, tm_mix, past_len, nsm)
    tm_ffn = min(tm_ffn, bsz * t)
    h0, conv0, kc, vc, s0, c0, n0, m0 = sample_state
    sv = {'h': h0, 'conv': jnp.transpose(conv0, (0, 2, 1, 3)),
          'k': jnp.transpose(kc, (0, 1, 3, 4, 2)), 'v': jnp.transpose(vc, (0, 1, 3, 4, 2)),
          's': jnp.transpose(s0, (0, 2, 3, 4, 1)), 'c': jnp.transpose(c0, (0, 2, 3, 4, 1)),
          'n': jnp.transpose(n0, (0, 2, 3, 1)).reshape(depth, GROUP_W, nsm),
          'm': jnp.transpose(m0, (0, 2, 1)), 'p': p_sample.reshape(depth, nsm, PLE_DIM)}
    p3 = p_prompt.reshape(depth, bsz * t, PLE_DIM)

    hp = x_prompt
    hs = x_sample.reshape(nsm, D_MODEL)
    pst, s_small, big = None, [], None
    for l in range(depth):
        y, *pst = _prompt_mixers(hp, cw, l, tm_mix, pst)
        hp = _prompt_ffn(hp.reshape(bsz * t, D_MODEL), y.reshape(bsz * t, D_MODEL), p3, cw, l,
                         tm_ffn).reshape(bsz, t, D_MODEL)
        hs, small, big = _sample_layer(hs, sv, big, cw, l)
        s_small.append(small)
    stack = lambda sts, i: jnp.stack([s[i] for s in sts])
    hl, conv, kst, vst, sst, cst, mst = pst
    s_t = jnp.swapaxes(sst, -1, -2).reshape(depth, bsz, 2, 2, HEAD_DIM, 2, HEAD_DIM)
    s_hgrn = jnp.stack([s_t[:, :, :, 0, :, 0, :], s_t[:, :, :, 1, :, 1, :]], axis=3)
    c_t = cst.reshape(depth, bsz, 2, 2, 2, HEAD_DIM, LANES)
    c_rows = jnp.stack([c_t[:, :, :, 0, 0], c_t[:, :, :, 1, 1]], axis=3)
    c_rows = c_rows.reshape(depth, bsz, N_HEADS, HEAD_DIM, LANES)
    prompt_out = (hl[:, :, 0], conv[:, :, 8 - (CONV_W - 1):], kst.reshape(depth, bsz, WINDOW, 2, HEAD_DIM),
                  vst.reshape(depth, bsz, WINDOW, 2, HEAD_DIM),
                  s_hgrn.reshape(depth, bsz, N_HEADS, HEAD_DIM, HEAD_DIM), c_rows[..., 0:HEAD_DIM],
                  c_rows[..., HEAD_DIM], mst[:, :, 0, 0:N_HEADS])
    hn, convn, nn, mn = (stack(s_small, i) for i in range(4))
    sample_out = (hn, jnp.transpose(convn, (0, 2, 1, 3)),
                  jnp.transpose(big['k'], (0, 1, 4, 2, 3)), jnp.transpose(big['v'], (0, 1, 4, 2, 3)),
                  jnp.transpose(big['s'], (0, 4, 1, 2, 3)), jnp.transpose(big['c'], (0, 4, 1, 2, 3)),
                  jnp.transpose(nn.reshape(depth, N_HEADS, HEAD_DIM, nsm), (0, 3, 1, 2)),
                  jnp.transpose(mn, (0, 2, 1)))
    return (hp, hs.reshape(x_sample.shape)) + prompt_out + sample_out


def kernel(x_prompt, x_sample, p_prompt, p_sample, state_rglru_h, state_rglru_conv, cache_swa_k, cache_swa_v, state_hgrn_s, state_mlstm_c, state_mlstm_n, state_mlstm_m, norm1_g, w_in, conv_w, conv_b, lru_wr, lru_br, lru_wi, lru_bi, lru_lam, q_norm_g, k_norm_g, attn_sinks, hgrn_gamma, hgrn_norm_g, mlstm_ib, mlstm_fb, mlstm_norm_g, w_out, norm2_g, w_up, w_down, w_ple_gate, w_ple_proj):
    w = {'norm1_g': norm1_g, 'w_in': w_in, 'conv_w': conv_w, 'conv_b': conv_b, 'lru_wr': lru_wr,
         'lru_br': lru_br, 'lru_wi': lru_wi, 'lru_bi': lru_bi, 'lru_lam': lru_lam, 'q_norm_g': q_norm_g,
         'k_norm_g': k_norm_g, 'attn_sinks': attn_sinks, 'hgrn_gamma': hgrn_gamma,
         'hgrn_norm_g': hgrn_norm_g, 'mlstm_ib': mlstm_ib, 'mlstm_fb': mlstm_fb,
         'mlstm_norm_g': mlstm_norm_g, 'w_out': w_out, 'norm2_g': norm2_g, 'w_up': w_up,
         'w_down': w_down, 'w_ple_gate': w_ple_gate, 'w_ple_proj': w_ple_proj}
    st = (state_rglru_h, state_rglru_conv, cache_swa_k, cache_swa_v, state_hgrn_s, state_mlstm_c,
          state_mlstm_n, state_mlstm_m)
    past_len = 8192
    return _run(x_prompt, x_sample, p_prompt, p_sample, st, w, past_len)
```

```python
import functools
import types

import jax
import jax.numpy as jnp
import numpy as np
from jax import lax
from jax.experimental import pallas as pl
from jax.experimental.pallas import tpu as pltpu

F32 = jnp.float32
BF16 = jnp.bfloat16

D_MODEL = 1024
GROUP_W = 256
HEAD_DIM = 64
N_HEADS = 4
EPS = 1e-6
NEG_BIG = -1e30
LRU_C = 8.0
CONV_W = 4
ROT_DIM = 16
ROPE_THETA = 500000.0
WINDOW = 128
D_FF = 4096
PLE_DIM = 256
LANES = 128
SUBLANES = 8
BLK = 128

C_AX, C_AG, C_BQ, C_BK, C_BV = 0, 256, 512, 768, 896
C_CQ, C_CF, C_CI, C_CG = 1024, 1280, 1536, 1792
C_DQ, C_DK, C_DV, C_DO = 2048, 2304, 2560, 2816
C_GATES = 3072
D_IN = 3080
N_IN = 3200

V_G1, V_G2, V_CB, V_BR, V_BI, V_LAM = 0, 1024, 2048, 2304, 2560, 2816
V_QG, V_KG, V_SINK, V_HG, V_MG, V_IB, V_FB = 3072, 3328, 3456, 3584, 3840, 4096, 4224
N_VEC = 4352

R_HG, R_MG, R_GB = 0, 256, 512
N_COL = 520

FM_HF, FM_HK, FM_HQ, FM_HV, FM_CG = 0, 256, 512, 768, 1024
FM_MK, FM_MQ, FM_MV, FM_DO = 1280, 1536, 1792, 2048
FM_DEC, FM_W, FM_DEN, FM_ENEG = 2304, 2312, 2320, 2328
FM_ROWS = 2336

VMEM_LIMIT = 56 * 1024 * 1024


def _dot(a, b):
    return jnp.dot(a, b, preferred_element_type=F32)


def _dot_nt(a, b):
    return lax.dot_general(a, b, (((1,), (1,)), ((), ())), preferred_element_type=F32)


def _sigmoid(x):
    return jax.nn.sigmoid(x)


def _gelu_tanh(x):
    return 0.5 * x * (1.0 + jnp.tanh(0.7978845608028654 * (x + 0.044715 * (x * x * x))))


def _log_sigmoid(x):
    return jnp.minimum(x, 0.0) - jnp.log1p(jnp.exp(-jnp.abs(x)))


def _softplus(x):
    return jnp.maximum(x, 0.0) + jnp.log1p(jnp.exp(-jnp.abs(x)))


def _rms_rows(x, g):
    return x * lax.rsqrt(jnp.mean(x * x, axis=-1, keepdims=True) + EPS) * g


def _seg_mean_sq(x, ones_bd):
    sq = x * x
    hi = sq.astype(BF16)
    lo = (sq - hi.astype(F32)).astype(BF16)
    return (_dot(hi, ones_bd) + _dot(lo, ones_bd)) * (1.0 / HEAD_DIM)


def _seg_rms(x, g, ones_bd):
    return x * lax.rsqrt(_seg_mean_sq(x, ones_bd) + EPS) * g


def _rope(x, c, sa, sb):
    w = x.shape[1]
    up = pltpu.roll(x, w - ROT_DIM // 2, 1)
    dn = pltpu.roll(x, ROT_DIM // 2, 1)
    return x * c + up * sa + dn * sb


def _lb_from_gamma(gammas, layer):
    mx = functools.reduce(jnp.maximum, gammas)
    e = [jnp.exp(g - mx) for g in gammas]
    tot = functools.reduce(lambda a, b: a + b, e)
    lb = jnp.zeros_like(tot)
    for i in range(1, layer + 1):
        lb = lb + e[i] / tot
    return lb


def _iota(shape, axis):
    return lax.broadcasted_iota(jnp.int32, shape, axis)


def _rglru_gates(xc, wr, wi, br, bi, lam):
    xcb = xc.astype(BF16)
    r = _sigmoid(_dot(xcb, wr) + br)
    ig = _sigmoid(_dot(xcb, wi) + bi)
    log_a = (-LRU_C) * r * _softplus(-lam)
    a = jnp.exp(log_a)
    y = 1.0 - a * a
    root = jnp.where(y > 0.0, y * lax.rsqrt(y), 0.0)
    return a, root * (ig * xc)


def _swa_block(q, k, v, k_prev, v_prev, sink, first):
    kk = jnp.concatenate([k_prev, k], axis=0)
    vv = jnp.concatenate([v_prev, v], axis=0)
    qi = _iota((BLK, 2 * BLK), 0)
    kj = _iota((BLK, 2 * BLK), 1)
    valid = (kj > qi) & (kj <= qi + WINDOW) & ((kj >= BLK) | jnp.logical_not(first))
    lane = _iota((2 * BLK, LANES), 1)
    one_col = jnp.where(lane == HEAD_DIM, 1.0, 0.0)
    vaug = (jnp.where(lane < HEAD_DIM, vv, one_col).astype(BF16),
            jnp.where(lane < HEAD_DIM, pltpu.roll(vv, HEAD_DIM, 1), one_col).astype(BF16))
    qs = q * (HEAD_DIM ** -0.5)
    outs = []
    for h in range(N_HEADS):
        kv = h // 2
        qh = qs[:, h * HEAD_DIM:(h + 1) * HEAD_DIM].astype(BF16)
        kh = kk[:, kv * HEAD_DIM:(kv + 1) * HEAD_DIM].astype(BF16)
        s = jnp.where(valid, _dot_nt(qh, kh), NEG_BIG)
        sk = sink[:, h:h + 1]
        mx = jnp.maximum(jnp.max(s, axis=-1, keepdims=True), sk)
        p = jnp.exp(s - mx)
        o = _dot(p.astype(BF16), vaug[kv])
        den = o[:, HEAD_DIM:HEAD_DIM + 1] + jnp.exp(sk - mx)
        outs.append(o[:, 0:HEAD_DIM] / den)
    return jnp.concatenate(outs, axis=1)


def _head_masks(rows, dtype):
    lane = _iota((rows, LANES), 1)
    return (jnp.where(lane < HEAD_DIM, 1.0, 0.0).astype(dtype),
            jnp.where(lane >= HEAD_DIM, 1.0, 0.0).astype(dtype))


def _pair_scores(qe, ke_b, hm):
    n = qe.shape[0]
    if n != hm[0].shape[0]:
        hm = _head_masks(n, BF16)
    res = []
    for p in range(2):
        sl = slice(p * LANES, (p + 1) * LANES)
        qb = qe[:, sl].astype(BF16)
        lhs = jnp.concatenate([qb * hm[0], qb * hm[1]], axis=0)
        pr = _dot_nt(lhs, ke_b[:, sl])
        res += [pr[:n], pr[n:]]
    return res


def _hgrn_block(cq, cf, ci, cg, lb, hg, sst_ref, ones_bd, msmall_ref, mbig_ref, hm, sub):
    q = cq * _sigmoid(cq)
    f = lb + (1.0 - lb) * _sigmoid(cf)
    logf = jnp.log(f)
    k = 1.0 - f

    att = [s_ * msmall_ref[0] for s_ in _pair_scores(q, k.astype(BF16), hm)]

    c = logf
    tot = logf
    for lev in range(1, 4):
        half = 1 << (lev - 1)
        right = (sub & half) != 0
        tot_l = pltpu.roll(tot, half, 0)
        tot_r = pltpu.roll(tot, BLK - half, 0)
        e = jnp.exp(jnp.where(right, c, tot - c))
        sc = _pair_scores(q * e, (k * e).astype(BF16), hm)
        m = msmall_ref[lev]
        att = [a_ + s_ * m for a_, s_ in zip(att, sc)]
        c = c + jnp.where(right, tot_l, 0.0)
        tot = tot + jnp.where(right, tot_l, tot_r)

    pieces, carry = [], None
    for g in range(BLK // SUBLANES):
        rows = slice(g * SUBLANES, (g + 1) * SUBLANES)
        pieces.append(c[rows] if carry is None else c[rows] + carry)
        t_g = tot[g * SUBLANES:g * SUBLANES + 1]
        carry = t_g if carry is None else carry + t_g
    yield
    b = jnp.concatenate(pieces, axis=0)
    btot = carry

    for lev in range(4, 8):
        half = 1 << (lev - 1)
        nblk = BLK // (2 * half)
        qr, kf = [], []
        for i in range(nblk):
            lo = i * 2 * half
            mid = lo + half
            bref = b[mid - 1:mid]
            qr.append(q[mid:mid + half] * jnp.exp(b[mid:mid + half] - bref))
            kf.append(k[lo:mid] * jnp.exp(bref - b[lo:mid]))
            kf.append(jnp.zeros((half, GROUP_W), F32))
        sc = _pair_scores(jnp.concatenate(qr, axis=0), jnp.concatenate(kf, axis=0).astype(BF16), hm)
        m = mbig_ref[lev - 4]
        zero = jnp.zeros((half, BLK), F32)
        new = []
        for a_, s_ in zip(att, sc):
            u = s_ * m
            parts = []
            for i in range(nblk):
                parts += [zero, u[i * half:(i + 1) * half]]
            new.append(a_ + jnp.concatenate(parts, axis=0))
        att = new

    yield
    qe = (q * jnp.exp(b)).astype(BF16)
    ke = (k * jnp.exp(btot - b)).astype(BF16)
    etot = jnp.exp(btot)
    row_l = _iota((BLK, BLK), 0)
    col_l = _iota((BLK, BLK), 1)
    same_head = (row_l >= HEAD_DIM) == (col_l >= HEAD_DIM)
    outs = []
    for p in range(2):
        sl = slice(p * LANES, (p + 1) * LANES)
        st = sst_ref[p]
        vp = ci[:, sl]
        vb = vp.astype(BF16)
        a2 = jnp.concatenate([att[2 * p], att[2 * p + 1]], axis=1).astype(BF16)
        v2 = jnp.concatenate([vb * hm[0], vb * hm[1]], axis=0)
        o = _dot_nt(qe[:, sl], st.astype(BF16)) + _dot(a2, v2)
        upd = _dot(vp.T.astype(BF16), ke[:, sl])
        sst_ref[p] = st * etot[:, sl] + jnp.where(same_head, upd, 0.0)
        outs.append(o)
    o = jnp.concatenate(outs, axis=1)
    return _seg_rms(o, hg, ones_bd) * (cg * _sigmoid(cg))


def _mlstm_tile_gates(gcols, gb, m0, triu):
    tm = gcols.shape[0]
    gt = gcols.T[0:SUBLANES, :] + gb
    lf = _log_sigmoid(gt)
    hi = lf.astype(BF16)
    r1 = lf - hi.astype(F32)
    mid = r1.astype(BF16)
    lo = (r1 - mid.astype(F32)).astype(BF16)
    parts = jnp.concatenate([hi.astype(F32), mid.astype(F32), lo.astype(F32)], axis=0)
    cs = _dot(parts, triu)
    fcum = cs[0:SUBLANES] + cs[SUBLANES:2 * SUBLANES] + cs[2 * SUBLANES:3 * SUBLANES]
    fcum = pltpu.roll(fcum, N_HEADS, 0)
    g = gt - fcum
    pad = jnp.zeros((LANES - 2 * SUBLANES, tm), F32)
    cols = jnp.concatenate([g, fcum, pad], axis=0).T
    f_c = pltpu.roll(cols, LANES - SUBLANES, 1)
    sub = _iota((tm, LANES), 0) & (SUBLANES - 1)
    cm = cols
    s = 1
    while s < SUBLANES:
        cm = jnp.maximum(cm, jnp.where(sub >= s, pltpu.roll(cm, s, 0), NEG_BIG))
        s *= 2
    carry = m0
    ms = []
    for grp in range(tm // SUBLANES):
        m_g = jnp.maximum(cm[grp * SUBLANES:(grp + 1) * SUBLANES], carry)
        ms.append(m_g)
        carry = m_g[SUBLANES - 1:SUBLANES]
    m_c = jnp.concatenate(ms, axis=0)
    eneg_c = jnp.exp(-(f_c + m_c))
    m_new = f_c[tm - 1:tm] + carry
    return g, cols, m_c, eneg_c, m_new


def _mlstm_block(dq, dk, dv, do, g_rows, g_c, m_c, eneg_c, m_prev, mg, cst_ref, ones_bd, hm, hmf):
    k = dk * (HEAD_DIM ** -0.5)
    m_end = m_c[BLK - 1:BLK]
    inter = jnp.exp(m_prev - m_c)
    wend = jnp.exp(g_c - m_end)
    dec0 = jnp.exp(m_prev - m_end)
    mrun_c = lambda h: m_c[:, h:h + 1]
    inter_c = lambda h: inter[:, h:h + 1]
    eneg_c_ = lambda h: eneg_c[:, h:h + 1]
    wend_c = lambda h: wend[:, h:h + 1]
    g = g_rows
    lane = _iota((BLK, LANES), 1)
    low = lane < HEAD_DIM
    one_col = jnp.where(lane == HEAD_DIM, 1.0, 0.0)
    causal = _iota((BLK, BLK), 1) <= _iota((BLK, BLK), 0)

    outs = []
    for p in range(2):
        sl = slice(p * LANES, (p + 1) * LANES)
        kp, vp = k[:, sl], dv[:, sl]
        qb = dq[:, sl].astype(BF16)
        qm = [qb * hm[0], qb * hm[1]]
        sc2 = _dot_nt(jnp.concatenate(qm, axis=0), kp.astype(BF16))
        v_sw = pltpu.roll(vp, HEAD_DIM, 1)
        hv = []
        for hh in range(2):
            h = 2 * p + hh
            sc = sc2[hh * BLK:(hh + 1) * BLK]
            w = jnp.exp(jnp.where(causal, g[h:h + 1, :] - mrun_c(h), NEG_BIG))
            sw = (sc * w).astype(BF16)
            vaug = jnp.where(low, vp if hh == 0 else v_sw, one_col).astype(BF16)
            cst = cst_ref[h]
            nd = inter_c(h) * _dot(qm[hh], cst.astype(BF16)) + _dot(sw, vaug)
            den = nd[:, HEAD_DIM:HEAD_DIM + 1]
            hv.append(nd / jnp.maximum(jnp.abs(den), eneg_c_(h)))
            kw = kp * (wend_c(h) * hmf[hh])
            cst_ref[h] = dec0[:, h:h + 1] * cst + _dot(kw.T.astype(BF16), vaug)
        outs.append(jnp.where(low, hv[0], pltpu.roll(hv[1], HEAD_DIM, 1)))
        yield
    hcat = jnp.concatenate(outs, axis=1)
    return _seg_rms(hcat, mg, ones_bd) * _sigmoid(do)


def _mixer_kernel(hc_ref, hn_ref, pv_ref, win_ref, convw_ref, wr_ref, wi_ref, rr_ref, rbase_ref, gamma_ref,
                  ones_ref, msmall_ref, mbig_ref, gb_ref, triu_ref, *rest, layer, tm):
    y_ref, hl_ref, conv_ref, kst_ref, vst_ref, sst_ref, cst_ref, mst_ref, u_scr, nb_scr = rest[-10:]
    t = pl.program_id(0)
    states = (hl_ref, conv_ref, kst_ref, vst_ref, sst_ref, cst_ref, mst_ref)

    @pl.when(t == 0)
    def _init():
        for ref in states:
            ref[...] = jnp.zeros_like(ref)

    if layer == 0:
        hl_ref, conv_ref, kst_ref, vst_ref, sst_ref, cst_ref, mst_ref = (ref.at[0] for ref in states)

    bsz = hc_ref.shape[0]
    g1 = pv_ref[layer:layer + 1, V_G1:V_G1 + D_MODEL]
    cols = (((C_AX, C_BQ - C_AX),), ((C_BQ, C_CQ - C_BQ),),
            ((C_CQ, 2 * GROUP_W), (C_CI, 2 * GROUP_W)),
            ((C_DQ, 2 * GROUP_W), (C_DV, N_IN - C_DV)))

    def normalise(src_ref):
        for b in range(bsz):
            nb_scr[b * tm:(b + 1) * tm, :] = _rms_rows(src_ref[b], g1).astype(BF16)

    def project(group_ids):
        for gid in group_ids:
            for c0, w in cols[gid]:
                u_scr[:, c0:c0 + w] = _dot_nt(nb_scr[...], win_ref[c0:c0 + w, :])
                yield

    phases = ((0, 1), (2, 3))

    @pl.when(t == 0)
    def _prologue():
        normalise(hc_ref)
        _round_robin([(lambda: True, project(phases[0]))])

    def vec(off, w):
        return pv_ref[layer:layer + 1, off:off + w]

    ctxs = []
    for b in range(bsz):
        proj = functools.partial(lambda c0, w, b: u_scr[b * tm:(b + 1) * tm, c0:c0 + w], b=b)
        ctxs.append(types.SimpleNamespace(
            t=t, proj=proj, vec=vec, layer=layer, tm=tm, ones_bd=ones_ref[...], convw_ref=convw_ref,
            wr_ref=wr_ref, wi_ref=wi_ref, rr_ref=rr_ref, rbase_ref=rbase_ref, gamma_ref=gamma_ref,
            msmall_ref=msmall_ref, mbig_ref=mbig_ref, gb_ref=gb_ref, triu_ref=triu_ref, y_ref=y_ref.at[b],
            hl_ref=hl_ref.at[b], conv_ref=conv_ref.at[b], kst_ref=kst_ref.at[b], vst_ref=vst_ref.at[b],
            sst_ref=sst_ref.at[b], cst_ref=cst_ref.at[b], mst_ref=mst_ref.at[b],
            hm=_head_masks(BLK, BF16), hmf=_head_masks(1, F32)))
    groups = (_group_a, _group_b, _group_c, _group_d)
    for pi, phase in enumerate(phases):
        if pi == 1:
            normalise(hn_ref)
        tasks = [(lambda: True, project(phases[1 - pi]))]
        for c in ctxs:
            for gid in phase:
                tasks.append((lambda: True, groups[gid](c)))
        _round_robin(tasks)


def _round_robin(tasks):
    tasks = list(tasks)
    while tasks:
        for task in list(tasks):
            ready, gen = task
            if not ready():
                continue
            try:
                next(gen)
            except StopIteration:
                tasks.remove(task)


def _group_a(c):
    proj, vec, tm, layer = c.proj, c.vec, c.tm, c.layer
    conv_ref, convw_ref, wr_ref, wi_ref, hl_ref, y_ref = (c.conv_ref, c.convw_ref, c.wr_ref, c.wi_ref,
                                                          c.hl_ref, c.y_ref)
    xa = proj(C_AX, GROUP_W)
    ga = proj(C_AG, GROUP_W)
    cw = convw_ref[layer]
    tail = conv_ref[...]
    sub8 = _iota((SUBLANES, GROUP_W), 0)

    def shifted(j):
        r = pltpu.roll(xa, j, 0)
        head = jnp.where(sub8 < j, pltpu.roll(tail, j, 0), r[0:SUBLANES])
        return jnp.concatenate([head, r[SUBLANES:]], axis=0)

    xc = (vec(V_CB, GROUP_W) + shifted(3) * cw[0:1, :] + shifted(2) * cw[1:2, :]
          + shifted(1) * cw[2:3, :] + xa * cw[3:4, :])
    conv_ref[...] = xa[tm - SUBLANES:tm]
    yield
    a, bx = _rglru_gates(xc, wr_ref[...], wi_ref[...], vec(V_BR, GROUP_W), vec(V_BI, GROUP_W),
                         vec(V_LAM, GROUP_W))
    sub_t = _iota((tm, GROUP_W), 0) & (SUBLANES - 1)
    s = 1
    while s < SUBLANES:
        keep = sub_t >= s
        a_s = pltpu.roll(a, s, 0)
        b_s = pltpu.roll(bx, s, 0)
        bx = jnp.where(keep, a * b_s + bx, bx)
        a = jnp.where(keep, a * a_s, a)
        s *= 2
    carry = hl_ref[...]
    hs = []
    for g in range(tm // SUBLANES):
        rows = slice(g * SUBLANES, (g + 1) * SUBLANES)
        hg_ = a[rows] * carry + bx[rows]
        hs.append(hg_)
        carry = hg_[SUBLANES - 1:SUBLANES]
    hseq = jnp.concatenate(hs, axis=0)
    hl_ref[...] = carry
    yield
    y_ref[:, 0:GROUP_W] = (hseq * _gelu_tanh(ga)).astype(y_ref.dtype)


def _group_b(c):
    proj, vec, tm, t, ones_bd = c.proj, c.vec, c.tm, c.t, c.ones_bd
    rr_ref, rbase_ref, kst_ref, vst_ref, y_ref = c.rr_ref, c.rbase_ref, c.kst_ref, c.vst_ref, c.y_ref
    cb = rbase_ref[:, 0:LANES]
    sb_ = rbase_ref[:, LANES:2 * LANES]
    rc = cb * rr_ref[0] - sb_ * rr_ref[1]
    ra = sb_ * rr_ref[2] + cb * rr_ref[3]
    rb = sb_ * rr_ref[4] + cb * rr_ref[5]
    q = _seg_rms(proj(C_BQ, GROUP_W), vec(V_QG, GROUP_W), ones_bd)
    q = _rope(q, jnp.concatenate([rc, rc], 1), jnp.concatenate([ra, ra], 1), jnp.concatenate([rb, rb], 1))
    k = _seg_rms(proj(C_BK, LANES), vec(V_KG, LANES), ones_bd[0:LANES, 0:LANES])
    k = _rope(k, rc, ra, rb)
    v = proj(C_BV, LANES)
    sink = vec(V_SINK, LANES)
    k_prev, v_prev = kst_ref[...], vst_ref[...]
    for j in range(tm // BLK):
        rs = slice(j * BLK, (j + 1) * BLK)
        first = (t == 0) if j == 0 else False
        yb = _swa_block(q[rs], k[rs], v[rs], k_prev, v_prev, sink, first)
        y_ref[rs, GROUP_W:2 * GROUP_W] = yb.astype(y_ref.dtype)
        k_prev, v_prev = k[rs], v[rs]
        yield
    kst_ref[...] = k_prev
    vst_ref[...] = v_prev


def _group_c(c):
    proj, vec, tm, layer, ones_bd, hm = c.proj, c.vec, c.tm, c.layer, c.ones_bd, c.hm
    gamma_ref, sst_ref, msmall_ref, mbig_ref, y_ref = c.gamma_ref, c.sst_ref, c.msmall_ref, c.mbig_ref, c.y_ref
    lb = _lb_from_gamma([gamma_ref[i:i + 1, :] for i in range(gamma_ref.shape[0])], layer)
    cq, cf, ci, cg = (proj(C_CQ, GROUP_W), proj(C_CF, GROUP_W), proj(C_CI, GROUP_W), proj(C_CG, GROUP_W))
    hg = vec(V_HG, GROUP_W)
    sub = _iota((BLK, GROUP_W), 0) & (SUBLANES - 1)
    for j in range(tm // BLK):
        rs = slice(j * BLK, (j + 1) * BLK)
        yc = yield from _hgrn_block(cq[rs], cf[rs], ci[rs], cg[rs], lb, hg, sst_ref, ones_bd, msmall_ref, mbig_ref,
                         hm, sub)
        y_ref[rs, 2 * GROUP_W:3 * GROUP_W] = yc.astype(y_ref.dtype)
        yield


def _group_d(c):
    proj, vec, tm, layer, ones_bd, hm, hmf = c.proj, c.vec, c.tm, c.layer, c.ones_bd, c.hm, c.hmf
    gb_ref, triu_ref, mst_ref, cst_ref, y_ref = c.gb_ref, c.triu_ref, c.mst_ref, c.cst_ref, c.y_ref
    dq, dk, dv, do = (proj(C_DQ, GROUP_W), proj(C_DK, GROUP_W), proj(C_DV, GROUP_W), proj(C_DO, GROUP_W))
    mg = vec(V_MG, GROUP_W)
    m_prev = mst_ref[...]
    g_rows, g_c, m_c, eneg_c, m_new = _mlstm_tile_gates(proj(C_GATES, LANES), gb_ref[layer], m_prev,
                                                       triu_ref[...])
    mst_ref[...] = m_new
    yield
    for j in range(tm // BLK):
        rs = slice(j * BLK, (j + 1) * BLK)
        yd = yield from _mlstm_block(dq[rs], dk[rs], dv[rs], do[rs], g_rows[:, rs], g_c[rs], m_c[rs], eneg_c[rs],
                          m_prev, mg, cst_ref, ones_bd, hm, hmf)
        y_ref[rs, 3 * GROUP_W:4 * GROUP_W] = yd.astype(y_ref.dtype)
        m_prev = m_c[(j + 1) * BLK - 1:(j + 1) * BLK]
        yield


def _full(shape):
    nd = len(shape)
    return pl.BlockSpec(shape, lambda *_: (0,) * nd)


def _layer_block(shape, layer, single_buffer=False):
    nd = len(shape) - 1
    kw = {'pipeline_mode': pl.Buffered(1)} if single_buffer else {}
    return pl.BlockSpec((None,) + tuple(shape[1:]), lambda *_: (layer,) + (0,) * nd, **kw)


def _prompt_mixers(h, cw, layer, tm, prev):
    bsz, t, _ = h.shape
    nt = t // tm
    kern = functools.partial(_mixer_kernel, layer=layer, tm=tm)
    in_specs = [pl.BlockSpec((bsz, tm, D_MODEL), lambda i: (0, i, 0)),
                pl.BlockSpec((bsz, tm, D_MODEL), lambda i: (0, jnp.minimum(i + 1, nt - 1), 0)),
                _full(cw['vecs'].shape), _layer_block(cw['w_in_p'].shape, layer, single_buffer=True),
                _full(cw['conv_w'].shape), _layer_block(cw['wr_bd'].shape, layer),
                _layer_block(cw['wi_bd'].shape, layer), _full(cw['rope_r'].shape),
                pl.BlockSpec((None, 1, 2 * LANES), lambda i: (i, 0, 0)),
                _full(cw['gamma'].shape), _full(cw['ones_bd'].shape), _full(cw['lvl_small'].shape),
                _full(cw['lvl_big'].shape), _full(cw['gate_bias'].shape), _full(cw['triu'].shape)]
    depth = cw['vecs'].shape[0]
    st_shapes = [(depth, bsz) + s for s in ((1, GROUP_W), (8, GROUP_W), (BLK, LANES), (BLK, LANES),
                                            (2, LANES, LANES), (N_HEADS, LANES, LANES), (1, LANES))]
    out_shape = ([jax.ShapeDtypeStruct((bsz, t, D_MODEL), BF16)]
                 + [jax.ShapeDtypeStruct(s, F32) for s in st_shapes])
    st_specs = [_full(s) if prev is None else _layer_block(s, layer) for s in st_shapes]
    chain = [] if prev is None else list(prev)
    n_in = len(in_specs)
    return pl.pallas_call(
        kern, grid=(nt,), in_specs=in_specs + [pl.BlockSpec(memory_space=pl.ANY)] * len(chain),
        out_specs=[pl.BlockSpec((bsz, tm, D_MODEL), lambda i: (0, i, 0))] + st_specs, out_shape=out_shape,
        input_output_aliases={n_in + i: 1 + i for i in range(len(chain))},
        scratch_shapes=[pltpu.VMEM((bsz * tm, N_IN), F32), pltpu.VMEM((bsz * tm, D_MODEL), BF16)],
        compiler_params=pltpu.CompilerParams(dimension_semantics=("arbitrary",),
                                             vmem_limit_bytes=VMEM_LIMIT),
        name=f"prompt_mixers_l{layer}",
    )(h, h, cw['vecs'], cw['w_in_p'], cw['conv_w'], cw['wr_bd'], cw['wi_bd'], cw['rope_r'], cw['rope_base'],
      cw['gamma'], cw['ones_bd'], cw['lvl_small'], cw['lvl_big'], cw['gate_bias'], cw['triu'], *chain)


def _ffn_math(h, yb, p, g2, wout_ref, wup_ref, wdn_ref, wg_ref, wp_ref):
    h = h + _dot(yb, wout_ref[...])
    nb = _rms_rows(h, g2).astype(BF16)
    acc = h
    step = 1024
    for c in range(0, D_FF, step):
        f = jnp.maximum(_dot(nb, wup_ref[:, c:c + step]), 0.0)
        acc = acc + _dot((f * f).astype(BF16), wdn_ref[c:c + step, :])
    gate = _sigmoid(_dot(acc.astype(BF16), wg_ref[...]))
    return acc + gate * _dot(p.astype(BF16), wp_ref[...])


def _ffn_kernel(h_ref, y_ref, p_ref, pv_ref, wout_ref, wup_ref, wdn_ref, wg_ref, wp_ref, o_ref, *, layer):
    g2 = pv_ref[layer:layer + 1, V_G2:V_G2 + D_MODEL]
    o_ref[...] = _ffn_math(h_ref[...], y_ref[...], p_ref[...], g2, wout_ref, wup_ref, wdn_ref, wg_ref, wp_ref)


def _ffn_weight_specs(cw, layer):
    names = ['w_out', 'w_up', 'w_down', 'w_gate', 'w_proj']
    return [cw[n] for n in names], [_layer_block(cw[n].shape, layer, single_buffer=True) for n in names]


def _prompt_ffn(h2, y2, p3, cw, layer, tm):
    n = h2.shape[0]
    row = lambda w: pl.BlockSpec((tm, w), lambda i: (i, 0))
    ws, wspecs = _ffn_weight_specs(cw, layer)
    return pl.pallas_call(
        functools.partial(_ffn_kernel, layer=layer), grid=(n // tm,),
        in_specs=[row(D_MODEL), row(D_MODEL), pl.BlockSpec((None, tm, PLE_DIM), lambda i: (layer, i, 0)),
                  _full(cw['vecs'].shape)] + wspecs,
        out_specs=row(D_MODEL), out_shape=jax.ShapeDtypeStruct((n, D_MODEL), F32),
        compiler_params=pltpu.CompilerParams(dimension_semantics=("arbitrary",),
                                             vmem_limit_bytes=VMEM_LIMIT),
        name=f"prompt_ffn_l{layer}",
    )(h2, y2, p3, cw['vecs'], *ws)


def _sample_pre_kernel(x_ref, pv_ref, wt_ref, convw_ref, wr_ref, wi_ref, ones_ref, rope_ref, gcol_ref,
                       cols_ref, h0_ref, conv0_ref, n0_ref, m0_ref,
                       ya_ref, hn_ref, convn_ref, q_ref, kt_ref, vt_ref, fm_ref, nn_ref, mn_ref, *, layer):
    def vec(off, w):
        return pv_ref[layer:layer + 1, off:off + w]

    n = _rms_rows(x_ref[...], vec(V_G1, D_MODEL))
    nb = n.astype(BF16)
    n_t = n.T.astype(BF16)
    ones_bd = ones_ref[...]

    def proj(r0, cnt):
        return _dot_nt(nb, wt_ref[r0:r0 + cnt, :])

    def proj_t(r0, cnt):
        return _dot(wt_ref[r0:r0 + cnt, :], n_t)

    xa = proj(C_AX, GROUP_W)
    ga = proj(C_AG, GROUP_W)
    cw = convw_ref[layer]
    xc = (vec(V_CB, GROUP_W) + conv0_ref[0] * cw[0:1, :] + conv0_ref[1] * cw[1:2, :]
          + conv0_ref[2] * cw[2:3, :] + xa * cw[3:4, :])
    convn_ref[0] = conv0_ref[1]
    convn_ref[1] = conv0_ref[2]
    convn_ref[2] = xa
    a, bx = _rglru_gates(xc, wr_ref[...], wi_ref[...], vec(V_BR, GROUP_W), vec(V_BI, GROUP_W),
                         vec(V_LAM, GROUP_W))
    hn = a * h0_ref[...] + bx
    hn_ref[...] = hn
    ya_ref[...] = hn * _gelu_tanh(ga)

    rc, ra, rb = rope_ref[0:1, :], rope_ref[1:2, :], rope_ref[2:3, :]
    q = _seg_rms(proj(C_BQ, GROUP_W), vec(V_QG, GROUP_W), ones_bd)
    q_ref[...] = _rope(q, jnp.concatenate([rc, rc], 1), jnp.concatenate([ra, ra], 1),
                       jnp.concatenate([rb, rb], 1))
    k = _seg_rms(proj(C_BK, LANES), vec(V_KG, LANES), ones_bd[0:LANES, 0:LANES])
    kt_ref[...] = _rope(k, rc, ra, rb).T
    vt_ref[...] = proj(C_BV, LANES).T

    lb = _lb_from_gamma([gcol_ref[i] for i in range(gcol_ref.shape[0])], layer)
    cq, cf, ci, cg = (proj_t(C_CQ, GROUP_W), proj_t(C_CF, GROUP_W), proj_t(C_CI, GROUP_W),
                      proj_t(C_CG, GROUP_W))
    f = lb + (1.0 - lb) * _sigmoid(cf)
    fm_ref[FM_HF:FM_HF + GROUP_W, :] = f
    fm_ref[FM_HK:FM_HK + GROUP_W, :] = 1.0 - f
    fm_ref[FM_HQ:FM_HQ + GROUP_W, :] = cq * _sigmoid(cq)
    fm_ref[FM_HV:FM_HV + GROUP_W, :] = ci
    fm_ref[FM_CG:FM_CG + GROUP_W, :] = cg * _sigmoid(cg)

    dq, dk, dv, do = (proj_t(C_DQ, GROUP_W), proj_t(C_DK, GROUP_W), proj_t(C_DV, GROUP_W),
                      proj_t(C_DO, GROUP_W))
    g8 = proj_t(C_GATES, 2 * N_HEADS) + cols_ref[layer, R_GB:R_GB + 2 * N_HEADS, :]
    ig = g8[0:N_HEADS, :]
    lf = _log_sigmoid(g8)[N_HEADS:2 * N_HEADS, :]
    a_int = lf + m0_ref[...]
    m_new = jnp.maximum(a_int, ig)
    dec = jnp.exp(a_int - m_new)
    w = jnp.exp(ig - m_new)
    mn_ref[...] = m_new
    km = dk * (HEAD_DIM ** -0.5)
    dens = []
    for h in range(N_HEADS):
        sl = slice(h * HEAD_DIM, (h + 1) * HEAD_DIM)
        nn_h = dec[h:h + 1, :] * n0_ref[sl, :] + w[h:h + 1, :] * km[sl, :]
        nn_ref[sl, :] = nn_h
        dens.append(jnp.sum(dq[sl, :] * nn_h, axis=0, keepdims=True))
    pad = jnp.zeros((SUBLANES - N_HEADS, x_ref.shape[0]), F32)
    fm_ref[FM_MK:FM_MK + GROUP_W, :] = km
    fm_ref[FM_MQ:FM_MQ + GROUP_W, :] = dq
    fm_ref[FM_MV:FM_MV + GROUP_W, :] = dv
    fm_ref[FM_DO:FM_DO + GROUP_W, :] = _sigmoid(do)
    fm_ref[FM_DEC:FM_DEC + SUBLANES, :] = jnp.concatenate([dec, pad], 0)
    fm_ref[FM_W:FM_W + SUBLANES, :] = jnp.concatenate([w, pad], 0)
    fm_ref[FM_DEN:FM_DEN + SUBLANES, :] = jnp.concatenate(dens + [pad], 0)
    fm_ref[FM_ENEG:FM_ENEG + SUBLANES, :] = jnp.concatenate([jnp.exp(-m_new), pad], 0)


def _own_slab(ref, first_layer):
    if not first_layer:
        return ref
    ref[1:] = jnp.zeros((ref.shape[0] - 1,) + ref.shape[1:], ref.dtype)
    return ref.at[0]


def _sample_attn_kernel(q_ref, kn_ref, vn_ref, kc_ref, vc_ref, sink_ref, *rest, first_layer):
    ko_ref, vo_ref, o_ref = rest[-3:]
    ko_ref, vo_ref = _own_slab(ko_ref, first_layer), _own_slab(vo_ref, first_layer)
    sb = q_ref.shape[0]
    rows = 2 * HEAD_DIM
    lane = _iota((rows, WINDOW), 1)
    kn = kn_ref[...].reshape(rows, sb)
    vn = vn_ref[...].reshape(rows, sb)
    for s in range(sb):
        kt = pltpu.roll(kc_ref[s].reshape(rows, WINDOW), WINDOW - 1, 1)
        ko_ref[s] = jnp.where(lane == WINDOW - 1, kn[:, s:s + 1], kt).reshape(2, HEAD_DIM, WINDOW)
        vt = pltpu.roll(vc_ref[s].reshape(rows, WINDOW), WINDOW - 1, 1)
        vo_ref[s] = jnp.where(lane == WINDOW - 1, vn[:, s:s + 1], vt).reshape(2, HEAD_DIM, WINDOW)
    for kv in range(2):
        kk = ko_ref[:, kv].astype(BF16)
        vv = vo_ref[:, kv].astype(BF16)
        s_ = jnp.einsum('bqc,bcj->bqj', q_ref[:, kv].astype(BF16), kk,
                        preferred_element_type=F32) * (HEAD_DIM ** -0.5)
        sk = sink_ref[kv]
        mx = jnp.maximum(jnp.max(s_, axis=-1, keepdims=True), sk)
        p = jnp.exp(s_ - mx)
        den = jnp.sum(p, axis=-1, keepdims=True) + jnp.exp(sk - mx)
        o = jnp.einsum('bqj,bcj->bqc', p.astype(BF16), vv, preferred_element_type=F32)
        o_ref[:, kv] = o / den


def _sample_state_kernel(fm_ref, s_ref, c_ref, *rest, first_layer):
    so_ref, co_ref, oh_ref, om_ref = rest[-4:]
    so_ref, co_ref = _own_slab(so_ref, first_layer), _own_slab(co_ref, first_layer)
    h = pl.program_id(0)
    r0 = pl.multiple_of(h * HEAD_DIM, HEAD_DIM)
    hv = fm_ref[pl.ds(FM_HV + r0, HEAD_DIM), :]
    mv = fm_ref[pl.ds(FM_MV + r0, HEAD_DIM), :]
    dec = fm_ref[pl.ds(FM_DEC + h, 1), :]
    w = fm_ref[pl.ds(FM_W + h, 1), :]

    def body(d, carry):
        acc_h, acc_m = carry
        r = r0 + d
        s_new = fm_ref[pl.ds(FM_HF + r, 1), :] * s_ref[d] + fm_ref[pl.ds(FM_HK + r, 1), :] * hv
        so_ref[d] = s_new
        c_new = dec * c_ref[d] + (w * fm_ref[pl.ds(FM_MK + r, 1), :]) * mv
        co_ref[d] = c_new
        return (acc_h + fm_ref[pl.ds(FM_HQ + r, 1), :] * s_new,
                acc_m + fm_ref[pl.ds(FM_MQ + r, 1), :] * c_new)

    zero = jnp.zeros((HEAD_DIM, fm_ref.shape[1]), F32)
    acc_h, acc_m = lax.fori_loop(0, HEAD_DIM, body, (zero, zero), unroll=4)
    oh_ref[...] = acc_h
    om_ref[...] = acc_m


def _sample_post_kernel(h_ref, ya_ref, yb_ref, oh_ref, om_ref, fm_ref, cols_ref, p_ref, pv_ref,
                        wout_ref, wup_ref, wdn_ref, wg_ref, wp_ref, o_ref, acc_ref, nb_ref, *, layer):
    k = pl.program_id(0)

    @pl.when(k == 0)
    def _head():
        def head_rms(x):
            return x * lax.rsqrt(jnp.mean(x * x, axis=0, keepdims=True) + EPS)

        yc, yd = [], []
        for h in range(N_HEADS):
            yc.append(head_rms(oh_ref[h]))
            den = fm_ref[FM_DEN + h:FM_DEN + h + 1, :]
            eneg = fm_ref[FM_ENEG + h:FM_ENEG + h + 1, :]
            yd.append(head_rms(om_ref[h] / jnp.maximum(jnp.abs(den), eneg)))
        yc_ = (jnp.concatenate(yc, 0) * cols_ref[layer, R_HG:R_HG + GROUP_W, :]
               * fm_ref[FM_CG:FM_CG + GROUP_W, :])
        yd_ = (jnp.concatenate(yd, 0) * cols_ref[layer, R_MG:R_MG + GROUP_W, :]
               * fm_ref[FM_DO:FM_DO + GROUP_W, :])
        y = jnp.concatenate([ya_ref[...], yb_ref[...], yc_.T, yd_.T], axis=1).astype(BF16)
        h1 = h_ref[...] + _dot(y, wout_ref[...])
        acc_ref[...] = h1
        nb_ref[...] = _rms_rows(h1, pv_ref[layer:layer + 1, V_G2:V_G2 + D_MODEL]).astype(BF16)

    f = jnp.maximum(_dot(nb_ref[...], wup_ref[...]), 0.0)
    acc_ref[...] += _dot((f * f).astype(BF16), wdn_ref[...])

    @pl.when(k == pl.num_programs(0) - 1)
    def _tail():
        acc = acc_ref[...]
        gate = _sigmoid(_dot(acc.astype(BF16), wg_ref[...]))
        o_ref[...] = acc + gate * _dot(p_ref[...].astype(BF16), wp_ref[...])


def _call_full(kern, args, specs, out_shape, name):
    specs = [(_full(a.shape) if s is None else s) for a, s in zip(args, specs)]
    return pl.pallas_call(
        kern, grid=(1,), in_specs=specs,
        out_specs=tuple(_full(s.shape) for s in out_shape), out_shape=tuple(out_shape),
        compiler_params=pltpu.CompilerParams(dimension_semantics=("arbitrary",),
                                             vmem_limit_bytes=VMEM_LIMIT),
        name=name,
    )(*args)


def _sample_layer(h, sv, prev, cw, layer):
    nsm = h.shape[0]
    depth = cw['vecs'].shape[0]
    sd = lambda *shape: jax.ShapeDtypeStruct(shape, F32)
    g, l = sd(nsm, GROUP_W), sd(LANES, nsm)

    args = [h, cw['vecs'], cw['w_in_p'], cw['conv_w'], cw['wr_bd'], cw['wi_bd'], cw['ones_bd'],
            cw['rope_s'], cw['gamma_col'], cw['cols'], sv['h'], sv['conv'], sv['n'], sv['m']]
    specs = [None, None, _layer_block(cw['w_in_p'].shape, layer, single_buffer=True), None,
             _layer_block(cw['wr_bd'].shape, layer), _layer_block(cw['wi_bd'].shape, layer), None,
             None, None, None, _layer_block(sv['h'].shape, layer), _layer_block(sv['conv'].shape, layer),
             _layer_block(sv['n'].shape, layer), _layer_block(sv['m'].shape, layer)]
    outs = [g, g, sd(CONV_W - 1, nsm, GROUP_W), g, l, l, sd(FM_ROWS, nsm), sd(GROUP_W, nsm),
            sd(N_HEADS, nsm)]
    ya, hn, convn, q, kt, vt, fm, nn, mn = _call_full(
        functools.partial(_sample_pre_kernel, layer=layer), args, specs, outs, f"sample_pre_l{layer}")

    sb = 16
    nblk = nsm // sb
    q3 = jnp.pad(q.reshape(nsm, 2, 2, HEAD_DIM), ((0, 0), (0, 0), (0, SUBLANES - 2), (0, 0)))
    to_blocks = lambda a: a.reshape(2, HEAD_DIM, nblk, sb).transpose(2, 0, 1, 3)
    cshape = (depth, nsm, 2, HEAD_DIM, WINDOW)
    cspec = pl.BlockSpec((None, sb, 2, HEAD_DIM, WINDOW), lambda i: (layer, i, 0, 0, 0))
    nspec = pl.BlockSpec((None, 2, HEAD_DIM, sb), lambda i: (i, 0, 0, 0))
    qspec = pl.BlockSpec((sb, 2, SUBLANES, HEAD_DIM), lambda i: (i, 0, 0, 0))
    any_spec = pl.BlockSpec(memory_space=pl.ANY)
    cout = (pl.BlockSpec((depth, sb, 2, HEAD_DIM, WINDOW), lambda i: (0, i, 0, 0, 0)) if prev is None
            else cspec)
    chain = [] if prev is None else [prev['k'], prev['v']]
    ko, vo, o3 = pl.pallas_call(
        functools.partial(_sample_attn_kernel, first_layer=prev is None), grid=(nblk,),
        in_specs=[qspec, nspec, nspec, cspec, cspec, _layer_block(cw['sinks8'].shape, layer)]
        + [any_spec] * len(chain),
        out_specs=(cout, cout, qspec),
        out_shape=(sd(*cshape), sd(*cshape), sd(nsm, 2, SUBLANES, HEAD_DIM)),
        input_output_aliases={6 + i: i for i in range(len(chain))},
        compiler_params=pltpu.CompilerParams(dimension_semantics=("arbitrary",),
                                             vmem_limit_bytes=VMEM_LIMIT),
        name=f"sample_attn_l{layer}",
    )(q3, to_blocks(kt), to_blocks(vt), sv['k'], sv['v'], cw['sinks8'], *chain)
    yb = o3[:, :, 0:2, :].reshape(nsm, GROUP_W)

    sshape = (depth, N_HEADS, HEAD_DIM, HEAD_DIM, nsm)
    sspec = pl.BlockSpec((None, None, HEAD_DIM, HEAD_DIM, nsm), lambda i: (layer, i, 0, 0, 0))
    ospec = pl.BlockSpec((None, HEAD_DIM, nsm), lambda i: (i, 0, 0))
    sout = (pl.BlockSpec((depth, None, HEAD_DIM, HEAD_DIM, nsm), lambda i: (0, i, 0, 0, 0))
            if prev is None else sspec)
    chain = [] if prev is None else [prev['s'], prev['c']]
    so, co, oh, om = pl.pallas_call(
        functools.partial(_sample_state_kernel, first_layer=prev is None), grid=(N_HEADS,),
        in_specs=[_full(fm.shape), sspec, sspec] + [any_spec] * len(chain),
        out_specs=(sout, sout, ospec, ospec),
        out_shape=(sd(*sshape), sd(*sshape), sd(N_HEADS, HEAD_DIM, nsm), sd(N_HEADS, HEAD_DIM, nsm)),
        input_output_aliases={3 + i: i for i in range(len(chain))},
        compiler_params=pltpu.CompilerParams(dimension_semantics=("arbitrary",),
                                             vmem_limit_bytes=VMEM_LIMIT),
        name=f"sample_state_l{layer}",
    )(fm, sv['s'], sv['c'], *chain)

    ws, wspecs = _ffn_weight_specs(cw, layer)
    chunk = 1024
    wspecs[1] = pl.BlockSpec((None, D_MODEL, chunk), lambda k: (layer, 0, k))
    wspecs[2] = pl.BlockSpec((None, chunk, D_MODEL), lambda k: (layer, k, 0))
    post_args = [h, ya, yb, oh, om, fm, cw['cols'], sv['p'], cw['vecs']] + ws
    post_specs = ([_full(a.shape) for a in post_args[:7]] + [_layer_block(sv['p'].shape, layer),
                                                             _full(cw['vecs'].shape)] + wspecs)
    h_new = pl.pallas_call(
        functools.partial(_sample_post_kernel, layer=layer), grid=(D_FF // chunk,),
        in_specs=post_specs, out_specs=_full((nsm, D_MODEL)), out_shape=sd(nsm, D_MODEL),
        scratch_shapes=[pltpu.VMEM((nsm, D_MODEL), F32), pltpu.VMEM((nsm, D_MODEL), BF16)],
        compiler_params=pltpu.CompilerParams(dimension_semantics=("arbitrary",),
                                             vmem_limit_bytes=VMEM_LIMIT),
        name=f"sample_post_l{layer}",
    )(*post_args)
    small = (hn, convn, nn, mn)
    big = {'k': ko, 'v': vo, 's': so, 'c': co}
    return h_new, small, big


def _block_diag_all(w):
    depth = w.shape[0]
    rows = w.reshape(depth, GROUP_W, HEAD_DIM)
    idx = np.arange(GROUP_W) // HEAD_DIM
    mask = idx[:, None] == idx[None, :]
    return jnp.where(mask[None], jnp.tile(rows, (1, 1, N_HEADS)), 0.0)


def _rope_lane_freq():
    half = ROT_DIM // 2
    inv = np.power(np.float32(ROPE_THETA), -np.arange(half, dtype=np.float32) * np.float32(2.0 / ROT_DIM))
    dd = np.arange(LANES) % HEAD_DIM
    freq = np.where(dd < ROT_DIM, inv[dd % half], np.float32(0.0))
    m_a = (dd < half).astype(np.float32)
    m_b = ((dd >= half) & (dd < ROT_DIM)).astype(np.float32)
    return freq.astype(np.float32), m_a, m_b


def _rope_tables(pos):
    freq, m_a, m_b = _rope_lane_freq()
    ang = np.asarray(pos, np.float32)[:, None] * freq[None, :]
    cos, sin = np.cos(ang), np.sin(ang)
    return cos, -sin * m_a, sin * m_b


def _rope_split_tables(t, tm):
    freq, m_a, m_b = _rope_lane_freq()
    ang_r = np.arange(tm, dtype=np.float32)[:, None] * freq[None, :]
    cr, sr = np.cos(ang_r), np.sin(ang_r)
    rope_r = np.stack([cr, sr, -cr * m_a, -sr * m_a, cr * m_b, sr * m_b])
    ang_b = (np.arange(t // tm) * tm).astype(np.float32)[:, None] * freq[None, :]
    rope_base = np.concatenate([np.cos(ang_b), np.sin(ang_b)], 1)[:, None, :]
    return rope_r, rope_base


def _hgrn_level_masks():
    t = np.arange(BLK)[:, None]
    s = np.arange(BLK)[None, :]
    small = [t == s]
    big = []
    for lev in range(1, 8):
        half = 1 << (lev - 1)
        own = ((t >> lev) == (s >> lev)) & ((t & half) != 0) & ((s & half) == 0)
        if lev < 4:
            small.append(own)
        else:
            rows = np.concatenate([np.arange(m, m + half) for m in range(half, BLK, 2 * half)])
            big.append(own[rows])
    return np.stack(small).astype(np.float32), np.stack(big).astype(np.float32)


def _pad_last(v, width):
    return jnp.pad(v, ((0, 0), (0, width - v.shape[-1])))


def _common(w, t, tm_mix, past_len, nsm):
    depth = w['w_in'].shape[0]
    tile = lambda v, n: jnp.tile(v, (1, n))
    vecs = jnp.concatenate([
        w['norm1_g'], w['norm2_g'], w['conv_b'], w['lru_br'], w['lru_bi'], w['lru_lam'],
        tile(w['q_norm_g'], N_HEADS), tile(w['k_norm_g'], 2), _pad_last(w['attn_sinks'], LANES),
        tile(w['hgrn_norm_g'], N_HEADS), tile(w['mlstm_norm_g'], N_HEADS),
        _pad_last(w['mlstm_ib'], LANES), _pad_last(w['mlstm_fb'], LANES)], axis=1)
    cols = jnp.concatenate([tile(w['hgrn_norm_g'], N_HEADS), tile(w['mlstm_norm_g'], N_HEADS),
                            w['mlstm_ib'], w['mlstm_fb']], axis=1)
    w_in = w['w_in']
    gate_bias = jnp.concatenate([w['mlstm_ib'], w['mlstm_fb']], axis=1)
    lvl_small, lvl_big = _hgrn_level_masks()
    sinks = w['attn_sinks']
    z2 = jnp.zeros((depth, 2, SUBLANES - 2), F32)
    idx = np.arange(GROUP_W)
    rope_r, rope_base = _rope_split_tables(t, tm_mix)
    return {
        'vecs': vecs,
        'cols': jnp.broadcast_to(cols[:, :, None], cols.shape + (nsm,)),
        'gamma': w['hgrn_gamma'],
        'gamma_col': jnp.broadcast_to(w['hgrn_gamma'][:, :, None], w['hgrn_gamma'].shape + (nsm,)),
        'w_in_p': jnp.pad(jnp.swapaxes(w_in, 1, 2), ((0, 0), (0, N_IN - D_IN), (0, 0))).astype(BF16),
        'lvl_small': lvl_small, 'lvl_big': lvl_big,
        'gate_bias': jnp.broadcast_to(gate_bias[:, :, None], gate_bias.shape + (tm_mix,)),
        'triu': (np.arange(tm_mix)[:, None] <= np.arange(tm_mix)[None, :]).astype(np.float32),
        'conv_w': w['conv_w'],
        'wr_bd': _block_diag_all(w['lru_wr']).astype(BF16),
        'wi_bd': _block_diag_all(w['lru_wi']).astype(BF16),
        'ones_bd': jnp.asarray(idx[:, None] // HEAD_DIM == idx[None, :] // HEAD_DIM, BF16),
        'sinks8': jnp.concatenate([sinks.reshape(depth, 2, 2), z2], axis=2)[..., None],
        'rope_r': rope_r, 'rope_base': rope_base,
        'rope_s': np.concatenate(_rope_tables(past_len + np.arange(1)), axis=0),
        'w_out': w['w_out'].astype(BF16), 'w_up': w['w_up'].astype(BF16),
        'w_down': w['w_down'].astype(BF16), 'w_gate': w['w_ple_gate'].astype(BF16),
        'w_proj': w['w_ple_proj'].astype(BF16),
    }


def _run(x_prompt, x_sample, p_prompt, p_sample, sample_state, w, past_len, tm_mix=256, tm_ffn=512):
    depth = w['w_in'].shape[0]
    bsz, t, _ = x_prompt.shape
    nsm = x_sample.shape[0]
    cw = _common(w, t, tm_mix, past_len, nsm)
    tm_ffn = min(tm_ffn, bsz * t)
    h0, conv0, kc, vc, s0, c0, n0, m0 = sample_state
    sv = {'h': h0, 'conv': jnp.transpose(conv0, (0, 2, 1, 3)),
          'k': jnp.transpose(kc, (0, 1, 3, 4, 2)), 'v': jnp.transpose(vc, (0, 1, 3, 4, 2)),
          's': jnp.transpose(s0, (0, 2, 3, 4, 1)), 'c': jnp.transpose(c0, (0, 2, 3, 4, 1)),
          'n': jnp.transpose(n0, (0, 2, 3, 1)).reshape(depth, GROUP_W, nsm),
          'm': jnp.transpose(m0, (0, 2, 1)), 'p': p_sample.reshape(depth, nsm, PLE_DIM)}
    p3 = p_prompt.reshape(depth, bsz * t, PLE_DIM)

    hp = x_prompt
    hs = x_sample.reshape(nsm, D_MODEL)
    pst, s_small, big = None, [], None
    for l in range(depth):
        y, *pst = _prompt_mixers(hp, cw, l, tm_mix, pst)
        hp = _prompt_ffn(hp.reshape(bsz * t, D_MODEL), y.reshape(bsz * t, D_MODEL), p3, cw, l,
                         tm_ffn).reshape(bsz, t, D_MODEL)
        hs, small, big = _sample_layer(hs, sv, big, cw, l)
        s_small.append(small)
    stack = lambda sts, i: jnp.stack([s[i] for s in sts])
    hl, conv, kst, vst, sst, cst, mst = pst
    s_t = jnp.swapaxes(sst, -1, -2).reshape(depth, bsz, 2, 2, HEAD_DIM, 2, HEAD_DIM)
    s_hgrn = jnp.stack([s_t[:, :, :, 0, :, 0, :], s_t[:, :, :, 1, :, 1, :]], axis=3)
    c_t = cst.reshape(depth, bsz, 2, 2, 2, HEAD_DIM, LANES)
    c_rows = jnp.stack([c_t[:, :, :, 0, 0], c_t[:, :, :, 1, 1]], axis=3)
    c_rows = c_rows.reshape(depth, bsz, N_HEADS, HEAD_DIM, LANES)
    prompt_out = (hl[:, :, 0], conv[:, :, 8 - (CONV_W - 1):], kst.reshape(depth, bsz, WINDOW, 2, HEAD_DIM),
                  vst.reshape(depth, bsz, WINDOW, 2, HEAD_DIM),
                  s_hgrn.reshape(depth, bsz, N_HEADS, HEAD_DIM, HEAD_DIM), c_rows[..., 0:HEAD_DIM],
                  c_rows[..., HEAD_DIM], mst[:, :, 0, 0:N_HEADS])
    hn, convn, nn, mn = (stack(s_small, i) for i in range(4))
    sample_out = (hn, jnp.transpose(convn, (0, 2, 1, 3)),
                  jnp.transpose(big['k'], (0, 1, 4, 2, 3)), jnp.transpose(big['v'], (0, 1, 4, 2, 3)),
                  jnp.transpose(big['s'], (0, 4, 1, 2, 3)), jnp.transpose(big['c'], (0, 4, 1, 2, 3)),
                  jnp.transpose(nn.reshape(depth, N_HEADS, HEAD_DIM, nsm), (0, 3, 1, 2)),
                  jnp.transpose(mn, (0, 2, 1)))
    return (hp, hs.reshape(x_sample.shape)) + prompt_out + sample_out


def kernel(x_prompt, x_sample, p_prompt, p_sample, state_rglru_h, state_rglru_conv, cache_swa_k, cache_swa_v, state_hgrn_s, state_mlstm_c, state_mlstm_n, state_mlstm_m, norm1_g, w_in, conv_w, conv_b, lru_wr, lru_br, lru_wi, lru_bi, lru_lam, q_norm_g, k_norm_g, attn_sinks, hgrn_gamma, hgrn_norm_g, mlstm_ib, mlstm_fb, mlstm_norm_g, w_out, norm2_g, w_up, w_down, w_ple_gate, w_ple_proj):
    w = {'norm1_g': norm1_g, 'w_in': w_in, 'conv_w': conv_w, 'conv_b': conv_b, 'lru_wr': lru_wr,
         'lru_br': lru_br, 'lru_wi': lru_wi, 'lru_bi': lru_bi, 'lru_lam': lru_lam, 'q_norm_g': q_norm_g,
         'k_norm_g': k_norm_g, 'attn_sinks': attn_sinks, 'hgrn_gamma': hgrn_gamma,
         'hgrn_norm_g': hgrn_norm_g, 'mlstm_ib': mlstm_ib, 'mlstm_fb': mlstm_fb,
         'mlstm_norm_g': mlstm_norm_g, 'w_out': w_out, 'norm2_g': norm2_g, 'w_up': w_up,
         'w_down': w_down, 'w_ple_gate': w_ple_gate, 'w_ple_proj': w_ple_proj}
    st = (state_rglru_h, state_rglru_conv, cache_swa_k, cache_swa_v, state_hgrn_s, state_mlstm_c,
          state_mlstm_n, state_mlstm_m)
    past_len = 8192
    return _run(x_prompt, x_sample, p_prompt, p_sample, st, w, past_len)
```

```python
import functools
import types

import jax
import jax.numpy as jnp
import numpy as np
from jax import lax
from jax.experimental import pallas as pl
from jax.experimental.pallas import tpu as pltpu

F32 = jnp.float32
BF16 = jnp.bfloat16

D_MODEL = 1024
GROUP_W = 256
HEAD_DIM = 64
N_HEADS = 4
EPS = 1e-6
NEG_BIG = -1e30
LRU_C = 8.0
CONV_W = 4
ROT_DIM = 16
ROPE_THETA = 500000.0
WINDOW = 128
D_FF = 4096
PLE_DIM = 256
LANES = 128
SUBLANES = 8
BLK = 128

C_AX, C_AG, C_BQ, C_BK, C_BV = 0, 256, 512, 768, 896
C_CQ, C_CF, C_CI, C_CG = 1024, 1280, 1536, 1792
C_DQ, C_DK, C_DV, C_DO = 2048, 2304, 2560, 2816
C_GATES = 3072
D_IN = 3080
N_IN = 3200

V_G1, V_G2, V_CB, V_BR, V_BI, V_LAM = 0, 1024, 2048, 2304, 2560, 2816
V_QG, V_KG, V_SINK, V_HG, V_MG, V_IB, V_FB = 3072, 3328, 3456, 3584, 3840, 4096, 4224
N_VEC = 4352

R_HG, R_MG, R_GB = 0, 256, 512
N_COL = 520

FM_HF, FM_HK, FM_HQ, FM_HV, FM_CG = 0, 256, 512, 768, 1024
FM_MK, FM_MQ, FM_MV, FM_DO = 1280, 1536, 1792, 2048
FM_DEC, FM_W, FM_DEN, FM_ENEG = 2304, 2312, 2320, 2328
FM_ROWS = 2336

VMEM_LIMIT = 56 * 1024 * 1024


def _dot(a, b):
    return jnp.dot(a, b, preferred_element_type=F32)


def _dot_nt(a, b):
    return lax.dot_general(a, b, (((1,), (1,)), ((), ())), preferred_element_type=F32)


def _sigmoid(x):
    return jax.nn.sigmoid(x)


def _gelu_tanh(x):
    return 0.5 * x * (1.0 + jnp.tanh(0.7978845608028654 * (x + 0.044715 * (x * x * x))))


def _log_sigmoid(x):
    return jnp.minimum(x, 0.0) - jnp.log1p(jnp.exp(-jnp.abs(x)))


def _softplus(x):
    return jnp.maximum(x, 0.0) + jnp.log1p(jnp.exp(-jnp.abs(x)))


def _rms_rows(x, g):
    return x * lax.rsqrt(jnp.mean(x * x, axis=-1, keepdims=True) + EPS) * g


def _seg_mean_sq(x, ones_bd):
    sq = x * x
    hi = sq.astype(BF16)
    lo = (sq - hi.astype(F32)).astype(BF16)
    return (_dot(hi, ones_bd) + _dot(lo, ones_bd)) * (1.0 / HEAD_DIM)


def _seg_rms(x, g, ones_bd):
    return x * lax.rsqrt(_seg_mean_sq(x, ones_bd) + EPS) * g


def _rope(x, c, sa, sb):
    w = x.shape[1]
    up = pltpu.roll(x, w - ROT_DIM // 2, 1)
    dn = pltpu.roll(x, ROT_DIM // 2, 1)
    return x * c + up * sa + dn * sb


def _lb_from_gamma(gammas, layer):
    mx = functools.reduce(jnp.maximum, gammas)
    e = [jnp.exp(g - mx) for g in gammas]
    tot = functools.reduce(lambda a, b: a + b, e)
    lb = jnp.zeros_like(tot)
    for i in range(1, layer + 1):
        lb = lb + e[i] / tot
    return lb


def _iota(shape, axis):
    return lax.broadcasted_iota(jnp.int32, shape, axis)


def _rglru_gates(xc, wr, wi, br, bi, lam):
    xcb = xc.astype(BF16)
    r = _sigmoid(_dot(xcb, wr) + br)
    ig = _sigmoid(_dot(xcb, wi) + bi)
    log_a = (-LRU_C) * r * _softplus(-lam)
    a = jnp.exp(log_a)
    y = 1.0 - a * a
    root = jnp.where(y > 0.0, y * lax.rsqrt(y), 0.0)
    return a, root * (ig * xc)


def _swa_block(q, k, v, k_prev, v_prev, sink, first):
    kk = jnp.concatenate([k_prev, k], axis=0)
    vv = jnp.concatenate([v_prev, v], axis=0)
    qi = _iota((BLK, 2 * BLK), 0)
    kj = _iota((BLK, 2 * BLK), 1)
    valid = (kj > qi) & (kj <= qi + WINDOW) & ((kj >= BLK) | jnp.logical_not(first))
    lane = _iota((2 * BLK, LANES), 1)
    one_col = jnp.where(lane == HEAD_DIM, 1.0, 0.0)
    vaug = (jnp.where(lane < HEAD_DIM, vv, one_col).astype(BF16),
            jnp.where(lane < HEAD_DIM, pltpu.roll(vv, HEAD_DIM, 1), one_col).astype(BF16))
    qs = q * (HEAD_DIM ** -0.5)
    outs = []
    for h in range(N_HEADS):
        kv = h // 2
        qh = qs[:, h * HEAD_DIM:(h + 1) * HEAD_DIM].astype(BF16)
        kh = kk[:, kv * HEAD_DIM:(kv + 1) * HEAD_DIM].astype(BF16)
        s = jnp.where(valid, _dot_nt(qh, kh), NEG_BIG)
        sk = sink[:, h:h + 1]
        mx = jnp.maximum(jnp.max(s, axis=-1, keepdims=True), sk)
        p = jnp.exp(s - mx)
        o = _dot(p.astype(BF16), vaug[kv])
        den = o[:, HEAD_DIM:HEAD_DIM + 1] + jnp.exp(sk - mx)
        outs.append(o[:, 0:HEAD_DIM] / den)
    return jnp.concatenate(outs, axis=1)


def _head_masks(rows, dtype):
    lane = _iota((rows, LANES), 1)
    return (jnp.where(lane < HEAD_DIM, 1.0, 0.0).astype(dtype),
            jnp.where(lane >= HEAD_DIM, 1.0, 0.0).astype(dtype))


def _pair_scores(qe, ke_b, hm):
    n = qe.shape[0]
    if n != hm[0].shape[0]:
        hm = _head_masks(n, BF16)
    res = []
    for p in range(2):
        sl = slice(p * LANES, (p + 1) * LANES)
        qb = qe[:, sl].astype(BF16)
        lhs = jnp.concatenate([qb * hm[0], qb * hm[1]], axis=0)
        pr = _dot_nt(lhs, ke_b[:, sl])
        res += [pr[:n], pr[n:]]
    return res


def _hgrn_block(cq, cf, ci, cg, lb, hg, sst_ref, ones_bd, msmall_ref, mbig_ref, hm, sub):
    q = cq * _sigmoid(cq)
    f = lb + (1.0 - lb) * _sigmoid(cf)
    logf = jnp.log(f)
    k = 1.0 - f

    att = [s_ * msmall_ref[0] for s_ in _pair_scores(q, k.astype(BF16), hm)]

    c = logf
    tot = logf
    for lev in range(1, 4):
        half = 1 << (lev - 1)
        right = (sub & half) != 0
        tot_l = pltpu.roll(tot, half, 0)
        tot_r = pltpu.roll(tot, BLK - half, 0)
        e = jnp.exp(jnp.where(right, c, tot - c))
        sc = _pair_scores(q * e, (k * e).astype(BF16), hm)
        m = msmall_ref[lev]
        att = [a_ + s_ * m for a_, s_ in zip(att, sc)]
        c = c + jnp.where(right, tot_l, 0.0)
        tot = tot + jnp.where(right, tot_l, tot_r)

    pieces, carry = [], None
    for g in range(BLK // SUBLANES):
        rows = slice(g * SUBLANES, (g + 1) * SUBLANES)
        pieces.append(c[rows] if carry is None else c[rows] + carry)
        t_g = tot[g * SUBLANES:g * SUBLANES + 1]
        carry = t_g if carry is None else carry + t_g
    yield
    b = jnp.concatenate(pieces, axis=0)
    btot = carry

    for lev in range(4, 8):
        half = 1 << (lev - 1)
        nblk = BLK // (2 * half)
        qr, kf = [], []
        for i in range(nblk):
            lo = i * 2 * half
            mid = lo + half
            bref = b[mid - 1:mid]
            qr.append(q[mid:mid + half] * jnp.exp(b[mid:mid + half] - bref))
            kf.append(k[lo:mid] * jnp.exp(bref - b[lo:mid]))
            kf.append(jnp.zeros((half, GROUP_W), F32))
        sc = _pair_scores(jnp.concatenate(qr, axis=0), jnp.concatenate(kf, axis=0).astype(BF16), hm)
        m = mbig_ref[lev - 4]
        zero = jnp.zeros((half, BLK), F32)
        new = []
        for a_, s_ in zip(att, sc):
            u = s_ * m
            parts = []
            for i in range(nblk):
                parts += [zero, u[i * half:(i + 1) * half]]
            new.append(a_ + jnp.concatenate(parts, axis=0))
        att = new

    yield
    qe = (q * jnp.exp(b)).astype(BF16)
    ke = (k * jnp.exp(btot - b)).astype(BF16)
    etot = jnp.exp(btot)
    row_l = _iota((BLK, BLK), 0)
    col_l = _iota((BLK, BLK), 1)
    same_head = (row_l >= HEAD_DIM) == (col_l >= HEAD_DIM)
    outs = []
    for p in range(2):
        sl = slice(p * LANES, (p + 1) * LANES)
        st = sst_ref[p]
        vp = ci[:, sl]
        vb = vp.astype(BF16)
        a2 = jnp.concatenate([att[2 * p], att[2 * p + 1]], axis=1).astype(BF16)
        v2 = jnp.concatenate([vb * hm[0], vb * hm[1]], axis=0)
        o = _dot_nt(qe[:, sl], st.astype(BF16)) + _dot(a2, v2)
        upd = _dot(vp.T.astype(BF16), ke[:, sl])
        sst_ref[p] = st * etot[:, sl] + jnp.where(same_head, upd, 0.0)
        outs.append(o)
    o = jnp.concatenate(outs, axis=1)
    return _seg_rms(o, hg, ones_bd) * (cg * _sigmoid(cg))


def _mlstm_tile_gates(gcols, gb, m0, triu):
    tm = gcols.shape[0]
    gt = gcols.T[0:SUBLANES, :] + gb
    lf = _log_sigmoid(gt)
    hi = lf.astype(BF16)
    r1 = lf - hi.astype(F32)
    mid = r1.astype(BF16)
    lo = (r1 - mid.astype(F32)).astype(BF16)
    parts = jnp.concatenate([hi.astype(F32), mid.astype(F32), lo.astype(F32)], axis=0)
    cs = _dot(parts, triu)
    fcum = cs[0:SUBLANES] + cs[SUBLANES:2 * SUBLANES] + cs[2 * SUBLANES:3 * SUBLANES]
    fcum = pltpu.roll(fcum, N_HEADS, 0)
    g = gt - fcum
    pad = jnp.zeros((LANES - 2 * SUBLANES, tm), F32)
    cols = jnp.concatenate([g, fcum, pad], axis=0).T
    f_c = pltpu.roll(cols, LANES - SUBLANES, 1)
    sub = _iota((tm, LANES), 0) & (SUBLANES - 1)
    cm = cols
    s = 1
    while s < SUBLANES:
        cm = jnp.maximum(cm, jnp.where(sub >= s, pltpu.roll(cm, s, 0), NEG_BIG))
        s *= 2
    carry = m0
    ms = []
    for grp in range(tm // SUBLANES):
        m_g = jnp.maximum(cm[grp * SUBLANES:(grp + 1) * SUBLANES], carry)
        ms.append(m_g)
        carry = m_g[SUBLANES - 1:SUBLANES]
    m_c = jnp.concatenate(ms, axis=0)
    eneg_c = jnp.exp(-(f_c + m_c))
    m_new = f_c[tm - 1:tm] + carry
    return g, cols, m_c, eneg_c, m_new


def _mlstm_block(dq, dk, dv, do, g_rows, g_c, m_c, eneg_c, m_prev, mg, cst_ref, ones_bd, hm, hmf):
    k = dk * (HEAD_DIM ** -0.5)
    m_end = m_c[BLK - 1:BLK]
    inter = jnp.exp(m_prev - m_c)
    wend = jnp.exp(g_c - m_end)
    dec0 = jnp.exp(m_prev - m_end)
    mrun_c = lambda h: m_c[:, h:h + 1]
    inter_c = lambda h: inter[:, h:h + 1]
    eneg_c_ = lambda h: eneg_c[:, h:h + 1]
    wend_c = lambda h: wend[:, h:h + 1]
    g = g_rows
    lane = _iota((BLK, LANES), 1)
    low = lane < HEAD_DIM
    one_col = jnp.where(lane == HEAD_DIM, 1.0, 0.0)
    causal = _iota((BLK, BLK), 1) <= _iota((BLK, BLK), 0)

    outs = []
    for p in range(2):
        sl = slice(p * LANES, (p + 1) * LANES)
        kp, vp = k[:, sl], dv[:, sl]
        qb = dq[:, sl].astype(BF16)
        qm = [qb * hm[0], qb * hm[1]]
        sc2 = _dot_nt(jnp.concatenate(qm, axis=0), kp.astype(BF16))
        v_sw = pltpu.roll(vp, HEAD_DIM, 1)
        hv = []
        for hh in range(2):
            h = 2 * p + hh
            sc = sc2[hh * BLK:(hh + 1) * BLK]
            w = jnp.exp(jnp.where(causal, g[h:h + 1, :] - mrun_c(h), NEG_BIG))
            sw = (sc * w).astype(BF16)
            vaug = jnp.where(low, vp if hh == 0 else v_sw, one_col).astype(BF16)
            cst = cst_ref[h]
            nd = inter_c(h) * _dot(qm[hh], cst.astype(BF16)) + _dot(sw, vaug)
            den = nd[:, HEAD_DIM:HEAD_DIM + 1]
            hv.append(nd / jnp.maximum(jnp.abs(den), eneg_c_(h)))
            kw = kp * (wend_c(h) * hmf[hh])
            cst_ref[h] = dec0[:, h:h + 1] * cst + _dot(kw.T.astype(BF16), vaug)
        outs.append(jnp.where(low, hv[0], pltpu.roll(hv[1], HEAD_DIM, 1)))
        yield
    hcat = jnp.concatenate(outs, axis=1)
    return _seg_rms(hcat, mg, ones_bd) * _sigmoid(do)


def _mixer_kernel(hc_ref, hn_ref, pv_ref, win_ref, convw_ref, wr_ref, wi_ref, rr_ref, rbase_ref, gamma_ref,
                  ones_ref, msmall_ref, mbig_ref, gb_ref, triu_ref, *rest, layer, tm):
    y_ref, hl_ref, conv_ref, kst_ref, vst_ref, sst_ref, cst_ref, mst_ref, u_scr, nb_scr = rest[-10:]
    t = pl.program_id(0)
    states = (hl_ref, conv_ref, kst_ref, vst_ref, sst_ref, cst_ref, mst_ref)

    @pl.when(t == 0)
    def _init():
        for ref in states:
            ref[...] = jnp.zeros_like(ref)

    if layer == 0:
        hl_ref, conv_ref, kst_ref, vst_ref, sst_ref, cst_ref, mst_ref = (ref.at[0] for ref in states)

    bsz = hc_ref.shape[0]
    g1 = pv_ref[layer:layer + 1, V_G1:V_G1 + D_MODEL]
    cols = (((C_AX, C_BQ - C_AX),), ((C_BQ, C_CQ - C_BQ),),
            ((C_CQ, 2 * GROUP_W), (C_CI, 2 * GROUP_W)),
            ((C_DQ, 2 * GROUP_W), (C_DV, N_IN - C_DV)))

    def normalise(src_ref):
        for b in range(bsz):
            nb_scr[b * tm:(b + 1) * tm, :] = _rms_rows(src_ref[b], g1).astype(BF16)

    def project(group_ids):
        for gid in group_ids:
            for c0, w in cols[gid]:
                u_scr[:, c0:c0 + w] = _dot_nt(nb_scr[...], win_ref[c0:c0 + w, :])
                yield

    phases = ((0, 1), (2, 3))

    @pl.when(t == 0)
    def _prologue():
        normalise(hc_ref)
        _round_robin([(lambda: True, project(phases[0]))])

    def vec(off, w):
        return pv_ref[layer:layer + 1, off:off + w]

    ctxs = []
    for b in range(bsz):
        proj = functools.partial(lambda c0, w, b: u_scr[b * tm:(b + 1) * tm, c0:c0 + w], b=b)
        ctxs.append(types.SimpleNamespace(
            t=t, proj=proj, vec=vec, layer=layer, tm=tm, ones_bd=ones_ref[...], convw_ref=convw_ref,
            wr_ref=wr_ref, wi_ref=wi_ref, rr_ref=rr_ref, rbase_ref=rbase_ref, gamma_ref=gamma_ref,
            msmall_ref=msmall_ref, mbig_ref=mbig_ref, gb_ref=gb_ref, triu_ref=triu_ref, y_ref=y_ref.at[b],
            hl_ref=hl_ref.at[b], conv_ref=conv_ref.at[b], kst_ref=kst_ref.at[b], vst_ref=vst_ref.at[b],
            sst_ref=sst_ref.at[b], cst_ref=cst_ref.at[b], mst_ref=mst_ref.at[b],
            hm=_head_masks(BLK, BF16), hmf=_head_masks(1, F32)))
    groups = (_group_a, _group_b, _group_c, _group_d)
    for pi, phase in enumerate(phases):
        if pi == 1:
            normalise(hn_ref)
        tasks = [(lambda: True, project(phases[1 - pi]))]
        for c in ctxs:
            for gid in phase:
                tasks.append((lambda: True, groups[gid](c)))
        _round_robin(tasks)


def _round_robin(tasks):
    tasks = list(tasks)
    while tasks:
        for task in list(tasks):
            ready, gen = task
            if not ready():
                continue
            try:
                next(gen)
            except StopIteration:
                tasks.remove(task)


def _group_a(c):
    proj, vec, tm, layer = c.proj, c.vec, c.tm, c.layer
    conv_ref, convw_ref, wr_ref, wi_ref, hl_ref, y_ref = (c.conv_ref, c.convw_ref, c.wr_ref, c.wi_ref,
                                                          c.hl_ref, c.y_ref)
    xa = proj(C_AX, GROUP_W)
    ga = proj(C_AG, GROUP_W)
    cw = convw_ref[layer]
    tail = conv_ref[...]
    sub8 = _iota((SUBLANES, GROUP_W), 0)

    def shifted(j):
        r = pltpu.roll(xa, j, 0)
        head = jnp.where(sub8 < j, pltpu.roll(tail, j, 0), r[0:SUBLANES])
        return jnp.concatenate([head, r[SUBLANES:]], axis=0)

    xc = (vec(V_CB, GROUP_W) + shifted(3) * cw[0:1, :] + shifted(2) * cw[1:2, :]
          + shifted(1) * cw[2:3, :] + xa * cw[3:4, :])
    conv_ref[...] = xa[tm - SUBLANES:tm]
    yield
    a, bx = _rglru_gates(xc, wr_ref[...], wi_ref[...], vec(V_BR, GROUP_W), vec(V_BI, GROUP_W),
                         vec(V_LAM, GROUP_W))
    yield
    sub_t = _iota((tm, GROUP_W), 0) & (SUBLANES - 1)
    s = 1
    while s < SUBLANES:
        keep = sub_t >= s
        a_s = pltpu.roll(a, s, 0)
        b_s = pltpu.roll(bx, s, 0)
        bx = jnp.where(keep, a * b_s + bx, bx)
        a = jnp.where(keep, a * a_s, a)
        s *= 2
    carry = hl_ref[...]
    hs = []
    for g in range(tm // SUBLANES):
        rows = slice(g * SUBLANES, (g + 1) * SUBLANES)
        hg_ = a[rows] * carry + bx[rows]
        hs.append(hg_)
        carry = hg_[SUBLANES - 1:SUBLANES]
    hseq = jnp.concatenate(hs, axis=0)
    hl_ref[...] = carry
    yield
    y_ref[:, 0:GROUP_W] = (hseq * _gelu_tanh(ga)).astype(y_ref.dtype)


def _group_b(c):
    proj, vec, tm, t, ones_bd = c.proj, c.vec, c.tm, c.t, c.ones_bd
    rr_ref, rbase_ref, kst_ref, vst_ref, y_ref = c.rr_ref, c.rbase_ref, c.kst_ref, c.vst_ref, c.y_ref
    cb = rbase_ref[:, 0:LANES]
    sb_ = rbase_ref[:, LANES:2 * LANES]
    rc = cb * rr_ref[0] - sb_ * rr_ref[1]
    ra = sb_ * rr_ref[2] + cb * rr_ref[3]
    rb = sb_ * rr_ref[4] + cb * rr_ref[5]
    q = _seg_rms(proj(C_BQ, GROUP_W), vec(V_QG, GROUP_W), ones_bd)
    q = _rope(q, jnp.concatenate([rc, rc], 1), jnp.concatenate([ra, ra], 1), jnp.concatenate([rb, rb], 1))
    k = _seg_rms(proj(C_BK, LANES), vec(V_KG, LANES), ones_bd[0:LANES, 0:LANES])
    k = _rope(k, rc, ra, rb)
    v = proj(C_BV, LANES)
    sink = vec(V_SINK, LANES)
    yield
    k_prev, v_prev = kst_ref[...], vst_ref[...]
    for j in range(tm // BLK):
        rs = slice(j * BLK, (j + 1) * BLK)
        first = (t == 0) if j == 0 else False
        yb = _swa_block(q[rs], k[rs], v[rs], k_prev, v_prev, sink, first)
        y_ref[rs, GROUP_W:2 * GROUP_W] = yb.astype(y_ref.dtype)
        k_prev, v_prev = k[rs], v[rs]
        yield
    kst_ref[...] = k_prev
    vst_ref[...] = v_prev


def _group_c(c):
    proj, vec, tm, layer, ones_bd, hm = c.proj, c.vec, c.tm, c.layer, c.ones_bd, c.hm
    gamma_ref, sst_ref, msmall_ref, mbig_ref, y_ref = c.gamma_ref, c.sst_ref, c.msmall_ref, c.mbig_ref, c.y_ref
    lb = _lb_from_gamma([gamma_ref[i:i + 1, :] for i in range(gamma_ref.shape[0])], layer)
    cq, cf, ci, cg = (proj(C_CQ, GROUP_W), proj(C_CF, GROUP_W), proj(C_CI, GROUP_W), proj(C_CG, GROUP_W))
    hg = vec(V_HG, GROUP_W)
    sub = _iota((BLK, GROUP_W), 0) & (SUBLANES - 1)
    for j in range(tm // BLK):
        rs = slice(j * BLK, (j + 1) * BLK)
        yc = yield from _hgrn_block(cq[rs], cf[rs], ci[rs], cg[rs], lb, hg, sst_ref, ones_bd, msmall_ref, mbig_ref,
                         hm, sub)
        y_ref[rs, 2 * GROUP_W:3 * GROUP_W] = yc.astype(y_ref.dtype)
        yield


def _group_d(c):
    proj, vec, tm, layer, ones_bd, hm, hmf = c.proj, c.vec, c.tm, c.layer, c.ones_bd, c.hm, c.hmf
    gb_ref, triu_ref, mst_ref, cst_ref, y_ref = c.gb_ref, c.triu_ref, c.mst_ref, c.cst_ref, c.y_ref
    dq, dk, dv, do = (proj(C_DQ, GROUP_W), proj(C_DK, GROUP_W), proj(C_DV, GROUP_W), proj(C_DO, GROUP_W))
    mg = vec(V_MG, GROUP_W)
    m_prev = mst_ref[...]
    g_rows, g_c, m_c, eneg_c, m_new = _mlstm_tile_gates(proj(C_GATES, LANES), gb_ref[layer], m_prev,
                                                       triu_ref[...])
    mst_ref[...] = m_new
    yield
    for j in range(tm // BLK):
        rs = slice(j * BLK, (j + 1) * BLK)
        yd = yield from _mlstm_block(dq[rs], dk[rs], dv[rs], do[rs], g_rows[:, rs], g_c[rs], m_c[rs], eneg_c[rs],
                          m_prev, mg, cst_ref, ones_bd, hm, hmf)
        y_ref[rs, 3 * GROUP_W:4 * GROUP_W] = yd.astype(y_ref.dtype)
        m_prev = m_c[(j + 1) * BLK - 1:(j + 1) * BLK]
        yield


def _full(shape):
    nd = len(shape)
    return pl.BlockSpec(shape, lambda *_: (0,) * nd)


def _layer_block(shape, layer, single_buffer=False):
    nd = len(shape) - 1
    kw = {'pipeline_mode': pl.Buffered(1)} if single_buffer else {}
    return pl.BlockSpec((None,) + tuple(shape[1:]), lambda *_: (layer,) + (0,) * nd, **kw)


def _prompt_mixers(h, cw, layer, tm, prev):
    bsz, t, _ = h.shape
    nt = t // tm
    kern = functools.partial(_mixer_kernel, layer=layer, tm=tm)
    in_specs = [pl.BlockSpec((bsz, tm, D_MODEL), lambda i: (0, i, 0)),
                pl.BlockSpec((bsz, tm, D_MODEL), lambda i: (0, jnp.minimum(i + 1, nt - 1), 0)),
                _full(cw['vecs'].shape), _layer_block(cw['w_in_p'].shape, layer, single_buffer=True),
                _full(cw['conv_w'].shape), _layer_block(cw['wr_bd'].shape, layer),
                _layer_block(cw['wi_bd'].shape, layer), _full(cw['rope_r'].shape),
                pl.BlockSpec((None, 1, 2 * LANES), lambda i: (i, 0, 0)),
                _full(cw['gamma'].shape), _full(cw['ones_bd'].shape), _full(cw['lvl_small'].shape),
                _full(cw['lvl_big'].shape), _full(cw['gate_bias'].shape), _full(cw['triu'].shape)]
    depth = cw['vecs'].shape[0]
    st_shapes = [(depth, bsz) + s for s in ((1, GROUP_W), (8, GROUP_W), (BLK, LANES), (BLK, LANES),
                                            (2, LANES, LANES), (N_HEADS, LANES, LANES), (1, LANES))]
    out_shape = ([jax.ShapeDtypeStruct((bsz, t, D_MODEL), BF16)]
                 + [jax.ShapeDtypeStruct(s, F32) for s in st_shapes])
    st_specs = [_full(s) if prev is None else _layer_block(s, layer) for s in st_shapes]
    chain = [] if prev is None else list(prev)
    n_in = len(in_specs)
    return pl.pallas_call(
        kern, grid=(nt,), in_specs=in_specs + [pl.BlockSpec(memory_space=pl.ANY)] * len(chain),
        out_specs=[pl.BlockSpec((bsz, tm, D_MODEL), lambda i: (0, i, 0))] + st_specs, out_shape=out_shape,
        input_output_aliases={n_in + i: 1 + i for i in range(len(chain))},
        scratch_shapes=[pltpu.VMEM((bsz * tm, N_IN), F32), pltpu.VMEM((bsz * tm, D_MODEL), BF16)],
        compiler_params=pltpu.CompilerParams(dimension_semantics=("arbitrary",),
                                             vmem_limit_bytes=VMEM_LIMIT),
        name=f"prompt_mixers_l{layer}",
    )(h, h, cw['vecs'], cw['w_in_p'], cw['conv_w'], cw['wr_bd'], cw['wi_bd'], cw['rope_r'], cw['rope_base'],
      cw['gamma'], cw['ones_bd'], cw['lvl_small'], cw['lvl_big'], cw['gate_bias'], cw['triu'], *chain)


def _ffn_math(h, yb, p, g2, wout_ref, wup_ref, wdn_ref, wg_ref, wp_ref):
    h = h + _dot(yb, wout_ref[...])
    nb = _rms_rows(h, g2).astype(BF16)
    acc = h
    step = 1024
    for c in range(0, D_FF, step):
        f = jnp.maximum(_dot(nb, wup_ref[:, c:c + step]), 0.0)
        acc = acc + _dot((f * f).astype(BF16), wdn_ref[c:c + step, :])
    gate = _sigmoid(_dot(acc.astype(BF16), wg_ref[...]))
    return acc + gate * _dot(p.astype(BF16), wp_ref[...])


def _ffn_kernel(h_ref, y_ref, p_ref, pv_ref, wout_ref, wup_ref, wdn_ref, wg_ref, wp_ref, o_ref, *, layer):
    g2 = pv_ref[layer:layer + 1, V_G2:V_G2 + D_MODEL]
    o_ref[...] = _ffn_math(h_ref[...], y_ref[...], p_ref[...], g2, wout_ref, wup_ref, wdn_ref, wg_ref, wp_ref)


def _ffn_weight_specs(cw, layer):
    names = ['w_out', 'w_up', 'w_down', 'w_gate', 'w_proj']
    return [cw[n] for n in names], [_layer_block(cw[n].shape, layer, single_buffer=True) for n in names]


def _prompt_ffn(h2, y2, p3, cw, layer, tm):
    n = h2.shape[0]
    row = lambda w: pl.BlockSpec((tm, w), lambda i: (i, 0))
    ws, wspecs = _ffn_weight_specs(cw, layer)
    return pl.pallas_call(
        functools.partial(_ffn_kernel, layer=layer), grid=(n // tm,),
        in_specs=[row(D_MODEL), row(D_MODEL), pl.BlockSpec((None, tm, PLE_DIM), lambda i: (layer, i, 0)),
                  _full(cw['vecs'].shape)] + wspecs,
        out_specs=row(D_MODEL), out_shape=jax.ShapeDtypeStruct((n, D_MODEL), F32),
        compiler_params=pltpu.CompilerParams(dimension_semantics=("arbitrary",),
                                             vmem_limit_bytes=VMEM_LIMIT),
        name=f"prompt_ffn_l{layer}",
    )(h2, y2, p3, cw['vecs'], *ws)


def _sample_pre_kernel(x_ref, pv_ref, wt_ref, convw_ref, wr_ref, wi_ref, ones_ref, rope_ref, gcol_ref,
                       cols_ref, h0_ref, conv0_ref, n0_ref, m0_ref,
                       ya_ref, hn_ref, convn_ref, q_ref, kt_ref, vt_ref, fm_ref, nn_ref, mn_ref, *, layer):
    def vec(off, w):
        return pv_ref[layer:layer + 1, off:off + w]

    n = _rms_rows(x_ref[...], vec(V_G1, D_MODEL))
    nb = n.astype(BF16)
    n_t = n.T.astype(BF16)
    ones_bd = ones_ref[...]

    def proj(r0, cnt):
        return _dot_nt(nb, wt_ref[r0:r0 + cnt, :])

    def proj_t(r0, cnt):
        return _dot(wt_ref[r0:r0 + cnt, :], n_t)

    xa = proj(C_AX, GROUP_W)
    ga = proj(C_AG, GROUP_W)
    cw = convw_ref[layer]
    xc = (vec(V_CB, GROUP_W) + conv0_ref[0] * cw[0:1, :] + conv0_ref[1] * cw[1:2, :]
          + conv0_ref[2] * cw[2:3, :] + xa * cw[3:4, :])
    convn_ref[0] = conv0_ref[1]
    convn_ref[1] = conv0_ref[2]
    convn_ref[2] = xa
    a, bx = _rglru_gates(xc, wr_ref[...], wi_ref[...], vec(V_BR, GROUP_W), vec(V_BI, GROUP_W),
                         vec(V_LAM, GROUP_W))
    hn = a * h0_ref[...] + bx
    hn_ref[...] = hn
    ya_ref[...] = hn * _gelu_tanh(ga)

    rc, ra, rb = rope_ref[0:1, :], rope_ref[1:2, :], rope_ref[2:3, :]
    q = _seg_rms(proj(C_BQ, GROUP_W), vec(V_QG, GROUP_W), ones_bd)
    q_ref[...] = _rope(q, jnp.concatenate([rc, rc], 1), jnp.concatenate([ra, ra], 1),
                       jnp.concatenate([rb, rb], 1))
    k = _seg_rms(proj(C_BK, LANES), vec(V_KG, LANES), ones_bd[0:LANES, 0:LANES])
    kt_ref[...] = _rope(k, rc, ra, rb).T
    vt_ref[...] = proj(C_BV, LANES).T

    lb = _lb_from_gamma([gcol_ref[i] for i in range(gcol_ref.shape[0])], layer)
    cq, cf, ci, cg = (proj_t(C_CQ, GROUP_W), proj_t(C_CF, GROUP_W), proj_t(C_CI, GROUP_W),
                      proj_t(C_CG, GROUP_W))
    f = lb + (1.0 - lb) * _sigmoid(cf)
    fm_ref[FM_HF:FM_HF + GROUP_W, :] = f
    fm_ref[FM_HK:FM_HK + GROUP_W, :] = 1.0 - f
    fm_ref[FM_HQ:FM_HQ + GROUP_W, :] = cq * _sigmoid(cq)
    fm_ref[FM_HV:FM_HV + GROUP_W, :] = ci
    fm_ref[FM_CG:FM_CG + GROUP_W, :] = cg * _sigmoid(cg)

    dq, dk, dv, do = (proj_t(C_DQ, GROUP_W), proj_t(C_DK, GROUP_W), proj_t(C_DV, GROUP_W),
                      proj_t(C_DO, GROUP_W))
    g8 = proj_t(C_GATES, 2 * N_HEADS) + cols_ref[layer, R_GB:R_GB + 2 * N_HEADS, :]
    ig = g8[0:N_HEADS, :]
    lf = _log_sigmoid(g8)[N_HEADS:2 * N_HEADS, :]
    a_int = lf + m0_ref[...]
    m_new = jnp.maximum(a_int, ig)
    dec = jnp.exp(a_int - m_new)
    w = jnp.exp(ig - m_new)
    mn_ref[...] = m_new
    km = dk * (HEAD_DIM ** -0.5)
    dens = []
    for h in range(N_HEADS):
        sl = slice(h * HEAD_DIM, (h + 1) * HEAD_DIM)
        nn_h = dec[h:h + 1, :] * n0_ref[sl, :] + w[h:h + 1, :] * km[sl, :]
        nn_ref[sl, :] = nn_h
        dens.append(jnp.sum(dq[sl, :] * nn_h, axis=0, keepdims=True))
    pad = jnp.zeros((SUBLANES - N_HEADS, x_ref.shape[0]), F32)
    fm_ref[FM_MK:FM_MK + GROUP_W, :] = km
    fm_ref[FM_MQ:FM_MQ + GROUP_W, :] = dq
    fm_ref[FM_MV:FM_MV + GROUP_W, :] = dv
    fm_ref[FM_DO:FM_DO + GROUP_W, :] = _sigmoid(do)
    fm_ref[FM_DEC:FM_DEC + SUBLANES, :] = jnp.concatenate([dec, pad], 0)
    fm_ref[FM_W:FM_W + SUBLANES, :] = jnp.concatenate([w, pad], 0)
    fm_ref[FM_DEN:FM_DEN + SUBLANES, :] = jnp.concatenate(dens + [pad], 0)
    fm_ref[FM_ENEG:FM_ENEG + SUBLANES, :] = jnp.concatenate([jnp.exp(-m_new), pad], 0)


def _own_slab(ref, first_layer):
    if not first_layer:
        return ref
    ref[1:] = jnp.zeros((ref.shape[0] - 1,) + ref.shape[1:], ref.dtype)
    return ref.at[0]


def _sample_attn_kernel(q_ref, kn_ref, vn_ref, kc_ref, vc_ref, sink_ref, *rest, first_layer):
    ko_ref, vo_ref, o_ref = rest[-3:]
    ko_ref, vo_ref = _own_slab(ko_ref, first_layer), _own_slab(vo_ref, first_layer)
    sb = q_ref.shape[0]
    rows = 2 * HEAD_DIM
    lane = _iota((rows, WINDOW), 1)
    kn = kn_ref[...].reshape(rows, sb)
    vn = vn_ref[...].reshape(rows, sb)
    for s in range(sb):
        kt = pltpu.roll(kc_ref[s].reshape(rows, WINDOW), WINDOW - 1, 1)
        ko_ref[s] = jnp.where(lane == WINDOW - 1, kn[:, s:s + 1], kt).reshape(2, HEAD_DIM, WINDOW)
        vt = pltpu.roll(vc_ref[s].reshape(rows, WINDOW), WINDOW - 1, 1)
        vo_ref[s] = jnp.where(lane == WINDOW - 1, vn[:, s:s + 1], vt).reshape(2, HEAD_DIM, WINDOW)
    for kv in range(2):
        kk = ko_ref[:, kv].astype(BF16)
        vv = vo_ref[:, kv].astype(BF16)
        s_ = jnp.einsum('bqc,bcj->bqj', q_ref[:, kv].astype(BF16), kk,
                        preferred_element_type=F32) * (HEAD_DIM ** -0.5)
        sk = sink_ref[kv]
        mx = jnp.maximum(jnp.max(s_, axis=-1, keepdims=True), sk)
        p = jnp.exp(s_ - mx)
        den = jnp.sum(p, axis=-1, keepdims=True) + jnp.exp(sk - mx)
        o = jnp.einsum('bqj,bcj->bqc', p.astype(BF16), vv, preferred_element_type=F32)
        o_ref[:, kv] = o / den


def _sample_state_kernel(fm_ref, s_ref, c_ref, *rest, first_layer):
    so_ref, co_ref, oh_ref, om_ref = rest[-4:]
    so_ref, co_ref = _own_slab(so_ref, first_layer), _own_slab(co_ref, first_layer)
    h = pl.program_id(0)
    r0 = pl.multiple_of(h * HEAD_DIM, HEAD_DIM)
    hv = fm_ref[pl.ds(FM_HV + r0, HEAD_DIM), :]
    mv = fm_ref[pl.ds(FM_MV + r0, HEAD_DIM), :]
    dec = fm_ref[pl.ds(FM_DEC + h, 1), :]
    w = fm_ref[pl.ds(FM_W + h, 1), :]

    def body(d, carry):
        acc_h, acc_m = carry
        r = r0 + d
        s_new = fm_ref[pl.ds(FM_HF + r, 1), :] * s_ref[d] + fm_ref[pl.ds(FM_HK + r, 1), :] * hv
        so_ref[d] = s_new
        c_new = dec * c_ref[d] + (w * fm_ref[pl.ds(FM_MK + r, 1), :]) * mv
        co_ref[d] = c_new
        return (acc_h + fm_ref[pl.ds(FM_HQ + r, 1), :] * s_new,
                acc_m + fm_ref[pl.ds(FM_MQ + r, 1), :] * c_new)

    zero = jnp.zeros((HEAD_DIM, fm_ref.shape[1]), F32)
    acc_h, acc_m = lax.fori_loop(0, HEAD_DIM, body, (zero, zero), unroll=4)
    oh_ref[...] = acc_h
    om_ref[...] = acc_m


def _sample_post_kernel(h_ref, ya_ref, yb_ref, oh_ref, om_ref, fm_ref, cols_ref, p_ref, pv_ref,
                        wout_ref, wup_ref, wdn_ref, wg_ref, wp_ref, o_ref, acc_ref, nb_ref, *, layer):
    k = pl.program_id(0)

    @pl.when(k == 0)
    def _head():
        def head_rms(x):
            return x * lax.rsqrt(jnp.mean(x * x, axis=0, keepdims=True) + EPS)

        yc, yd = [], []
        for h in range(N_HEADS):
            yc.append(head_rms(oh_ref[h]))
            den = fm_ref[FM_DEN + h:FM_DEN + h + 1, :]
            eneg = fm_ref[FM_ENEG + h:FM_ENEG + h + 1, :]
            yd.append(head_rms(om_ref[h] / jnp.maximum(jnp.abs(den), eneg)))
        yc_ = (jnp.concatenate(yc, 0) * cols_ref[layer, R_HG:R_HG + GROUP_W, :]
               * fm_ref[FM_CG:FM_CG + GROUP_W, :])
        yd_ = (jnp.concatenate(yd, 0) * cols_ref[layer, R_MG:R_MG + GROUP_W, :]
               * fm_ref[FM_DO:FM_DO + GROUP_W, :])
        y = jnp.concatenate([ya_ref[...], yb_ref[...], yc_.T, yd_.T], axis=1).astype(BF16)
        h1 = h_ref[...] + _dot(y, wout_ref[...])
        acc_ref[...] = h1
        nb_ref[...] = _rms_rows(h1, pv_ref[layer:layer + 1, V_G2:V_G2 + D_MODEL]).astype(BF16)

    f = jnp.maximum(_dot(nb_ref[...], wup_ref[...]), 0.0)
    acc_ref[...] += _dot((f * f).astype(BF16), wdn_ref[...])

    @pl.when(k == pl.num_programs(0) - 1)
    def _tail():
        acc = acc_ref[...]
        gate = _sigmoid(_dot(acc.astype(BF16), wg_ref[...]))
        o_ref[...] = acc + gate * _dot(p_ref[...].astype(BF16), wp_ref[...])


def _call_full(kern, args, specs, out_shape, name):
    specs = [(_full(a.shape) if s is None else s) for a, s in zip(args, specs)]
    return pl.pallas_call(
        kern, grid=(1,), in_specs=specs,
        out_specs=tuple(_full(s.shape) for s in out_shape), out_shape=tuple(out_shape),
        compiler_params=pltpu.CompilerParams(dimension_semantics=("arbitrary",),
                                             vmem_limit_bytes=VMEM_LIMIT),
        name=name,
    )(*args)


def _sample_layer(h, sv, prev, cw, layer):
    nsm = h.shape[0]
    depth = cw['vecs'].shape[0]
    sd = lambda *shape: jax.ShapeDtypeStruct(shape, F32)
    g, l = sd(nsm, GROUP_W), sd(LANES, nsm)

    args = [h, cw['vecs'], cw['w_in_p'], cw['conv_w'], cw['wr_bd'], cw['wi_bd'], cw['ones_bd'],
            cw['rope_s'], cw['gamma_col'], cw['cols'], sv['h'], sv['conv'], sv['n'], sv['m']]
    specs = [None, None, _layer_block(cw['w_in_p'].shape, layer, single_buffer=True), None,
             _layer_block(cw['wr_bd'].shape, layer), _layer_block(cw['wi_bd'].shape, layer), None,
             None, None, None, _layer_block(sv['h'].shape, layer), _layer_block(sv['conv'].shape, layer),
             _layer_block(sv['n'].shape, layer), _layer_block(sv['m'].shape, layer)]
    outs = [g, g, sd(CONV_W - 1, nsm, GROUP_W), g, l, l, sd(FM_ROWS, nsm), sd(GROUP_W, nsm),
            sd(N_HEADS, nsm)]
    ya, hn, convn, q, kt, vt, fm, nn, mn = _call_full(
        functools.partial(_sample_pre_kernel, layer=layer), args, specs, outs, f"sample_pre_l{layer}")

    sb = 16
    nblk = nsm // sb
    q3 = jnp.pad(q.reshape(nsm, 2, 2, HEAD_DIM), ((0, 0), (0, 0), (0, SUBLANES - 2), (0, 0)))
    to_blocks = lambda a: a.reshape(2, HEAD_DIM, nblk, sb).transpose(2, 0, 1, 3)
    cshape = (depth, nsm, 2, HEAD_DIM, WINDOW)
    cspec = pl.BlockSpec((None, sb, 2, HEAD_DIM, WINDOW), lambda i: (layer, i, 0, 0, 0))
    nspec = pl.BlockSpec((None, 2, HEAD_DIM, sb), lambda i: (i, 0, 0, 0))
    qspec = pl.BlockSpec((sb, 2, SUBLANES, HEAD_DIM), lambda i: (i, 0, 0, 0))
    any_spec = pl.BlockSpec(memory_space=pl.ANY)
    cout = (pl.BlockSpec((depth, sb, 2, HEAD_DIM, WINDOW), lambda i: (0, i, 0, 0, 0)) if prev is None
            else cspec)
    chain = [] if prev is None else [prev['k'], prev['v']]
    ko, vo, o3 = pl.pallas_call(
        functools.partial(_sample_attn_kernel, first_layer=prev is None), grid=(nblk,),
        in_specs=[qspec, nspec, nspec, cspec, cspec, _layer_block(cw['sinks8'].shape, layer)]
        + [any_spec] * len(chain),
        out_specs=(cout, cout, qspec),
        out_shape=(sd(*cshape), sd(*cshape), sd(nsm, 2, SUBLANES, HEAD_DIM)),
        input_output_aliases={6 + i: i for i in range(len(chain))},
        compiler_params=pltpu.CompilerParams(dimension_semantics=("arbitrary",),
                                             vmem_limit_bytes=VMEM_LIMIT),
        name=f"sample_attn_l{layer}",
    )(q3, to_blocks(kt), to_blocks(vt), sv['k'], sv['v'], cw['sinks8'], *chain)
    yb = o3[:, :, 0:2, :].reshape(nsm, GROUP_W)

    sshape = (depth, N_HEADS, HEAD_DIM, HEAD_DIM, nsm)
    sspec = pl.BlockSpec((None, None, HEAD_DIM, HEAD_DIM, nsm), lambda i: (layer, i, 0, 0, 0))
    ospec = pl.BlockSpec((None, HEAD_DIM, nsm), lambda i: (i, 0, 0))
    sout = (pl.BlockSpec((depth, None, HEAD_DIM, HEAD_DIM, nsm), lambda i: (0, i, 0, 0, 0))
            if prev is None else sspec)
    chain = [] if prev is None else [prev['s'], prev['c']]
    so, co, oh, om = pl.pallas_call(
        functools.partial(_sample_state_kernel, first_layer=prev is None), grid=(N_HEADS,),
        in_specs=[_full(fm.shape), sspec, sspec] + [any_spec] * len(chain),
        out_specs=(sout, sout, ospec, ospec),
        out_shape=(sd(*sshape), sd(*sshape), sd(N_HEADS, HEAD_DIM, nsm), sd(N_HEADS, HEAD_DIM, nsm)),
        input_output_aliases={3 + i: i for i in range(len(chain))},
        compiler_params=pltpu.CompilerParams(dimension_semantics=("arbitrary",),
                                             vmem_limit_bytes=VMEM_LIMIT),
        name=f"sample_state_l{layer}",
    )(fm, sv['s'], sv['c'], *chain)

    ws, wspecs = _ffn_weight_specs(cw, layer)
    chunk = 1024
    wspecs[1] = pl.BlockSpec((None, D_MODEL, chunk), lambda k: (layer, 0, k))
    wspecs[2] = pl.BlockSpec((None, chunk, D_MODEL), lambda k: (layer, k, 0))
    post_args = [h, ya, yb, oh, om, fm, cw['cols'], sv['p'], cw['vecs']] + ws
    post_specs = ([_full(a.shape) for a in post_args[:7]] + [_layer_block(sv['p'].shape, layer),
                                                             _full(cw['vecs'].shape)] + wspecs)
    h_new = pl.pallas_call(
        functools.partial(_sample_post_kernel, layer=layer), grid=(D_FF // chunk,),
        in_specs=post_specs, out_specs=_full((nsm, D_MODEL)), out_shape=sd(nsm, D_MODEL),
        scratch_shapes=[pltpu.VMEM((nsm, D_MODEL), F32), pltpu.VMEM((nsm, D_MODEL), BF16)],
        compiler_params=pltpu.CompilerParams(dimension_semantics=("arbitrary",),
                                             vmem_limit_bytes=VMEM_LIMIT),
        name=f"sample_post_l{layer}",
    )(*post_args)
    small = (hn, convn, nn, mn)
    big = {'k': ko, 'v': vo, 's': so, 'c': co}
    return h_new, small, big


def _block_diag_all(w):
    depth = w.shape[0]
    rows = w.reshape(depth, GROUP_W, HEAD_DIM)
    idx = np.arange(GROUP_W) // HEAD_DIM
    mask = idx[:, None] == idx[None, :]
    return jnp.where(mask[None], jnp.tile(rows, (1, 1, N_HEADS)), 0.0)


def _rope_lane_freq():
    half = ROT_DIM // 2
    inv = np.power(np.float32(ROPE_THETA), -np.arange(half, dtype=np.float32) * np.float32(2.0 / ROT_DIM))
    dd = np.arange(LANES) % HEAD_DIM
    freq = np.where(dd < ROT_DIM, inv[dd % half], np.float32(0.0))
    m_a = (dd < half).astype(np.float32)
    m_b = ((dd >= half) & (dd < ROT_DIM)).astype(np.float32)
    return freq.astype(np.float32), m_a, m_b


def _rope_tables(pos):
    freq, m_a, m_b = _rope_lane_freq()
    ang = np.asarray(pos, np.float32)[:, None] * freq[None, :]
    cos, sin = np.cos(ang), np.sin(ang)
    return cos, -sin * m_a, sin * m_b


def _rope_split_tables(t, tm):
    freq, m_a, m_b = _rope_lane_freq()
    ang_r = np.arange(tm, dtype=np.float32)[:, None] * freq[None, :]
    cr, sr = np.cos(ang_r), np.sin(ang_r)
    rope_r = np.stack([cr, sr, -cr * m_a, -sr * m_a, cr * m_b, sr * m_b])
    ang_b = (np.arange(t // tm) * tm).astype(np.float32)[:, None] * freq[None, :]
    rope_base = np.concatenate([np.cos(ang_b), np.sin(ang_b)], 1)[:, None, :]
    return rope_r, rope_base


def _hgrn_level_masks():
    t = np.arange(BLK)[:, None]
    s = np.arange(BLK)[None, :]
    small = [t == s]
    big = []
    for lev in range(1, 8):
        half = 1 << (lev - 1)
        own = ((t >> lev) == (s >> lev)) & ((t & half) != 0) & ((s & half) == 0)
        if lev < 4:
            small.append(own)
        else:
            rows = np.concatenate([np.arange(m, m + half) for m in range(half, BLK, 2 * half)])
            big.append(own[rows])
    return np.stack(small).astype(np.float32), np.stack(big).astype(np.float32)


def _pad_last(v, width):
    return jnp.pad(v, ((0, 0), (0, width - v.shape[-1])))


def _common(w, t, tm_mix, past_len, nsm):
    depth = w['w_in'].shape[0]
    tile = lambda v, n: jnp.tile(v, (1, n))
    vecs = jnp.concatenate([
        w['norm1_g'], w['norm2_g'], w['conv_b'], w['lru_br'], w['lru_bi'], w['lru_lam'],
        tile(w['q_norm_g'], N_HEADS), tile(w['k_norm_g'], 2), _pad_last(w['attn_sinks'], LANES),
        tile(w['hgrn_norm_g'], N_HEADS), tile(w['mlstm_norm_g'], N_HEADS),
        _pad_last(w['mlstm_ib'], LANES), _pad_last(w['mlstm_fb'], LANES)], axis=1)
    cols = jnp.concatenate([tile(w['hgrn_norm_g'], N_HEADS), tile(w['mlstm_norm_g'], N_HEADS),
                            w['mlstm_ib'], w['mlstm_fb']], axis=1)
    w_in = w['w_in']
    gate_bias = jnp.concatenate([w['mlstm_ib'], w['mlstm_fb']], axis=1)
    lvl_small, lvl_big = _hgrn_level_masks()
    sinks = w['attn_sinks']
    z2 = jnp.zeros((depth, 2, SUBLANES - 2), F32)
    idx = np.arange(GROUP_W)
    rope_r, rope_base = _rope_split_tables(t, tm_mix)
    return {
        'vecs': vecs,
        'cols': jnp.broadcast_to(cols[:, :, None], cols.shape + (nsm,)),
        'gamma': w['hgrn_gamma'],
        'gamma_col': jnp.broadcast_to(w['hgrn_gamma'][:, :, None], w['hgrn_gamma'].shape + (nsm,)),
        'w_in_p': jnp.pad(jnp.swapaxes(w_in, 1, 2), ((0, 0), (0, N_IN - D_IN), (0, 0))).astype(BF16),
        'lvl_small': lvl_small, 'lvl_big': lvl_big,
        'gate_bias': jnp.broadcast_to(gate_bias[:, :, None], gate_bias.shape + (tm_mix,)),
        'triu': (np.arange(tm_mix)[:, None] <= np.arange(tm_mix)[None, :]).astype(np.float32),
        'conv_w': w['conv_w'],
        'wr_bd': _block_diag_all(w['lru_wr']).astype(BF16),
        'wi_bd': _block_diag_all(w['lru_wi']).astype(BF16),
        'ones_bd': jnp.asarray(idx[:, None] // HEAD_DIM == idx[None, :] // HEAD_DIM, BF16),
        'sinks8': jnp.concatenate([sinks.reshape(depth, 2, 2), z2], axis=2)[..., None],
        'rope_r': rope_r, 'rope_base': rope_base,
        'rope_s': np.concatenate(_rope_tables(past_len + np.arange(1)), axis=0),
        'w_out': w['w_out'].astype(BF16), 'w_up': w['w_up'].astype(BF16),
        'w_down': w['w_down'].astype(BF16), 'w_gate': w['w_ple_gate'].astype(BF16),
        'w_proj': w['w_ple_proj'].astype(BF16),
    }


def _run(x_prompt, x_sample, p_prompt, p_sample, sample_state, w, past_len, tm_mix=256, tm_ffn=512):
    depth = w['w_in'].shape[0]
    bsz, t, _ = x_prompt.shape
    nsm = x_sample.shape[0]
    cw = _common(w, t, tm_mix, past_len, nsm)
    tm_ffn = min(tm_ffn, bsz * t)
    h0, conv0, kc, vc, s0, c0, n0, m0 = sample_state
    sv = {'h': h0, 'conv': jnp.transpose(conv0, (0, 2, 1, 3)),
          'k': jnp.transpose(kc, (0, 1, 3, 4, 2)), 'v': jnp.transpose(vc, (0, 1, 3, 4, 2)),
          's': jnp.transpose(s0, (0, 2, 3, 4, 1)), 'c': jnp.transpose(c0, (0, 2, 3, 4, 1)),
          'n': jnp.transpose(n0, (0, 2, 3, 1)).reshape(depth, GROUP_W, nsm),
          'm': jnp.transpose(m0, (0, 2, 1)), 'p': p_sample.reshape(depth, nsm, PLE_DIM)}
    p3 = p_prompt.reshape(depth, bsz * t, PLE_DIM)

    hp = x_prompt
    hs = x_sample.reshape(nsm, D_MODEL)
    pst, s_small, big = None, [], None
    for l in range(depth):
        y, *pst = _prompt_mixers(hp, cw, l, tm_mix, pst)
        hp = _prompt_ffn(hp.reshape(bsz * t, D_MODEL), y.reshape(bsz * t, D_MODEL), p3, cw, l,
                         tm_ffn).reshape(bsz, t, D_MODEL)
        hs, small, big = _sample_layer(hs, sv, big, cw, l)
        s_small.append(small)
    stack = lambda sts, i: jnp.stack([s[i] for s in sts])
    hl, conv, kst, vst, sst, cst, mst = pst
    s_t = jnp.swapaxes(sst, -1, -2).reshape(depth, bsz, 2, 2, HEAD_DIM, 2, HEAD_DIM)
    s_hgrn = jnp.stack([s_t[:, :, :, 0, :, 0, :], s_t[:, :, :, 1, :, 1, :]], axis=3)
    c_t = cst.reshape(depth, bsz, 2, 2, 2, HEAD_DIM, LANES)
    c_rows = jnp.stack([c_t[:, :, :, 0, 0], c_t[:, :, :, 1, 1]], axis=3)
    c_rows = c_rows.reshape(depth, bsz, N_HEADS, HEAD_DIM, LANES)
    prompt_out = (hl[:, :, 0], conv[:, :, 8 - (CONV_W - 1):], kst.reshape(depth, bsz, WINDOW, 2, HEAD_DIM),
                  vst.reshape(depth, bsz, WINDOW, 2, HEAD_DIM),
                  s_hgrn.reshape(depth, bsz, N_HEADS, HEAD_DIM, HEAD_DIM), c_rows[..., 0:HEAD_DIM],
                  c_rows[..., HEAD_DIM], mst[:, :, 0, 0:N_HEADS])
    hn, convn, nn, mn = (stack(s_small, i) for i in range(4))
    sample_out = (hn, jnp.transpose(convn, (0, 2, 1, 3)),
                  jnp.transpose(big['k'], (0, 1, 4, 2, 3)), jnp.transpose(big['v'], (0, 1, 4, 2, 3)),
                  jnp.transpose(big['s'], (0, 4, 1, 2, 3)), jnp.transpose(big['c'], (0, 4, 1, 2, 3)),
                  jnp.transpose(nn.reshape(depth, N_HEADS, HEAD_DIM, nsm), (0, 3, 1, 2)),
                  jnp.transpose(mn, (0, 2, 1)))
    return (hp, hs.reshape(x_sample.shape)) + prompt_out + sample_out


def kernel(x_prompt, x_sample, p_prompt, p_sample, state_rglru_h, state_rglru_conv, cache_swa_k, cache_swa_v, state_hgrn_s, state_mlstm_c, state_mlstm_n, state_mlstm_m, norm1_g, w_in, conv_w, conv_b, lru_wr, lru_br, lru_wi, lru_bi, lru_lam, q_norm_g, k_norm_g, attn_sinks, hgrn_gamma, hgrn_norm_g, mlstm_ib, mlstm_fb, mlstm_norm_g, w_out, norm2_g, w_up, w_down, w_ple_gate, w_ple_proj):
    w = {'norm1_g': norm1_g, 'w_in': w_in, 'conv_w': conv_w, 'conv_b': conv_b, 'lru_wr': lru_wr,
         'lru_br': lru_br, 'lru_wi': lru_wi, 'lru_bi': lru_bi, 'lru_lam': lru_lam, 'q_norm_g': q_norm_g,
         'k_norm_g': k_norm_g, 'attn_sinks': attn_sinks, 'hgrn_gamma': hgrn_gamma,
         'hgrn_norm_g': hgrn_norm_g, 'mlstm_ib': mlstm_ib, 'mlstm_fb': mlstm_fb,
         'mlstm_norm_g': mlstm_norm_g, 'w_out': w_out, 'norm2_g': norm2_g, 'w_up': w_up,
         'w_down': w_down, 'w_ple_gate': w_ple_gate, 'w_ple_proj': w_ple_proj}
    st = (state_rglru_h, state_rglru_conv, cache_swa_k, cache_swa_v, state_hgrn_s, state_mlstm_c,
          state_mlstm_n, state_mlstm_m)
    past_len = 8192
    return _run(x_prompt, x_sample, p_prompt, p_sample, st, w, past_len)
```

```python
import functools
import types

import jax
import jax.numpy as jnp
import numpy as np
from jax import lax
from jax.experimental import pallas as pl
from jax.experimental.pallas import tpu as pltpu

F32 = jnp.float32
BF16 = jnp.bfloat16

D_MODEL = 1024
GROUP_W = 256
HEAD_DIM = 64
N_HEADS = 4
EPS = 1e-6
NEG_BIG = -1e30
LRU_C = 8.0
CONV_W = 4
ROT_DIM = 16
ROPE_THETA = 500000.0
WINDOW = 128
D_FF = 4096
PLE_DIM = 256
LANES = 128
SUBLANES = 8
BLK = 128

C_AX, C_AG, C_BQ, C_BK, C_BV = 0, 256, 512, 768, 896
C_CQ, C_CF, C_CI, C_CG = 1024, 1280, 1536, 1792
C_DQ, C_DK, C_DV, C_DO = 2048, 2304, 2560, 2816
C_GATES = 3072
D_IN = 3080
N_IN = 3200

V_G1, V_G2, V_CB, V_BR, V_BI, V_LAM = 0, 1024, 2048, 2304, 2560, 2816
V_QG, V_KG, V_SINK, V_HG, V_MG, V_IB, V_FB = 3072, 3328, 3456, 3584, 3840, 4096, 4224
N_VEC = 4352

R_HG, R_MG, R_GB = 0, 256, 512
N_COL = 520

FM_HF, FM_HK, FM_HQ, FM_HV, FM_CG = 0, 256, 512, 768, 1024
FM_MK, FM_MQ, FM_MV, FM_DO = 1280, 1536, 1792, 2048
FM_DEC, FM_W, FM_DEN, FM_ENEG = 2304, 2312, 2320, 2328
FM_ROWS = 2336

VMEM_LIMIT = 56 * 1024 * 1024


def _dot(a, b):
    return jnp.dot(a, b, preferred_element_type=F32)


def _dot_nt(a, b):
    return lax.dot_general(a, b, (((1,), (1,)), ((), ())), preferred_element_type=F32)


def _sigmoid(x):
    return jax.nn.sigmoid(x)


def _gelu_tanh(x):
    return 0.5 * x * (1.0 + jnp.tanh(0.7978845608028654 * (x + 0.044715 * (x * x * x))))


def _log_sigmoid(x):
    return jnp.minimum(x, 0.0) - jnp.log1p(jnp.exp(-jnp.abs(x)))


def _softplus(x):
    return jnp.maximum(x, 0.0) + jnp.log1p(jnp.exp(-jnp.abs(x)))


def _rms_rows(x, g):
    return x * lax.rsqrt(jnp.mean(x * x, axis=-1, keepdims=True) + EPS) * g


def _seg_mean_sq(x, ones_bd):
    sq = x * x
    hi = sq.astype(BF16)
    lo = (sq - hi.astype(F32)).astype(BF16)
    return (_dot(hi, ones_bd) + _dot(lo, ones_bd)) * (1.0 / HEAD_DIM)


def _seg_rms(x, g, ones_bd):
    return x * lax.rsqrt(_seg_mean_sq(x, ones_bd) + EPS) * g


def _rope(x, c, sa, sb):
    w = x.shape[1]
    up = pltpu.roll(x, w - ROT_DIM // 2, 1)
    dn = pltpu.roll(x, ROT_DIM // 2, 1)
    return x * c + up * sa + dn * sb


def _lb_from_gamma(gammas, layer):
    mx = functools.reduce(jnp.maximum, gammas)
    e = [jnp.exp(g - mx) for g in gammas]
    tot = functools.reduce(lambda a, b: a + b, e)
    lb = jnp.zeros_like(tot)
    for i in range(1, layer + 1):
        lb = lb + e[i] / tot
    return lb


def _iota(shape, axis):
    return lax.broadcasted_iota(jnp.int32, shape, axis)


def _rglru_gates(xc, wr, wi, br, bi, lam):
    xcb = xc.astype(BF16)
    r = _sigmoid(_dot(xcb, wr) + br)
    ig = _sigmoid(_dot(xcb, wi) + bi)
    log_a = (-LRU_C) * r * _softplus(-lam)
    a = jnp.exp(log_a)
    y = 1.0 - a * a
    root = jnp.where(y > 0.0, y * lax.rsqrt(y), 0.0)
    return a, root * (ig * xc)


def _swa_block(q, k, v, k_prev, v_prev, sink, first):
    kk = jnp.concatenate([k_prev, k], axis=0)
    vv = jnp.concatenate([v_prev, v], axis=0)
    qi = _iota((BLK, 2 * BLK), 0)
    kj = _iota((BLK, 2 * BLK), 1)
    valid = (kj > qi) & (kj <= qi + WINDOW) & ((kj >= BLK) | jnp.logical_not(first))
    lane = _iota((2 * BLK, LANES), 1)
    one_col = jnp.where(lane == HEAD_DIM, 1.0, 0.0)
    vaug = (jnp.where(lane < HEAD_DIM, vv, one_col).astype(BF16),
            jnp.where(lane < HEAD_DIM, pltpu.roll(vv, HEAD_DIM, 1), one_col).astype(BF16))
    qs = q * (HEAD_DIM ** -0.5)
    outs = []
    for h in range(N_HEADS):
        kv = h // 2
        qh = qs[:, h * HEAD_DIM:(h + 1) * HEAD_DIM].astype(BF16)
        kh = kk[:, kv * HEAD_DIM:(kv + 1) * HEAD_DIM].astype(BF16)
        s = jnp.where(valid, _dot_nt(qh, kh), NEG_BIG)
        sk = sink[:, h:h + 1]
        mx = jnp.maximum(jnp.max(s, axis=-1, keepdims=True), sk)
        p = jnp.exp(s - mx)
        o = _dot(p.astype(BF16), vaug[kv])
        den = o[:, HEAD_DIM:HEAD_DIM + 1] + jnp.exp(sk - mx)
        outs.append(o[:, 0:HEAD_DIM] / den)
    return jnp.concatenate(outs, axis=1)


def _head_masks(rows, dtype):
    lane = _iota((rows, LANES), 1)
    return (jnp.where(lane < HEAD_DIM, 1.0, 0.0).astype(dtype),
            jnp.where(lane >= HEAD_DIM, 1.0, 0.0).astype(dtype))


def _pair_scores(qe, ke_b, hm):
    res = []
    for p in range(2):
        sl = slice(p * LANES, (p + 1) * LANES)
        kp = ke_b[:, sl]
        pr = _dot_nt(qe[:, sl].astype(BF16), jnp.concatenate([kp * hm[0], kp * hm[1]], axis=0))
        res += [pr[:, :BLK], pr[:, BLK:]]
    return res


def _hgrn_block(cq, cf, ci, cg, lb, hg, sst_ref, ones_bd, msmall_ref, mbig_ref, hm, sub):
    q = cq * _sigmoid(cq)
    f = lb + (1.0 - lb) * _sigmoid(cf)
    logf = jnp.log(f)
    k = 1.0 - f

    att = [s_ * msmall_ref[0] for s_ in _pair_scores(q, k.astype(BF16), hm)]

    c = logf
    tot = logf
    for lev in range(1, 4):
        half = 1 << (lev - 1)
        right = (sub & half) != 0
        tot_l = pltpu.roll(tot, half, 0)
        tot_r = pltpu.roll(tot, BLK - half, 0)
        e = jnp.exp(jnp.where(right, c, tot - c))
        sc = _pair_scores(q * e, (k * e).astype(BF16), hm)
        m = msmall_ref[lev]
        att = [a_ + s_ * m for a_, s_ in zip(att, sc)]
        c = c + jnp.where(right, tot_l, 0.0)
        tot = tot + jnp.where(right, tot_l, tot_r)

    pieces, carry = [], None
    for g in range(BLK // SUBLANES):
        rows = slice(g * SUBLANES, (g + 1) * SUBLANES)
        pieces.append(c[rows] if carry is None else c[rows] + carry)
        t_g = tot[g * SUBLANES:g * SUBLANES + 1]
        carry = t_g if carry is None else carry + t_g
    yield
    b = jnp.concatenate(pieces, axis=0)
    btot = carry

    for lev in range(4, 8):
        half = 1 << (lev - 1)
        nblk = BLK // (2 * half)
        qr, kf = [], []
        for i in range(nblk):
            lo = i * 2 * half
            mid = lo + half
            bref = b[mid - 1:mid]
            qr.append(q[mid:mid + half] * jnp.exp(b[mid:mid + half] - bref))
            kf.append(k[lo:mid] * jnp.exp(bref - b[lo:mid]))
            kf.append(jnp.zeros((half, GROUP_W), F32))
        sc = _pair_scores(jnp.concatenate(qr, axis=0), jnp.concatenate(kf, axis=0).astype(BF16), hm)
        m = mbig_ref[lev - 4]
        zero = jnp.zeros((half, BLK), F32)
        new = []
        for a_, s_ in zip(att, sc):
            u = s_ * m
            parts = []
            for i in range(nblk):
                parts += [zero, u[i * half:(i + 1) * half]]
            new.append(a_ + jnp.concatenate(parts, axis=0))
        att = new

    yield
    qe = (q * jnp.exp(b)).astype(BF16)
    ke = (k * jnp.exp(btot - b)).astype(BF16)
    etot = jnp.exp(btot)
    row_l = _iota((BLK, BLK), 0)
    col_l = _iota((BLK, BLK), 1)
    same_head = (row_l >= HEAD_DIM) == (col_l >= HEAD_DIM)
    outs = []
    for p in range(2):
        sl = slice(p * LANES, (p + 1) * LANES)
        st = sst_ref[p]
        vp = ci[:, sl]
        vb = vp.astype(BF16)
        a2 = jnp.concatenate([att[2 * p], att[2 * p + 1]], axis=1).astype(BF16)
        v2 = jnp.concatenate([vb * hm[0], vb * hm[1]], axis=0)
        o = _dot_nt(qe[:, sl], st.astype(BF16)) + _dot(a2, v2)
        upd = _dot(vp.T.astype(BF16), ke[:, sl])
        sst_ref[p] = st * etot[:, sl] + jnp.where(same_head, upd, 0.0)
        outs.append(o)
    o = jnp.concatenate(outs, axis=1)
    return _seg_rms(o, hg, ones_bd) * (cg * _sigmoid(cg))


def _mlstm_tile_gates(gcols, gb, m0, triu):
    tm = gcols.shape[0]
    gt = gcols.T[0:SUBLANES, :] + gb
    lf = _log_sigmoid(gt)
    hi = lf.astype(BF16)
    r1 = lf - hi.astype(F32)
    mid = r1.astype(BF16)
    lo = (r1 - mid.astype(F32)).astype(BF16)
    parts = jnp.concatenate([hi.astype(F32), mid.astype(F32), lo.astype(F32)], axis=0)
    cs = _dot(parts, triu)
    fcum = cs[0:SUBLANES] + cs[SUBLANES:2 * SUBLANES] + cs[2 * SUBLANES:3 * SUBLANES]
    fcum = pltpu.roll(fcum, N_HEADS, 0)
    g = gt - fcum
    pad = jnp.zeros((LANES - 2 * SUBLANES, tm), F32)
    cols = jnp.concatenate([g, fcum, pad], axis=0).T
    f_c = pltpu.roll(cols, LANES - SUBLANES, 1)
    sub = _iota((tm, LANES), 0) & (SUBLANES - 1)
    cm = cols
    s = 1
    while s < SUBLANES:
        cm = jnp.maximum(cm, jnp.where(sub >= s, pltpu.roll(cm, s, 0), NEG_BIG))
        s *= 2
    carry = m0
    ms = []
    for grp in range(tm // SUBLANES):
        m_g = jnp.maximum(cm[grp * SUBLANES:(grp + 1) * SUBLANES], carry)
        ms.append(m_g)
        carry = m_g[SUBLANES - 1:SUBLANES]
    m_c = jnp.concatenate(ms, axis=0)
    eneg_c = jnp.exp(-(f_c + m_c))
    m_new = f_c[tm - 1:tm] + carry
    return g, cols, m_c, eneg_c, m_new


def _mlstm_block(dq, dk, dv, do, g_rows, g_c, m_c, eneg_c, m_prev, mg, cst_ref, ones_bd, hm, hmf):
    k = dk * (HEAD_DIM ** -0.5)
    m_end = m_c[BLK - 1:BLK]
    inter = jnp.exp(m_prev - m_c)
    wend = jnp.exp(g_c - m_end)
    dec0 = jnp.exp(m_prev - m_end)
    mrun_c = lambda h: m_c[:, h:h + 1]
    inter_c = lambda h: inter[:, h:h + 1]
    eneg_c_ = lambda h: eneg_c[:, h:h + 1]
    wend_c = lambda h: wend[:, h:h + 1]
    g = g_rows
    lane = _iota((BLK, LANES), 1)
    low = lane < HEAD_DIM
    one_col = jnp.where(lane == HEAD_DIM, 1.0, 0.0)
    causal = _iota((BLK, BLK), 1) <= _iota((BLK, BLK), 0)

    outs = []
    for p in range(2):
        sl = slice(p * LANES, (p + 1) * LANES)
        kp, vp = k[:, sl], dv[:, sl]
        qb = dq[:, sl].astype(BF16)
        qm = [qb * hm[0], qb * hm[1]]
        sc2 = _dot_nt(jnp.concatenate(qm, axis=0), kp.astype(BF16))
        v_sw = pltpu.roll(vp, HEAD_DIM, 1)
        hv = []
        for hh in range(2):
            h = 2 * p + hh
            sc = sc2[hh * BLK:(hh + 1) * BLK]
            w = jnp.exp(jnp.where(causal, g[h:h + 1, :] - mrun_c(h), NEG_BIG))
            sw = (sc * w).astype(BF16)
            vaug = jnp.where(low, vp if hh == 0 else v_sw, one_col).astype(BF16)
            cst = cst_ref[h]
            nd = inter_c(h) * _dot(qm[hh], cst.astype(BF16)) + _dot(sw, vaug)
            den = nd[:, HEAD_DIM:HEAD_DIM + 1]
            hv.append(nd / jnp.maximum(jnp.abs(den), eneg_c_(h)))
            kw = kp * (wend_c(h) * hmf[hh])
            cst_ref[h] = dec0[:, h:h + 1] * cst + _dot(kw.T.astype(BF16), vaug)
        outs.append(jnp.where(low, hv[0], pltpu.roll(hv[1], HEAD_DIM, 1)))
        yield
    hcat = jnp.concatenate(outs, axis=1)
    return _seg_rms(hcat, mg, ones_bd) * _sigmoid(do)


def _mixer_kernel(hc_ref, hn_ref, pv_ref, win_ref, convw_ref, wr_ref, wi_ref, rr_ref, rbase_ref, gamma_ref,
                  ones_ref, msmall_ref, mbig_ref, gb_ref, triu_ref, *rest, layer, tm):
    y_ref, hl_ref, conv_ref, kst_ref, vst_ref, sst_ref, cst_ref, mst_ref, u_scr, nb_scr = rest[-10:]
    t = pl.program_id(0)
    states = (hl_ref, conv_ref, kst_ref, vst_ref, sst_ref, cst_ref, mst_ref)

    @pl.when(t == 0)
    def _init():
        for ref in states:
            ref[...] = jnp.zeros_like(ref)

    if layer == 0:
        hl_ref, conv_ref, kst_ref, vst_ref, sst_ref, cst_ref, mst_ref = (ref.at[0] for ref in states)

    bsz = hc_ref.shape[0]
    g1 = pv_ref[layer:layer + 1, V_G1:V_G1 + D_MODEL]
    cols = (((C_AX, C_BQ - C_AX),), ((C_BQ, C_CQ - C_BQ),),
            ((C_CQ, 2 * GROUP_W), (C_CI, 2 * GROUP_W)),
            ((C_DQ, 2 * GROUP_W), (C_DV, N_IN - C_DV)))

    def normalise(src_ref):
        for b in range(bsz):
            nb_scr[b * tm:(b + 1) * tm, :] = _rms_rows(src_ref[b], g1).astype(BF16)

    def project(group_ids):
        for gid in group_ids:
            for c0, w in cols[gid]:
                u_scr[:, c0:c0 + w] = _dot_nt(nb_scr[...], win_ref[c0:c0 + w, :])
                yield

    phases = ((0, 1), (2, 3))

    @pl.when(t == 0)
    def _prologue():
        normalise(hc_ref)
        _round_robin([(lambda: True, project(phases[0]))])

    def vec(off, w):
        return pv_ref[layer:layer + 1, off:off + w]

    ctxs = []
    for b in range(bsz):
        proj = functools.partial(lambda c0, w, b: u_scr[b * tm:(b + 1) * tm, c0:c0 + w], b=b)
        ctxs.append(types.SimpleNamespace(
            t=t, proj=proj, vec=vec, layer=layer, tm=tm, ones_bd=ones_ref[...], convw_ref=convw_ref,
            wr_ref=wr_ref, wi_ref=wi_ref, rr_ref=rr_ref, rbase_ref=rbase_ref, gamma_ref=gamma_ref,
            msmall_ref=msmall_ref, mbig_ref=mbig_ref, gb_ref=gb_ref, triu_ref=triu_ref, y_ref=y_ref.at[b],
            hl_ref=hl_ref.at[b], conv_ref=conv_ref.at[b], kst_ref=kst_ref.at[b], vst_ref=vst_ref.at[b],
            sst_ref=sst_ref.at[b], cst_ref=cst_ref.at[b], mst_ref=mst_ref.at[b],
            hm=_head_masks(BLK, BF16), hmf=_head_masks(1, F32)))
    groups = (_group_a, _group_b, _group_c, _group_d)
    for pi, phase in enumerate(phases):
        if pi == 1:
            normalise(hn_ref)
        tasks = [(lambda: True, project(phases[1 - pi]))]
        for c in ctxs:
            for gid in phase:
                tasks.append((lambda: True, groups[gid](c)))
        _round_robin(tasks)


def _round_robin(tasks):
    tasks = list(tasks)
    while tasks:
        for task in list(tasks):
            ready, gen = task
            if not ready():
                continue
            try:
                next(gen)
            except StopIteration:
                tasks.remove(task)


def _group_a(c):
    proj, vec, tm, layer = c.proj, c.vec, c.tm, c.layer
    conv_ref, convw_ref, wr_ref, wi_ref, hl_ref, y_ref = (c.conv_ref, c.convw_ref, c.wr_ref, c.wi_ref,
                                                          c.hl_ref, c.y_ref)
    xa = proj(C_AX, GROUP_W)
    ga = proj(C_AG, GROUP_W)
    cw = convw_ref[layer]
    tail = conv_ref[...]
    sub8 = _iota((SUBLANES, GROUP_W), 0)

    def shifted(j):
        r = pltpu.roll(xa, j, 0)
        head = jnp.where(sub8 < j, pltpu.roll(tail, j, 0), r[0:SUBLANES])
        return jnp.concatenate([head, r[SUBLANES:]], axis=0)

    xc = (vec(V_CB, GROUP_W) + shifted(3) * cw[0:1, :] + shifted(2) * cw[1:2, :]
          + shifted(1) * cw[2:3, :] + xa * cw[3:4, :])
    conv_ref[...] = xa[tm - SUBLANES:tm]
    yield
    a, bx = _rglru_gates(xc, wr_ref[...], wi_ref[...], vec(V_BR, GROUP_W), vec(V_BI, GROUP_W),
                         vec(V_LAM, GROUP_W))
    yield
    sub_t = _iota((tm, GROUP_W), 0) & (SUBLANES - 1)
    s = 1
    while s < SUBLANES:
        keep = sub_t >= s
        a_s = pltpu.roll(a, s, 0)
        b_s = pltpu.roll(bx, s, 0)
        bx = jnp.where(keep, a * b_s + bx, bx)
        a = jnp.where(keep, a * a_s, a)
        s *= 2
    carry = hl_ref[...]
    hs = []
    for g in range(tm // SUBLANES):
        rows = slice(g * SUBLANES, (g + 1) * SUBLANES)
        hg_ = a[rows] * carry + bx[rows]
        hs.append(hg_)
        carry = hg_[SUBLANES - 1:SUBLANES]
    hseq = jnp.concatenate(hs, axis=0)
    hl_ref[...] = carry
    yield
    y_ref[:, 0:GROUP_W] = (hseq * _gelu_tanh(ga)).astype(y_ref.dtype)


def _group_b(c):
    proj, vec, tm, t, ones_bd = c.proj, c.vec, c.tm, c.t, c.ones_bd
    rr_ref, rbase_ref, kst_ref, vst_ref, y_ref = c.rr_ref, c.rbase_ref, c.kst_ref, c.vst_ref, c.y_ref
    cb = rbase_ref[:, 0:LANES]
    sb_ = rbase_ref[:, LANES:2 * LANES]
    rc = cb * rr_ref[0] - sb_ * rr_ref[1]
    ra = sb_ * rr_ref[2] + cb * rr_ref[3]
    rb = sb_ * rr_ref[4] + cb * rr_ref[5]
    q = _seg_rms(proj(C_BQ, GROUP_W), vec(V_QG, GROUP_W), ones_bd)
    q = jnp.concatenate([_rope(q[:, 0:LANES], rc, ra, rb), _rope(q[:, LANES:], rc, ra, rb)], axis=1)
    k = _seg_rms(proj(C_BK, LANES), vec(V_KG, LANES), ones_bd[0:LANES, 0:LANES])
    k = _rope(k, rc, ra, rb)
    v = proj(C_BV, LANES)
    sink = vec(V_SINK, LANES)
    yield
    k_prev, v_prev = kst_ref[...], vst_ref[...]
    for j in range(tm // BLK):
        rs = slice(j * BLK, (j + 1) * BLK)
        first = (t == 0) if j == 0 else False
        yb = _swa_block(q[rs], k[rs], v[rs], k_prev, v_prev, sink, first)
        y_ref[rs, GROUP_W:2 * GROUP_W] = yb.astype(y_ref.dtype)
        k_prev, v_prev = k[rs], v[rs]
        yield
    kst_ref[...] = k_prev
    vst_ref[...] = v_prev


def _group_c(c):
    proj, vec, tm, layer, ones_bd, hm = c.proj, c.vec, c.tm, c.layer, c.ones_bd, c.hm
    gamma_ref, sst_ref, msmall_ref, mbig_ref, y_ref = c.gamma_ref, c.sst_ref, c.msmall_ref, c.mbig_ref, c.y_ref
    lb = _lb_from_gamma([gamma_ref[i:i + 1, :] for i in range(gamma_ref.shape[0])], layer)
    cq, cf, ci, cg = (proj(C_CQ, GROUP_W), proj(C_CF, GROUP_W), proj(C_CI, GROUP_W), proj(C_CG, GROUP_W))
    hg = vec(V_HG, GROUP_W)
    sub = _iota((BLK, GROUP_W), 0) & (SUBLANES - 1)
    for j in range(tm // BLK):
        rs = slice(j * BLK, (j + 1) * BLK)
        yc = yield from _hgrn_block(cq[rs], cf[rs], ci[rs], cg[rs], lb, hg, sst_ref, ones_bd, msmall_ref,
                                    mbig_ref, hm, sub)
        y_ref[rs, 2 * GROUP_W:3 * GROUP_W] = yc.astype(y_ref.dtype)
        yield


def _group_d(c):
    proj, vec, tm, layer, ones_bd, hm, hmf = c.proj, c.vec, c.tm, c.layer, c.ones_bd, c.hm, c.hmf
    gb_ref, triu_ref, mst_ref, cst_ref, y_ref = c.gb_ref, c.triu_ref, c.mst_ref, c.cst_ref, c.y_ref
    dq, dk, dv, do = (proj(C_DQ, GROUP_W), proj(C_DK, GROUP_W), proj(C_DV, GROUP_W), proj(C_DO, GROUP_W))
    mg = vec(V_MG, GROUP_W)
    m_prev = mst_ref[...]
    g_rows, g_c, m_c, eneg_c, m_new = _mlstm_tile_gates(proj(C_GATES, LANES), gb_ref[layer], m_prev,
                                                       triu_ref[...])
    mst_ref[...] = m_new
    yield
    for j in range(tm // BLK):
        rs = slice(j * BLK, (j + 1) * BLK)
        yd = yield from _mlstm_block(dq[rs], dk[rs], dv[rs], do[rs], g_rows[:, rs], g_c[rs], m_c[rs], eneg_c[rs],
                          m_prev, mg, cst_ref, ones_bd, hm, hmf)
        y_ref[rs, 3 * GROUP_W:4 * GROUP_W] = yd.astype(y_ref.dtype)
        m_prev = m_c[(j + 1) * BLK - 1:(j + 1) * BLK]
        yield


def _full(shape):
    nd = len(shape)
    return pl.BlockSpec(shape, lambda *_: (0,) * nd)


def _layer_block(shape, layer, single_buffer=False):
    nd = len(shape) - 1
    kw = {'pipeline_mode': pl.Buffered(1)} if single_buffer else {}
    return pl.BlockSpec((None,) + tuple(shape[1:]), lambda *_: (layer,) + (0,) * nd, **kw)


def _prompt_mixers(h, cw, layer, tm, prev):
    bsz, t, _ = h.shape
    nt = t // tm
    kern = functools.partial(_mixer_kernel, layer=layer, tm=tm)
    in_specs = [pl.BlockSpec((bsz, tm, D_MODEL), lambda i: (0, i, 0)),
                pl.BlockSpec((bsz, tm, D_MODEL), lambda i: (0, jnp.minimum(i + 1, nt - 1), 0)),
                _full(cw['vecs'].shape), _layer_block(cw['w_in_p'].shape, layer, single_buffer=True),
                _full(cw['conv_w'].shape), _layer_block(cw['wr_bd'].shape, layer),
                _layer_block(cw['wi_bd'].shape, layer), _full(cw['rope_r'].shape),
                pl.BlockSpec((None, 1, 2 * LANES), lambda i: (i, 0, 0)),
                _full(cw['gamma'].shape), _full(cw['ones_bd'].shape), _full(cw['lvl_small'].shape),
                _full(cw['lvl_big'].shape), _full(cw['gate_bias'].shape), _full(cw['triu'].shape)]
    depth = cw['vecs'].shape[0]
    st_shapes = [(depth, bsz) + s for s in ((1, GROUP_W), (8, GROUP_W), (BLK, LANES), (BLK, LANES),
                                            (2, LANES, LANES), (N_HEADS, LANES, LANES), (1, LANES))]
    out_shape = ([jax.ShapeDtypeStruct((bsz, t, D_MODEL), BF16)]
                 + [jax.ShapeDtypeStruct(s, F32) for s in st_shapes])
    st_specs = [_full(s) if prev is None else _layer_block(s, layer) for s in st_shapes]
    chain = [] if prev is None else list(prev)
    n_in = len(in_specs)
    return pl.pallas_call(
        kern, grid=(nt,), in_specs=in_specs + [pl.BlockSpec(memory_space=pl.ANY)] * len(chain),
        out_specs=[pl.BlockSpec((bsz, tm, D_MODEL), lambda i: (0, i, 0))] + st_specs, out_shape=out_shape,
        input_output_aliases={n_in + i: 1 + i for i in range(len(chain))},
        scratch_shapes=[pltpu.VMEM((bsz * tm, N_IN), F32), pltpu.VMEM((bsz * tm, D_MODEL), BF16)],
        compiler_params=pltpu.CompilerParams(dimension_semantics=("arbitrary",),
                                             vmem_limit_bytes=VMEM_LIMIT),
        name=f"prompt_mixers_l{layer}",
    )(h, h, cw['vecs'], cw['w_in_p'], cw['conv_w'], cw['wr_bd'], cw['wi_bd'], cw['rope_r'], cw['rope_base'],
      cw['gamma'], cw['ones_bd'], cw['lvl_small'], cw['lvl_big'], cw['gate_bias'], cw['triu'], *chain)


def _ffn_math(h, yb, p, g2, wout_ref, wup_ref, wdn_ref, wg_ref, wp_ref):
    h = h + _dot(yb, wout_ref[...])
    nb = _rms_rows(h, g2).astype(BF16)
    acc = h
    step = 1024
    for c in range(0, D_FF, step):
        f = jnp.maximum(_dot(nb, wup_ref[:, c:c + step]), 0.0)
        acc = acc + _dot((f * f).astype(BF16), wdn_ref[c:c + step, :])
    gate = _sigmoid(_dot(acc.astype(BF16), wg_ref[...]))
    return acc + gate * _dot(p.astype(BF16), wp_ref[...])


def _ffn_kernel(h_ref, y_ref, p_ref, pv_ref, wout_ref, wup_ref, wdn_ref, wg_ref, wp_ref, o_ref, *, layer):
    g2 = pv_ref[layer:layer + 1, V_G2:V_G2 + D_MODEL]
    o_ref[...] = _ffn_math(h_ref[...], y_ref[...], p_ref[...], g2, wout_ref, wup_ref, wdn_ref, wg_ref, wp_ref)


def _ffn_weight_specs(cw, layer):
    names = ['w_out', 'w_up', 'w_down', 'w_gate', 'w_proj']
    return [cw[n] for n in names], [_layer_block(cw[n].shape, layer, single_buffer=True) for n in names]


def _prompt_ffn(h2, y2, p3, cw, layer, tm):
    n = h2.shape[0]
    row = lambda w: pl.BlockSpec((tm, w), lambda i: (i, 0))
    ws, wspecs = _ffn_weight_specs(cw, layer)
    return pl.pallas_call(
        functools.partial(_ffn_kernel, layer=layer), grid=(n // tm,),
        in_specs=[row(D_MODEL), row(D_MODEL), pl.BlockSpec((None, tm, PLE_DIM), lambda i: (layer, i, 0)),
                  _full(cw['vecs'].shape)] + wspecs,
        out_specs=row(D_MODEL), out_shape=jax.ShapeDtypeStruct((n, D_MODEL), F32),
        compiler_params=pltpu.CompilerParams(dimension_semantics=("arbitrary",),
                                             vmem_limit_bytes=VMEM_LIMIT),
        name=f"prompt_ffn_l{layer}",
    )(h2, y2, p3, cw['vecs'], *ws)


def _sample_pre_kernel(x_ref, pv_ref, wt_ref, convw_ref, wr_ref, wi_ref, ones_ref, rope_ref, gcol_ref,
                       cols_ref, h0_ref, conv0_ref, n0_ref, m0_ref,
                       ya_ref, hn_ref, convn_ref, q_ref, kt_ref, vt_ref, fm_ref, nn_ref, mn_ref, *, layer):
    def vec(off, w):
        return pv_ref[layer:layer + 1, off:off + w]

    n = _rms_rows(x_ref[...], vec(V_G1, D_MODEL))
    nb = n.astype(BF16)
    n_t = n.T.astype(BF16)
    ones_bd = ones_ref[...]

    def proj(r0, cnt):
        return _dot_nt(nb, wt_ref[r0:r0 + cnt, :])

    def proj_t(r0, cnt):
        return _dot(wt_ref[r0:r0 + cnt, :], n_t)

    xa = proj(C_AX, GROUP_W)
    ga = proj(C_AG, GROUP_W)
    cw = convw_ref[layer]
    xc = (vec(V_CB, GROUP_W) + conv0_ref[0] * cw[0:1, :] + conv0_ref[1] * cw[1:2, :]
          + conv0_ref[2] * cw[2:3, :] + xa * cw[3:4, :])
    convn_ref[0] = conv0_ref[1]
    convn_ref[1] = conv0_ref[2]
    convn_ref[2] = xa
    a, bx = _rglru_gates(xc, wr_ref[...], wi_ref[...], vec(V_BR, GROUP_W), vec(V_BI, GROUP_W),
                         vec(V_LAM, GROUP_W))
    hn = a * h0_ref[...] + bx
    hn_ref[...] = hn
    ya_ref[...] = hn * _gelu_tanh(ga)

    rc, ra, rb = rope_ref[0:1, :], rope_ref[1:2, :], rope_ref[2:3, :]
    q = _seg_rms(proj(C_BQ, GROUP_W), vec(V_QG, GROUP_W), ones_bd)
    q_ref[...] = _rope(q, jnp.concatenate([rc, rc], 1), jnp.concatenate([ra, ra], 1),
                       jnp.concatenate([rb, rb], 1))
    k = _seg_rms(proj(C_BK, LANES), vec(V_KG, LANES), ones_bd[0:LANES, 0:LANES])
    kt_ref[...] = _rope(k, rc, ra, rb).T
    vt_ref[...] = proj(C_BV, LANES).T

    lb = _lb_from_gamma([gcol_ref[i] for i in range(gcol_ref.shape[0])], layer)
    cq, cf, ci, cg = (proj_t(C_CQ, GROUP_W), proj_t(C_CF, GROUP_W), proj_t(C_CI, GROUP_W),
                      proj_t(C_CG, GROUP_W))
    f = lb + (1.0 - lb) * _sigmoid(cf)
    fm_ref[FM_HF:FM_HF + GROUP_W, :] = f
    fm_ref[FM_HK:FM_HK + GROUP_W, :] = 1.0 - f
    fm_ref[FM_HQ:FM_HQ + GROUP_W, :] = cq * _sigmoid(cq)
    fm_ref[FM_HV:FM_HV + GROUP_W, :] = ci
    fm_ref[FM_CG:FM_CG + GROUP_W, :] = cg * _sigmoid(cg)

    dq, dk, dv, do = (proj_t(C_DQ, GROUP_W), proj_t(C_DK, GROUP_W), proj_t(C_DV, GROUP_W),
                      proj_t(C_DO, GROUP_W))
    g8 = proj_t(C_GATES, 2 * N_HEADS) + cols_ref[layer, R_GB:R_GB + 2 * N_HEADS, :]
    ig = g8[0:N_HEADS, :]
    lf = _log_sigmoid(g8)[N_HEADS:2 * N_HEADS, :]
    a_int = lf + m0_ref[...]
    m_new = jnp.maximum(a_int, ig)
    dec = jnp.exp(a_int - m_new)
    w = jnp.exp(ig - m_new)
    mn_ref[...] = m_new
    km = dk * (HEAD_DIM ** -0.5)
    dens = []
    for h in range(N_HEADS):
        sl = slice(h * HEAD_DIM, (h + 1) * HEAD_DIM)
        nn_h = dec[h:h + 1, :] * n0_ref[sl, :] + w[h:h + 1, :] * km[sl, :]
        nn_ref[sl, :] = nn_h
        dens.append(jnp.sum(dq[sl, :] * nn_h, axis=0, keepdims=True))
    pad = jnp.zeros((SUBLANES - N_HEADS, x_ref.shape[0]), F32)
    fm_ref[FM_MK:FM_MK + GROUP_W, :] = km
    fm_ref[FM_MQ:FM_MQ + GROUP_W, :] = dq
    fm_ref[FM_MV:FM_MV + GROUP_W, :] = dv
    fm_ref[FM_DO:FM_DO + GROUP_W, :] = _sigmoid(do)
    fm_ref[FM_DEC:FM_DEC + SUBLANES, :] = jnp.concatenate([dec, pad], 0)
    fm_ref[FM_W:FM_W + SUBLANES, :] = jnp.concatenate([w, pad], 0)
    fm_ref[FM_DEN:FM_DEN + SUBLANES, :] = jnp.concatenate(dens + [pad], 0)
    fm_ref[FM_ENEG:FM_ENEG + SUBLANES, :] = jnp.concatenate([jnp.exp(-m_new), pad], 0)


def _own_slab(ref, first_layer):
    if not first_layer:
        return ref
    ref[1:] = jnp.zeros((ref.shape[0] - 1,) + ref.shape[1:], ref.dtype)
    return ref.at[0]


def _sample_attn_kernel(q_ref, kn_ref, vn_ref, kc_ref, vc_ref, sink_ref, *rest, first_layer):
    ko_ref, vo_ref, o_ref = rest[-3:]
    ko_ref, vo_ref = _own_slab(ko_ref, first_layer), _own_slab(vo_ref, first_layer)
    sb = q_ref.shape[0]
    rows = 2 * HEAD_DIM
    lane = _iota((rows, WINDOW), 1)
    kn = kn_ref[...].reshape(rows, sb)
    vn = vn_ref[...].reshape(rows, sb)
    for s in range(sb):
        kt = pltpu.roll(kc_ref[s].reshape(rows, WINDOW), WINDOW - 1, 1)
        ko_ref[s] = jnp.where(lane == WINDOW - 1, kn[:, s:s + 1], kt).reshape(2, HEAD_DIM, WINDOW)
        vt = pltpu.roll(vc_ref[s].reshape(rows, WINDOW), WINDOW - 1, 1)
        vo_ref[s] = jnp.where(lane == WINDOW - 1, vn[:, s:s + 1], vt).reshape(2, HEAD_DIM, WINDOW)
    for kv in range(2):
        kk = ko_ref[:, kv].astype(BF16)
        vv = vo_ref[:, kv].astype(BF16)
        s_ = jnp.einsum('bqc,bcj->bqj', q_ref[:, kv].astype(BF16), kk,
                        preferred_element_type=F32) * (HEAD_DIM ** -0.5)
        sk = sink_ref[kv]
        mx = jnp.maximum(jnp.max(s_, axis=-1, keepdims=True), sk)
        p = jnp.exp(s_ - mx)
        den = jnp.sum(p, axis=-1, keepdims=True) + jnp.exp(sk - mx)
        o = jnp.einsum('bqj,bcj->bqc', p.astype(BF16), vv, preferred_element_type=F32)
        o_ref[:, kv] = o / den


def _sample_state_kernel(fm_ref, s_ref, c_ref, *rest, first_layer):
    so_ref, co_ref, oh_ref, om_ref = rest[-4:]
    so_ref, co_ref = _own_slab(so_ref, first_layer), _own_slab(co_ref, first_layer)
    h = pl.program_id(0)
    r0 = pl.multiple_of(h * HEAD_DIM, HEAD_DIM)
    hv = fm_ref[pl.ds(FM_HV + r0, HEAD_DIM), :]
    mv = fm_ref[pl.ds(FM_MV + r0, HEAD_DIM), :]
    dec = fm_ref[pl.ds(FM_DEC + h, 1), :]
    w = fm_ref[pl.ds(FM_W + h, 1), :]

    def body(d, carry):
        acc_h, acc_m = carry
        r = r0 + d
        s_new = fm_ref[pl.ds(FM_HF + r, 1), :] * s_ref[d] + fm_ref[pl.ds(FM_HK + r, 1), :] * hv
        so_ref[d] = s_new
        c_new = dec * c_ref[d] + (w * fm_ref[pl.ds(FM_MK + r, 1), :]) * mv
        co_ref[d] = c_new
        return (acc_h + fm_ref[pl.ds(FM_HQ + r, 1), :] * s_new,
                acc_m + fm_ref[pl.ds(FM_MQ + r, 1), :] * c_new)

    zero = jnp.zeros((HEAD_DIM, fm_ref.shape[1]), F32)
    acc_h, acc_m = lax.fori_loop(0, HEAD_DIM, body, (zero, zero), unroll=4)
    oh_ref[...] = acc_h
    om_ref[...] = acc_m


def _sample_post_kernel(h_ref, ya_ref, yb_ref, oh_ref, om_ref, fm_ref, cols_ref, p_ref, pv_ref,
                        wout_ref, wup_ref, wdn_ref, wg_ref, wp_ref, o_ref, acc_ref, nb_ref, *, layer):
    k = pl.program_id(0)

    @pl.when(k == 0)
    def _head():
        def head_rms(x):
            return x * lax.rsqrt(jnp.mean(x * x, axis=0, keepdims=True) + EPS)

        yc, yd = [], []
        for h in range(N_HEADS):
            yc.append(head_rms(oh_ref[h]))
            den = fm_ref[FM_DEN + h:FM_DEN + h + 1, :]
            eneg = fm_ref[FM_ENEG + h:FM_ENEG + h + 1, :]
            yd.append(head_rms(om_ref[h] / jnp.maximum(jnp.abs(den), eneg)))
        yc_ = (jnp.concatenate(yc, 0) * cols_ref[layer, R_HG:R_HG + GROUP_W, :]
               * fm_ref[FM_CG:FM_CG + GROUP_W, :])
        yd_ = (jnp.concatenate(yd, 0) * cols_ref[layer, R_MG:R_MG + GROUP_W, :]
               * fm_ref[FM_DO:FM_DO + GROUP_W, :])
        y = jnp.concatenate([ya_ref[...], yb_ref[...], yc_.T, yd_.T], axis=1).astype(BF16)
        h1 = h_ref[...] + _dot(y, wout_ref[...])
        acc_ref[...] = h1
        nb_ref[...] = _rms_rows(h1, pv_ref[layer:layer + 1, V_G2:V_G2 + D_MODEL]).astype(BF16)

    f = jnp.maximum(_dot(nb_ref[...], wup_ref[...]), 0.0)
    acc_ref[...] += _dot((f * f).astype(BF16), wdn_ref[...])

    @pl.when(k == pl.num_programs(0) - 1)
    def _tail():
        acc = acc_ref[...]
        gate = _sigmoid(_dot(acc.astype(BF16), wg_ref[...]))
        o_ref[...] = acc + gate * _dot(p_ref[...].astype(BF16), wp_ref[...])


def _call_full(kern, args, specs, out_shape, name):
    specs = [(_full(a.shape) if s is None else s) for a, s in zip(args, specs)]
    return pl.pallas_call(
        kern, grid=(1,), in_specs=specs,
        out_specs=tuple(_full(s.shape) for s in out_shape), out_shape=tuple(out_shape),
        compiler_params=pltpu.CompilerParams(dimension_semantics=("arbitrary",),
                                             vmem_limit_bytes=VMEM_LIMIT),
        name=name,
    )(*args)


def _sample_layer(h, sv, prev, cw, layer):
    nsm = h.shape[0]
    depth = cw['vecs'].shape[0]
    sd = lambda *shape: jax.ShapeDtypeStruct(shape, F32)
    g, l = sd(nsm, GROUP_W), sd(LANES, nsm)

    args = [h, cw['vecs'], cw['w_in_p'], cw['conv_w'], cw['wr_bd'], cw['wi_bd'], cw['ones_bd'],
            cw['rope_s'], cw['gamma_col'], cw['cols'], sv['h'], sv['conv'], sv['n'], sv['m']]
    specs = [None, None, _layer_block(cw['w_in_p'].shape, layer, single_buffer=True), None,
             _layer_block(cw['wr_bd'].shape, layer), _layer_block(cw['wi_bd'].shape, layer), None,
             None, None, None, _layer_block(sv['h'].shape, layer), _layer_block(sv['conv'].shape, layer),
             _layer_block(sv['n'].shape, layer), _layer_block(sv['m'].shape, layer)]
    outs = [g, g, sd(CONV_W - 1, nsm, GROUP_W), g, l, l, sd(FM_ROWS, nsm), sd(GROUP_W, nsm),
            sd(N_HEADS, nsm)]
    ya, hn, convn, q, kt, vt, fm, nn, mn = _call_full(
        functools.partial(_sample_pre_kernel, layer=layer), args, specs, outs, f"sample_pre_l{layer}")

    sb = 16
    nblk = nsm // sb
    q3 = jnp.pad(q.reshape(nsm, 2, 2, HEAD_DIM), ((0, 0), (0, 0), (0, SUBLANES - 2), (0, 0)))
    to_blocks = lambda a: a.reshape(2, HEAD_DIM, nblk, sb).transpose(2, 0, 1, 3)
    cshape = (depth, nsm, 2, HEAD_DIM, WINDOW)
    cspec = pl.BlockSpec((None, sb, 2, HEAD_DIM, WINDOW), lambda i: (layer, i, 0, 0, 0))
    nspec = pl.BlockSpec((None, 2, HEAD_DIM, sb), lambda i: (i, 0, 0, 0))
    qspec = pl.BlockSpec((sb, 2, SUBLANES, HEAD_DIM), lambda i: (i, 0, 0, 0))
    any_spec = pl.BlockSpec(memory_space=pl.ANY)
    cout = (pl.BlockSpec((depth, sb, 2, HEAD_DIM, WINDOW), lambda i: (0, i, 0, 0, 0)) if prev is None
            else cspec)
    chain = [] if prev is None else [prev['k'], prev['v']]
    ko, vo, o3 = pl.pallas_call(
        functools.partial(_sample_attn_kernel, first_layer=prev is None), grid=(nblk,),
        in_specs=[qspec, nspec, nspec, cspec, cspec, _layer_block(cw['sinks8'].shape, layer)]
        + [any_spec] * len(chain),
        out_specs=(cout, cout, qspec),
        out_shape=(sd(*cshape), sd(*cshape), sd(nsm, 2, SUBLANES, HEAD_DIM)),
        input_output_aliases={6 + i: i for i in range(len(chain))},
        compiler_params=pltpu.CompilerParams(dimension_semantics=("arbitrary",),
                                             vmem_limit_bytes=VMEM_LIMIT),
        name=f"sample_attn_l{layer}",
    )(q3, to_blocks(kt), to_blocks(vt), sv['k'], sv['v'], cw['sinks8'], *chain)
    yb = o3[:, :, 0:2, :].reshape(nsm, GROUP_W)

    sshape = (depth, N_HEADS, HEAD_DIM, HEAD_DIM, nsm)
    sspec = pl.BlockSpec((None, None, HEAD_DIM, HEAD_DIM, nsm), lambda i: (layer, i, 0, 0, 0))
    ospec = pl.BlockSpec((None, HEAD_DIM, nsm), lambda i: (i, 0, 0))
    sout = (pl.BlockSpec((depth, None, HEAD_DIM, HEAD_DIM, nsm), lambda i: (0, i, 0, 0, 0))
            if prev is None else sspec)
    chain = [] if prev is None else [prev['s'], prev['c']]
    so, co, oh, om = pl.pallas_call(
        functools.partial(_sample_state_kernel, first_layer=prev is None), grid=(N_HEADS,),
        in_specs=[_full(fm.shape), sspec, sspec] + [any_spec] * len(chain),
        out_specs=(sout, sout, ospec, ospec),
        out_shape=(sd(*sshape), sd(*sshape), sd(N_HEADS, HEAD_DIM, nsm), sd(N_HEADS, HEAD_DIM, nsm)),
        input_output_aliases={3 + i: i for i in range(len(chain))},
        compiler_params=pltpu.CompilerParams(dimension_semantics=("arbitrary",),
                                             vmem_limit_bytes=VMEM_LIMIT),
        name=f"sample_state_l{layer}",
    )(fm, sv['s'], sv['c'], *chain)

    ws, wspecs = _ffn_weight_specs(cw, layer)
    chunk = 1024
    wspecs[1] = pl.BlockSpec((None, D_MODEL, chunk), lambda k: (layer, 0, k))
    wspecs[2] = pl.BlockSpec((None, chunk, D_MODEL), lambda k: (layer, k, 0))
    post_args = [h, ya, yb, oh, om, fm, cw['cols'], sv['p'], cw['vecs']] + ws
    post_specs = ([_full(a.shape) for a in post_args[:7]] + [_layer_block(sv['p'].shape, layer),
                                                             _full(cw['vecs'].shape)] + wspecs)
    h_new = pl.pallas_call(
        functools.partial(_sample_post_kernel, layer=layer), grid=(D_FF // chunk,),
        in_specs=post_specs, out_specs=_full((nsm, D_MODEL)), out_shape=sd(nsm, D_MODEL),
        scratch_shapes=[pltpu.VMEM((nsm, D_MODEL), F32), pltpu.VMEM((nsm, D_MODEL), BF16)],
        compiler_params=pltpu.CompilerParams(dimension_semantics=("arbitrary",),
                                             vmem_limit_bytes=VMEM_LIMIT),
        name=f"sample_post_l{layer}",
    )(*post_args)
    small = (hn, convn, nn, mn)
    big = {'k': ko, 'v': vo, 's': so, 'c': co}
    return h_new, small, big


def _block_diag_all(w):
    depth = w.shape[0]
    rows = w.reshape(depth, GROUP_W, HEAD_DIM)
    idx = np.arange(GROUP_W) // HEAD_DIM
    mask = idx[:, None] == idx[None, :]
    return jnp.where(mask[None], jnp.tile(rows, (1, 1, N_HEADS)), 0.0)


def _rope_lane_freq():
    half = ROT_DIM // 2
    inv = np.power(np.float32(ROPE_THETA), -np.arange(half, dtype=np.float32) * np.float32(2.0 / ROT_DIM))
    dd = np.arange(LANES) % HEAD_DIM
    freq = np.where(dd < ROT_DIM, inv[dd % half], np.float32(0.0))
    m_a = (dd < half).astype(np.float32)
    m_b = ((dd >= half) & (dd < ROT_DIM)).astype(np.float32)
    return freq.astype(np.float32), m_a, m_b


def _rope_tables(pos):
    freq, m_a, m_b = _rope_lane_freq()
    ang = np.asarray(pos, np.float32)[:, None] * freq[None, :]
    cos, sin = np.cos(ang), np.sin(ang)
    return cos, -sin * m_a, sin * m_b


def _rope_split_tables(t, tm):
    freq, m_a, m_b = _rope_lane_freq()
    ang_r = np.arange(tm, dtype=np.float32)[:, None] * freq[None, :]
    cr, sr = np.cos(ang_r), np.sin(ang_r)
    rope_r = np.stack([cr, sr, -cr * m_a, -sr * m_a, cr * m_b, sr * m_b])
    ang_b = (np.arange(t // tm) * tm).astype(np.float32)[:, None] * freq[None, :]
    rope_base = np.concatenate([np.cos(ang_b), np.sin(ang_b)], 1)[:, None, :]
    return rope_r, rope_base


def _hgrn_level_masks():
    t = np.arange(BLK)[:, None]
    s = np.arange(BLK)[None, :]
    small = [t == s]
    big = []
    for lev in range(1, 8):
        half = 1 << (lev - 1)
        own = ((t >> lev) == (s >> lev)) & ((t & half) != 0) & ((s & half) == 0)
        if lev < 4:
            small.append(own)
        else:
            rows = np.concatenate([np.arange(m, m + half) for m in range(half, BLK, 2 * half)])
            big.append(own[rows])
    return np.stack(small).astype(np.float32), np.stack(big).astype(np.float32)


def _pad_last(v, width):
    return jnp.pad(v, ((0, 0), (0, width - v.shape[-1])))


def _common(w, t, tm_mix, past_len, nsm):
    depth = w['w_in'].shape[0]
    tile = lambda v, n: jnp.tile(v, (1, n))
    vecs = jnp.concatenate([
        w['norm1_g'], w['norm2_g'], w['conv_b'], w['lru_br'], w['lru_bi'], w['lru_lam'],
        tile(w['q_norm_g'], N_HEADS), tile(w['k_norm_g'], 2), _pad_last(w['attn_sinks'], LANES),
        tile(w['hgrn_norm_g'], N_HEADS), tile(w['mlstm_norm_g'], N_HEADS),
        _pad_last(w['mlstm_ib'], LANES), _pad_last(w['mlstm_fb'], LANES)], axis=1)
    cols = jnp.concatenate([tile(w['hgrn_norm_g'], N_HEADS), tile(w['mlstm_norm_g'], N_HEADS),
                            w['mlstm_ib'], w['mlstm_fb']], axis=1)
    w_in = w['w_in']
    gate_bias = jnp.concatenate([w['mlstm_ib'], w['mlstm_fb']], axis=1)
    lvl_small, lvl_big = _hgrn_level_masks()
    sinks = w['attn_sinks']
    z2 = jnp.zeros((depth, 2, SUBLANES - 2), F32)
    idx = np.arange(GROUP_W)
    rope_r, rope_base = _rope_split_tables(t, tm_mix)
    return {
        'vecs': vecs,
        'cols': jnp.broadcast_to(cols[:, :, None], cols.shape + (nsm,)),
        'gamma': w['hgrn_gamma'],
        'gamma_col': jnp.broadcast_to(w['hgrn_gamma'][:, :, None], w['hgrn_gamma'].shape + (nsm,)),
        'w_in_p': jnp.pad(jnp.swapaxes(w_in, 1, 2), ((0, 0), (0, N_IN - D_IN), (0, 0))).astype(BF16),
        'lvl_small': lvl_small, 'lvl_big': lvl_big,
        'gate_bias': jnp.broadcast_to(gate_bias[:, :, None], gate_bias.shape + (tm_mix,)),
        'triu': (np.arange(tm_mix)[:, None] <= np.arange(tm_mix)[None, :]).astype(np.float32),
        'conv_w': w['conv_w'],
        'wr_bd': _block_diag_all(w['lru_wr']).astype(BF16),
        'wi_bd': _block_diag_all(w['lru_wi']).astype(BF16),
        'ones_bd': jnp.asarray(idx[:, None] // HEAD_DIM == idx[None, :] // HEAD_DIM, BF16),
        'sinks8': jnp.concatenate([sinks.reshape(depth, 2, 2), z2], axis=2)[..., None],
        'rope_r': rope_r, 'rope_base': rope_base,
        'rope_s': np.concatenate(_rope_tables(past_len + np.arange(1)), axis=0),
        'w_out': w['w_out'].astype(BF16), 'w_up': w['w_up'].astype(BF16),
        'w_down': w['w_down'].astype(BF16), 'w_gate': w['w_ple_gate'].astype(BF16),
        'w_proj': w['w_ple_proj'].astype(BF16),
    }


def _run(x_prompt, x_sample, p_prompt, p_sample, sample_state, w, past_len, tm_mix=256, tm_ffn=512):
    depth = w['w_in'].shape[0]
    bsz, t, _ = x_prompt.shape
    nsm = x_sample.shape[0]
    cw = _common(w, t, tm_mix, past_len, nsm)
    tm_ffn = min(tm_ffn, bsz * t)
    h0, conv0, kc, vc, s0, c0, n0, m0 = sample_state
    sv = {'h': h0, 'conv': jnp.transpose(conv0, (0, 2, 1, 3)),
          'k': jnp.transpose(kc, (0, 1, 3, 4, 2)), 'v': jnp.transpose(vc, (0, 1, 3, 4, 2)),
          's': jnp.transpose(s0, (0, 2, 3, 4, 1)), 'c': jnp.transpose(c0, (0, 2, 3, 4, 1)),
          'n': jnp.transpose(n0, (0, 2, 3, 1)).reshape(depth, GROUP_W, nsm),
          'm': jnp.transpose(m0, (0, 2, 1)), 'p': p_sample.reshape(depth, nsm, PLE_DIM)}
    p3 = p_prompt.reshape(depth, bsz * t, PLE_DIM)

    hp = x_prompt
    hs = x_sample.reshape(nsm, D_MODEL)
    pst, s_small, big = None, [], None
    for l in range(depth):
        y, *pst = _prompt_mixers(hp, cw, l, tm_mix, pst)
        hp = _prompt_ffn(hp.reshape(bsz * t, D_MODEL), y.reshape(bsz * t, D_MODEL), p3, cw, l,
                         tm_ffn).reshape(bsz, t, D_MODEL)
        hs, small, big = _sample_layer(hs, sv, big, cw, l)
        s_small.append(small)
    stack = lambda sts, i: jnp.stack([s[i] for s in sts])
    hl, conv, kst, vst, sst, cst, mst = pst
    s_t = jnp.swapaxes(sst, -1, -2).reshape(depth, bsz, 2, 2, HEAD_DIM, 2, HEAD_DIM)
    s_hgrn = jnp.stack([s_t[:, :, :, 0, :, 0, :], s_t[:, :, :, 1, :, 1, :]], axis=3)
    c_t = cst.reshape(depth, bsz, 2, 2, 2, HEAD_DIM, LANES)
    c_rows = jnp.stack([c_t[:, :, :, 0, 0], c_t[:, :, :, 1, 1]], axis=3)
    c_rows = c_rows.reshape(depth, bsz, N_HEADS, HEAD_DIM, LANES)
    prompt_out = (hl[:, :, 0], conv[:, :, 8 - (CONV_W - 1):], kst.reshape(depth, bsz, WINDOW, 2, HEAD_DIM),
                  vst.reshape(depth, bsz, WINDOW, 2, HEAD_DIM),
                  s_hgrn.reshape(depth, bsz, N_HEADS, HEAD_DIM, HEAD_DIM), c_rows[..., 0:HEAD_DIM],
                  c_rows[..., HEAD_DIM], mst[:, :, 0, 0:N_HEADS])
    hn, convn, nn, mn = (stack(s_small, i) for i in range(4))
    sample_out = (hn, jnp.transpose(convn, (0, 2, 1, 3)),
                  jnp.transpose(big['k'], (0, 1, 4, 2, 3)), jnp.transpose(big['v'], (0, 1, 4, 2, 3)),
                  jnp.transpose(big['s'], (0, 4, 1, 2, 3)), jnp.transpose(big['c'], (0, 4, 1, 2, 3)),
                  jnp.transpose(nn.reshape(depth, N_HEADS, HEAD_DIM, nsm), (0, 3, 1, 2)),
                  jnp.transpose(mn, (0, 2, 1)))
    return (hp, hs.reshape(x_sample.shape)) + prompt_out + sample_out


def kernel(x_prompt, x_sample, p_prompt, p_sample, state_rglru_h, state_rglru_conv, cache_swa_k, cache_swa_v, state_hgrn_s, state_mlstm_c, state_mlstm_n, state_mlstm_m, norm1_g, w_in, conv_w, conv_b, lru_wr, lru_br, lru_wi, lru_bi, lru_lam, q_norm_g, k_norm_g, attn_sinks, hgrn_gamma, hgrn_norm_g, mlstm_ib, mlstm_fb, mlstm_norm_g, w_out, norm2_g, w_up, w_down, w_ple_gate, w_ple_proj):
    w = {'norm1_g': norm1_g, 'w_in': w_in, 'conv_w': conv_w, 'conv_b': conv_b, 'lru_wr': lru_wr,
         'lru_br': lru_br, 'lru_wi': lru_wi, 'lru_bi': lru_bi, 'lru_lam': lru_lam, 'q_norm_g': q_norm_g,
         'k_norm_g': k_norm_g, 'attn_sinks': attn_sinks, 'hgrn_gamma': hgrn_gamma,
         'hgrn_norm_g': hgrn_norm_g, 'mlstm_ib': mlstm_ib, 'mlstm_fb': mlstm_fb,
         'mlstm_norm_g': mlstm_norm_g, 'w_out': w_out, 'norm2_g': norm2_g, 'w_up': w_up,
         'w_down': w_down, 'w_ple_gate': w_ple_gate, 'w_ple_proj': w_ple_proj}
    st = (state_rglru_h, state_rglru_conv, cache_swa_k, cache_swa_v, state_hgrn_s, state_mlstm_c,
          state_mlstm_n, state_mlstm_m)
    past_len = 8192
    return _run(x_prompt, x_sample, p_prompt, p_sample, st, w, past_len)
```

```python
import functools
import types

import jax
import jax.numpy as jnp
import numpy as np
from jax import lax
from jax.experimental import pallas as pl
from jax.experimental.pallas import tpu as pltpu

F32 = jnp.float32
BF16 = jnp.bfloat16

D_MODEL = 1024
GROUP_W = 256
HEAD_DIM = 64
N_HEADS = 4
EPS = 1e-6
NEG_BIG = -1e30
LRU_C = 8.0
CONV_W = 4
ROT_DIM = 16
ROPE_THETA = 500000.0
WINDOW = 128
D_FF = 4096
PLE_DIM = 256
LANES = 128
SUBLANES = 8
BLK = 128

C_AX, C_AG, C_BQ, C_BK, C_BV = 0, 256, 512, 768, 896
C_CQ, C_CF, C_CI, C_CG = 1024, 1280, 1536, 1792
C_DQ, C_DK, C_DV, C_DO = 2048, 2304, 2560, 2816
C_GATES = 3072
D_IN = 3080
N_IN = 3200

V_G1, V_G2, V_CB, V_BR, V_BI, V_LAM = 0, 1024, 2048, 2304, 2560, 2816
V_QG, V_KG, V_SINK, V_HG, V_MG, V_IB, V_FB = 3072, 3328, 3456, 3584, 3840, 4096, 4224
N_VEC = 4352

R_HG, R_MG, R_GB = 0, 256, 512
N_COL = 520

FM_HF, FM_HK, FM_HQ, FM_HV, FM_CG = 0, 256, 512, 768, 1024
FM_MK, FM_MQ, FM_MV, FM_DO = 1280, 1536, 1792, 2048
FM_DEC, FM_W, FM_DEN, FM_ENEG = 2304, 2312, 2320, 2328
FM_ROWS = 2336

VMEM_LIMIT = 56 * 1024 * 1024


def _dot(a, b):
    return jnp.dot(a, b, preferred_element_type=F32)


def _dot_nt(a, b):
    return lax.dot_general(a, b, (((1,), (1,)), ((), ())), preferred_element_type=F32)


def _sigmoid(x):
    return jax.nn.sigmoid(x)


def _gelu_tanh(x):
    return 0.5 * x * (1.0 + jnp.tanh(0.7978845608028654 * (x + 0.044715 * (x * x * x))))


def _log_sigmoid(x):
    return jnp.minimum(x, 0.0) - jnp.log1p(jnp.exp(-jnp.abs(x)))


def _softplus(x):
    return jnp.maximum(x, 0.0) + jnp.log1p(jnp.exp(-jnp.abs(x)))


def _rms_rows(x, g):
    return x * lax.rsqrt(jnp.mean(x * x, axis=-1, keepdims=True) + EPS) * g


def _seg_mean_sq(x, ones_bd):
    sq = x * x
    hi = sq.astype(BF16)
    lo = (sq - hi.astype(F32)).astype(BF16)
    return (_dot(hi, ones_bd) + _dot(lo, ones_bd)) * (1.0 / HEAD_DIM)


def _seg_rms(x, g, ones_bd):
    return x * lax.rsqrt(_seg_mean_sq(x, ones_bd) + EPS) * g


def _rope(x, c, sa, sb):
    w = x.shape[1]
    up = pltpu.roll(x, w - ROT_DIM // 2, 1)
    dn = pltpu.roll(x, ROT_DIM // 2, 1)
    return x * c + up * sa + dn * sb


def _lb_from_gamma(gammas, layer):
    mx = functools.reduce(jnp.maximum, gammas)
    e = [jnp.exp(g - mx) for g in gammas]
    tot = functools.reduce(lambda a, b: a + b, e)
    lb = jnp.zeros_like(tot)
    for i in range(1, layer + 1):
        lb = lb + e[i] / tot
    return lb


def _iota(shape, axis):
    return lax.broadcasted_iota(jnp.int32, shape, axis)


def _rglru_gates(xc, wr, wi, br, bi, lam):
    xcb = xc.astype(BF16)
    r = _sigmoid(_dot(xcb, wr) + br)
    ig = _sigmoid(_dot(xcb, wi) + bi)
    log_a = (-LRU_C) * r * _softplus(-lam)
    a = jnp.exp(log_a)
    y = 1.0 - a * a
    root = jnp.where(y > 0.0, y * lax.rsqrt(y), 0.0)
    return a, root * (ig * xc)


def _swa_block(q, k, v, k_prev, v_prev, sink, first):
    kk = jnp.concatenate([k_prev, k], axis=0)
    vv = jnp.concatenate([v_prev, v], axis=0)
    qi = _iota((BLK, 2 * BLK), 0)
    kj = _iota((BLK, 2 * BLK), 1)
    valid = (kj > qi) & (kj <= qi + WINDOW) & ((kj >= BLK) | jnp.logical_not(first))
    lane = _iota((2 * BLK, LANES), 1)
    one_col = jnp.where(lane == HEAD_DIM, 1.0, 0.0)
    vaug = (jnp.where(lane < HEAD_DIM, vv, one_col).astype(BF16),
            jnp.where(lane < HEAD_DIM, pltpu.roll(vv, HEAD_DIM, 1), one_col).astype(BF16))
    qs = q * (HEAD_DIM ** -0.5)
    outs = []
    for h in range(N_HEADS):
        kv = h // 2
        qh = qs[:, h * HEAD_DIM:(h + 1) * HEAD_DIM].astype(BF16)
        kh = kk[:, kv * HEAD_DIM:(kv + 1) * HEAD_DIM].astype(BF16)
        s = jnp.where(valid, _dot_nt(qh, kh), NEG_BIG)
        sk = sink[:, h:h + 1]
        mx = jnp.maximum(jnp.max(s, axis=-1, keepdims=True), sk)
        p = jnp.exp(s - mx)
        o = _dot(p.astype(BF16), vaug[kv])
        den = o[:, HEAD_DIM:HEAD_DIM + 1] + jnp.exp(sk - mx)
        outs.append(o[:, 0:HEAD_DIM] / den)
    return jnp.concatenate(outs, axis=1)


def _head_masks(rows, dtype):
    lane = _iota((rows, LANES), 1)
    return (jnp.where(lane < HEAD_DIM, 1.0, 0.0).astype(dtype),
            jnp.where(lane >= HEAD_DIM, 1.0, 0.0).astype(dtype))


def _pair_scores(qe, ke_b, hm):
    res = []
    for p in range(2):
        sl = slice(p * LANES, (p + 1) * LANES)
        kp = ke_b[:, sl]
        pr = _dot_nt(qe[:, sl].astype(BF16), jnp.concatenate([kp * hm[0], kp * hm[1]], axis=0))
        res += [pr[:, :BLK], pr[:, BLK:]]
    return res


def _hgrn_block(cq, cf, ci, cg, lb, hg, sst_ref, ones_bd, msmall_ref, mbig_ref, hm, sub):
    q = cq * _sigmoid(cq)
    f = lb + (1.0 - lb) * _sigmoid(cf)
    logf = jnp.log(f)
    k = 1.0 - f

    att = [s_ * msmall_ref[0] for s_ in _pair_scores(q, k.astype(BF16), hm)]

    c = logf
    tot = logf
    for lev in range(1, 4):
        half = 1 << (lev - 1)
        right = (sub & half) != 0
        tot_l = pltpu.roll(tot, half, 0)
        tot_r = pltpu.roll(tot, BLK - half, 0)
        e = jnp.exp(jnp.where(right, c, tot - c))
        sc = _pair_scores(q * e, (k * e).astype(BF16), hm)
        m = msmall_ref[lev]
        att = [a_ + s_ * m for a_, s_ in zip(att, sc)]
        c = c + jnp.where(right, tot_l, 0.0)
        tot = tot + jnp.where(right, tot_l, tot_r)

    pieces, carry = [], None
    for g in range(BLK // SUBLANES):
        rows = slice(g * SUBLANES, (g + 1) * SUBLANES)
        pieces.append(c[rows] if carry is None else c[rows] + carry)
        t_g = tot[g * SUBLANES:g * SUBLANES + 1]
        carry = t_g if carry is None else carry + t_g
    yield
    b = jnp.concatenate(pieces, axis=0)
    btot = carry

    for lev in range(4, 8):
        half = 1 << (lev - 1)
        nblk = BLK // (2 * half)
        qr, kf = [], []
        for i in range(nblk):
            lo = i * 2 * half
            mid = lo + half
            bref = b[mid - 1:mid]
            qr.append(q[mid:mid + half] * jnp.exp(b[mid:mid + half] - bref))
            kf.append(k[lo:mid] * jnp.exp(bref - b[lo:mid]))
            kf.append(jnp.zeros((half, GROUP_W), F32))
        sc = _pair_scores(jnp.concatenate(qr, axis=0), jnp.concatenate(kf, axis=0).astype(BF16), hm)
        m = mbig_ref[lev - 4]
        zero = jnp.zeros((half, BLK), F32)
        new = []
        for a_, s_ in zip(att, sc):
            u = s_ * m
            parts = []
            for i in range(nblk):
                parts += [zero, u[i * half:(i + 1) * half]]
            new.append(a_ + jnp.concatenate(parts, axis=0))
        att = new

    yield
    qe = (q * jnp.exp(b)).astype(BF16)
    ke = (k * jnp.exp(btot - b)).astype(BF16)
    etot = jnp.exp(btot)
    row_l = _iota((BLK, BLK), 0)
    col_l = _iota((BLK, BLK), 1)
    same_head = (row_l >= HEAD_DIM) == (col_l >= HEAD_DIM)
    outs = []
    for p in range(2):
        sl = slice(p * LANES, (p + 1) * LANES)
        st = sst_ref[p]
        vp = ci[:, sl]
        vb = vp.astype(BF16)
        a2 = jnp.concatenate([att[2 * p], att[2 * p + 1]], axis=1).astype(BF16)
        v2 = jnp.concatenate([vb * hm[0], vb * hm[1]], axis=0)
        o = _dot_nt(qe[:, sl], st.astype(BF16)) + _dot(a2, v2)
        upd = _dot(vp.T.astype(BF16), ke[:, sl])
        sst_ref[p] = st * etot[:, sl] + jnp.where(same_head, upd, 0.0)
        outs.append(o)
    o = jnp.concatenate(outs, axis=1)
    return _seg_rms(o, hg, ones_bd) * (cg * _sigmoid(cg))


def _mlstm_tile_gates(gcols, gb, m0, triu):
    tm = gcols.shape[0]
    gt = gcols.T[0:SUBLANES, :] + gb
    lf = _log_sigmoid(gt)
    hi = lf.astype(BF16)
    r1 = lf - hi.astype(F32)
    mid = r1.astype(BF16)
    lo = (r1 - mid.astype(F32)).astype(BF16)
    parts = jnp.concatenate([hi.astype(F32), mid.astype(F32), lo.astype(F32)], axis=0)
    cs = _dot(parts, triu)
    fcum = cs[0:SUBLANES] + cs[SUBLANES:2 * SUBLANES] + cs[2 * SUBLANES:3 * SUBLANES]
    fcum = pltpu.roll(fcum, N_HEADS, 0)
    g = gt - fcum
    pad = jnp.zeros((LANES - 2 * SUBLANES, tm), F32)
    cols = jnp.concatenate([g, fcum, pad], axis=0).T
    f_c = pltpu.roll(cols, LANES - SUBLANES, 1)
    sub = _iota((tm, LANES), 0) & (SUBLANES - 1)
    cm = cols
    s = 1
    while s < SUBLANES:
        cm = jnp.maximum(cm, jnp.where(sub >= s, pltpu.roll(cm, s, 0), NEG_BIG))
        s *= 2
    carry = m0
    ms = []
    for grp in range(tm // SUBLANES):
        m_g = jnp.maximum(cm[grp * SUBLANES:(grp + 1) * SUBLANES], carry)
        ms.append(m_g)
        carry = m_g[SUBLANES - 1:SUBLANES]
    m_c = jnp.concatenate(ms, axis=0)
    eneg_c = jnp.exp(-(f_c + m_c))
    m_new = f_c[tm - 1:tm] + carry
    return g, cols, m_c, eneg_c, m_new


def _mlstm_block(dq, dk, dv, do, g_rows, g_c, m_c, eneg_c, m_prev, mg, cst_ref, ones_bd, hm, hmf):
    k = dk * (HEAD_DIM ** -0.5)
    m_end = m_c[BLK - 1:BLK]
    inter = jnp.exp(m_prev - m_c)
    wend = jnp.exp(g_c - m_end)
    dec0 = jnp.exp(m_prev - m_end)
    mrun_c = lambda h: m_c[:, h:h + 1]
    inter_c = lambda h: inter[:, h:h + 1]
    eneg_c_ = lambda h: eneg_c[:, h:h + 1]
    wend_c = lambda h: wend[:, h:h + 1]
    g = g_rows
    lane = _iota((BLK, LANES), 1)
    low = lane < HEAD_DIM
    one_col = jnp.where(lane == HEAD_DIM, 1.0, 0.0)
    causal = _iota((BLK, BLK), 1) <= _iota((BLK, BLK), 0)

    outs = []
    for p in range(2):
        sl = slice(p * LANES, (p + 1) * LANES)
        kp, vp = k[:, sl], dv[:, sl]
        qb = dq[:, sl].astype(BF16)
        qm = [qb * hm[0], qb * hm[1]]
        sc2 = _dot_nt(jnp.concatenate(qm, axis=0), kp.astype(BF16))
        v_sw = pltpu.roll(vp, HEAD_DIM, 1)
        hv = []
        for hh in range(2):
            h = 2 * p + hh
            sc = sc2[hh * BLK:(hh + 1) * BLK]
            w = jnp.exp(jnp.where(causal, g[h:h + 1, :] - mrun_c(h), NEG_BIG))
            sw = (sc * w).astype(BF16)
            vaug = jnp.where(low, vp if hh == 0 else v_sw, one_col).astype(BF16)
            cst = cst_ref[h]
            nd = inter_c(h) * _dot(qm[hh], cst.astype(BF16)) + _dot(sw, vaug)
            den = nd[:, HEAD_DIM:HEAD_DIM + 1]
            hv.append(nd / jnp.maximum(jnp.abs(den), eneg_c_(h)))
            kw = kp * (wend_c(h) * hmf[hh])
            cst_ref[h] = dec0[:, h:h + 1] * cst + _dot(kw.T.astype(BF16), vaug)
        outs.append(jnp.where(low, hv[0], pltpu.roll(hv[1], HEAD_DIM, 1)))
        yield
    hcat = jnp.concatenate(outs, axis=1)
    return _seg_rms(hcat, mg, ones_bd) * _sigmoid(do)


def _mixer_kernel(hc_ref, hn_ref, pv_ref, win_ref, convw_ref, wr_ref, wi_ref, rr_ref, rbase_ref, gamma_ref,
                  ones_ref, msmall_ref, mbig_ref, gb_ref, triu_ref, *rest, layer, tm):
    y_ref, hl_ref, conv_ref, kst_ref, vst_ref, sst_ref, cst_ref, mst_ref, u_scr, nb_scr = rest[-10:]
    t = pl.program_id(0)
    states = (hl_ref, conv_ref, kst_ref, vst_ref, sst_ref, cst_ref, mst_ref)

    @pl.when(t == 0)
    def _init():
        for ref in states:
            ref[...] = jnp.zeros_like(ref)

    if layer == 0:
        hl_ref, conv_ref, kst_ref, vst_ref, sst_ref, cst_ref, mst_ref = (ref.at[0] for ref in states)

    bsz = hc_ref.shape[0]
    g1 = pv_ref[layer:layer + 1, V_G1:V_G1 + D_MODEL]
    cols = (((C_AX, C_BQ - C_AX),), ((C_BQ, C_CQ - C_BQ),),
            ((C_CQ, 2 * GROUP_W), (C_CI, 2 * GROUP_W)),
            ((C_DQ, 2 * GROUP_W), (C_DV, N_IN - C_DV)))

    def normalise(src_ref):
        for b in range(bsz):
            nb_scr[b * tm:(b + 1) * tm, :] = _rms_rows(src_ref[b], g1).astype(BF16)

    def project(group_ids):
        for gid in group_ids:
            for c0, w in cols[gid]:
                u_scr[:, c0:c0 + w] = _dot_nt(nb_scr[...], win_ref[c0:c0 + w, :])
                yield

    phases = ((0, 1), (2, 3))

    @pl.when(t == 0)
    def _prologue():
        normalise(hc_ref)
        _round_robin([(lambda: True, project(phases[0]))])

    def vec(off, w):
        return pv_ref[layer:layer + 1, off:off + w]

    ctxs = []
    for b in range(bsz):
        proj = functools.partial(lambda c0, w, b: u_scr[b * tm:(b + 1) * tm, c0:c0 + w], b=b)
        ctxs.append(types.SimpleNamespace(
            t=t, proj=proj, vec=vec, layer=layer, tm=tm, ones_bd=ones_ref[...], convw_ref=convw_ref,
            wr_ref=wr_ref, wi_ref=wi_ref, rr_ref=rr_ref, rbase_ref=rbase_ref, gamma_ref=gamma_ref,
            msmall_ref=msmall_ref, mbig_ref=mbig_ref, gb_ref=gb_ref, triu_ref=triu_ref, y_ref=y_ref.at[b],
            hl_ref=hl_ref.at[b], conv_ref=conv_ref.at[b], kst_ref=kst_ref.at[b], vst_ref=vst_ref.at[b],
            sst_ref=sst_ref.at[b], cst_ref=cst_ref.at[b], mst_ref=mst_ref.at[b],
            hm=_head_masks(BLK, BF16), hmf=_head_masks(1, F32)))
    groups = (_group_a, _group_b, _group_c, _group_d)
    for pi, phase in enumerate(phases):
        if pi == 1:
            normalise(hn_ref)
        tasks = [(lambda: True, project(phases[1 - pi]))]
        pairs = ([(c, gid) for c in ctxs for gid in phase] if pi == 0
                 else [(c, gid) for gid in phase for c in ctxs])
        for c, gid in pairs:
            tasks.append((lambda: True, groups[gid](c)))
        _round_robin(tasks)


def _round_robin(tasks):
    tasks = list(tasks)
    while tasks:
        for task in list(tasks):
            ready, gen = task
            if not ready():
                continue
            try:
                next(gen)
            except StopIteration:
                tasks.remove(task)


def _group_a(c):
    proj, vec, tm, layer = c.proj, c.vec, c.tm, c.layer
    conv_ref, convw_ref, wr_ref, wi_ref, hl_ref, y_ref = (c.conv_ref, c.convw_ref, c.wr_ref, c.wi_ref,
                                                          c.hl_ref, c.y_ref)
    xa = proj(C_AX, GROUP_W)
    ga = proj(C_AG, GROUP_W)
    cw = convw_ref[layer]
    tail = conv_ref[...]
    sub8 = _iota((SUBLANES, GROUP_W), 0)

    def shifted(j):
        r = pltpu.roll(xa, j, 0)
        head = jnp.where(sub8 < j, pltpu.roll(tail, j, 0), r[0:SUBLANES])
        return jnp.concatenate([head, r[SUBLANES:]], axis=0)

    xc = (vec(V_CB, GROUP_W) + shifted(3) * cw[0:1, :] + shifted(2) * cw[1:2, :]
          + shifted(1) * cw[2:3, :] + xa * cw[3:4, :])
    conv_ref[...] = xa[tm - SUBLANES:tm]
    yield
    a, bx = _rglru_gates(xc, wr_ref[...], wi_ref[...], vec(V_BR, GROUP_W), vec(V_BI, GROUP_W),
                         vec(V_LAM, GROUP_W))
    yield
    sub_t = _iota((tm, GROUP_W), 0) & (SUBLANES - 1)
    s = 1
    while s < SUBLANES:
        keep = sub_t >= s
        a_s = pltpu.roll(a, s, 0)
        b_s = pltpu.roll(bx, s, 0)
        bx = jnp.where(keep, a * b_s + bx, bx)
        a = jnp.where(keep, a * a_s, a)
        s *= 2
    carry = hl_ref[...]
    hs = []
    for g in range(tm // SUBLANES):
        rows = slice(g * SUBLANES, (g + 1) * SUBLANES)
        hg_ = a[rows] * carry + bx[rows]
        hs.append(hg_)
        carry = hg_[SUBLANES - 1:SUBLANES]
    hseq = jnp.concatenate(hs, axis=0)
    hl_ref[...] = carry
    yield
    y_ref[:, 0:GROUP_W] = (hseq * _gelu_tanh(ga)).astype(y_ref.dtype)


def _group_b(c):
    proj, vec, tm, t, ones_bd = c.proj, c.vec, c.tm, c.t, c.ones_bd
    rr_ref, rbase_ref, kst_ref, vst_ref, y_ref = c.rr_ref, c.rbase_ref, c.kst_ref, c.vst_ref, c.y_ref
    cb = rbase_ref[:, 0:LANES]
    sb_ = rbase_ref[:, LANES:2 * LANES]
    rc = cb * rr_ref[0] - sb_ * rr_ref[1]
    ra = sb_ * rr_ref[2] + cb * rr_ref[3]
    rb = sb_ * rr_ref[4] + cb * rr_ref[5]
    q = _seg_rms(proj(C_BQ, GROUP_W), vec(V_QG, GROUP_W), ones_bd)
    q = jnp.concatenate([_rope(q[:, 0:LANES], rc, ra, rb), _rope(q[:, LANES:], rc, ra, rb)], axis=1)
    k = _seg_rms(proj(C_BK, LANES), vec(V_KG, LANES), ones_bd[0:LANES, 0:LANES])
    k = _rope(k, rc, ra, rb)
    v = proj(C_BV, LANES)
    sink = vec(V_SINK, LANES)
    yield
    k_prev, v_prev = kst_ref[...], vst_ref[...]
    for j in range(tm // BLK):
        rs = slice(j * BLK, (j + 1) * BLK)
        first = (t == 0) if j == 0 else False
        yb = _swa_block(q[rs], k[rs], v[rs], k_prev, v_prev, sink, first)
        y_ref[rs, GROUP_W:2 * GROUP_W] = yb.astype(y_ref.dtype)
        k_prev, v_prev = k[rs], v[rs]
        yield
    kst_ref[...] = k_prev
    vst_ref[...] = v_prev


def _group_c(c):
    proj, vec, tm, layer, ones_bd, hm = c.proj, c.vec, c.tm, c.layer, c.ones_bd, c.hm
    gamma_ref, sst_ref, msmall_ref, mbig_ref, y_ref = c.gamma_ref, c.sst_ref, c.msmall_ref, c.mbig_ref, c.y_ref
    lb = _lb_from_gamma([gamma_ref[i:i + 1, :] for i in range(gamma_ref.shape[0])], layer)
    cq, cf, ci, cg = (proj(C_CQ, GROUP_W), proj(C_CF, GROUP_W), proj(C_CI, GROUP_W), proj(C_CG, GROUP_W))
    hg = vec(V_HG, GROUP_W)
    sub = _iota((BLK, GROUP_W), 0) & (SUBLANES - 1)
    for j in range(tm // BLK):
        rs = slice(j * BLK, (j + 1) * BLK)
        yc = yield from _hgrn_block(cq[rs], cf[rs], ci[rs], cg[rs], lb, hg, sst_ref, ones_bd, msmall_ref,
                                    mbig_ref, hm, sub)
        y_ref[rs, 2 * GROUP_W:3 * GROUP_W] = yc.astype(y_ref.dtype)
        yield


def _group_d(c):
    proj, vec, tm, layer, ones_bd, hm, hmf = c.proj, c.vec, c.tm, c.layer, c.ones_bd, c.hm, c.hmf
    gb_ref, triu_ref, mst_ref, cst_ref, y_ref = c.gb_ref, c.triu_ref, c.mst_ref, c.cst_ref, c.y_ref
    dq, dk, dv, do = (proj(C_DQ, GROUP_W), proj(C_DK, GROUP_W), proj(C_DV, GROUP_W), proj(C_DO, GROUP_W))
    mg = vec(V_MG, GROUP_W)
    m_prev = mst_ref[...]
    g_rows, g_c, m_c, eneg_c, m_new = _mlstm_tile_gates(proj(C_GATES, LANES), gb_ref[layer], m_prev,
                                                       triu_ref[...])
    mst_ref[...] = m_new
    yield
    for j in range(tm // BLK):
        rs = slice(j * BLK, (j + 1) * BLK)
        yd = yield from _mlstm_block(dq[rs], dk[rs], dv[rs], do[rs], g_rows[:, rs], g_c[rs], m_c[rs], eneg_c[rs],
                          m_prev, mg, cst_ref, ones_bd, hm, hmf)
        y_ref[rs, 3 * GROUP_W:4 * GROUP_W] = yd.astype(y_ref.dtype)
        m_prev = m_c[(j + 1) * BLK - 1:(j + 1) * BLK]
        yield


def _full(shape):
    nd = len(shape)
    return pl.BlockSpec(shape, lambda *_: (0,) * nd)


def _layer_block(shape, layer, single_buffer=False):
    nd = len(shape) - 1
    kw = {'pipeline_mode': pl.Buffered(1)} if single_buffer else {}
    return pl.BlockSpec((None,) + tuple(shape[1:]), lambda *_: (layer,) + (0,) * nd, **kw)


def _prompt_mixers(h, cw, layer, tm, prev):
    bsz, t, _ = h.shape
    nt = t // tm
    kern = functools.partial(_mixer_kernel, layer=layer, tm=tm)
    in_specs = [pl.BlockSpec((bsz, tm, D_MODEL), lambda i: (0, i, 0)),
                pl.BlockSpec((bsz, tm, D_MODEL), lambda i: (0, jnp.minimum(i + 1, nt - 1), 0)),
                _full(cw['vecs'].shape), _layer_block(cw['w_in_p'].shape, layer, single_buffer=True),
                _full(cw['conv_w'].shape), _layer_block(cw['wr_bd'].shape, layer),
                _layer_block(cw['wi_bd'].shape, layer), _full(cw['rope_r'].shape),
                pl.BlockSpec((None, 1, 2 * LANES), lambda i: (i, 0, 0)),
                _full(cw['gamma'].shape), _full(cw['ones_bd'].shape), _full(cw['lvl_small'].shape),
                _full(cw['lvl_big'].shape), _full(cw['gate_bias'].shape), _full(cw['triu'].shape)]
    depth = cw['vecs'].shape[0]
    st_shapes = [(depth, bsz) + s for s in ((1, GROUP_W), (8, GROUP_W), (BLK, LANES), (BLK, LANES),
                                            (2, LANES, LANES), (N_HEADS, LANES, LANES), (1, LANES))]
    out_shape = ([jax.ShapeDtypeStruct((bsz, t, D_MODEL), BF16)]
                 + [jax.ShapeDtypeStruct(s, F32) for s in st_shapes])
    st_specs = [_full(s) if prev is None else _layer_block(s, layer) for s in st_shapes]
    chain = [] if prev is None else list(prev)
    n_in = len(in_specs)
    return pl.pallas_call(
        kern, grid=(nt,), in_specs=in_specs + [pl.BlockSpec(memory_space=pl.ANY)] * len(chain),
        out_specs=[pl.BlockSpec((bsz, tm, D_MODEL), lambda i: (0, i, 0))] + st_specs, out_shape=out_shape,
        input_output_aliases={n_in + i: 1 + i for i in range(len(chain))},
        scratch_shapes=[pltpu.VMEM((bsz * tm, N_IN), F32), pltpu.VMEM((bsz * tm, D_MODEL), BF16)],
        compiler_params=pltpu.CompilerParams(dimension_semantics=("arbitrary",),
                                             vmem_limit_bytes=VMEM_LIMIT),
        name=f"prompt_mixers_l{layer}",
    )(h, h, cw['vecs'], cw['w_in_p'], cw['conv_w'], cw['wr_bd'], cw['wi_bd'], cw['rope_r'], cw['rope_base'],
      cw['gamma'], cw['ones_bd'], cw['lvl_small'], cw['lvl_big'], cw['gate_bias'], cw['triu'], *chain)


def _ffn_math(h, yb, p, g2, wout_ref, wup_ref, wdn_ref, wg_ref, wp_ref):
    h = h + _dot(yb, wout_ref[...])
    nb = _rms_rows(h, g2).astype(BF16)
    acc = h
    step = 1024
    for c in range(0, D_FF, step):
        f = jnp.maximum(_dot(nb, wup_ref[:, c:c + step]), 0.0)
        acc = acc + _dot((f * f).astype(BF16), wdn_ref[c:c + step, :])
    gate = _sigmoid(_dot(acc.astype(BF16), wg_ref[...]))
    return acc + gate * _dot(p.astype(BF16), wp_ref[...])


def _ffn_kernel(h_ref, y_ref, p_ref, pv_ref, wout_ref, wup_ref, wdn_ref, wg_ref, wp_ref, o_ref, *, layer):
    g2 = pv_ref[layer:layer + 1, V_G2:V_G2 + D_MODEL]
    o_ref[...] = _ffn_math(h_ref[...], y_ref[...], p_ref[...], g2, wout_ref, wup_ref, wdn_ref, wg_ref, wp_ref)


def _ffn_weight_specs(cw, layer):
    names = ['w_out', 'w_up', 'w_down', 'w_gate', 'w_proj']
    return [cw[n] for n in names], [_layer_block(cw[n].shape, layer, single_buffer=True) for n in names]


def _prompt_ffn(h2, y2, p3, cw, layer, tm):
    n = h2.shape[0]
    row = lambda w: pl.BlockSpec((tm, w), lambda i: (i, 0))
    ws, wspecs = _ffn_weight_specs(cw, layer)
    return pl.pallas_call(
        functools.partial(_ffn_kernel, layer=layer), grid=(n // tm,),
        in_specs=[row(D_MODEL), row(D_MODEL), pl.BlockSpec((None, tm, PLE_DIM), lambda i: (layer, i, 0)),
                  _full(cw['vecs'].shape)] + wspecs,
        out_specs=row(D_MODEL), out_shape=jax.ShapeDtypeStruct((n, D_MODEL), F32),
        compiler_params=pltpu.CompilerParams(dimension_semantics=("arbitrary",),
                                             vmem_limit_bytes=VMEM_LIMIT),
        name=f"prompt_ffn_l{layer}",
    )(h2, y2, p3, cw['vecs'], *ws)


def _sample_pre_kernel(x_ref, pv_ref, wt_ref, convw_ref, wr_ref, wi_ref, ones_ref, rope_ref, gcol_ref,
                       cols_ref, h0_ref, conv0_ref, n0_ref, m0_ref,
                       ya_ref, hn_ref, convn_ref, q_ref, kt_ref, vt_ref, fm_ref, nn_ref, mn_ref, *, layer):
    def vec(off, w):
        return pv_ref[layer:layer + 1, off:off + w]

    n = _rms_rows(x_ref[...], vec(V_G1, D_MODEL))
    nb = n.astype(BF16)
    n_t = n.T.astype(BF16)
    ones_bd = ones_ref[...]

    def proj(r0, cnt):
        return _dot_nt(nb, wt_ref[r0:r0 + cnt, :])

    def proj_t(r0, cnt):
        return _dot(wt_ref[r0:r0 + cnt, :], n_t)

    xa = proj(C_AX, GROUP_W)
    ga = proj(C_AG, GROUP_W)
    cw = convw_ref[layer]
    xc = (vec(V_CB, GROUP_W) + conv0_ref[0] * cw[0:1, :] + conv0_ref[1] * cw[1:2, :]
          + conv0_ref[2] * cw[2:3, :] + xa * cw[3:4, :])
    convn_ref[0] = conv0_ref[1]
    convn_ref[1] = conv0_ref[2]
    convn_ref[2] = xa
    a, bx = _rglru_gates(xc, wr_ref[...], wi_ref[...], vec(V_BR, GROUP_W), vec(V_BI, GROUP_W),
                         vec(V_LAM, GROUP_W))
    hn = a * h0_ref[...] + bx
    hn_ref[...] = hn
    ya_ref[...] = hn * _gelu_tanh(ga)

    rc, ra, rb = rope_ref[0:1, :], rope_ref[1:2, :], rope_ref[2:3, :]
    q = _seg_rms(proj(C_BQ, GROUP_W), vec(V_QG, GROUP_W), ones_bd)
    q_ref[...] = _rope(q, jnp.concatenate([rc, rc], 1), jnp.concatenate([ra, ra], 1),
                       jnp.concatenate([rb, rb], 1))
    k = _seg_rms(proj(C_BK, LANES), vec(V_KG, LANES), ones_bd[0:LANES, 0:LANES])
    kt_ref[...] = _rope(k, rc, ra, rb).T
    vt_ref[...] = proj(C_BV, LANES).T

    lb = _lb_from_gamma([gcol_ref[i] for i in range(gcol_ref.shape[0])], layer)
    cq, cf, ci, cg = (proj_t(C_CQ, GROUP_W), proj_t(C_CF, GROUP_W), proj_t(C_CI, GROUP_W),
                      proj_t(C_CG, GROUP_W))
    f = lb + (1.0 - lb) * _sigmoid(cf)
    fm_ref[FM_HF:FM_HF + GROUP_W, :] = f
    fm_ref[FM_HK:FM_HK + GROUP_W, :] = 1.0 - f
    fm_ref[FM_HQ:FM_HQ + GROUP_W, :] = cq * _sigmoid(cq)
    fm_ref[FM_HV:FM_HV + GROUP_W, :] = ci
    fm_ref[FM_CG:FM_CG + GROUP_W, :] = cg * _sigmoid(cg)

    dq, dk, dv, do = (proj_t(C_DQ, GROUP_W), proj_t(C_DK, GROUP_W), proj_t(C_DV, GROUP_W),
                      proj_t(C_DO, GROUP_W))
    g8 = proj_t(C_GATES, 2 * N_HEADS) + cols_ref[layer, R_GB:R_GB + 2 * N_HEADS, :]
    ig = g8[0:N_HEADS, :]
    lf = _log_sigmoid(g8)[N_HEADS:2 * N_HEADS, :]
    a_int = lf + m0_ref[...]
    m_new = jnp.maximum(a_int, ig)
    dec = jnp.exp(a_int - m_new)
    w = jnp.exp(ig - m_new)
    mn_ref[...] = m_new
    km = dk * (HEAD_DIM ** -0.5)
    dens = []
    for h in range(N_HEADS):
        sl = slice(h * HEAD_DIM, (h + 1) * HEAD_DIM)
        nn_h = dec[h:h + 1, :] * n0_ref[sl, :] + w[h:h + 1, :] * km[sl, :]
        nn_ref[sl, :] = nn_h
        dens.append(jnp.sum(dq[sl, :] * nn_h, axis=0, keepdims=True))
    pad = jnp.zeros((SUBLANES - N_HEADS, x_ref.shape[0]), F32)
    fm_ref[FM_MK:FM_MK + GROUP_W, :] = km
    fm_ref[FM_MQ:FM_MQ + GROUP_W, :] = dq
    fm_ref[FM_MV:FM_MV + GROUP_W, :] = dv
    fm_ref[FM_DO:FM_DO + GROUP_W, :] = _sigmoid(do)
    fm_ref[FM_DEC:FM_DEC + SUBLANES, :] = jnp.concatenate([dec, pad], 0)
    fm_ref[FM_W:FM_W + SUBLANES, :] = jnp.concatenate([w, pad], 0)
    fm_ref[FM_DEN:FM_DEN + SUBLANES, :] = jnp.concatenate(dens + [pad], 0)
    fm_ref[FM_ENEG:FM_ENEG + SUBLANES, :] = jnp.concatenate([jnp.exp(-m_new), pad], 0)


def _own_slab(ref, first_layer):
    if not first_layer:
        return ref
    ref[1:] = jnp.zeros((ref.shape[0] - 1,) + ref.shape[1:], ref.dtype)
    return ref.at[0]


def _sample_attn_kernel(q_ref, kn_ref, vn_ref, kc_ref, vc_ref, sink_ref, *rest, first_layer):
    ko_ref, vo_ref, o_ref = rest[-3:]
    ko_ref, vo_ref = _own_slab(ko_ref, first_layer), _own_slab(vo_ref, first_layer)
    sb = q_ref.shape[0]
    rows = 2 * HEAD_DIM
    lane = _iota((rows, WINDOW), 1)
    kn = kn_ref[...].reshape(rows, sb)
    vn = vn_ref[...].reshape(rows, sb)
    for s in range(sb):
        kt = pltpu.roll(kc_ref[s].reshape(rows, WINDOW), WINDOW - 1, 1)
        ko_ref[s] = jnp.where(lane == WINDOW - 1, kn[:, s:s + 1], kt).reshape(2, HEAD_DIM, WINDOW)
        vt = pltpu.roll(vc_ref[s].reshape(rows, WINDOW), WINDOW - 1, 1)
        vo_ref[s] = jnp.where(lane == WINDOW - 1, vn[:, s:s + 1], vt).reshape(2, HEAD_DIM, WINDOW)
    for kv in range(2):
        kk = ko_ref[:, kv].astype(BF16)
        vv = vo_ref[:, kv].astype(BF16)
        s_ = jnp.einsum('bqc,bcj->bqj', q_ref[:, kv].astype(BF16), kk,
                        preferred_element_type=F32) * (HEAD_DIM ** -0.5)
        sk = sink_ref[kv]
        mx = jnp.maximum(jnp.max(s_, axis=-1, keepdims=True), sk)
        p = jnp.exp(s_ - mx)
        den = jnp.sum(p, axis=-1, keepdims=True) + jnp.exp(sk - mx)
        o = jnp.einsum('bqj,bcj->bqc', p.astype(BF16), vv, preferred_element_type=F32)
        o_ref[:, kv] = o / den


def _sample_state_kernel(fm_ref, s_ref, c_ref, *rest, first_layer):
    so_ref, co_ref, oh_ref, om_ref = rest[-4:]
    so_ref, co_ref = _own_slab(so_ref, first_layer), _own_slab(co_ref, first_layer)
    h = pl.program_id(0)
    r0 = pl.multiple_of(h * HEAD_DIM, HEAD_DIM)
    hv = fm_ref[pl.ds(FM_HV + r0, HEAD_DIM), :]
    mv = fm_ref[pl.ds(FM_MV + r0, HEAD_DIM), :]
    dec = fm_ref[pl.ds(FM_DEC + h, 1), :]
    w = fm_ref[pl.ds(FM_W + h, 1), :]

    def body(d, carry):
        acc_h, acc_m = carry
        r = r0 + d
        s_new = fm_ref[pl.ds(FM_HF + r, 1), :] * s_ref[d] + fm_ref[pl.ds(FM_HK + r, 1), :] * hv
        so_ref[d] = s_new
        c_new = dec * c_ref[d] + (w * fm_ref[pl.ds(FM_MK + r, 1), :]) * mv
        co_ref[d] = c_new
        return (acc_h + fm_ref[pl.ds(FM_HQ + r, 1), :] * s_new,
                acc_m + fm_ref[pl.ds(FM_MQ + r, 1), :] * c_new)

    zero = jnp.zeros((HEAD_DIM, fm_ref.shape[1]), F32)
    acc_h, acc_m = lax.fori_loop(0, HEAD_DIM, body, (zero, zero), unroll=4)
    oh_ref[...] = acc_h
    om_ref[...] = acc_m


def _sample_post_kernel(h_ref, ya_ref, yb_ref, oh_ref, om_ref, fm_ref, cols_ref, p_ref, pv_ref,
                        wout_ref, wup_ref, wdn_ref, wg_ref, wp_ref, o_ref, acc_ref, nb_ref, *, layer):
    k = pl.program_id(0)

    @pl.when(k == 0)
    def _head():
        def head_rms(x):
            return x * lax.rsqrt(jnp.mean(x * x, axis=0, keepdims=True) + EPS)

        yc, yd = [], []
        for h in range(N_HEADS):
            yc.append(head_rms(oh_ref[h]))
            den = fm_ref[FM_DEN + h:FM_DEN + h + 1, :]
            eneg = fm_ref[FM_ENEG + h:FM_ENEG + h + 1, :]
            yd.append(head_rms(om_ref[h] / jnp.maximum(jnp.abs(den), eneg)))
        yc_ = (jnp.concatenate(yc, 0) * cols_ref[layer, R_HG:R_HG + GROUP_W, :]
               * fm_ref[FM_CG:FM_CG + GROUP_W, :])
        yd_ = (jnp.concatenate(yd, 0) * cols_ref[layer, R_MG:R_MG + GROUP_W, :]
               * fm_ref[FM_DO:FM_DO + GROUP_W, :])
        y = jnp.concatenate([ya_ref[...], yb_ref[...], yc_.T, yd_.T], axis=1).astype(BF16)
        h1 = h_ref[...] + _dot(y, wout_ref[...])
        acc_ref[...] = h1
        nb_ref[...] = _rms_rows(h1, pv_ref[layer:layer + 1, V_G2:V_G2 + D_MODEL]).astype(BF16)

    f = jnp.maximum(_dot(nb_ref[...], wup_ref[...]), 0.0)
    acc_ref[...] += _dot((f * f).astype(BF16), wdn_ref[...])

    @pl.when(k == pl.num_programs(0) - 1)
    def _tail():
        acc = acc_ref[...]
        gate = _sigmoid(_dot(acc.astype(BF16), wg_ref[...]))
        o_ref[...] = acc + gate * _dot(p_ref[...].astype(BF16), wp_ref[...])


def _call_full(kern, args, specs, out_shape, name):
    specs = [(_full(a.shape) if s is None else s) for a, s in zip(args, specs)]
    return pl.pallas_call(
        kern, grid=(1,), in_specs=specs,
        out_specs=tuple(_full(s.shape) for s in out_shape), out_shape=tuple(out_shape),
        compiler_params=pltpu.CompilerParams(dimension_semantics=("arbitrary",),
                                             vmem_limit_bytes=VMEM_LIMIT),
        name=name,
    )(*args)


def _sample_layer(h, sv, prev, cw, layer):
    nsm = h.shape[0]
    depth = cw['vecs'].shape[0]
    sd = lambda *shape: jax.ShapeDtypeStruct(shape, F32)
    g, l = sd(nsm, GROUP_W), sd(LANES, nsm)

    args = [h, cw['vecs'], cw['w_in_p'], cw['conv_w'], cw['wr_bd'], cw['wi_bd'], cw['ones_bd'],
            cw['rope_s'], cw['gamma_col'], cw['cols'], sv['h'], sv['conv'], sv['n'], sv['m']]
    specs = [None, None, _layer_block(cw['w_in_p'].shape, layer, single_buffer=True), None,
             _layer_block(cw['wr_bd'].shape, layer), _layer_block(cw['wi_bd'].shape, layer), None,
             None, None, None, _layer_block(sv['h'].shape, layer), _layer_block(sv['conv'].shape, layer),
             _layer_block(sv['n'].shape, layer), _layer_block(sv['m'].shape, layer)]
    outs = [g, g, sd(CONV_W - 1, nsm, GROUP_W), g, l, l, sd(FM_ROWS, nsm), sd(GROUP_W, nsm),
            sd(N_HEADS, nsm)]
    ya, hn, convn, q, kt, vt, fm, nn, mn = _call_full(
        functools.partial(_sample_pre_kernel, layer=layer), args, specs, outs, f"sample_pre_l{layer}")

    sb = 16
    nblk = nsm // sb
    q3 = jnp.pad(q.reshape(nsm, 2, 2, HEAD_DIM), ((0, 0), (0, 0), (0, SUBLANES - 2), (0, 0)))
    to_blocks = lambda a: a.reshape(2, HEAD_DIM, nblk, sb).transpose(2, 0, 1, 3)
    cshape = (depth, nsm, 2, HEAD_DIM, WINDOW)
    cspec = pl.BlockSpec((None, sb, 2, HEAD_DIM, WINDOW), lambda i: (layer, i, 0, 0, 0))
    nspec = pl.BlockSpec((None, 2, HEAD_DIM, sb), lambda i: (i, 0, 0, 0))
    qspec = pl.BlockSpec((sb, 2, SUBLANES, HEAD_DIM), lambda i: (i, 0, 0, 0))
    any_spec = pl.BlockSpec(memory_space=pl.ANY)
    cout = (pl.BlockSpec((depth, sb, 2, HEAD_DIM, WINDOW), lambda i: (0, i, 0, 0, 0)) if prev is None
            else cspec)
    chain = [] if prev is None else [prev['k'], prev['v']]
    ko, vo, o3 = pl.pallas_call(
        functools.partial(_sample_attn_kernel, first_layer=prev is None), grid=(nblk,),
        in_specs=[qspec, nspec, nspec, cspec, cspec, _layer_block(cw['sinks8'].shape, layer)]
        + [any_spec] * len(chain),
        out_specs=(cout, cout, qspec),
        out_shape=(sd(*cshape), sd(*cshape), sd(nsm, 2, SUBLANES, HEAD_DIM)),
        input_output_aliases={6 + i: i for i in range(len(chain))},
        compiler_params=pltpu.CompilerParams(dimension_semantics=("arbitrary",),
                                             vmem_limit_bytes=VMEM_LIMIT),
        name=f"sample_attn_l{layer}",
    )(q3, to_blocks(kt), to_blocks(vt), sv['k'], sv['v'], cw['sinks8'], *chain)
    yb = o3[:, :, 0:2, :].reshape(nsm, GROUP_W)

    sshape = (depth, N_HEADS, HEAD_DIM, HEAD_DIM, nsm)
    sspec = pl.BlockSpec((None, None, HEAD_DIM, HEAD_DIM, nsm), lambda i: (layer, i, 0, 0, 0))
    ospec = pl.BlockSpec((None, HEAD_DIM, nsm), lambda i: (i, 0, 0))
    sout = (pl.BlockSpec((depth, None, HEAD_DIM, HEAD_DIM, nsm), lambda i: (0, i, 0, 0, 0))
            if prev is None else sspec)
    chain = [] if prev is None else [prev['s'], prev['c']]
    so, co, oh, om = pl.pallas_call(
        functools.partial(_sample_state_kernel, first_layer=prev is None), grid=(N_HEADS,),
        in_specs=[_full(fm.shape), sspec, sspec] + [any_spec] * len(chain),
        out_specs=(sout, sout, ospec, ospec),
        out_shape=(sd(*sshape), sd(*sshape), sd(N_HEADS, HEAD_DIM, nsm), sd(N_HEADS, HEAD_DIM, nsm)),
        input_output_aliases={3 + i: i for i in range(len(chain))},
        compiler_params=pltpu.CompilerParams(dimension_semantics=("arbitrary",),
                                             vmem_limit_bytes=VMEM_LIMIT),
        name=f"sample_state_l{layer}",
    )(fm, sv['s'], sv['c'], *chain)

    ws, wspecs = _ffn_weight_specs(cw, layer)
    chunk = 1024
    wspecs[1] = pl.BlockSpec((None, D_MODEL, chunk), lambda k: (layer, 0, k))
    wspecs[2] = pl.BlockSpec((None, chunk, D_MODEL), lambda k: (layer, k, 0))
    post_args = [h, ya, yb, oh, om, fm, cw['cols'], sv['p'], cw['vecs']] + ws
    post_specs = ([_full(a.shape) for a in post_args[:7]] + [_layer_block(sv['p'].shape, layer),
                                                             _full(cw['vecs'].shape)] + wspecs)
    h_new = pl.pallas_call(
        functools.partial(_sample_post_kernel, layer=layer), grid=(D_FF // chunk,),
        in_specs=post_specs, out_specs=_full((nsm, D_MODEL)), out_shape=sd(nsm, D_MODEL),
        scratch_shapes=[pltpu.VMEM((nsm, D_MODEL), F32), pltpu.VMEM((nsm, D_MODEL), BF16)],
        compiler_params=pltpu.CompilerParams(dimension_semantics=("arbitrary",),
                                             vmem_limit_bytes=VMEM_LIMIT),
        name=f"sample_post_l{layer}",
    )(*post_args)
    small = (hn, convn, nn, mn)
    big = {'k': ko, 'v': vo, 's': so, 'c': co}
    return h_new, small, big


def _block_diag_all(w):
    depth = w.shape[0]
    rows = w.reshape(depth, GROUP_W, HEAD_DIM)
    idx = np.arange(GROUP_W) // HEAD_DIM
    mask = idx[:, None] == idx[None, :]
    return jnp.where(mask[None], jnp.tile(rows, (1, 1, N_HEADS)), 0.0)


def _rope_lane_freq():
    half = ROT_DIM // 2
    inv = np.power(np.float32(ROPE_THETA), -np.arange(half, dtype=np.float32) * np.float32(2.0 / ROT_DIM))
    dd = np.arange(LANES) % HEAD_DIM
    freq = np.where(dd < ROT_DIM, inv[dd % half], np.float32(0.0))
    m_a = (dd < half).astype(np.float32)
    m_b = ((dd >= half) & (dd < ROT_DIM)).astype(np.float32)
    return freq.astype(np.float32), m_a, m_b


def _rope_tables(pos):
    freq, m_a, m_b = _rope_lane_freq()
    ang = np.asarray(pos, np.float32)[:, None] * freq[None, :]
    cos, sin = np.cos(ang), np.sin(ang)
    return cos, -sin * m_a, sin * m_b


def _rope_split_tables(t, tm):
    freq, m_a, m_b = _rope_lane_freq()
    ang_r = np.arange(tm, dtype=np.float32)[:, None] * freq[None, :]
    cr, sr = np.cos(ang_r), np.sin(ang_r)
    rope_r = np.stack([cr, sr, -cr * m_a, -sr * m_a, cr * m_b, sr * m_b])
    ang_b = (np.arange(t // tm) * tm).astype(np.float32)[:, None] * freq[None, :]
    rope_base = np.concatenate([np.cos(ang_b), np.sin(ang_b)], 1)[:, None, :]
    return rope_r, rope_base


def _hgrn_level_masks():
    t = np.arange(BLK)[:, None]
    s = np.arange(BLK)[None, :]
    small = [t == s]
    big = []
    for lev in range(1, 8):
        half = 1 << (lev - 1)
        own = ((t >> lev) == (s >> lev)) & ((t & half) != 0) & ((s & half) == 0)
        if lev < 4:
            small.append(own)
        else:
            rows = np.concatenate([np.arange(m, m + half) for m in range(half, BLK, 2 * half)])
            big.append(own[rows])
    return np.stack(small).astype(np.float32), np.stack(big).astype(np.float32)


def _pad_last(v, width):
    return jnp.pad(v, ((0, 0), (0, width - v.shape[-1])))


def _common(w, t, tm_mix, past_len, nsm):
    depth = w['w_in'].shape[0]
    tile = lambda v, n: jnp.tile(v, (1, n))
    vecs = jnp.concatenate([
        w['norm1_g'], w['norm2_g'], w['conv_b'], w['lru_br'], w['lru_bi'], w['lru_lam'],
        tile(w['q_norm_g'], N_HEADS), tile(w['k_norm_g'], 2), _pad_last(w['attn_sinks'], LANES),
        tile(w['hgrn_norm_g'], N_HEADS), tile(w['mlstm_norm_g'], N_HEADS),
        _pad_last(w['mlstm_ib'], LANES), _pad_last(w['mlstm_fb'], LANES)], axis=1)
    cols = jnp.concatenate([tile(w['hgrn_norm_g'], N_HEADS), tile(w['mlstm_norm_g'], N_HEADS),
                            w['mlstm_ib'], w['mlstm_fb']], axis=1)
    w_in = w['w_in']
    gate_bias = jnp.concatenate([w['mlstm_ib'], w['mlstm_fb']], axis=1)
    lvl_small, lvl_big = _hgrn_level_masks()
    sinks = w['attn_sinks']
    z2 = jnp.zeros((depth, 2, SUBLANES - 2), F32)
    idx = np.arange(GROUP_W)
    rope_r, rope_base = _rope_split_tables(t, tm_mix)
    return {
        'vecs': vecs,
        'cols': jnp.broadcast_to(cols[:, :, None], cols.shape + (nsm,)),
        'gamma': w['hgrn_gamma'],
        'gamma_col': jnp.broadcast_to(w['hgrn_gamma'][:, :, None], w['hgrn_gamma'].shape + (nsm,)),
        'w_in_p': jnp.pad(jnp.swapaxes(w_in, 1, 2), ((0, 0), (0, N_IN - D_IN), (0, 0))).astype(BF16),
        'lvl_small': lvl_small, 'lvl_big': lvl_big,
        'gate_bias': jnp.broadcast_to(gate_bias[:, :, None], gate_bias.shape + (tm_mix,)),
        'triu': (np.arange(tm_mix)[:, None] <= np.arange(tm_mix)[None, :]).astype(np.float32),
        'conv_w': w['conv_w'],
        'wr_bd': _block_diag_all(w['lru_wr']).astype(BF16),
        'wi_bd': _block_diag_all(w['lru_wi']).astype(BF16),
        'ones_bd': jnp.asarray(idx[:, None] // HEAD_DIM == idx[None, :] // HEAD_DIM, BF16),
        'sinks8': jnp.concatenate([sinks.reshape(depth, 2, 2), z2], axis=2)[..., None],
        'rope_r': rope_r, 'rope_base': rope_base,
        'rope_s': np.concatenate(_rope_tables(past_len + np.arange(1)), axis=0),
        'w_out': w['w_out'].astype(BF16), 'w_up': w['w_up'].astype(BF16),
        'w_down': w['w_down'].astype(BF16), 'w_gate': w['w_ple_gate'].astype(BF16),
        'w_proj': w['w_ple_proj'].astype(BF16),
    }


def _run(x_prompt, x_sample, p_prompt, p_sample, sample_state, w, past_len, tm_mix=256, tm_ffn=512):
    depth = w['w_in'].shape[0]
    bsz, t, _ = x_prompt.shape
    nsm = x_sample.shape[0]
    cw = _common(w, t, tm_mix, past_len, nsm)
    tm_ffn = min(tm_ffn, bsz * t)
    h0, conv0, kc, vc, s0, c0, n0, m0 = sample_state
    sv = {'h': h0, 'conv': jnp.transpose(conv0, (0, 2, 1, 3)),
          'k': jnp.transpose(kc, (0, 1, 3, 4, 2)), 'v': jnp.transpose(vc, (0, 1, 3, 4, 2)),
          's': jnp.transpose(s0, (0, 2, 3, 4, 1)), 'c': jnp.transpose(c0, (0, 2, 3, 4, 1)),
          'n': jnp.transpose(n0, (0, 2, 3, 1)).reshape(depth, GROUP_W, nsm),
          'm': jnp.transpose(m0, (0, 2, 1)), 'p': p_sample.reshape(depth, nsm, PLE_DIM)}
    p3 = p_prompt.reshape(depth, bsz * t, PLE_DIM)

    hp = x_prompt
    hs = x_sample.reshape(nsm, D_MODEL)
    pst, s_small, big = None, [], None
    for l in range(depth):
        y, *pst = _prompt_mixers(hp, cw, l, tm_mix, pst)
        hp = _prompt_ffn(hp.reshape(bsz * t, D_MODEL), y.reshape(bsz * t, D_MODEL), p3, cw, l,
                         tm_ffn).reshape(bsz, t, D_MODEL)
        hs, small, big = _sample_layer(hs, sv, big, cw, l)
        s_small.append(small)
    stack = lambda sts, i: jnp.stack([s[i] for s in sts])
    hl, conv, kst, vst, sst, cst, mst = pst
    s_t = jnp.swapaxes(sst, -1, -2).reshape(depth, bsz, 2, 2, HEAD_DIM, 2, HEAD_DIM)
    s_hgrn = jnp.stack([s_t[:, :, :, 0, :, 0, :], s_t[:, :, :, 1, :, 1, :]], axis=3)
    c_t = cst.reshape(depth, bsz, 2, 2, 2, HEAD_DIM, LANES)
    c_rows = jnp.stack([c_t[:, :, :, 0, 0], c_t[:, :, :, 1, 1]], axis=3)
    c_rows = c_rows.reshape(depth, bsz, N_HEADS, HEAD_DIM, LANES)
    prompt_out = (hl[:, :, 0], conv[:, :, 8 - (CONV_W - 1):], kst.reshape(depth, bsz, WINDOW, 2, HEAD_DIM),
                  vst.reshape(depth, bsz, WINDOW, 2, HEAD_DIM),
                  s_hgrn.reshape(depth, bsz, N_HEADS, HEAD_DIM, HEAD_DIM), c_rows[..., 0:HEAD_DIM],
                  c_rows[..., HEAD_DIM], mst[:, :, 0, 0:N_HEADS])
    hn, convn, nn, mn = (stack(s_small, i) for i in range(4))
    sample_out = (hn, jnp.transpose(convn, (0, 2, 1, 3)),
                  jnp.transpose(big['k'], (0, 1, 4, 2, 3)), jnp.transpose(big['v'], (0, 1, 4, 2, 3)),
                  jnp.transpose(big['s'], (0, 4, 1, 2, 3)), jnp.transpose(big['c'], (0, 4, 1, 2, 3)),
                  jnp.transpose(nn.reshape(depth, N_HEADS, HEAD_DIM, nsm), (0, 3, 1, 2)),
                  jnp.transpose(mn, (0, 2, 1)))
    return (hp, hs.reshape(x_sample.shape)) + prompt_out + sample_out


def kernel(x_prompt, x_sample, p_prompt, p_sample, state_rglru_h, state_rglru_conv, cache_swa_k, cache_swa_v, state_hgrn_s, state_mlstm_c, state_mlstm_n, state_mlstm_m, norm1_g, w_in, conv_w, conv_b, lru_wr, lru_br, lru_wi, lru_bi, lru_lam, q_norm_g, k_norm_g, attn_sinks, hgrn_gamma, hgrn_norm_g, mlstm_ib, mlstm_fb, mlstm_norm_g, w_out, norm2_g, w_up, w_down, w_ple_gate, w_ple_proj):
    w = {'norm1_g': norm1_g, 'w_in': w_in, 'conv_w': conv_w, 'conv_b': conv_b, 'lru_wr': lru_wr,
         'lru_br': lru_br, 'lru_wi': lru_wi, 'lru_bi': lru_bi, 'lru_lam': lru_lam, 'q_norm_g': q_norm_g,
         'k_norm_g': k_norm_g, 'attn_sinks': attn_sinks, 'hgrn_gamma': hgrn_gamma,
         'hgrn_norm_g': hgrn_norm_g, 'mlstm_ib': mlstm_ib, 'mlstm_fb': mlstm_fb,
         'mlstm_norm_g': mlstm_norm_g, 'w_out': w_out, 'norm2_g': norm2_g, 'w_up': w_up,
         'w_down': w_down, 'w_ple_gate': w_ple_gate, 'w_ple_proj': w_ple_proj}
    st = (state_rglru_h, state_rglru_conv, cache_swa_k, cache_swa_v, state_hgrn_s, state_mlstm_c,
          state_mlstm_n, state_mlstm_m)
    past_len = 8192
    return _run(x_prompt, x_sample, p_prompt, p_sample, st, w, past_len)
```

```python
import functools
import types

import jax
import jax.numpy as jnp
import numpy as np
from jax import lax
from jax.experimental import pallas as pl
from jax.experimental.pallas import tpu as pltpu

F32 = jnp.float32
BF16 = jnp.bfloat16

D_MODEL = 1024
GROUP_W = 256
HEAD_DIM = 64
N_HEADS = 4
EPS = 1e-6
NEG_BIG = -1e30
LRU_C = 8.0
CONV_W = 4
ROT_DIM = 16
ROPE_THETA = 500000.0
WINDOW = 128
D_FF = 4096
PLE_DIM = 256
LANES = 128
SUBLANES = 8
BLK = 128

C_AX, C_AG, C_BQ, C_BK, C_BV = 0, 256, 512, 768, 896
C_CQ, C_CF, C_CI, C_CG = 1024, 1280, 1536, 1792
C_DQ, C_DK, C_DV, C_DO = 2048, 2304, 2560, 2816
C_GATES = 3072
D_IN = 3080
N_IN = 3200

V_G1, V_G2, V_CB, V_BR, V_BI, V_LAM = 0, 1024, 2048, 2304, 2560, 2816
V_QG, V_KG, V_SINK, V_HG, V_MG, V_IB, V_FB = 3072, 3328, 3456, 3584, 3840, 4096, 4224
N_VEC = 4352

R_HG, R_MG, R_GB = 0, 256, 512
N_COL = 520

FM_HF, FM_HK, FM_HQ, FM_HV, FM_CG = 0, 256, 512, 768, 1024
FM_MK, FM_MQ, FM_MV, FM_DO = 1280, 1536, 1792, 2048
FM_DEC, FM_W, FM_DEN, FM_ENEG = 2304, 2312, 2320, 2328
FM_ROWS = 2336

VMEM_LIMIT = 56 * 1024 * 1024


def _dot(a, b):
    return jnp.dot(a, b, preferred_element_type=F32)


def _dot_nt(a, b):
    return lax.dot_general(a, b, (((1,), (1,)), ((), ())), preferred_element_type=F32)


def _sigmoid(x):
    return jax.nn.sigmoid(x)


def _gelu_tanh(x):
    return 0.5 * x * (1.0 + jnp.tanh(0.7978845608028654 * (x + 0.044715 * (x * x * x))))


def _log_sigmoid(x):
    return jnp.minimum(x, 0.0) - jnp.log1p(jnp.exp(-jnp.abs(x)))


def _softplus(x):
    return jnp.maximum(x, 0.0) + jnp.log1p(jnp.exp(-jnp.abs(x)))


def _rms_rows(x, g):
    return x * lax.rsqrt(jnp.mean(x * x, axis=-1, keepdims=True) + EPS) * g


def _seg_mean_sq(x, ones_bd):
    sq = x * x
    hi = sq.astype(BF16)
    lo = (sq - hi.astype(F32)).astype(BF16)
    return (_dot(hi, ones_bd) + _dot(lo, ones_bd)) * (1.0 / HEAD_DIM)


def _seg_rms(x, g, ones_bd):
    return x * lax.rsqrt(_seg_mean_sq(x, ones_bd) + EPS) * g


def _rope(x, c, sa, sb):
    w = x.shape[1]
    up = pltpu.roll(x, w - ROT_DIM // 2, 1)
    dn = pltpu.roll(x, ROT_DIM // 2, 1)
    return x * c + up * sa + dn * sb


def _lb_from_gamma(gammas, layer):
    mx = functools.reduce(jnp.maximum, gammas)
    e = [jnp.exp(g - mx) for g in gammas]
    tot = functools.reduce(lambda a, b: a + b, e)
    lb = jnp.zeros_like(tot)
    for i in range(1, layer + 1):
        lb = lb + e[i] / tot
    return lb


def _iota(shape, axis):
    return lax.broadcasted_iota(jnp.int32, shape, axis)


def _rglru_gates(xc, wr, wi, br, bi, lam):
    xcb = xc.astype(BF16)
    r = _sigmoid(_dot(xcb, wr) + br)
    ig = _sigmoid(_dot(xcb, wi) + bi)
    log_a = (-LRU_C) * r * _softplus(-lam)
    a = jnp.exp(log_a)
    y = 1.0 - a * a
    root = jnp.where(y > 0.0, y * lax.rsqrt(y), 0.0)
    return a, root * (ig * xc)


def _swa_block(q, k, v, k_prev, v_prev, sink, first):
    kk = jnp.concatenate([k_prev, k], axis=0)
    vv = jnp.concatenate([v_prev, v], axis=0)
    qi = _iota((BLK, 2 * BLK), 0)
    kj = _iota((BLK, 2 * BLK), 1)
    valid = (kj > qi) & (kj <= qi + WINDOW) & ((kj >= BLK) | jnp.logical_not(first))
    lane = _iota((2 * BLK, LANES), 1)
    one_col = jnp.where(lane == HEAD_DIM, 1.0, 0.0)
    vaug = (jnp.where(lane < HEAD_DIM, vv, one_col).astype(BF16),
            jnp.where(lane < HEAD_DIM, pltpu.roll(vv, HEAD_DIM, 1), one_col).astype(BF16))
    qs = q * (HEAD_DIM ** -0.5)
    outs = []
    for h in range(N_HEADS):
        kv = h // 2
        qh = qs[:, h * HEAD_DIM:(h + 1) * HEAD_DIM].astype(BF16)
        kh = kk[:, kv * HEAD_DIM:(kv + 1) * HEAD_DIM].astype(BF16)
        s = jnp.where(valid, _dot_nt(qh, kh), NEG_BIG)
        sk = sink[:, h:h + 1]
        mx = jnp.maximum(jnp.max(s, axis=-1, keepdims=True), sk)
        p = jnp.exp(s - mx)
        o = _dot(p.astype(BF16), vaug[kv])
        den = o[:, HEAD_DIM:HEAD_DIM + 1] + jnp.exp(sk - mx)
        outs.append(o[:, 0:HEAD_DIM] / den)
    return jnp.concatenate(outs, axis=1)


def _head_masks(rows, dtype):
    lane = _iota((rows, LANES), 1)
    return (jnp.where(lane < HEAD_DIM, 1.0, 0.0).astype(dtype),
            jnp.where(lane >= HEAD_DIM, 1.0, 0.0).astype(dtype))


def _pair_scores(qe, ke_b, hm):
    res = []
    for p in range(2):
        sl = slice(p * LANES, (p + 1) * LANES)
        kp = ke_b[:, sl]
        pr = _dot_nt(qe[:, sl].astype(BF16), jnp.concatenate([kp * hm[0], kp * hm[1]], axis=0))
        res += [pr[:, :BLK], pr[:, BLK:]]
    return res


def _hgrn_block(cq, cf, ci, cg, lb, hg, sst_ref, ones_bd, msmall_ref, mbig_ref, hm, sub):
    q = cq * _sigmoid(cq)
    f = lb + (1.0 - lb) * _sigmoid(cf)
    logf = jnp.log(f)
    k = 1.0 - f

    att = [s_ * msmall_ref[0] for s_ in _pair_scores(q, k.astype(BF16), hm)]

    c = logf
    tot = logf
    for lev in range(1, 4):
        half = 1 << (lev - 1)
        right = (sub & half) != 0
        tot_l = pltpu.roll(tot, half, 0)
        tot_r = pltpu.roll(tot, BLK - half, 0)
        e = jnp.exp(jnp.where(right, c, tot - c))
        sc = _pair_scores(q * e, (k * e).astype(BF16), hm)
        m = msmall_ref[lev]
        att = [a_ + s_ * m for a_, s_ in zip(att, sc)]
        c = c + jnp.where(right, tot_l, 0.0)
        tot = tot + jnp.where(right, tot_l, tot_r)

    pieces, carry = [], None
    for g in range(BLK // SUBLANES):
        rows = slice(g * SUBLANES, (g + 1) * SUBLANES)
        pieces.append(c[rows] if carry is None else c[rows] + carry)
        t_g = tot[g * SUBLANES:g * SUBLANES + 1]
        carry = t_g if carry is None else carry + t_g
    yield
    b = jnp.concatenate(pieces, axis=0)
    btot = carry

    for lev in range(4, 8):
        half = 1 << (lev - 1)
        nblk = BLK // (2 * half)
        qr, kf = [], []
        for i in range(nblk):
            lo = i * 2 * half
            mid = lo + half
            bref = b[mid - 1:mid]
            qr.append(q[mid:mid + half] * jnp.exp(b[mid:mid + half] - bref))
            kf.append(k[lo:mid] * jnp.exp(bref - b[lo:mid]))
            kf.append(jnp.zeros((half, GROUP_W), F32))
        sc = _pair_scores(jnp.concatenate(qr, axis=0), jnp.concatenate(kf, axis=0).astype(BF16), hm)
        m = mbig_ref[lev - 4]
        zero = jnp.zeros((half, BLK), F32)
        new = []
        for a_, s_ in zip(att, sc):
            u = s_ * m
            parts = []
            for i in range(nblk):
                parts += [zero, u[i * half:(i + 1) * half]]
            new.append(a_ + jnp.concatenate(parts, axis=0))
        att = new

    yield
    qe = (q * jnp.exp(b)).astype(BF16)
    ke = (k * jnp.exp(btot - b)).astype(BF16)
    etot = jnp.exp(btot)
    row_l = _iota((BLK, BLK), 0)
    col_l = _iota((BLK, BLK), 1)
    same_head = (row_l >= HEAD_DIM) == (col_l >= HEAD_DIM)
    outs = []
    for p in range(2):
        sl = slice(p * LANES, (p + 1) * LANES)
        st = sst_ref[p]
        vp = ci[:, sl]
        vb = vp.astype(BF16)
        a2 = jnp.concatenate([att[2 * p], att[2 * p + 1]], axis=1).astype(BF16)
        v2 = jnp.concatenate([vb * hm[0], vb * hm[1]], axis=0)
        o = _dot_nt(qe[:, sl], st.astype(BF16)) + _dot(a2, v2)
        upd = _dot(vp.T.astype(BF16), ke[:, sl])
        sst_ref[p] = st * etot[:, sl] + jnp.where(same_head, upd, 0.0)
        outs.append(o)
    o = jnp.concatenate(outs, axis=1)
    return _seg_rms(o, hg, ones_bd) * (cg * _sigmoid(cg))


def _mlstm_tile_gates(gcols, gb, m0, triu):
    tm = gcols.shape[0]
    gt = gcols.T[0:SUBLANES, :] + gb
    lf = _log_sigmoid(gt)
    hi = lf.astype(BF16)
    r1 = lf - hi.astype(F32)
    mid = r1.astype(BF16)
    lo = (r1 - mid.astype(F32)).astype(BF16)
    parts = jnp.concatenate([hi.astype(F32), mid.astype(F32), lo.astype(F32)], axis=0)
    cs = _dot(parts, triu)
    fcum = cs[0:SUBLANES] + cs[SUBLANES:2 * SUBLANES] + cs[2 * SUBLANES:3 * SUBLANES]
    fcum = pltpu.roll(fcum, N_HEADS, 0)
    g = gt - fcum
    pad = jnp.zeros((LANES - 2 * SUBLANES, tm), F32)
    cols = jnp.concatenate([g, fcum, pad], axis=0).T
    f_c = pltpu.roll(cols, LANES - SUBLANES, 1)
    sub = _iota((tm, LANES), 0) & (SUBLANES - 1)
    cm = cols
    s = 1
    while s < SUBLANES:
        cm = jnp.maximum(cm, jnp.where(sub >= s, pltpu.roll(cm, s, 0), NEG_BIG))
        s *= 2
    carry = m0
    ms = []
    for grp in range(tm // SUBLANES):
        m_g = jnp.maximum(cm[grp * SUBLANES:(grp + 1) * SUBLANES], carry)
        ms.append(m_g)
        carry = m_g[SUBLANES - 1:SUBLANES]
    m_c = jnp.concatenate(ms, axis=0)
    eneg_c = jnp.exp(-(f_c + m_c))
    m_new = f_c[tm - 1:tm] + carry
    return g, cols, m_c, eneg_c, m_new


def _mlstm_block(dq, dk, dv, do, g_rows, g_c, m_c, eneg_c, m_prev, mg, cst_ref, ones_bd, hm, hmf):
    k = dk * (HEAD_DIM ** -0.5)
    m_end = m_c[BLK - 1:BLK]
    inter = jnp.exp(m_prev - m_c)
    wend = jnp.exp(g_c - m_end)
    dec0 = jnp.exp(m_prev - m_end)
    mrun_c = lambda h: m_c[:, h:h + 1]
    inter_c = lambda h: inter[:, h:h + 1]
    eneg_c_ = lambda h: eneg_c[:, h:h + 1]
    wend_c = lambda h: wend[:, h:h + 1]
    g = g_rows
    lane = _iota((BLK, LANES), 1)
    low = lane < HEAD_DIM
    one_col = jnp.where(lane == HEAD_DIM, 1.0, 0.0)
    causal = _iota((BLK, BLK), 1) <= _iota((BLK, BLK), 0)

    outs = []
    for p in range(2):
        sl = slice(p * LANES, (p + 1) * LANES)
        kp, vp = k[:, sl], dv[:, sl]
        qb = dq[:, sl].astype(BF16)
        qm = [qb * hm[0], qb * hm[1]]
        sc2 = _dot_nt(jnp.concatenate(qm, axis=0), kp.astype(BF16))
        v_sw = pltpu.roll(vp, HEAD_DIM, 1)
        hv = []
        for hh in range(2):
            h = 2 * p + hh
            sc = sc2[hh * BLK:(hh + 1) * BLK]
            w = jnp.exp(jnp.where(causal, g[h:h + 1, :] - mrun_c(h), NEG_BIG))
            sw = (sc * w).astype(BF16)
            vaug = jnp.where(low, vp if hh == 0 else v_sw, one_col).astype(BF16)
            cst = cst_ref[h]
            nd = inter_c(h) * _dot(qm[hh], cst.astype(BF16)) + _dot(sw, vaug)
            den = nd[:, HEAD_DIM:HEAD_DIM + 1]
            hv.append(nd / jnp.maximum(jnp.abs(den), eneg_c_(h)))
            kw = kp * (wend_c(h) * hmf[hh])
            cst_ref[h] = dec0[:, h:h + 1] * cst + _dot(kw.T.astype(BF16), vaug)
        outs.append(jnp.where(low, hv[0], pltpu.roll(hv[1], HEAD_DIM, 1)))
        yield
    hcat = jnp.concatenate(outs, axis=1)
    return _seg_rms(hcat, mg, ones_bd) * _sigmoid(do)


def _mixer_kernel(hc_ref, hn_ref, pv_ref, win_ref, convw_ref, wr_ref, wi_ref, rr_ref, rbase_ref, gamma_ref,
                  ones_ref, msmall_ref, mbig_ref, gb_ref, triu_ref, *rest, layer, tm):
    y_ref, hl_ref, conv_ref, kst_ref, vst_ref, sst_ref, cst_ref, mst_ref, u_scr, nb_scr = rest[-10:]
    t = pl.program_id(0)
    states = (hl_ref, conv_ref, kst_ref, vst_ref, sst_ref, cst_ref, mst_ref)

    @pl.when(t == 0)
    def _init():
        for ref in states:
            ref[...] = jnp.zeros_like(ref)

    if layer == 0:
        hl_ref, conv_ref, kst_ref, vst_ref, sst_ref, cst_ref, mst_ref = (ref.at[0] for ref in states)

    bsz = hc_ref.shape[0]
    g1 = pv_ref[layer:layer + 1, V_G1:V_G1 + D_MODEL]
    cols = (((C_AX, C_BQ - C_AX),), ((C_BQ, C_CQ - C_BQ),),
            ((C_CQ, 2 * GROUP_W), (C_CI, 2 * GROUP_W)),
            ((C_DQ, 2 * GROUP_W), (C_DV, N_IN - C_DV)))

    def normalise(src_ref):
        for b in range(bsz):
            nb_scr[b * tm:(b + 1) * tm, :] = _rms_rows(src_ref[b], g1).astype(BF16)

    def project(group_ids):
        for gid in group_ids:
            for c0, w in cols[gid]:
                u_scr[:, c0:c0 + w] = _dot_nt(nb_scr[...], win_ref[c0:c0 + w, :])
                yield

    phases = ((0, 1), (2, 3))

    @pl.when(t == 0)
    def _prologue():
        normalise(hc_ref)
        _round_robin([(lambda: True, project(phases[0]))])

    def vec(off, w):
        return pv_ref[layer:layer + 1, off:off + w]

    ctxs = []
    for b in range(bsz):
        proj = functools.partial(lambda c0, w, b: u_scr[b * tm:(b + 1) * tm, c0:c0 + w], b=b)
        ctxs.append(types.SimpleNamespace(
            t=t, proj=proj, vec=vec, layer=layer, tm=tm, ones_bd=ones_ref[...], convw_ref=convw_ref,
            wr_ref=wr_ref, wi_ref=wi_ref, rr_ref=rr_ref, rbase_ref=rbase_ref, gamma_ref=gamma_ref,
            msmall_ref=msmall_ref, mbig_ref=mbig_ref, gb_ref=gb_ref, triu_ref=triu_ref, y_ref=y_ref.at[b],
            hl_ref=hl_ref.at[b], conv_ref=conv_ref.at[b], kst_ref=kst_ref.at[b], vst_ref=vst_ref.at[b],
            sst_ref=sst_ref.at[b], cst_ref=cst_ref.at[b], mst_ref=mst_ref.at[b],
            hm=_head_masks(BLK, BF16), hmf=_head_masks(1, F32)))
    groups = (_group_a, _group_b, _group_c, _group_d)
    for pi, phase in enumerate(phases):
        if pi == 1:
            normalise(hn_ref)
        tasks = [(lambda: True, project(phases[1 - pi]))]
        pairs = ([(c, gid) for c in ctxs for gid in phase] if pi == 0
                 else [(c, gid) for gid in phase for c in ctxs])
        for c, gid in pairs:
            tasks.append((lambda: True, groups[gid](c)))
        _round_robin(tasks)


def _round_robin(tasks):
    tasks = list(tasks)
    while tasks:
        for task in list(tasks):
            ready, gen = task
            if not ready():
                continue
            try:
                next(gen)
            except StopIteration:
                tasks.remove(task)


def _group_a(c):
    proj, vec, tm, layer = c.proj, c.vec, c.tm, c.layer
    conv_ref, convw_ref, wr_ref, wi_ref, hl_ref, y_ref = (c.conv_ref, c.convw_ref, c.wr_ref, c.wi_ref,
                                                          c.hl_ref, c.y_ref)
    xa = proj(C_AX, GROUP_W)
    ga = proj(C_AG, GROUP_W)
    cw = convw_ref[layer]
    tail = conv_ref[...]
    sub8 = _iota((SUBLANES, GROUP_W), 0)

    def shifted(j):
        r = pltpu.roll(xa, j, 0)
        head = jnp.where(sub8 < j, pltpu.roll(tail, j, 0), r[0:SUBLANES])
        return jnp.concatenate([head, r[SUBLANES:]], axis=0)

    xc = (vec(V_CB, GROUP_W) + shifted(3) * cw[0:1, :] + shifted(2) * cw[1:2, :]
          + shifted(1) * cw[2:3, :] + xa * cw[3:4, :])
    conv_ref[...] = xa[tm - SUBLANES:tm]
    yield
    a, bx = _rglru_gates(xc, wr_ref[...], wi_ref[...], vec(V_BR, GROUP_W), vec(V_BI, GROUP_W),
                         vec(V_LAM, GROUP_W))
    yield
    sub_t = _iota((tm, GROUP_W), 0) & (SUBLANES - 1)
    s = 1
    while s < SUBLANES:
        keep = sub_t >= s
        a_s = pltpu.roll(a, s, 0)
        b_s = pltpu.roll(bx, s, 0)
        bx = jnp.where(keep, a * b_s + bx, bx)
        a = jnp.where(keep, a * a_s, a)
        s *= 2
    carry = hl_ref[...]
    hs = []
    for g in range(tm // SUBLANES):
        rows = slice(g * SUBLANES, (g + 1) * SUBLANES)
        hg_ = a[rows] * carry + bx[rows]
        hs.append(hg_)
        carry = hg_[SUBLANES - 1:SUBLANES]
    hseq = jnp.concatenate(hs, axis=0)
    hl_ref[...] = carry
    yield
    y_ref[:, 0:GROUP_W] = (hseq * _gelu_tanh(ga)).astype(y_ref.dtype)


def _group_b(c):
    proj, vec, tm, t, ones_bd = c.proj, c.vec, c.tm, c.t, c.ones_bd
    rr_ref, rbase_ref, kst_ref, vst_ref, y_ref = c.rr_ref, c.rbase_ref, c.kst_ref, c.vst_ref, c.y_ref
    cb = rbase_ref[:, 0:LANES]
    sb_ = rbase_ref[:, LANES:2 * LANES]
    rc = cb * rr_ref[0] - sb_ * rr_ref[1]
    ra = sb_ * rr_ref[2] + cb * rr_ref[3]
    rb = sb_ * rr_ref[4] + cb * rr_ref[5]
    q = _seg_rms(proj(C_BQ, GROUP_W), vec(V_QG, GROUP_W), ones_bd)
    q = jnp.concatenate([_rope(q[:, 0:LANES], rc, ra, rb), _rope(q[:, LANES:], rc, ra, rb)], axis=1)
    k = _seg_rms(proj(C_BK, LANES), vec(V_KG, LANES), ones_bd[0:LANES, 0:LANES])
    k = _rope(k, rc, ra, rb)
    v = proj(C_BV, LANES)
    sink = vec(V_SINK, LANES)
    yield
    k_prev, v_prev = kst_ref[...], vst_ref[...]
    for j in range(tm // BLK):
        rs = slice(j * BLK, (j + 1) * BLK)
        first = (t == 0) if j == 0 else False
        yb = _swa_block(q[rs], k[rs], v[rs], k_prev, v_prev, sink, first)
        y_ref[rs, GROUP_W:2 * GROUP_W] = yb.astype(y_ref.dtype)
        k_prev, v_prev = k[rs], v[rs]
        yield
    kst_ref[...] = k_prev
    vst_ref[...] = v_prev


def _group_c(c):
    proj, vec, tm, layer, ones_bd, hm = c.proj, c.vec, c.tm, c.layer, c.ones_bd, c.hm
    gamma_ref, sst_ref, msmall_ref, mbig_ref, y_ref = c.gamma_ref, c.sst_ref, c.msmall_ref, c.mbig_ref, c.y_ref
    lb = _lb_from_gamma([gamma_ref[i:i + 1, :] for i in range(gamma_ref.shape[0])], layer)
    cq, cf, ci, cg = (proj(C_CQ, GROUP_W), proj(C_CF, GROUP_W), proj(C_CI, GROUP_W), proj(C_CG, GROUP_W))
    hg = vec(V_HG, GROUP_W)
    sub = _iota((BLK, GROUP_W), 0) & (SUBLANES - 1)
    for j in range(tm // BLK):
        rs = slice(j * BLK, (j + 1) * BLK)
        yc = yield from _hgrn_block(cq[rs], cf[rs], ci[rs], cg[rs], lb, hg, sst_ref, ones_bd, msmall_ref,
                                    mbig_ref, hm, sub)
        y_ref[rs, 2 * GROUP_W:3 * GROUP_W] = yc.astype(y_ref.dtype)
        yield


def _group_d(c):
    proj, vec, tm, layer, ones_bd, hm, hmf = c.proj, c.vec, c.tm, c.layer, c.ones_bd, c.hm, c.hmf
    gb_ref, triu_ref, mst_ref, cst_ref, y_ref = c.gb_ref, c.triu_ref, c.mst_ref, c.cst_ref, c.y_ref
    dq, dk, dv, do = (proj(C_DQ, GROUP_W), proj(C_DK, GROUP_W), proj(C_DV, GROUP_W), proj(C_DO, GROUP_W))
    mg = vec(V_MG, GROUP_W)
    m_prev = mst_ref[...]
    g_rows, g_c, m_c, eneg_c, m_new = _mlstm_tile_gates(proj(C_GATES, LANES), gb_ref[layer], m_prev,
                                                       triu_ref[...])
    mst_ref[...] = m_new
    yield
    for j in range(tm // BLK):
        rs = slice(j * BLK, (j + 1) * BLK)
        yd = yield from _mlstm_block(dq[rs], dk[rs], dv[rs], do[rs], g_rows[:, rs], g_c[rs], m_c[rs], eneg_c[rs],
                          m_prev, mg, cst_ref, ones_bd, hm, hmf)
        y_ref[rs, 3 * GROUP_W:4 * GROUP_W] = yd.astype(y_ref.dtype)
        m_prev = m_c[(j + 1) * BLK - 1:(j + 1) * BLK]
        yield


def _full(shape):
    nd = len(shape)
    return pl.BlockSpec(shape, lambda *_: (0,) * nd)


def _layer_block(shape, layer, single_buffer=False):
    nd = len(shape) - 1
    kw = {'pipeline_mode': pl.Buffered(1)} if single_buffer else {}
    return pl.BlockSpec((None,) + tuple(shape[1:]), lambda *_: (layer,) + (0,) * nd, **kw)


def _prompt_mixers(h, cw, layer, tm, prev):
    bsz, t, _ = h.shape
    nt = t // tm
    kern = functools.partial(_mixer_kernel, layer=layer, tm=tm)
    in_specs = [pl.BlockSpec((bsz, tm, D_MODEL), lambda i: (0, i, 0)),
                pl.BlockSpec((bsz, tm, D_MODEL), lambda i: (0, jnp.minimum(i + 1, nt - 1), 0)),
                _full(cw['vecs'].shape), _layer_block(cw['w_in_p'].shape, layer, single_buffer=True),
                _full(cw['conv_w'].shape), _layer_block(cw['wr_bd'].shape, layer),
                _layer_block(cw['wi_bd'].shape, layer), _full(cw['rope_r'].shape),
                pl.BlockSpec((None, 1, 2 * LANES), lambda i: (i, 0, 0)),
                _full(cw['gamma'].shape), _full(cw['ones_bd'].shape), _full(cw['lvl_small'].shape),
                _full(cw['lvl_big'].shape), _full(cw['gate_bias'].shape), _full(cw['triu'].shape)]
    depth = cw['vecs'].shape[0]
    st_shapes = [(depth, bsz) + s for s in ((1, GROUP_W), (8, GROUP_W), (BLK, LANES), (BLK, LANES),
                                            (2, LANES, LANES), (N_HEADS, LANES, LANES), (1, LANES))]
    out_shape = ([jax.ShapeDtypeStruct((bsz, t, D_MODEL), BF16)]
                 + [jax.ShapeDtypeStruct(s, F32) for s in st_shapes])
    st_specs = [_full(s) if prev is None else _layer_block(s, layer) for s in st_shapes]
    chain = [] if prev is None else list(prev)
    n_in = len(in_specs)
    return pl.pallas_call(
        kern, grid=(nt,), in_specs=in_specs + [pl.BlockSpec(memory_space=pl.ANY)] * len(chain),
        out_specs=[pl.BlockSpec((bsz, tm, D_MODEL), lambda i: (0, i, 0))] + st_specs, out_shape=out_shape,
        input_output_aliases={n_in + i: 1 + i for i in range(len(chain))},
        scratch_shapes=[pltpu.VMEM((bsz * tm, N_IN), F32), pltpu.VMEM((bsz * tm, D_MODEL), BF16)],
        compiler_params=pltpu.CompilerParams(dimension_semantics=("arbitrary",),
                                             vmem_limit_bytes=VMEM_LIMIT),
        name=f"prompt_mixers_l{layer}",
    )(h, h, cw['vecs'], cw['w_in_p'], cw['conv_w'], cw['wr_bd'], cw['wi_bd'], cw['rope_r'], cw['rope_base'],
      cw['gamma'], cw['ones_bd'], cw['lvl_small'], cw['lvl_big'], cw['gate_bias'], cw['triu'], *chain)


def _ffn_math(h, yb, p, g2, wout_ref, wup_ref, wdn_ref, wg_ref, wp_ref):
    h = h + _dot(yb, wout_ref[...])
    nb = _rms_rows(h, g2).astype(BF16)
    acc = h
    step = 1024
    for c in range(0, D_FF, step):
        f = jnp.maximum(_dot(nb, wup_ref[:, c:c + step]), 0.0)
        acc = acc + _dot((f * f).astype(BF16), wdn_ref[c:c + step, :])
    gate = _sigmoid(_dot(acc.astype(BF16), wg_ref[...]))
    return acc + gate * _dot(p.astype(BF16), wp_ref[...])


def _ffn_kernel(h_ref, y_ref, p_ref, pv_ref, wout_ref, wup_ref, wdn_ref, wg_ref, wp_ref, o_ref, *, layer):
    g2 = pv_ref[layer:layer + 1, V_G2:V_G2 + D_MODEL]
    o_ref[...] = _ffn_math(h_ref[...], y_ref[...], p_ref[...], g2, wout_ref, wup_ref, wdn_ref, wg_ref, wp_ref)


def _ffn_weight_specs(cw, layer):
    names = ['w_out', 'w_up', 'w_down', 'w_gate', 'w_proj']
    return [cw[n] for n in names], [_layer_block(cw[n].shape, layer, single_buffer=True) for n in names]


def _prompt_ffn(h2, y2, p3, cw, layer, tm):
    n = h2.shape[0]
    row = lambda w: pl.BlockSpec((tm, w), lambda i: (i, 0))
    ws, wspecs = _ffn_weight_specs(cw, layer)
    return pl.pallas_call(
        functools.partial(_ffn_kernel, layer=layer), grid=(n // tm,),
        in_specs=[row(D_MODEL), row(D_MODEL), pl.BlockSpec((None, tm, PLE_DIM), lambda i: (layer, i, 0)),
                  _full(cw['vecs'].shape)] + wspecs,
        out_specs=row(D_MODEL), out_shape=jax.ShapeDtypeStruct((n, D_MODEL), F32),
        compiler_params=pltpu.CompilerParams(dimension_semantics=("arbitrary",),
                                             vmem_limit_bytes=VMEM_LIMIT),
        name=f"prompt_ffn_l{layer}",
    )(h2, y2, p3, cw['vecs'], *ws)


def _sample_pre_kernel(x_ref, pv_ref, wt_ref, convw_ref, wr_ref, wi_ref, ones_ref, rope_ref, gcol_ref,
                       cols_ref, h0_ref, conv0_ref, n0_ref, m0_ref,
                       ya_ref, hn_ref, convn_ref, q_ref, kt_ref, vt_ref, fm_ref, nn_ref, mn_ref, *, layer):
    def vec(off, w):
        return pv_ref[layer:layer + 1, off:off + w]

    n = _rms_rows(x_ref[...], vec(V_G1, D_MODEL))
    nb = n.astype(BF16)
    n_t = n.T.astype(BF16)
    ones_bd = ones_ref[...]

    def proj(r0, cnt):
        return _dot_nt(nb, wt_ref[r0:r0 + cnt, :])

    def proj_t(r0, cnt):
        return _dot(wt_ref[r0:r0 + cnt, :], n_t)

    xa = proj(C_AX, GROUP_W)
    ga = proj(C_AG, GROUP_W)
    cw = convw_ref[layer]
    xc = (vec(V_CB, GROUP_W) + conv0_ref[0] * cw[0:1, :] + conv0_ref[1] * cw[1:2, :]
          + conv0_ref[2] * cw[2:3, :] + xa * cw[3:4, :])
    convn_ref[0] = conv0_ref[1]
    convn_ref[1] = conv0_ref[2]
    convn_ref[2] = xa
    a, bx = _rglru_gates(xc, wr_ref[...], wi_ref[...], vec(V_BR, GROUP_W), vec(V_BI, GROUP_W),
                         vec(V_LAM, GROUP_W))
    hn = a * h0_ref[...] + bx
    hn_ref[...] = hn
    ya_ref[...] = hn * _gelu_tanh(ga)

    rc, ra, rb = rope_ref[0:1, :], rope_ref[1:2, :], rope_ref[2:3, :]
    q = _seg_rms(proj(C_BQ, GROUP_W), vec(V_QG, GROUP_W), ones_bd)
    q_ref[...] = _rope(q, jnp.concatenate([rc, rc], 1), jnp.concatenate([ra, ra], 1),
                       jnp.concatenate([rb, rb], 1))
    k = _seg_rms(proj(C_BK, LANES), vec(V_KG, LANES), ones_bd[0:LANES, 0:LANES])
    kt_ref[...] = _rope(k, rc, ra, rb).T
    vt_ref[...] = proj(C_BV, LANES).T

    lb = _lb_from_gamma([gcol_ref[i] for i in range(gcol_ref.shape[0])], layer)
    cq, cf, ci, cg = (proj_t(C_CQ, GROUP_W), proj_t(C_CF, GROUP_W), proj_t(C_CI, GROUP_W),
                      proj_t(C_CG, GROUP_W))
    f = lb + (1.0 - lb) * _sigmoid(cf)
    fm_ref[FM_HF:FM_HF + GROUP_W, :] = f
    fm_ref[FM_HK:FM_HK + GROUP_W, :] = 1.0 - f
    fm_ref[FM_HQ:FM_HQ + GROUP_W, :] = cq * _sigmoid(cq)
    fm_ref[FM_HV:FM_HV + GROUP_W, :] = ci
    fm_ref[FM_CG:FM_CG + GROUP_W, :] = cg * _sigmoid(cg)

    dq, dk, dv, do = (proj_t(C_DQ, GROUP_W), proj_t(C_DK, GROUP_W), proj_t(C_DV, GROUP_W),
                      proj_t(C_DO, GROUP_W))
    g8 = proj_t(C_GATES, 2 * N_HEADS) + cols_ref[layer, R_GB:R_GB + 2 * N_HEADS, :]
    ig = g8[0:N_HEADS, :]
    lf = _log_sigmoid(g8)[N_HEADS:2 * N_HEADS, :]
    a_int = lf + m0_ref[...]
    m_new = jnp.maximum(a_int, ig)
    dec = jnp.exp(a_int - m_new)
    w = jnp.exp(ig - m_new)
    mn_ref[...] = m_new
    km = dk * (HEAD_DIM ** -0.5)
    dens = []
    for h in range(N_HEADS):
        sl = slice(h * HEAD_DIM, (h + 1) * HEAD_DIM)
        nn_h = dec[h:h + 1, :] * n0_ref[sl, :] + w[h:h + 1, :] * km[sl, :]
        nn_ref[sl, :] = nn_h
        dens.append(jnp.sum(dq[sl, :] * nn_h, axis=0, keepdims=True))
    pad = jnp.zeros((SUBLANES - N_HEADS, x_ref.shape[0]), F32)
    fm_ref[FM_MK:FM_MK + GROUP_W, :] = km
    fm_ref[FM_MQ:FM_MQ + GROUP_W, :] = dq
    fm_ref[FM_MV:FM_MV + GROUP_W, :] = dv
    fm_ref[FM_DO:FM_DO + GROUP_W, :] = _sigmoid(do)
    fm_ref[FM_DEC:FM_DEC + SUBLANES, :] = jnp.concatenate([dec, pad], 0)
    fm_ref[FM_W:FM_W + SUBLANES, :] = jnp.concatenate([w, pad], 0)
    fm_ref[FM_DEN:FM_DEN + SUBLANES, :] = jnp.concatenate(dens + [pad], 0)
    fm_ref[FM_ENEG:FM_ENEG + SUBLANES, :] = jnp.concatenate([jnp.exp(-m_new), pad], 0)


def _own_slab(ref, first_layer):
    if not first_layer:
        return ref
    ref[1:] = jnp.zeros((ref.shape[0] - 1,) + ref.shape[1:], ref.dtype)
    return ref.at[0]


def _sample_attn_kernel(q_ref, kn_ref, vn_ref, kc_ref, vc_ref, sink_ref, *rest, first_layer):
    ko_ref, vo_ref, o_ref = rest[-3:]
    ko_ref, vo_ref = _own_slab(ko_ref, first_layer), _own_slab(vo_ref, first_layer)
    sb = q_ref.shape[0]
    rows = 2 * HEAD_DIM
    lane = _iota((rows, WINDOW), 1)
    kn = kn_ref[...].reshape(rows, sb)
    vn = vn_ref[...].reshape(rows, sb)
    for s in range(sb):
        kt = pltpu.roll(kc_ref[s].reshape(rows, WINDOW), WINDOW - 1, 1)
        ko_ref[s] = jnp.where(lane == WINDOW - 1, kn[:, s:s + 1], kt).reshape(2, HEAD_DIM, WINDOW)
        vt = pltpu.roll(vc_ref[s].reshape(rows, WINDOW), WINDOW - 1, 1)
        vo_ref[s] = jnp.where(lane == WINDOW - 1, vn[:, s:s + 1], vt).reshape(2, HEAD_DIM, WINDOW)
    for kv in range(2):
        kk = ko_ref[:, kv].astype(BF16)
        vv = vo_ref[:, kv].astype(BF16)
        s_ = jnp.einsum('bqc,bcj->bqj', q_ref[:, kv].astype(BF16), kk,
                        preferred_element_type=F32) * (HEAD_DIM ** -0.5)
        sk = sink_ref[kv]
        mx = jnp.maximum(jnp.max(s_, axis=-1, keepdims=True), sk)
        p = jnp.exp(s_ - mx)
        den = jnp.sum(p, axis=-1, keepdims=True) + jnp.exp(sk - mx)
        o = jnp.einsum('bqj,bcj->bqc', p.astype(BF16), vv, preferred_element_type=F32)
        o_ref[:, kv] = o / den


def _sample_state_kernel(fm_ref, s_ref, c_ref, *rest, first_layer):
    so_ref, co_ref, oh_ref, om_ref = rest[-4:]
    so_ref, co_ref = _own_slab(so_ref, first_layer), _own_slab(co_ref, first_layer)
    nd = s_ref.shape[0]
    parts = HEAD_DIM // nd
    h = pl.program_id(0) // parts
    part = pl.program_id(0) % parts
    r0 = pl.multiple_of(h * HEAD_DIM, HEAD_DIM)
    hv = fm_ref[pl.ds(FM_HV + r0, HEAD_DIM), :]
    mv = fm_ref[pl.ds(FM_MV + r0, HEAD_DIM), :]
    dec = fm_ref[pl.ds(FM_DEC + h, 1), :]
    w = fm_ref[pl.ds(FM_W + h, 1), :]

    def body(d, carry):
        acc_h, acc_m = carry
        r = r0 + part * nd + d
        s_new = fm_ref[pl.ds(FM_HF + r, 1), :] * s_ref[d] + fm_ref[pl.ds(FM_HK + r, 1), :] * hv
        so_ref[d] = s_new
        c_new = dec * c_ref[d] + (w * fm_ref[pl.ds(FM_MK + r, 1), :]) * mv
        co_ref[d] = c_new
        return (acc_h + fm_ref[pl.ds(FM_HQ + r, 1), :] * s_new,
                acc_m + fm_ref[pl.ds(FM_MQ + r, 1), :] * c_new)

    zero = jnp.zeros((HEAD_DIM, fm_ref.shape[1]), F32)
    acc_h, acc_m = lax.fori_loop(0, nd, body, (zero, zero), unroll=4)

    @pl.when(part == 0)
    def _first():
        oh_ref[...] = acc_h
        om_ref[...] = acc_m

    @pl.when(part != 0)
    def _rest():
        oh_ref[...] += acc_h
        om_ref[...] += acc_m


def _sample_post_kernel(h_ref, ya_ref, yb_ref, oh_ref, om_ref, fm_ref, cols_ref, p_ref, pv_ref,
                        wout_ref, wup_ref, wdn_ref, wg_ref, wp_ref, o_ref, acc_ref, nb_ref, *, layer):
    k = pl.program_id(0)

    @pl.when(k == 0)
    def _head():
        def head_rms(x):
            return x * lax.rsqrt(jnp.mean(x * x, axis=0, keepdims=True) + EPS)

        yc, yd = [], []
        for h in range(N_HEADS):
            yc.append(head_rms(oh_ref[h]))
            den = fm_ref[FM_DEN + h:FM_DEN + h + 1, :]
            eneg = fm_ref[FM_ENEG + h:FM_ENEG + h + 1, :]
            yd.append(head_rms(om_ref[h] / jnp.maximum(jnp.abs(den), eneg)))
        yc_ = (jnp.concatenate(yc, 0) * cols_ref[layer, R_HG:R_HG + GROUP_W, :]
               * fm_ref[FM_CG:FM_CG + GROUP_W, :])
        yd_ = (jnp.concatenate(yd, 0) * cols_ref[layer, R_MG:R_MG + GROUP_W, :]
               * fm_ref[FM_DO:FM_DO + GROUP_W, :])
        y = jnp.concatenate([ya_ref[...], yb_ref[...], yc_.T, yd_.T], axis=1).astype(BF16)
        h1 = h_ref[...] + _dot(y, wout_ref[...])
        acc_ref[...] = h1
        nb_ref[...] = _rms_rows(h1, pv_ref[layer:layer + 1, V_G2:V_G2 + D_MODEL]).astype(BF16)

    f = jnp.maximum(_dot(nb_ref[...], wup_ref[...]), 0.0)
    acc_ref[...] += _dot((f * f).astype(BF16), wdn_ref[...])

    @pl.when(k == pl.num_programs(0) - 1)
    def _tail():
        acc = acc_ref[...]
        gate = _sigmoid(_dot(acc.astype(BF16), wg_ref[...]))
        o_ref[...] = acc + gate * _dot(p_ref[...].astype(BF16), wp_ref[...])


def _call_full(kern, args, specs, out_shape, name):
    specs = [(_full(a.shape) if s is None else s) for a, s in zip(args, specs)]
    return pl.pallas_call(
        kern, grid=(1,), in_specs=specs,
        out_specs=tuple(_full(s.shape) for s in out_shape), out_shape=tuple(out_shape),
        compiler_params=pltpu.CompilerParams(dimension_semantics=("arbitrary",),
                                             vmem_limit_bytes=VMEM_LIMIT),
        name=name,
    )(*args)


def _sample_layer(h, sv, prev, cw, layer):
    nsm = h.shape[0]
    depth = cw['vecs'].shape[0]
    sd = lambda *shape: jax.ShapeDtypeStruct(shape, F32)
    g, l = sd(nsm, GROUP_W), sd(LANES, nsm)

    args = [h, cw['vecs'], cw['w_in_p'], cw['conv_w'], cw['wr_bd'], cw['wi_bd'], cw['ones_bd'],
            cw['rope_s'], cw['gamma_col'], cw['cols'], sv['h'], sv['conv'], sv['n'], sv['m']]
    specs = [None, None, _layer_block(cw['w_in_p'].shape, layer, single_buffer=True), None,
             _layer_block(cw['wr_bd'].shape, layer), _layer_block(cw['wi_bd'].shape, layer), None,
             None, None, None, _layer_block(sv['h'].shape, layer), _layer_block(sv['conv'].shape, layer),
             _layer_block(sv['n'].shape, layer), _layer_block(sv['m'].shape, layer)]
    outs = [g, g, sd(CONV_W - 1, nsm, GROUP_W), g, l, l, sd(FM_ROWS, nsm), sd(GROUP_W, nsm),
            sd(N_HEADS, nsm)]
    ya, hn, convn, q, kt, vt, fm, nn, mn = _call_full(
        functools.partial(_sample_pre_kernel, layer=layer), args, specs, outs, f"sample_pre_l{layer}")

    sb = 16
    nblk = nsm // sb
    q3 = jnp.pad(q.reshape(nsm, 2, 2, HEAD_DIM), ((0, 0), (0, 0), (0, SUBLANES - 2), (0, 0)))
    to_blocks = lambda a: a.reshape(2, HEAD_DIM, nblk, sb).transpose(2, 0, 1, 3)
    cshape = (depth, nsm, 2, HEAD_DIM, WINDOW)
    cspec = pl.BlockSpec((None, sb, 2, HEAD_DIM, WINDOW), lambda i: (layer, i, 0, 0, 0))
    nspec = pl.BlockSpec((None, 2, HEAD_DIM, sb), lambda i: (i, 0, 0, 0))
    qspec = pl.BlockSpec((sb, 2, SUBLANES, HEAD_DIM), lambda i: (i, 0, 0, 0))
    any_spec = pl.BlockSpec(memory_space=pl.ANY)
    cout = (pl.BlockSpec((depth, sb, 2, HEAD_DIM, WINDOW), lambda i: (0, i, 0, 0, 0)) if prev is None
            else cspec)
    chain = [] if prev is None else [prev['k'], prev['v']]
    ko, vo, o3 = pl.pallas_call(
        functools.partial(_sample_attn_kernel, first_layer=prev is None), grid=(nblk,),
        in_specs=[qspec, nspec, nspec, cspec, cspec, _layer_block(cw['sinks8'].shape, layer)]
        + [any_spec] * len(chain),
        out_specs=(cout, cout, qspec),
        out_shape=(sd(*cshape), sd(*cshape), sd(nsm, 2, SUBLANES, HEAD_DIM)),
        input_output_aliases={6 + i: i for i in range(len(chain))},
        compiler_params=pltpu.CompilerParams(dimension_semantics=("arbitrary",),
                                             vmem_limit_bytes=VMEM_LIMIT),
        name=f"sample_attn_l{layer}",
    )(q3, to_blocks(kt), to_blocks(vt), sv['k'], sv['v'], cw['sinks8'], *chain)
    yb = o3[:, :, 0:2, :].reshape(nsm, GROUP_W)

    sshape = (depth, N_HEADS, HEAD_DIM, HEAD_DIM, nsm)
    parts = 4
    nd = HEAD_DIM // parts
    smap = lambda i: (layer, i // parts, i % parts, 0, 0)
    sspec = pl.BlockSpec((None, None, nd, HEAD_DIM, nsm), smap)
    ospec = pl.BlockSpec((None, HEAD_DIM, nsm), lambda i: (i // parts, 0, 0))
    sout = (pl.BlockSpec((depth, None, nd, HEAD_DIM, nsm), lambda i: (0, i // parts, i % parts, 0, 0))
            if prev is None else sspec)
    chain = [] if prev is None else [prev['s'], prev['c']]
    so, co, oh, om = pl.pallas_call(
        functools.partial(_sample_state_kernel, first_layer=prev is None), grid=(N_HEADS * parts,),
        in_specs=[_full(fm.shape), sspec, sspec] + [any_spec] * len(chain),
        out_specs=(sout, sout, ospec, ospec),
        out_shape=(sd(*sshape), sd(*sshape), sd(N_HEADS, HEAD_DIM, nsm), sd(N_HEADS, HEAD_DIM, nsm)),
        input_output_aliases={3 + i: i for i in range(len(chain))},
        compiler_params=pltpu.CompilerParams(dimension_semantics=("arbitrary",),
                                             vmem_limit_bytes=VMEM_LIMIT),
        name=f"sample_state_l{layer}",
    )(fm, sv['s'], sv['c'], *chain)

    ws, wspecs = _ffn_weight_specs(cw, layer)
    chunk = 1024
    wspecs[1] = pl.BlockSpec((None, D_MODEL, chunk), lambda k: (layer, 0, k))
    wspecs[2] = pl.BlockSpec((None, chunk, D_MODEL), lambda k: (layer, k, 0))
    post_args = [h, ya, yb, oh, om, fm, cw['cols'], sv['p'], cw['vecs']] + ws
    post_specs = ([_full(a.shape) for a in post_args[:7]] + [_layer_block(sv['p'].shape, layer),
                                                             _full(cw['vecs'].shape)] + wspecs)
    h_new = pl.pallas_call(
        functools.partial(_sample_post_kernel, layer=layer), grid=(D_FF // chunk,),
        in_specs=post_specs, out_specs=_full((nsm, D_MODEL)), out_shape=sd(nsm, D_MODEL),
        scratch_shapes=[pltpu.VMEM((nsm, D_MODEL), F32), pltpu.VMEM((nsm, D_MODEL), BF16)],
        compiler_params=pltpu.CompilerParams(dimension_semantics=("arbitrary",),
                                             vmem_limit_bytes=VMEM_LIMIT),
        name=f"sample_post_l{layer}",
    )(*post_args)
    small = (hn, convn, nn, mn)
    big = {'k': ko, 'v': vo, 's': so, 'c': co}
    return h_new, small, big


def _block_diag_all(w):
    depth = w.shape[0]
    rows = w.reshape(depth, GROUP_W, HEAD_DIM)
    idx = np.arange(GROUP_W) // HEAD_DIM
    mask = idx[:, None] == idx[None, :]
    return jnp.where(mask[None], jnp.tile(rows, (1, 1, N_HEADS)), 0.0)


def _rope_lane_freq():
    half = ROT_DIM // 2
    inv = np.power(np.float32(ROPE_THETA), -np.arange(half, dtype=np.float32) * np.float32(2.0 / ROT_DIM))
    dd = np.arange(LANES) % HEAD_DIM
    freq = np.where(dd < ROT_DIM, inv[dd % half], np.float32(0.0))
    m_a = (dd < half).astype(np.float32)
    m_b = ((dd >= half) & (dd < ROT_DIM)).astype(np.float32)
    return freq.astype(np.float32), m_a, m_b


def _rope_tables(pos):
    freq, m_a, m_b = _rope_lane_freq()
    ang = np.asarray(pos, np.float32)[:, None] * freq[None, :]
    cos, sin = np.cos(ang), np.sin(ang)
    return cos, -sin * m_a, sin * m_b


def _rope_split_tables(t, tm):
    freq, m_a, m_b = _rope_lane_freq()
    ang_r = np.arange(tm, dtype=np.float32)[:, None] * freq[None, :]
    cr, sr = np.cos(ang_r), np.sin(ang_r)
    rope_r = np.stack([cr, sr, -cr * m_a, -sr * m_a, cr * m_b, sr * m_b])
    ang_b = (np.arange(t // tm) * tm).astype(np.float32)[:, None] * freq[None, :]
    rope_base = np.concatenate([np.cos(ang_b), np.sin(ang_b)], 1)[:, None, :]
    return rope_r, rope_base


def _hgrn_level_masks():
    t = np.arange(BLK)[:, None]
    s = np.arange(BLK)[None, :]
    small = [t == s]
    big = []
    for lev in range(1, 8):
        half = 1 << (lev - 1)
        own = ((t >> lev) == (s >> lev)) & ((t & half) != 0) & ((s & half) == 0)
        if lev < 4:
            small.append(own)
        else:
            rows = np.concatenate([np.arange(m, m + half) for m in range(half, BLK, 2 * half)])
            big.append(own[rows])
    return np.stack(small).astype(np.float32), np.stack(big).astype(np.float32)


def _pad_last(v, width):
    return jnp.pad(v, ((0, 0), (0, width - v.shape[-1])))


def _common(w, t, tm_mix, past_len, nsm):
    depth = w['w_in'].shape[0]
    tile = lambda v, n: jnp.tile(v, (1, n))
    vecs = jnp.concatenate([
        w['norm1_g'], w['norm2_g'], w['conv_b'], w['lru_br'], w['lru_bi'], w['lru_lam'],
        tile(w['q_norm_g'], N_HEADS), tile(w['k_norm_g'], 2), _pad_last(w['attn_sinks'], LANES),
        tile(w['hgrn_norm_g'], N_HEADS), tile(w['mlstm_norm_g'], N_HEADS),
        _pad_last(w['mlstm_ib'], LANES), _pad_last(w['mlstm_fb'], LANES)], axis=1)
    cols = jnp.concatenate([tile(w['hgrn_norm_g'], N_HEADS), tile(w['mlstm_norm_g'], N_HEADS),
                            w['mlstm_ib'], w['mlstm_fb']], axis=1)
    w_in = w['w_in']
    gate_bias = jnp.concatenate([w['mlstm_ib'], w['mlstm_fb']], axis=1)
    lvl_small, lvl_big = _hgrn_level_masks()
    sinks = w['attn_sinks']
    z2 = jnp.zeros((depth, 2, SUBLANES - 2), F32)
    idx = np.arange(GROUP_W)
    rope_r, rope_base = _rope_split_tables(t, tm_mix)
    return {
        'vecs': vecs,
        'cols': jnp.broadcast_to(cols[:, :, None], cols.shape + (nsm,)),
        'gamma': w['hgrn_gamma'],
        'gamma_col': jnp.broadcast_to(w['hgrn_gamma'][:, :, None], w['hgrn_gamma'].shape + (nsm,)),
        'w_in_p': jnp.pad(jnp.swapaxes(w_in, 1, 2), ((0, 0), (0, N_IN - D_IN), (0, 0))).astype(BF16),
        'lvl_small': lvl_small, 'lvl_big': lvl_big,
        'gate_bias': jnp.broadcast_to(gate_bias[:, :, None], gate_bias.shape + (tm_mix,)),
        'triu': (np.arange(tm_mix)[:, None] <= np.arange(tm_mix)[None, :]).astype(np.float32),
        'conv_w': w['conv_w'],
        'wr_bd': _block_diag_all(w['lru_wr']).astype(BF16),
        'wi_bd': _block_diag_all(w['lru_wi']).astype(BF16),
        'ones_bd': jnp.asarray(idx[:, None] // HEAD_DIM == idx[None, :] // HEAD_DIM, BF16),
        'sinks8': jnp.concatenate([sinks.reshape(depth, 2, 2), z2], axis=2)[..., None],
        'rope_r': rope_r, 'rope_base': rope_base,
        'rope_s': np.concatenate(_rope_tables(past_len + np.arange(1)), axis=0),
        'w_out': w['w_out'].astype(BF16), 'w_up': w['w_up'].astype(BF16),
        'w_down': w['w_down'].astype(BF16), 'w_gate': w['w_ple_gate'].astype(BF16),
        'w_proj': w['w_ple_proj'].astype(BF16),
    }


def _run(x_prompt, x_sample, p_prompt, p_sample, sample_state, w, past_len, tm_mix=256, tm_ffn=512):
    depth = w['w_in'].shape[0]
    bsz, t, _ = x_prompt.shape
    nsm = x_sample.shape[0]
    cw = _common(w, t, tm_mix, past_len, nsm)
    tm_ffn = min(tm_ffn, bsz * t)
    h0, conv0, kc, vc, s0, c0, n0, m0 = sample_state
    sv = {'h': h0, 'conv': jnp.transpose(conv0, (0, 2, 1, 3)),
          'k': jnp.transpose(kc, (0, 1, 3, 4, 2)), 'v': jnp.transpose(vc, (0, 1, 3, 4, 2)),
          's': jnp.transpose(s0, (0, 2, 3, 4, 1)), 'c': jnp.transpose(c0, (0, 2, 3, 4, 1)),
          'n': jnp.transpose(n0, (0, 2, 3, 1)).reshape(depth, GROUP_W, nsm),
          'm': jnp.transpose(m0, (0, 2, 1)), 'p': p_sample.reshape(depth, nsm, PLE_DIM)}
    p3 = p_prompt.reshape(depth, bsz * t, PLE_DIM)

    hp = x_prompt
    hs = x_sample.reshape(nsm, D_MODEL)
    pst, s_small, big = None, [], None
    for l in range(depth):
        y, *pst = _prompt_mixers(hp, cw, l, tm_mix, pst)
        hp = _prompt_ffn(hp.reshape(bsz * t, D_MODEL), y.reshape(bsz * t, D_MODEL), p3, cw, l,
                         tm_ffn).reshape(bsz, t, D_MODEL)
        hs, small, big = _sample_layer(hs, sv, big, cw, l)
        s_small.append(small)
    stack = lambda sts, i: jnp.stack([s[i] for s in sts])
    hl, conv, kst, vst, sst, cst, mst = pst
    s_t = jnp.swapaxes(sst, -1, -2).reshape(depth, bsz, 2, 2, HEAD_DIM, 2, HEAD_DIM)
    s_hgrn = jnp.stack([s_t[:, :, :, 0, :, 0, :], s_t[:, :, :, 1, :, 1, :]], axis=3)
    c_t = cst.reshape(depth, bsz, 2, 2, 2, HEAD_DIM, LANES)
    c_rows = jnp.stack([c_t[:, :, :, 0, 0], c_t[:, :, :, 1, 1]], axis=3)
    c_rows = c_rows.reshape(depth, bsz, N_HEADS, HEAD_DIM, LANES)
    prompt_out = (hl[:, :, 0], conv[:, :, 8 - (CONV_W - 1):], kst.reshape(depth, bsz, WINDOW, 2, HEAD_DIM),
                  vst.reshape(depth, bsz, WINDOW, 2, HEAD_DIM),
                  s_hgrn.reshape(depth, bsz, N_HEADS, HEAD_DIM, HEAD_DIM), c_rows[..., 0:HEAD_DIM],
                  c_rows[..., HEAD_DIM], mst[:, :, 0, 0:N_HEADS])
    hn, convn, nn, mn = (stack(s_small, i) for i in range(4))
    sample_out = (hn, jnp.transpose(convn, (0, 2, 1, 3)),
                  jnp.transpose(big['k'], (0, 1, 4, 2, 3)), jnp.transpose(big['v'], (0, 1, 4, 2, 3)),
                  jnp.transpose(big['s'], (0, 4, 1, 2, 3)), jnp.transpose(big['c'], (0, 4, 1, 2, 3)),
                  jnp.transpose(nn.reshape(depth, N_HEADS, HEAD_DIM, nsm), (0, 3, 1, 2)),
                  jnp.transpose(mn, (0, 2, 1)))
    return (hp, hs.reshape(x_sample.shape)) + prompt_out + sample_out


def kernel(x_prompt, x_sample, p_prompt, p_sample, state_rglru_h, state_rglru_conv, cache_swa_k, cache_swa_v, state_hgrn_s, state_mlstm_c, state_mlstm_n, state_mlstm_m, norm1_g, w_in, conv_w, conv_b, lru_wr, lru_br, lru_wi, lru_bi, lru_lam, q_norm_g, k_norm_g, attn_sinks, hgrn_gamma, hgrn_norm_g, mlstm_ib, mlstm_fb, mlstm_norm_g, w_out, norm2_g, w_up, w_down, w_ple_gate, w_ple_proj):
    w = {'norm1_g': norm1_g, 'w_in': w_in, 'conv_w': conv_w, 'conv_b': conv_b, 'lru_wr': lru_wr,
         'lru_br': lru_br, 'lru_wi': lru_wi, 'lru_bi': lru_bi, 'lru_lam': lru_lam, 'q_norm_g': q_norm_g,
         'k_norm_g': k_norm_g, 'attn_sinks': attn_sinks, 'hgrn_gamma': hgrn_gamma,
         'hgrn_norm_g': hgrn_norm_g, 'mlstm_ib': mlstm_ib, 'mlstm_fb': mlstm_fb,
         'mlstm_norm_g': mlstm_norm_g, 'w_out': w_out, 'norm2_g': norm2_g, 'w_up': w_up,
         'w_down': w_down, 'w_ple_gate': w_ple_gate, 'w_ple_proj': w_ple_proj}
    st = (state_rglru_h, state_rglru_conv, cache_swa_k, cache_swa_v, state_hgrn_s, state_mlstm_c,
          state_mlstm_n, state_mlstm_m)
    past_len = 8192
    return _run(x_prompt, x_sample, p_prompt, p_sample, st, w, past_len)
```
